```python
import math
import jax, jax.numpy as jnp
from jax import lax
import numpy as np

D_MODEL = 2048
BATCH = 8
SEQ = 2048
DEPTH = 1

HEAD_DIM = 128
N_GDN_HEADS = 8
N_MOBA_HEADS = 8
GDN_WIDTH = N_GDN_HEADS * HEAD_DIM
MOBA_WIDTH = N_MOBA_HEADS * HEAD_DIM
MIX_WIDTH = GDN_WIDTH + MOBA_WIDTH
IN_PROJ_WIDTH = 4 * GDN_WIDTH + 2 * N_GDN_HEADS + 3 * MOBA_WIDTH
GDN_CONV = 4
GDN_CHUNK = 64
MOBA_BLOCK = 256
MOBA_TOPK = 3
MOBA_Q_CHUNK = 64
REL_BUCKETS = 32
REL_MAX_DIST = 128
MEM_LEN = 256
N_XATTN_HEADS = 4
XATTN_WIDTH = N_XATTN_HEADS * HEAD_DIM
D_FF = 5632
FFN_CONV = 3
EPS = 1e-6
NEG = -1e30

kernel_name = "hybrid_gdn_moba_parallel_heads"


def rmsnorm(x, g):
    xf = x.astype(jnp.float32)
    y = xf * lax.rsqrt(jnp.mean(xf * xf, axis=-1, keepdims=True) + EPS)
    return (y * g.astype(jnp.float32)).astype(x.dtype)


def l2norm(x):
    xf = x.astype(jnp.float32)
    return (xf * lax.rsqrt(jnp.sum(xf * xf, axis=-1, keepdims=True) + EPS)).astype(x.dtype)


def causal_dwconv(x, w):
    K = w.shape[0]
    S = x.shape[1]
    xp = jnp.pad(x, ((0, 0), (K - 1, 0), (0, 0)))
    return sum(xp[:, j:j + S] * w[j] for j in range(K))


def rel_bucket(rel):
    n = jnp.maximum(-rel, 0)
    max_exact = REL_BUCKETS // 2
    nf = jnp.maximum(n, 1).astype(jnp.float32)
    large = max_exact + (jnp.log(nf / max_exact) / math.log(REL_MAX_DIST / max_exact)
                         * (REL_BUCKETS - max_exact)).astype(jnp.int32)
    large = jnp.minimum(large, REL_BUCKETS - 1)
    return jnp.where(n < max_exact, n, large)


def gated_delta_rule(q, k, v, g, beta):
    out_dtype = v.dtype
    B, H, S, Dk = q.shape
    Dv = v.shape[-1]
    C = GDN_CHUNK
    N = S // C
    f32 = jnp.float32
    q = (q.astype(f32) * Dk ** -0.5).reshape(B, H, N, C, Dk)
    k = k.astype(f32).reshape(B, H, N, C, Dk)
    v = v.astype(f32).reshape(B, H, N, C, Dv)
    g = g.astype(f32).reshape(B, H, N, C)
    beta = beta.astype(f32).reshape(B, H, N, C)
    G = jnp.cumsum(g, axis=-1)
    tri_incl = jnp.tril(jnp.ones((C, C), dtype=bool))
    tri_strict = jnp.tril(jnp.ones((C, C), dtype=bool), -1)
    decay = jnp.exp(jnp.where(tri_incl, G[..., :, None] - G[..., None, :], NEG))
    kb = k * beta[..., None]
    A = jnp.where(tri_strict, jnp.einsum('bhnid,bhnjd->bhnij', kb, k) * decay, 0.0)
    eye = jnp.eye(C, dtype=f32)
    T = lax.linalg.triangular_solve(eye + A, jnp.broadcast_to(eye, A.shape),
                                    left_side=True, lower=True)
    u = jnp.einsum('bhnij,bhnjd->bhnid', T, v * beta[..., None])
    w = jnp.einsum('bhnij,bhnjd->bhnid', T, kb * jnp.exp(G)[..., None])
    qk = jnp.einsum('bhnid,bhnjd->bhnij', q, k) * decay
    q_dec = q * jnp.exp(G)[..., None]
    k_dec = k * jnp.exp(G[..., -1:] - G)[..., None]
    g_last = jnp.exp(G[..., -1])

    def step(state, xs):
        q_i, k_i, u_i, w_i, qk_i, gl_i = xs
        v_new = u_i - jnp.einsum('bhck,bhkv->bhcv', w_i, state)
        o_i = (jnp.einsum('bhck,bhkv->bhcv', q_i, state)
               + jnp.einsum('bhij,bhjv->bhiv', qk_i, v_new))
        state = state * gl_i[..., None, None] + jnp.einsum('bhck,bhcv->bhkv', k_i, v_new)
        return state, o_i

    xs = tuple(jnp.moveaxis(t, 2, 0) for t in (q_dec, k_dec, u, w, qk, g_last))
    state0 = jnp.zeros((B, H, Dk, Dv), f32)
    _, o = lax.scan(step, state0, xs)
    o = jnp.moveaxis(o, 0, 2).reshape(B, H, S, Dv)
    return o.astype(out_dtype)


def moba_attention(q, k, v, rel_bias):
    B, H, S, Dh = q.shape
    nb = -(-S // MOBA_BLOCK)
    s_pad = nb * MOBA_BLOCK
    pad = ((0, 0), (0, 0), (0, s_pad - S), (0, 0))
    k_blocks = jnp.pad(k, pad).reshape(B, H, nb, MOBA_BLOCK, Dh)
    v_blocks = jnp.pad(v, pad).reshape(B, H, nb, MOBA_BLOCK, Dh)
    k_mean = jnp.mean(k_blocks.astype(jnp.float32), axis=3).astype(k.dtype)
    n_sel = min(MOBA_TOPK, nb)
    scale = Dh ** -0.5
    n_chunks = S // MOBA_Q_CHUNK
    offs = jnp.arange(MOBA_BLOCK)
    rb_t = rel_bias.T
    head_idx = jnp.arange(H)[:, None, None, None]

    def per_batch(args):
        qb, kb, vb, kmb = args

        def per_chunk(c):
            q0 = c * MOBA_Q_CHUNK
            qc = lax.dynamic_slice_in_dim(qb, q0, MOBA_Q_CHUNK, axis=1)
            q_pos = q0 + jnp.arange(MOBA_Q_CHUNK)
            own = q0 // MOBA_BLOCK
            gate = jnp.einsum('hqd,hnd->hqn', qc, kmb).astype(jnp.float32)
            gate = jnp.where(jnp.arange(nb) < own, gate, NEG)
            _, idx = lax.top_k(gate, n_sel)
            valid = jnp.arange(n_sel) < own
            k_sel = jax.vmap(lambda blk, ix: blk[ix])(kb, idx)
            v_sel = jax.vmap(lambda blk, ix: blk[ix])(vb, idx)
            pos_sel = idx[..., None] * MOBA_BLOCK + offs
            bias_sel = rb_t[head_idx, rel_bucket(pos_sel - q_pos[None, :, None, None])]
            logit_sel = (jnp.einsum('hqd,hqnkd->hqnk', qc, k_sel).astype(jnp.float32) * scale
                         + bias_sel.astype(jnp.float32))
            logit_sel = jnp.where(valid[None, None, :, None], logit_sel, NEG)
            k_own = lax.dynamic_index_in_dim(kb, own, axis=1, keepdims=False)
            v_own = lax.dynamic_index_in_dim(vb, own, axis=1, keepdims=False)
            pos_own = own * MOBA_BLOCK + offs
            rel_own = pos_own[None, :] - q_pos[:, None]
            logit_own = (jnp.einsum('hqd,hkd->hqk', qc, k_own).astype(jnp.float32) * scale
                         + rb_t[:, rel_bucket(rel_own)].astype(jnp.float32))
            logit_own = jnp.where((rel_own <= 0)[None], logit_own, NEG)
            logits = jnp.concatenate(
                [logit_sel.reshape(H, MOBA_Q_CHUNK, n_sel * MOBA_BLOCK), logit_own], axis=-1)
            p = jax.nn.softmax(logits, axis=-1).astype(v.dtype)
            p_sel = p[..., :n_sel * MOBA_BLOCK].reshape(H, MOBA_Q_CHUNK, n_sel, MOBA_BLOCK)
            p_own = p[..., n_sel * MOBA_BLOCK:]
            return (jnp.einsum('hqnk,hqnkd->hqd', p_sel, v_sel)
                    + jnp.einsum('hqk,hkd->hqd', p_own, v_own))

        o = lax.map(per_chunk, jnp.arange(n_chunks))
        return o.transpose(1, 0, 2, 3).reshape(H, S, Dh)

    return lax.map(per_batch, (q, k_blocks, v_blocks, k_mean))


def to_heads(t, n_heads):
    B, S, _ = t.shape
    return t.reshape(B, S, n_heads, HEAD_DIM).transpose(0, 2, 1, 3)


def hybrid_mixer(x, norm_g, w_in, gdn_conv_w, a_log, dt_bias, gdn_norm_g,
                 moba_norm_g, rel_bias, w_out):
    B, S, _ = x.shape
    h = rmsnorm(x, norm_g)
    proj = h @ w_in
    i0 = 3 * GDN_WIDTH
    i1 = i0 + GDN_WIDTH
    i2 = i1 + N_GDN_HEADS
    i3 = i2 + N_GDN_HEADS
    qkv_a, z_a, b_a, a_a, qkv_b = (proj[..., :i0], proj[..., i0:i1], proj[..., i1:i2],
                                   proj[..., i2:i3], proj[..., i3:])
    qkv_a = jax.nn.silu(causal_dwconv(qkv_a, gdn_conv_w))
    q_a, k_a, v_a = [to_heads(t, N_GDN_HEADS) for t in jnp.split(qkv_a, 3, axis=-1)]
    beta = jax.nn.sigmoid(b_a).transpose(0, 2, 1)
    g = (-jnp.exp(a_log) * jax.nn.softplus(a_a + dt_bias)).transpose(0, 2, 1)
    o_a = gated_delta_rule(l2norm(q_a), l2norm(k_a), v_a, g, beta).transpose(0, 2, 1, 3)
    o_a = rmsnorm(o_a, gdn_norm_g) * jax.nn.silu(z_a.reshape(B, S, N_GDN_HEADS, HEAD_DIM))
    q_b, k_b, v_b = [to_heads(t, N_MOBA_HEADS) for t in jnp.split(qkv_b, 3, axis=-1)]
    o_b = moba_attention(q_b, k_b, v_b, rel_bias).transpose(0, 2, 1, 3)
    o_b = rmsnorm(o_b, moba_norm_g)
    o = jnp.concatenate([o_a.reshape(B, S, GDN_WIDTH), o_b.reshape(B, S, MOBA_WIDTH)], axis=-1)
    return o @ w_out


def cross_attention(x, mem, norm_g, mem_norm_g, w_xq, w_xkv, w_xo):
    B, S, _ = x.shape
    M = mem.shape[1]
    q = (rmsnorm(x, norm_g) @ w_xq).reshape(B, S, N_XATTN_HEADS, HEAD_DIM)
    kv = rmsnorm(mem, mem_norm_g) @ w_xkv
    k, v = [t.reshape(B, M, N_XATTN_HEADS, HEAD_DIM) for t in jnp.split(kv, 2, axis=-1)]
    logits = jnp.einsum('bshd,bmhd->bhsm', q, k).astype(jnp.float32) * HEAD_DIM ** -0.5
    p = jax.nn.softmax(logits, axis=-1).astype(v.dtype)
    o = jnp.einsum('bhsm,bmhd->bshd', p, v).reshape(B, S, XATTN_WIDTH)
    return o @ w_xo


def conv_ffn(x, norm_g, w_gate, w_up, conv_w, conv_b, w_down):
    h = rmsnorm(x, norm_g)
    gate = causal_dwconv(h @ w_gate, conv_w) + conv_b
    return (jax.nn.silu(gate) * (h @ w_up)) @ w_down


def setup_inputs(seed: int = 0) -> dict:
    key = jax.random.key(seed)
    ks = jax.random.split(key, 24)
    f32 = jnp.float32
    L = DEPTH

    def nrm(k, shape, scale):
        return jax.random.normal(k, shape, f32) * scale

    def gain(k, shape):
        return 1.0 + 0.02 * jax.random.normal(k, shape, f32)

    dt = jnp.exp(jax.random.uniform(ks[6], (L, N_GDN_HEADS), f32, math.log(1e-3), math.log(1e-1)))
    return {
        "x": nrm(ks[0], (BATCH, SEQ, D_MODEL), 1.0),
        "mem": nrm(ks[1], (BATCH, MEM_LEN, D_MODEL), 1.0),
        "mix_norm_g": gain(ks[2], (L, D_MODEL)),
        "w_in": nrm(ks[3], (L, D_MODEL, IN_PROJ_WIDTH), D_MODEL ** -0.5),
        "gdn_conv_w": nrm(ks[4], (L, GDN_CONV, 3 * GDN_WIDTH), GDN_CONV ** -0.5),
        "gdn_a_log": jnp.log(jax.random.uniform(ks[5], (L, N_GDN_HEADS), f32, 1.0, 16.0)),
        "gdn_dt_bias": dt + jnp.log(-jnp.expm1(-dt)),
        "gdn_norm_g": gain(ks[7], (L, HEAD_DIM)),
        "moba_norm_g": gain(ks[8], (L, HEAD_DIM)),
        "rel_bias": nrm(ks[9], (REL_BUCKETS, N_MOBA_HEADS), 0.5),
        "w_out": nrm(ks[10], (L, MIX_WIDTH, D_MODEL), MIX_WIDTH ** -0.5),
        "xattn_norm_g": gain(ks[11], (L, D_MODEL)),
        "mem_norm_g": gain(ks[12], (L, D_MODEL)),
        "w_xq": nrm(ks[13], (L, D_MODEL, XATTN_WIDTH), D_MODEL ** -0.5),
        "w_xkv": nrm(ks[14], (L, D_MODEL, 2 * XATTN_WIDTH), D_MODEL ** -0.5),
        "w_xo": nrm(ks[15], (L, XATTN_WIDTH, D_MODEL), XATTN_WIDTH ** -0.5),
        "ffn_norm_g": gain(ks[16], (L, D_MODEL)),
        "w_gate": nrm(ks[17], (L, D_MODEL, D_FF), D_MODEL ** -0.5),
        "w_up": nrm(ks[18], (L, D_MODEL, D_FF), D_MODEL ** -0.5),
        "ffn_conv_w": nrm(ks[19], (L, FFN_CONV, D_FF), FFN_CONV ** -0.5),
        "ffn_conv_b": nrm(ks[20], (L, D_FF), 0.02),
        "w_down": nrm(ks[21], (L, D_FF, D_MODEL), D_FF ** -0.5),
        "final_norm_g": gain(ks[22], (D_MODEL,)),
    }


def reference(x, mem, mix_norm_g, w_in, gdn_conv_w, gdn_a_log, gdn_dt_bias, gdn_norm_g,
              moba_norm_g, rel_bias, w_out, xattn_norm_g, mem_norm_g, w_xq, w_xkv, w_xo,
              ffn_norm_g, w_gate, w_up, ffn_conv_w, ffn_conv_b, w_down, final_norm_g):
    for l in range(DEPTH):
        x = x + hybrid_mixer(x, mix_norm_g[l], w_in[l], gdn_conv_w[l], gdn_a_log[l],
                             gdn_dt_bias[l], gdn_norm_g[l], moba_norm_g[l], rel_bias, w_out[l])
        x = x + cross_attention(x, mem, xattn_norm_g[l], mem_norm_g[l], w_xq[l], w_xkv[l], w_xo[l])
        x = x + conv_ffn(x, ffn_norm_g[l], w_gate[l], w_up[l], ffn_conv_w[l], ffn_conv_b[l], w_down[l])
    return rmsnorm(x, final_norm_g)
```

```python
import functools
import math

import jax
import jax.numpy as jnp
import numpy as np
from jax import lax
from jax.experimental import pallas as pl
from jax.experimental.pallas import tpu as pltpu

HEAD_DIM = 128
N_GDN_HEADS = 8
N_MOBA_HEADS = 8
GDN_WIDTH = N_GDN_HEADS * HEAD_DIM
MOBA_WIDTH = N_MOBA_HEADS * HEAD_DIM
GDN_CONV = 4
GDN_CHUNK = 128
MOBA_BLOCK = 256
MOBA_TOPK = 3
REL_BUCKETS = 32
REL_MAX_DIST = 128
N_XATTN_HEADS = 4
FFN_CONV = 3
EPS = 1e-6
NEG = -1e30
SUBLANES = 8
VMEM_LIMIT = 56 * 1024 * 1024

F32 = jnp.float32
BF16 = jnp.bfloat16


def _bdot(a, b):
    return jnp.dot(a.astype(BF16), b.astype(BF16), preferred_element_type=F32)


def _bdot_nt(a, b):
    return lax.dot_general(a.astype(BF16), b.astype(BF16), (((1,), (1,)), ((), ())),
                           preferred_element_type=F32)


def _bdot_tn(a, b):
    return lax.dot_general(a.astype(BF16), b.astype(BF16), (((0,), (0,)), ((), ())),
                           preferred_element_type=F32)


def _fdot(a, b):
    return jnp.dot(a, b, preferred_element_type=F32, precision=lax.Precision.HIGHEST)


def _sigmoid(x):
    return 1.0 / (1.0 + jnp.exp(-x))


def _silu(x):
    return x * _sigmoid(x)


def _rms(x, g):
    return x * lax.rsqrt(jnp.mean(x * x, axis=-1, keepdims=True) + EPS) * g


def _norm_matmul_kernel(x_ref, g_ref, w_ref, o_ref, h_ref):
    @pl.when(pl.program_id(1) == 0)
    def _():
        h_ref[...] = _rms(x_ref[...], g_ref[...]).astype(BF16)

    o_ref[...] = jnp.dot(h_ref[...], w_ref[...], preferred_element_type=F32).astype(o_ref.dtype)


def _norm_matmul(x, g, w, *, tm, tn, out_dtype=F32):
    m, k = x.shape
    n = w.shape[1]
    assert m % tm == 0 and n % tn == 0
    return pl.pallas_call(
        _norm_matmul_kernel,
        grid=(m // tm, n // tn),
        in_specs=[pl.BlockSpec((tm, k), lambda i, j: (i, 0)),
                  pl.BlockSpec((1, k), lambda i, j: (0, 0)),
                  pl.BlockSpec((k, tn), lambda i, j: (0, j))],
        out_specs=pl.BlockSpec((tm, tn), lambda i, j: (i, j)),
        out_shape=jax.ShapeDtypeStruct((m, n), out_dtype),
        scratch_shapes=[pltpu.VMEM((tm, k), BF16)],
        compiler_params=pltpu.CompilerParams(
            dimension_semantics=("parallel", "arbitrary"), vmem_limit_bytes=VMEM_LIMIT),
        name="norm_matmul",
    )(x, g.reshape(1, k), w)


def _unit_lower_inverse(a, row, col):
    eye = (row == col).astype(F32)
    blk = lambda n: (row // n) == (col // n)
    b16, b32, b64 = blk(16), blk(32), blk(64)
    d = jnp.where(b16, a, 0.0)
    t = eye - d
    p = _fdot(d, d)
    t = t + _fdot(t, p)
    p = _fdot(p, p)
    t = t + _fdot(t, p)
    p = _fdot(p, p)
    t = t + _fdot(t, p)
    for inner, outer in ((b16, b32), (b32, b64), (b64, None)):
        keep = ~inner if outer is None else (outer & ~inner)
        e = jnp.where(keep, a, 0.0)
        t = t - _fdot(t, _fdot(e, t))
    return t


def _gdn_kernel(alog_ref, dtb_ref, q_ref, k_ref, v_ref, z_ref, ba_ref, wq_ref, wk_ref, wv_ref,
                ng_ref, o_ref, qp_ref, kp_ref, vp_ref):
    h = pl.program_id(1)
    seq = q_ref.shape[1]
    C = GDN_CHUNK
    pad = SUBLANES

    for src, dst in ((q_ref, qp_ref), (k_ref, kp_ref), (v_ref, vp_ref)):
        dst[0:pad, :] = jnp.zeros((pad, HEAD_DIM), F32)
        dst[pad:pad + seq, :] = src[0]

    row = lax.broadcasted_iota(jnp.int32, (C, C), 0)
    col = lax.broadcasted_iota(jnp.int32, (C, C), 1)
    tri_incl = row >= col
    tri_strict = row > col
    tril_f = tri_incl.astype(F32)
    lane = lax.broadcasted_iota(jnp.int32, (C, HEAD_DIM), 1)
    neg_a = -jnp.exp(jnp.full((C, 1), alog_ref[h], F32))
    dt_bias = dtb_ref[h]
    scale = HEAD_DIM ** -0.5

    def conv_silu(xp_ref, w_ref, r0):
        win = xp_ref[pl.ds(r0, C + pad), :]
        w = w_ref[...]
        y = win[pad:pad + C] * w[3:4]
        for j in range(GDN_CONV - 1):
            s = GDN_CONV - 1 - j
            y = y + win[pad - s:pad - s + C] * w[j:j + 1]
        return _silu(y)

    def l2n(x):
        return x * lax.rsqrt(jnp.sum(x * x, axis=-1, keepdims=True) + EPS)

    def chunk(c, state):
        r0 = pl.multiple_of(c * C, C)
        q = l2n(conv_silu(qp_ref, wq_ref, r0)) * scale
        k = l2n(conv_silu(kp_ref, wk_ref, r0))
        v = conv_silu(vp_ref, wv_ref, r0)
        ba = ba_ref[0, pl.ds(r0, C), :]
        b_col = jnp.sum(jnp.where(lane == h, ba, 0.0), axis=1, keepdims=True)
        a_col = jnp.sum(jnp.where(lane == h + N_GDN_HEADS, ba, 0.0), axis=1, keepdims=True)
        beta = _sigmoid(b_col)
        xs = a_col + dt_bias
        softplus = jnp.maximum(xs, 0.0) + jnp.log1p(jnp.exp(-jnp.abs(xs)))
        g = neg_a * softplus
        gb = _fdot(tril_f, jnp.broadcast_to(g, (C, HEAD_DIM)))
        gr = gb.T
        decay = jnp.exp(jnp.where(tri_incl, gb - gr, NEG))
        g_last = gb[C - 1:C, :]
        eg = jnp.exp(gb)
        kb = k * beta
        a = jnp.where(tri_strict, _bdot_nt(kb, k) * decay, 0.0)
        t = _unit_lower_inverse(a, row, col)
        u = _bdot(t, v * beta)
        w = _bdot(t, kb * eg)
        qk = _bdot_nt(q, k) * decay
        q_dec = q * eg
        k_dec = k * jnp.exp(g_last - gb)
        v_new = u - _bdot(w, state)
        o = _bdot(q_dec, state) + _bdot(qk, v_new)
        state = state * jnp.exp(g_last) + _bdot_tn(k_dec, v_new)
        z = z_ref[0, pl.ds(r0, C), :]
        o_ref[0, pl.ds(r0, C), :] = (_rms(o, ng_ref[...]) * _silu(z)).astype(o_ref.dtype)
        return state

    lax.fori_loop(0, seq // C, chunk, jnp.zeros((HEAD_DIM, HEAD_DIM), F32))


def _gdn(pa, ba, conv_w, a_log, dt_bias, norm_g):
    b, s, _ = pa.shape
    H = N_GDN_HEADS
    head = lambda off: pl.BlockSpec((1, s, HEAD_DIM), lambda i, j: (i, 0, off + j))
    cw = lambda off: pl.BlockSpec((GDN_CONV, HEAD_DIM), lambda i, j: (0, off + j))
    smem = pl.BlockSpec(memory_space=pltpu.SMEM)
    return pl.pallas_call(
        _gdn_kernel,
        grid=(b, H),
        in_specs=[smem, smem, head(0), head(H), head(2 * H), head(3 * H),
                  pl.BlockSpec((1, s, HEAD_DIM), lambda i, j: (i, 0, 0)),
                  cw(0), cw(H), cw(2 * H),
                  pl.BlockSpec((1, HEAD_DIM), lambda i, j: (0, 0))],
        out_specs=pl.BlockSpec((1, s, HEAD_DIM), lambda i, j: (i, 0, j)),
        out_shape=jax.ShapeDtypeStruct((b, s, GDN_WIDTH), F32),
        scratch_shapes=[pltpu.VMEM((s + SUBLANES, HEAD_DIM), F32)] * 3,
        compiler_params=pltpu.CompilerParams(
            dimension_semantics=("parallel", "arbitrary"), vmem_limit_bytes=VMEM_LIMIT),
        name="gdn",
    )(a_log, dt_bias, pa, pa, pa, pa, ba, conv_w, conv_w, conv_w, norm_g.reshape(1, HEAD_DIM))


def _bucket_upper_bounds():
    n = np.arange(0, 4 * REL_MAX_DIST, dtype=np.int64)
    max_exact = REL_BUCKETS // 2
    nf = np.maximum(n, 1).astype(np.float32)
    large = max_exact + (np.log(nf / np.float32(max_exact)) / np.float32(math.log(REL_MAX_DIST / max_exact))
                         * np.float32(REL_BUCKETS - max_exact)).astype(np.int32)
    large = np.minimum(large, REL_BUCKETS - 1)
    bucket = np.where(n < max_exact, n, large)
    assert np.all(np.diff(bucket) >= 0) and bucket[-1] == REL_BUCKETS - 1
    return [int(np.searchsorted(bucket, b, side="right")) for b in range(REL_BUCKETS - 1)]


_BUCKET_UPPER = _bucket_upper_bounds()


def _moba_kernel(rb_ref, q_ref, k_ref, v_ref, ng_ref, o_ref, kb_ref, vb_ref, km_ref, bd_ref, bl_ref):
    h = pl.program_id(0)
    b = pl.program_id(1)
    i = pl.program_id(2)
    seq = k_ref.shape[1]
    nb = seq // MOBA_BLOCK
    T = MOBA_BLOCK
    scale = HEAD_DIM ** -0.5
    rr = lax.broadcasted_iota(jnp.int32, (T, T), 0)
    cc = lax.broadcasted_iota(jnp.int32, (T, T), 1)

    @pl.when(jnp.logical_and(b == 0, i == 0))
    def _():
        def bias_of(n):
            val = jnp.full((T, T), rb_ref[REL_BUCKETS - 1, h], F32)
            for bkt in range(REL_BUCKETS - 2, -1, -1):
                val = jnp.where(n < _BUCKET_UPPER[bkt], rb_ref[bkt, h], val)
            return val
        bd_ref[...] = bias_of(rr - cc)
        bl_ref[...] = bias_of(rr - cc + T)

    @pl.when(i == 0)
    def _():
        kf = k_ref[0]
        kb_ref[...] = kf.astype(BF16)
        vb_ref[...] = v_ref[0].astype(BF16)
        for n in range(nb):
            km_ref[n:n + 1, :] = jnp.mean(kf[n * T:(n + 1) * T], axis=0, keepdims=True)

    qf = q_ref[0]
    qb = qf.astype(BF16)

    gate = lax.dot_general(km_ref[...], qf, (((1,), (1,)), ((), ())),
                           preferred_element_type=F32, precision=lax.Precision.HIGHEST)
    blk = lax.broadcasted_iota(jnp.int32, (nb, T), 0)
    rank = jnp.zeros((nb, T), jnp.int32)
    for m in range(nb):
        gm = gate[m:m + 1, :]
        ahead = (gm > gate) | ((gm == gate) & (m < blk))
        rank = rank + jnp.where(ahead, 1, 0) * (m < i).astype(jnp.int32)
    sel_t = jnp.where((blk < i) & (rank < MOBA_TOPK), 1.0, 0.0)
    eye = (rr == cc).astype(BF16)
    sel = lax.dot_general(eye, sel_t.astype(BF16), (((1,), (1,)), ((), ())),
                          preferred_element_type=F32)
    blk_lane = lax.broadcasted_iota(jnp.int32, (T, nb), 1)

    def scores(j):
        kj = kb_ref[pl.ds(pl.multiple_of(j * T, T), T), :]
        return _bdot_nt(qb, kj) * scale

    def attend(j, s, carry):
        m_prev, l_prev, acc = carry
        m_new = jnp.maximum(m_prev, jnp.max(s, axis=-1, keepdims=True))
        alpha = jnp.exp(m_prev - m_new)
        p = jnp.exp(s - m_new)
        l_new = alpha * l_prev + jnp.sum(p, axis=-1, keepdims=True)
        vj = vb_ref[pl.ds(pl.multiple_of(j * T, T), T), :]
        return m_new, l_new, alpha * acc + _bdot(p, vj)

    def selected(j):
        return jnp.sum(jnp.where(blk_lane == j, sel, 0.0), axis=1, keepdims=True) > 0.5

    s_own = jnp.where(rr >= cc, scores(i) + bd_ref[...], NEG)
    carry = (jnp.full((T, 1), NEG, F32), jnp.zeros((T, 1), F32), jnp.zeros((T, HEAD_DIM), F32))
    carry = attend(i, s_own, carry)

    def prev_block(carry):
        j = i - 1
        s = jnp.where(selected(j), scores(j) + bl_ref[...], NEG)
        return attend(j, s, carry)

    carry = lax.cond(i >= 1, prev_block, lambda c: c, carry)

    far_bias = rb_ref[REL_BUCKETS - 1, h]

    def far_block(j, carry):
        s = jnp.where(selected(j), scores(j) + far_bias, NEG)
        return attend(j, s, carry)

    _, l_fin, acc = lax.fori_loop(0, jnp.maximum(i - 1, 0), far_block, carry)
    o = acc / l_fin
    o_ref[0] = _rms(o, ng_ref[...]).astype(o_ref.dtype)


def _moba(pb, rel_bias, norm_g):
    b, s, _ = pb.shape
    H = N_MOBA_HEADS
    T = MOBA_BLOCK
    assert s % T == 0
    nb = s // T
    return pl.pallas_call(
        _moba_kernel,
        grid=(H, b, nb),
        in_specs=[pl.BlockSpec(memory_space=pltpu.SMEM),
                  pl.BlockSpec((1, T, HEAD_DIM), lambda h, bb, i: (bb, i, h)),
                  pl.BlockSpec((1, s, HEAD_DIM), lambda h, bb, i: (bb, 0, H + h)),
                  pl.BlockSpec((1, s, HEAD_DIM), lambda h, bb, i: (bb, 0, 2 * H + h)),
                  pl.BlockSpec((1, HEAD_DIM), lambda h, bb, i: (0, 0))],
        out_specs=pl.BlockSpec((1, T, HEAD_DIM), lambda h, bb, i: (bb, i, h)),
        out_shape=jax.ShapeDtypeStruct((b, s, MOBA_WIDTH), F32),
        scratch_shapes=[pltpu.VMEM((s, HEAD_DIM), BF16), pltpu.VMEM((s, HEAD_DIM), BF16),
                        pltpu.VMEM((nb, HEAD_DIM), F32),
                        pltpu.VMEM((T, T), F32), pltpu.VMEM((T, T), F32)],
        compiler_params=pltpu.CompilerParams(
            dimension_semantics=("arbitrary", "arbitrary", "arbitrary"), vmem_limit_bytes=VMEM_LIMIT),
        name="moba",
    )(rel_bias, pb, pb, pb, norm_g.reshape(1, HEAD_DIM))


def _out_proj_kernel(x_ref, oa_ref, ob_ref, wa_ref, wb_ref, o_ref):
    o_ref[...] = (x_ref[...] + _bdot(oa_ref[...], wa_ref[...]) + _bdot(ob_ref[...], wb_ref[...]))


def _out_proj(x, oa, ob, wa, wb, *, tm, tn):
    m, d = x.shape
    ka, kb = oa.shape[1], ob.shape[1]
    return pl.pallas_call(
        _out_proj_kernel,
        grid=(m // tm, d // tn),
        in_specs=[pl.BlockSpec((tm, tn), lambda i, j: (i, j)),
                  pl.BlockSpec((tm, ka), lambda i, j: (i, 0)),
                  pl.BlockSpec((tm, kb), lambda i, j: (i, 0)),
                  pl.BlockSpec((ka, tn), lambda i, j: (0, j)),
                  pl.BlockSpec((kb, tn), lambda i, j: (0, j))],
        out_specs=pl.BlockSpec((tm, tn), lambda i, j: (i, j)),
        out_shape=jax.ShapeDtypeStruct((m, d), F32),
        compiler_params=pltpu.CompilerParams(
            dimension_semantics=("parallel", "arbitrary"), vmem_limit_bytes=VMEM_LIMIT),
        name="out_proj",
    )(x, oa, ob, wa, wb)


def _xattn_kernel(x_ref, g_ref, wq_ref, kv_ref, wo_ref, o_ref):
    x = x_ref[0]
    hn = _rms(x, g_ref[...])
    q = _bdot(hn, wq_ref[...])
    kv = kv_ref[0]
    width = N_XATTN_HEADS * HEAD_DIM
    outs = []
    for hd in range(N_XATTN_HEADS):
        sl = slice(hd * HEAD_DIM, (hd + 1) * HEAD_DIM)
        kh = kv[:, sl]
        vh = kv[:, width + hd * HEAD_DIM: width + (hd + 1) * HEAD_DIM]
        s = _bdot_nt(q[:, sl], kh) * HEAD_DIM ** -0.5
        p = jnp.exp(s - jnp.max(s, axis=-1, keepdims=True))
        p = p / jnp.sum(p, axis=-1, keepdims=True)
        outs.append(_bdot(p, vh))
    o = jnp.concatenate(outs, axis=-1)
    o_ref[0] = x + _bdot(o, wo_ref[...])


def _xattn(x, g, wq, kv, wo, *, ts):
    b, s, d = x.shape
    mlen = kv.shape[1]
    width = wq.shape[1]
    return pl.pallas_call(
        _xattn_kernel,
        grid=(b, s // ts),
        in_specs=[pl.BlockSpec((1, ts, d), lambda i, j: (i, j, 0)),
                  pl.BlockSpec((1, d), lambda i, j: (0, 0)),
                  pl.BlockSpec((d, width), lambda i, j: (0, 0)),
                  pl.BlockSpec((1, mlen, 2 * width), lambda i, j: (i, 0, 0)),
                  pl.BlockSpec((width, d), lambda i, j: (0, 0))],
        out_specs=pl.BlockSpec((1, ts, d), lambda i, j: (i, j, 0)),
        out_shape=jax.ShapeDtypeStruct((b, s, d), F32),
        compiler_params=pltpu.CompilerParams(
            dimension_semantics=("parallel", "parallel"), vmem_limit_bytes=VMEM_LIMIT),
        name="xattn",
    )(x, g.reshape(1, d), wq, kv, wo)


def _ffn_kernel(x_ref, halo_ref, g_ref, wg_ref, wu_ref, cw_ref, cb_ref, wd_ref, fg_ref, o_ref,
                h_ref, acc_ref, *, tiles_per_seq):
    i = pl.program_id(0)
    f = pl.program_id(1)
    tm = x_ref.shape[0]
    pad = SUBLANES

    @pl.when(f == 0)
    def _():
        halo = _rms(halo_ref[...], g_ref[...])
        halo = jnp.where(i % tiles_per_seq == 0, 0.0, halo)
        h_ref[0:pad, :] = halo.astype(BF16)
        h_ref[pad:pad + tm, :] = _rms(x_ref[...], g_ref[...]).astype(BF16)
        acc_ref[...] = jnp.zeros_like(acc_ref)

    hx = h_ref[...]
    gp = jnp.dot(hx, wg_ref[...], preferred_element_type=F32)
    up = jnp.dot(hx[pad:], wu_ref[...], preferred_element_type=F32)
    cw = cw_ref[...]
    gate = gp[pad:pad + tm] * cw[2:3] + gp[pad - 1:pad - 1 + tm] * cw[1:2] \
        + gp[pad - 2:pad - 2 + tm] * cw[0:1] + cb_ref[...]
    act = _silu(gate) * up
    acc_ref[...] += jnp.dot(act.astype(BF16), wd_ref[...], preferred_element_type=F32)

    @pl.when(f == pl.num_programs(1) - 1)
    def _():
        o_ref[...] = _rms(x_ref[...] + acc_ref[...], fg_ref[...])


def _ffn(x, g, wg, wu, cw, cb, wd, fg, *, seq, tm, tf):
    m, d = x.shape
    ff = wg.shape[1]
    assert seq % tm == 0 and ff % tf == 0 and tm % SUBLANES == 0
    hb = tm // SUBLANES
    return pl.pallas_call(
        functools.partial(_ffn_kernel, tiles_per_seq=seq // tm),
        grid=(m // tm, ff // tf),
        in_specs=[pl.BlockSpec((tm, d), lambda i, f: (i, 0)),
                  pl.BlockSpec((SUBLANES, d), lambda i, f: (jnp.maximum(i * hb - 1, 0), 0)),
                  pl.BlockSpec((1, d), lambda i, f: (0, 0)),
                  pl.BlockSpec((d, tf), lambda i, f: (0, f)),
                  pl.BlockSpec((d, tf), lambda i, f: (0, f)),
                  pl.BlockSpec((FFN_CONV, tf), lambda i, f: (0, f)),
                  pl.BlockSpec((1, tf), lambda i, f: (0, f)),
                  pl.BlockSpec((tf, d), lambda i, f: (f, 0)),
                  pl.BlockSpec((1, d), lambda i, f: (0, 0))],
        out_specs=pl.BlockSpec((tm, d), lambda i, f: (i, 0)),
        out_shape=jax.ShapeDtypeStruct((m, d), F32),
        scratch_shapes=[pltpu.VMEM((tm + SUBLANES, d), BF16), pltpu.VMEM((tm, d), F32)],
        compiler_params=pltpu.CompilerParams(
            dimension_semantics=("parallel", "arbitrary"), vmem_limit_bytes=VMEM_LIMIT),
        name="ffn",
    )(x, x, g.reshape(1, d), wg, wu, cw, cb.reshape(1, ff), wd, fg.reshape(1, d))


def _layer(x, mem, mix_norm_g, w_in, gdn_conv_w, a_log, dt_bias, gdn_norm_g, moba_norm_g, rel_bias,
           w_out, xattn_norm_g, mem_norm_g, w_xq, w_xkv, w_xo, ffn_norm_g, w_gate, w_up, ffn_conv_w,
           ffn_conv_b, w_down, final_g, *, last):
    b, s, d = x.shape
    m = b * s
    x2 = x.reshape(m, d)
    i1 = 4 * GDN_WIDTH
    i3 = i1 + 2 * N_GDN_HEADS
    w_a = w_in[:, :i1].astype(BF16)
    w_ba = jnp.pad(w_in[:, i1:i3], ((0, 0), (0, HEAD_DIM - 2 * N_GDN_HEADS))).astype(BF16)
    w_b = w_in[:, i3:].astype(BF16)
    tm = min(512, m)
    pa = _norm_matmul(x2, mix_norm_g, w_a, tm=tm, tn=1024).reshape(b, s, i1)
    ba = _norm_matmul(x2, mix_norm_g, w_ba, tm=tm, tn=HEAD_DIM).reshape(b, s, HEAD_DIM)
    pb = _norm_matmul(x2, mix_norm_g, w_b, tm=tm, tn=1024).reshape(b, s, 3 * MOBA_WIDTH)
    o_a = _gdn(pa, ba, gdn_conv_w, a_log, dt_bias, gdn_norm_g)
    o_b = _moba(pb, rel_bias, moba_norm_g)
    x1 = _out_proj(x2, o_a.reshape(m, GDN_WIDTH), o_b.reshape(m, MOBA_WIDTH),
                   w_out[:GDN_WIDTH].astype(BF16), w_out[GDN_WIDTH:].astype(BF16), tm=tm, tn=1024)
    mlen = mem.shape[1]
    kv = _norm_matmul(mem.reshape(b * mlen, d), mem_norm_g, w_xkv.astype(BF16),
                      tm=min(512, b * mlen), tn=1024).reshape(b, mlen, -1)
    x2b = _xattn(x1.reshape(b, s, d), xattn_norm_g, w_xq.astype(BF16), kv, w_xo.astype(BF16),
                 ts=min(512, s))
    assert last, "the final rmsnorm is fused into the last layer's ffn"
    y = _ffn(x2b.reshape(m, d), ffn_norm_g, w_gate.astype(BF16), w_up.astype(BF16), ffn_conv_w,
             ffn_conv_b, w_down.astype(BF16), final_g, seq=s, tm=min(512, s), tf=512)
    return y.reshape(b, s, d)


def kernel(x, mem, mix_norm_g, w_in, gdn_conv_w, gdn_a_log, gdn_dt_bias, gdn_norm_g, moba_norm_g,
           rel_bias, w_out, xattn_norm_g, mem_norm_g, w_xq, w_xkv, w_xo, ffn_norm_g, w_gate, w_up,
           ffn_conv_w, ffn_conv_b, w_down, final_norm_g):
    depth = mix_norm_g.shape[0]
    assert depth == 1
    l = 0
    return _layer(x, mem, mix_norm_g[l], w_in[l], gdn_conv_w[l], gdn_a_log[l], gdn_dt_bias[l],
                  gdn_norm_g[l], moba_norm_g[l], rel_bias, w_out[l], xattn_norm_g[l], mem_norm_g[l],
                  w_xq[l], w_xkv[l], w_xo[l], ffn_norm_g[l], w_gate[l], w_up[l], ffn_conv_w[l],
                  ffn_conv_b[l], w_down[l], final_norm_g, last=True)
```

```python
import functools
import math

import jax
import jax.numpy as jnp
import numpy as np
from jax import lax
from jax.experimental import pallas as pl
from jax.experimental.pallas import tpu as pltpu

HEAD_DIM = 128
N_GDN_HEADS = 8
N_MOBA_HEADS = 8
GDN_WIDTH = N_GDN_HEADS * HEAD_DIM
MOBA_WIDTH = N_MOBA_HEADS * HEAD_DIM
GDN_CONV = 4
GDN_CHUNK = 256
MOBA_BLOCK = 256
MOBA_TOPK = 3
REL_BUCKETS = 32
REL_MAX_DIST = 128
N_XATTN_HEADS = 4
FFN_CONV = 3
EPS = 1e-6
NEG = -1e30
SUBLANES = 8
VMEM_LIMIT = 56 * 1024 * 1024

F32 = jnp.float32
BF16 = jnp.bfloat16


def _bdot(a, b):
    return jnp.dot(a.astype(BF16), b.astype(BF16), preferred_element_type=F32)


def _bdot_nt(a, b):
    return lax.dot_general(a.astype(BF16), b.astype(BF16), (((1,), (1,)), ((), ())),
                           preferred_element_type=F32)


def _bdot_tn(a, b):
    return lax.dot_general(a.astype(BF16), b.astype(BF16), (((0,), (0,)), ((), ())),
                           preferred_element_type=F32)


def _fdot(a, b):
    return jnp.dot(a, b, preferred_element_type=F32, precision=lax.Precision.HIGHEST)


def _sigmoid(x):
    return 1.0 / (1.0 + jnp.exp(-x))


def _silu(x):
    return x * _sigmoid(x)


def _rms(x, g):
    return x * lax.rsqrt(jnp.mean(x * x, axis=-1, keepdims=True) + EPS) * g


def _norm_matmul_kernel(x_ref, g_ref, w_ref, o_ref, h_ref):
    @pl.when(pl.program_id(1) == 0)
    def _():
        h_ref[...] = _rms(x_ref[...], g_ref[...]).astype(BF16)

    o_ref[...] = jnp.dot(h_ref[...], w_ref[...], preferred_element_type=F32).astype(o_ref.dtype)


def _norm_matmul(x, g, w, *, tm, tn, out_dtype=F32):
    m, k = x.shape
    n = w.shape[1]
    assert m % tm == 0 and n % tn == 0
    return pl.pallas_call(
        _norm_matmul_kernel,
        grid=(m // tm, n // tn),
        in_specs=[pl.BlockSpec((tm, k), lambda i, j: (i, 0)),
                  pl.BlockSpec((1, k), lambda i, j: (0, 0)),
                  pl.BlockSpec((k, tn), lambda i, j: (0, j))],
        out_specs=pl.BlockSpec((tm, tn), lambda i, j: (i, j)),
        out_shape=jax.ShapeDtypeStruct((m, n), out_dtype),
        scratch_shapes=[pltpu.VMEM((tm, k), BF16)],
        compiler_params=pltpu.CompilerParams(
            dimension_semantics=("parallel", "arbitrary"), vmem_limit_bytes=VMEM_LIMIT),
        name="norm_matmul",
    )(x, g.reshape(1, k), w)


def _unit_lower_inverse(mats, row, col):
    c = mats[0].shape[0]
    eye = (row == col).astype(F32)
    blk = lambda n: (row // n) == (col // n)
    inner = blk(16)
    ds = [jnp.where(inner, a, 0.0) for a in mats]
    ts = [eye - d for d in ds]
    ps = [_bdot(d, d) for d in ds]
    for step in range(3):
        ts = [t + _bdot(t, p) for t, p in zip(ts, ps)]
        if step < 2:
            ps = [_bdot(p, p) for p in ps]
    n = 32
    while n <= c:
        outer = blk(n) if n < c else None
        keep = ~inner if outer is None else (outer & ~inner)
        xs = [_bdot(jnp.where(keep, a, 0.0), t) for a, t in zip(mats, ts)]
        ts = [t - _bdot(t, x) for t, x in zip(ts, xs)]
        inner = outer
        n *= 2
    return ts


def _gdn_kernel(alog_ref, dtb_ref, q_ref, k_ref, v_ref, z_ref, ba_ref, wq_ref, wk_ref, wv_ref,
                ng_ref, o_ref, qp_ref, kp_ref, vp_ref, u_ref, wqd_ref, qk_ref, kw_ref, bc_ref, gl_ref,
                *, group):
    h = pl.program_id(1)
    seq = q_ref.shape[1]
    C = GDN_CHUNK
    D = HEAD_DIM
    pad = SUBLANES
    n_chunks = seq // C

    for src, dst in ((q_ref, qp_ref), (k_ref, kp_ref), (v_ref, vp_ref)):
        dst[0:pad, :] = jnp.zeros((pad, D), F32)
        dst[pad:pad + seq, :] = src[0]

    row = lax.broadcasted_iota(jnp.int32, (C, C), 0)
    col = lax.broadcasted_iota(jnp.int32, (C, C), 1)
    tri_incl = row >= col
    tri_strict = row > col
    lane = lax.broadcasted_iota(jnp.int32, (C, D), 1)
    neg_a = -jnp.exp(jnp.full((C, 1), alog_ref[h], F32))
    dt_bias = dtb_ref[h]
    scale = D ** -0.5

    def conv_silu(xp_ref, w_ref, r0):
        win = xp_ref[pl.ds(r0, C + pad), :]
        w = w_ref[...]
        y = win[pad:pad + C] * w[3:4]
        for j in range(GDN_CONV - 1):
            s = GDN_CONV - 1 - j
            y = y + win[pad - s:pad - s + C] * w[j:j + 1]
        return _silu(y)

    def l2n(x):
        return x * lax.rsqrt(jnp.sum(x * x, axis=-1, keepdims=True) + EPS)

    def prepare(grp, carry):
        cs = [grp * group + i for i in range(group)]
        r0s = [pl.multiple_of(c * C, C) for c in cs]
        pre = []
        for r0 in r0s:
            q = l2n(conv_silu(qp_ref, wq_ref, r0)) * scale
            k = l2n(conv_silu(kp_ref, wk_ref, r0))
            v = conv_silu(vp_ref, wv_ref, r0)
            ba = ba_ref[0, pl.ds(r0, C), :]
            b_col = jnp.sum(jnp.where(lane == h, ba, 0.0), axis=1, keepdims=True)
            a_col = jnp.sum(jnp.where(lane == h + N_GDN_HEADS, ba, 0.0), axis=1, keepdims=True)
            beta = _sigmoid(b_col)
            xs = a_col + dt_bias
            softplus = jnp.maximum(xs, 0.0) + jnp.log1p(jnp.exp(-jnp.abs(xs)))
            pre.append((q, k, v, beta, neg_a * softplus))
        gbs = [_fdot(tri_incl.astype(F32), jnp.broadcast_to(p[4], (C, D))) for p in pre]
        kk_qks = [_bdot_nt(jnp.concatenate([k * beta, q], axis=0), k) for q, k, v, beta, _ in pre]
        reps = C // D
        mats, mids = [], []
        for (q, k, v, beta, _), gb, kk_qk in zip(pre, gbs, kk_qks):
            g_i = jnp.concatenate([gb] * reps, axis=1)
            g_j = jnp.concatenate([gb.T] * reps, axis=0)
            decay = jnp.exp(jnp.where(tri_incl, g_i - g_j, NEG))
            mats.append(jnp.where(tri_strict, kk_qk[:C] * decay, 0.0))
            mids.append((kk_qk[C:] * decay, jnp.exp(gb), gb[C - 1:C, :]))
        ts = _unit_lower_inverse(mats, row, col)
        uws = [_bdot(t, jnp.concatenate([v * beta, k * beta * eg], axis=1))
               for t, (q, k, v, beta, _), (_, eg, _) in zip(ts, pre, mids)]
        transs = [_bdot_tn(k * jnp.exp(g_last - gb), uw)
                  for uw, (q, k, v, beta, _), gb, (_, _, g_last) in zip(uws, pre, gbs, mids)]
        for c, r0, uw, trans, (q, k, v, beta, _), (qk, eg, g_last) in zip(cs, r0s, uws, transs, pre, mids):
            u_ref[pl.ds(r0, C), :] = uw[:, :D]
            wqd_ref[c, 0:C, :] = uw[:, D:].astype(BF16)
            wqd_ref[c, C:2 * C, :] = (q * eg).astype(BF16)
            qk_ref[pl.ds(r0, C), :] = qk.astype(BF16)
            bc_ref[c] = trans[:, :D]
            kw_ref[c] = trans[:, D:].astype(BF16)
            gl_ref[c] = jnp.exp(g_last)
        return carry

    lax.fori_loop(0, n_chunks // group, prepare, 0)

    def scan(c, state):
        r0 = pl.multiple_of(c * C, C)
        sb = state.astype(BF16)
        ws_qs = jnp.dot(wqd_ref[c], sb, preferred_element_type=F32)
        v_new = u_ref[pl.ds(r0, C), :] - ws_qs[:C]
        o = ws_qs[C:] + jnp.dot(qk_ref[pl.ds(r0, C), :], v_new.astype(BF16), preferred_element_type=F32)
        z = z_ref[0, pl.ds(r0, C), :]
        o_ref[0, pl.ds(r0, C), :] = (_rms(o, ng_ref[...]) * _silu(z)).astype(o_ref.dtype)
        return state * gl_ref[c] - jnp.dot(kw_ref[c], sb, preferred_element_type=F32) + bc_ref[c]

    lax.fori_loop(0, n_chunks, scan, jnp.zeros((D, D), F32))


def _gdn(pa, ba, conv_w, a_log, dt_bias, norm_g, *, group=4):
    b, s, _ = pa.shape
    H = N_GDN_HEADS
    C = GDN_CHUNK
    assert s % C == 0
    n_chunks = s // C
    head = lambda off: pl.BlockSpec((1, s, HEAD_DIM), lambda i, j: (i, 0, off + j))
    cw = lambda off: pl.BlockSpec((GDN_CONV, HEAD_DIM), lambda i, j: (0, off + j))
    smem = pl.BlockSpec(memory_space=pltpu.SMEM)
    return pl.pallas_call(
        functools.partial(_gdn_kernel, group=math.gcd(group, n_chunks)),
        grid=(b, H),
        in_specs=[smem, smem, head(0), head(H), head(2 * H), head(3 * H),
                  pl.BlockSpec((1, s, HEAD_DIM), lambda i, j: (i, 0, 0)),
                  cw(0), cw(H), cw(2 * H),
                  pl.BlockSpec((1, HEAD_DIM), lambda i, j: (0, 0))],
        out_specs=pl.BlockSpec((1, s, HEAD_DIM), lambda i, j: (i, 0, j)),
        out_shape=jax.ShapeDtypeStruct((b, s, GDN_WIDTH), F32),
        scratch_shapes=[pltpu.VMEM((s + SUBLANES, HEAD_DIM), F32)] * 3 + [
            pltpu.VMEM((s, HEAD_DIM), F32),
            pltpu.VMEM((n_chunks, 2 * C, HEAD_DIM), BF16),
            pltpu.VMEM((s, C), BF16),
            pltpu.VMEM((n_chunks, HEAD_DIM, HEAD_DIM), BF16),
            pltpu.VMEM((n_chunks, HEAD_DIM, HEAD_DIM), F32),
            pltpu.VMEM((n_chunks, 1, HEAD_DIM), F32)],
        compiler_params=pltpu.CompilerParams(
            dimension_semantics=("parallel", "arbitrary"), vmem_limit_bytes=VMEM_LIMIT),
        name="gdn",
    )(a_log, dt_bias, pa, pa, pa, pa, ba, conv_w, conv_w, conv_w, norm_g.reshape(1, HEAD_DIM))


def _bucket_upper_bounds():
    n = np.arange(0, 4 * REL_MAX_DIST, dtype=np.int64)
    max_exact = REL_BUCKETS // 2
    nf = np.maximum(n, 1).astype(np.float32)
    large = max_exact + (np.log(nf / np.float32(max_exact)) / np.float32(math.log(REL_MAX_DIST / max_exact))
                         * np.float32(REL_BUCKETS - max_exact)).astype(np.int32)
    large = np.minimum(large, REL_BUCKETS - 1)
    bucket = np.where(n < max_exact, n, large)
    assert np.all(np.diff(bucket) >= 0) and bucket[-1] == REL_BUCKETS - 1
    return [int(np.searchsorted(bucket, b, side="right")) for b in range(REL_BUCKETS - 1)]


_BUCKET_UPPER = _bucket_upper_bounds()


def _moba_kernel(rb_ref, q_ref, k_ref, v_ref, ng_ref, o_ref, kb_ref, vb_ref, km_ref, bd_ref, bl_ref):
    h = pl.program_id(0)
    b = pl.program_id(1)
    i = pl.program_id(2)
    seq = k_ref.shape[1]
    nb = seq // MOBA_BLOCK
    T = MOBA_BLOCK
    scale = HEAD_DIM ** -0.5
    rr = lax.broadcasted_iota(jnp.int32, (T, T), 0)
    cc = lax.broadcasted_iota(jnp.int32, (T, T), 1)

    @pl.when(jnp.logical_and(b == 0, i == 0))
    def _():
        def bias_of(n):
            val = jnp.full((T, T), rb_ref[REL_BUCKETS - 1, h], F32)
            for bkt in range(REL_BUCKETS - 2, -1, -1):
                val = jnp.where(n < _BUCKET_UPPER[bkt], rb_ref[bkt, h], val)
            return val
        bd_ref[...] = bias_of(rr - cc)
        bl_ref[...] = bias_of(rr - cc + T)

    @pl.when(i == 0)
    def _():
        kf = k_ref[0]
        kb_ref[...] = kf.astype(BF16)
        vb_ref[...] = v_ref[0].astype(BF16)
        for n in range(nb):
            km_ref[n:n + 1, :] = jnp.mean(kf[n * T:(n + 1) * T], axis=0, keepdims=True)

    qf = q_ref[0]
    qb = qf.astype(BF16)

    gate = lax.dot_general(km_ref[...], qf, (((1,), (1,)), ((), ())),
                           preferred_element_type=F32, precision=lax.Precision.HIGHEST)
    blk = lax.broadcasted_iota(jnp.int32, (nb, T), 0)
    rank = jnp.zeros((nb, T), jnp.int32)
    for m in range(nb):
        gm = gate[m:m + 1, :]
        ahead = (gm > gate) | ((gm == gate) & (m < blk))
        rank = rank + jnp.where(ahead, 1, 0) * (m < i).astype(jnp.int32)
    sel_t = jnp.where((blk < i) & (rank < MOBA_TOPK), 1.0, 0.0)
    eye = (rr == cc).astype(BF16)
    sel = lax.dot_general(eye, sel_t.astype(BF16), (((1,), (1,)), ((), ())),
                          preferred_element_type=F32)
    blk_lane = lax.broadcasted_iota(jnp.int32, (T, nb), 1)

    def scores(j):
        kj = kb_ref[pl.ds(pl.multiple_of(j * T, T), T), :]
        return _bdot_nt(qb, kj) * scale

    def attend(j, s, carry):
        m_prev, l_prev, acc = carry
        m_new = jnp.maximum(m_prev, jnp.max(s, axis=-1, keepdims=True))
        alpha = jnp.exp(m_prev - m_new)
        p = jnp.exp(s - m_new)
        l_new = alpha * l_prev + jnp.sum(p, axis=-1, keepdims=True)
        vj = vb_ref[pl.ds(pl.multiple_of(j * T, T), T), :]
        return m_new, l_new, alpha * acc + _bdot(p, vj)

    def selected(j):
        return jnp.sum(jnp.where(blk_lane == j, sel, 0.0), axis=1, keepdims=True) > 0.5

    s_own = jnp.where(rr >= cc, scores(i) + bd_ref[...], NEG)
    carry = (jnp.full((T, 1), NEG, F32), jnp.zeros((T, 1), F32), jnp.zeros((T, HEAD_DIM), F32))
    carry = attend(i, s_own, carry)

    def prev_block(carry):
        j = i - 1
        s = jnp.where(selected(j), scores(j) + bl_ref[...], NEG)
        return attend(j, s, carry)

    carry = lax.cond(i >= 1, prev_block, lambda c: c, carry)

    far_bias = rb_ref[REL_BUCKETS - 1, h]

    def far_block(j, carry):
        s = jnp.where(selected(j), scores(j) + far_bias, NEG)
        return attend(j, s, carry)

    _, l_fin, acc = lax.fori_loop(0, jnp.maximum(i - 1, 0), far_block, carry)
    o = acc / l_fin
    o_ref[0] = _rms(o, ng_ref[...]).astype(o_ref.dtype)


def _moba(pb, rel_bias, norm_g):
    b, s, _ = pb.shape
    H = N_MOBA_HEADS
    T = MOBA_BLOCK
    assert s % T == 0
    nb = s // T
    return pl.pallas_call(
        _moba_kernel,
        grid=(H, b, nb),
        in_specs=[pl.BlockSpec(memory_space=pltpu.SMEM),
                  pl.BlockSpec((1, T, HEAD_DIM), lambda h, bb, i: (bb, i, h)),
                  pl.BlockSpec((1, s, HEAD_DIM), lambda h, bb, i: (bb, 0, H + h)),
                  pl.BlockSpec((1, s, HEAD_DIM), lambda h, bb, i: (bb, 0, 2 * H + h)),
                  pl.BlockSpec((1, HEAD_DIM), lambda h, bb, i: (0, 0))],
        out_specs=pl.BlockSpec((1, T, HEAD_DIM), lambda h, bb, i: (bb, i, h)),
        out_shape=jax.ShapeDtypeStruct((b, s, MOBA_WIDTH), F32),
        scratch_shapes=[pltpu.VMEM((s, HEAD_DIM), BF16), pltpu.VMEM((s, HEAD_DIM), BF16),
                        pltpu.VMEM((nb, HEAD_DIM), F32),
                        pltpu.VMEM((T, T), F32), pltpu.VMEM((T, T), F32)],
        compiler_params=pltpu.CompilerParams(
            dimension_semantics=("arbitrary", "arbitrary", "arbitrary"), vmem_limit_bytes=VMEM_LIMIT),
        name="moba",
    )(rel_bias, pb, pb, pb, norm_g.reshape(1, HEAD_DIM))


def _out_proj_kernel(x_ref, oa_ref, ob_ref, wa_ref, wb_ref, o_ref):
    o_ref[...] = (x_ref[...] + _bdot(oa_ref[...], wa_ref[...]) + _bdot(ob_ref[...], wb_ref[...]))


def _out_proj(x, oa, ob, wa, wb, *, tm, tn):
    m, d = x.shape
    ka, kb = oa.shape[1], ob.shape[1]
    return pl.pallas_call(
        _out_proj_kernel,
        grid=(m // tm, d // tn),
        in_specs=[pl.BlockSpec((tm, tn), lambda i, j: (i, j)),
                  pl.BlockSpec((tm, ka), lambda i, j: (i, 0)),
                  pl.BlockSpec((tm, kb), lambda i, j: (i, 0)),
                  pl.BlockSpec((ka, tn), lambda i, j: (0, j)),
                  pl.BlockSpec((kb, tn), lambda i, j: (0, j))],
        out_specs=pl.BlockSpec((tm, tn), lambda i, j: (i, j)),
        out_shape=jax.ShapeDtypeStruct((m, d), F32),
        compiler_params=pltpu.CompilerParams(
            dimension_semantics=("parallel", "arbitrary"), vmem_limit_bytes=VMEM_LIMIT),
        name="out_proj",
    )(x, oa, ob, wa, wb)


def _xattn_kernel(x_ref, g_ref, wq_ref, kv_ref, wo_ref, o_ref):
    x = x_ref[0]
    hn = _rms(x, g_ref[...])
    q = _bdot(hn, wq_ref[...])
    kv = kv_ref[0]
    width = N_XATTN_HEADS * HEAD_DIM
    outs = []
    for hd in range(N_XATTN_HEADS):
        sl = slice(hd * HEAD_DIM, (hd + 1) * HEAD_DIM)
        kh = kv[:, sl]
        vh = kv[:, width + hd * HEAD_DIM: width + (hd + 1) * HEAD_DIM]
        s = _bdot_nt(q[:, sl], kh) * HEAD_DIM ** -0.5
        p = jnp.exp(s - jnp.max(s, axis=-1, keepdims=True))
        p = p / jnp.sum(p, axis=-1, keepdims=True)
        outs.append(_bdot(p, vh))
    o = jnp.concatenate(outs, axis=-1)
    o_ref[0] = x + _bdot(o, wo_ref[...])


def _xattn(x, g, wq, kv, wo, *, ts):
    b, s, d = x.shape
    mlen = kv.shape[1]
    width = wq.shape[1]
    return pl.pallas_call(
        _xattn_kernel,
        grid=(b, s // ts),
        in_specs=[pl.BlockSpec((1, ts, d), lambda i, j: (i, j, 0)),
                  pl.BlockSpec((1, d), lambda i, j: (0, 0)),
                  pl.BlockSpec((d, width), lambda i, j: (0, 0)),
                  pl.BlockSpec((1, mlen, 2 * width), lambda i, j: (i, 0, 0)),
                  pl.BlockSpec((width, d), lambda i, j: (0, 0))],
        out_specs=pl.BlockSpec((1, ts, d), lambda i, j: (i, j, 0)),
        out_shape=jax.ShapeDtypeStruct((b, s, d), F32),
        compiler_params=pltpu.CompilerParams(
            dimension_semantics=("parallel", "parallel"), vmem_limit_bytes=VMEM_LIMIT),
        name="xattn",
    )(x, g.reshape(1, d), wq, kv, wo)


def _ffn_kernel(x_ref, halo_ref, g_ref, wg_ref, wu_ref, cw_ref, cb_ref, wd_ref, fg_ref, o_ref,
                h_ref, acc_ref, *, tiles_per_seq):
    i = pl.program_id(0)
    f = pl.program_id(1)
    tm = x_ref.shape[0]
    pad = SUBLANES

    @pl.when(f == 0)
    def _():
        halo = _rms(halo_ref[...], g_ref[...])
        halo = jnp.where(i % tiles_per_seq == 0, 0.0, halo)
        h_ref[0:pad, :] = halo.astype(BF16)
        h_ref[pad:pad + tm, :] = _rms(x_ref[...], g_ref[...]).astype(BF16)
        acc_ref[...] = jnp.zeros_like(acc_ref)

    hx = h_ref[...]
    gp = jnp.dot(hx, wg_ref[...], preferred_element_type=F32)
    up = jnp.dot(hx[pad:], wu_ref[...], preferred_element_type=F32)
    cw = cw_ref[...]
    gate = gp[pad:pad + tm] * cw[2:3] + gp[pad - 1:pad - 1 + tm] * cw[1:2] \
        + gp[pad - 2:pad - 2 + tm] * cw[0:1] + cb_ref[...]
    act = _silu(gate) * up
    acc_ref[...] += jnp.dot(act.astype(BF16), wd_ref[...], preferred_element_type=F32)

    @pl.when(f == pl.num_programs(1) - 1)
    def _():
        o_ref[...] = _rms(x_ref[...] + acc_ref[...], fg_ref[...])


def _ffn(x, g, wg, wu, cw, cb, wd, fg, *, seq, tm, tf):
    m, d = x.shape
    ff = wg.shape[1]
    assert seq % tm == 0 and ff % tf == 0 and tm % SUBLANES == 0
    hb = tm // SUBLANES
    return pl.pallas_call(
        functools.partial(_ffn_kernel, tiles_per_seq=seq // tm),
        grid=(m // tm, ff // tf),
        in_specs=[pl.BlockSpec((tm, d), lambda i, f: (i, 0)),
                  pl.BlockSpec((SUBLANES, d), lambda i, f: (jnp.maximum(i * hb - 1, 0), 0)),
                  pl.BlockSpec((1, d), lambda i, f: (0, 0)),
                  pl.BlockSpec((d, tf), lambda i, f: (0, f)),
                  pl.BlockSpec((d, tf), lambda i, f: (0, f)),
                  pl.BlockSpec((FFN_CONV, tf), lambda i, f: (0, f)),
                  pl.BlockSpec((1, tf), lambda i, f: (0, f)),
                  pl.BlockSpec((tf, d), lambda i, f: (f, 0)),
                  pl.BlockSpec((1, d), lambda i, f: (0, 0))],
        out_specs=pl.BlockSpec((tm, d), lambda i, f: (i, 0)),
        out_shape=jax.ShapeDtypeStruct((m, d), F32),
        scratch_shapes=[pltpu.VMEM((tm + SUBLANES, d), BF16), pltpu.VMEM((tm, d), F32)],
        compiler_params=pltpu.CompilerParams(
            dimension_semantics=("parallel", "arbitrary"), vmem_limit_bytes=VMEM_LIMIT),
        name="ffn",
    )(x, x, g.reshape(1, d), wg, wu, cw, cb.reshape(1, ff), wd, fg.reshape(1, d))


def _layer(x, mem, mix_norm_g, w_in, gdn_conv_w, a_log, dt_bias, gdn_norm_g, moba_norm_g, rel_bias,
           w_out, xattn_norm_g, mem_norm_g, w_xq, w_xkv, w_xo, ffn_norm_g, w_gate, w_up, ffn_conv_w,
           ffn_conv_b, w_down, final_g, *, last):
    b, s, d = x.shape
    m = b * s
    x2 = x.reshape(m, d)
    i1 = 4 * GDN_WIDTH
    i3 = i1 + 2 * N_GDN_HEADS
    w_a = w_in[:, :i1].astype(BF16)
    w_ba = jnp.pad(w_in[:, i1:i3], ((0, 0), (0, HEAD_DIM - 2 * N_GDN_HEADS))).astype(BF16)
    w_b = w_in[:, i3:].astype(BF16)
    tm = min(512, m)
    pa = _norm_matmul(x2, mix_norm_g, w_a, tm=tm, tn=1024).reshape(b, s, i1)
    ba = _norm_matmul(x2, mix_norm_g, w_ba, tm=tm, tn=HEAD_DIM).reshape(b, s, HEAD_DIM)
    pb = _norm_matmul(x2, mix_norm_g, w_b, tm=tm, tn=1024).reshape(b, s, 3 * MOBA_WIDTH)
    o_a = _gdn(pa, ba, gdn_conv_w, a_log, dt_bias, gdn_norm_g)
    o_b = _moba(pb, rel_bias, moba_norm_g)
    x1 = _out_proj(x2, o_a.reshape(m, GDN_WIDTH), o_b.reshape(m, MOBA_WIDTH),
                   w_out[:GDN_WIDTH].astype(BF16), w_out[GDN_WIDTH:].astype(BF16), tm=tm, tn=1024)
    mlen = mem.shape[1]
    kv = _norm_matmul(mem.reshape(b * mlen, d), mem_norm_g, w_xkv.astype(BF16),
                      tm=min(512, b * mlen), tn=1024).reshape(b, mlen, -1)
    x2b = _xattn(x1.reshape(b, s, d), xattn_norm_g, w_xq.astype(BF16), kv, w_xo.astype(BF16),
                 ts=min(512, s))
    assert last, "the final rmsnorm is fused into the last layer's ffn"
    y = _ffn(x2b.reshape(m, d), ffn_norm_g, w_gate.astype(BF16), w_up.astype(BF16), ffn_conv_w,
             ffn_conv_b, w_down.astype(BF16), final_g, seq=s, tm=min(512, s), tf=512)
    return y.reshape(b, s, d)


def kernel(x, mem, mix_norm_g, w_in, gdn_conv_w, gdn_a_log, gdn_dt_bias, gdn_norm_g, moba_norm_g,
           rel_bias, w_out, xattn_norm_g, mem_norm_g, w_xq, w_xkv, w_xo, ffn_norm_g, w_gate, w_up,
           ffn_conv_w, ffn_conv_b, w_down, final_norm_g):
    depth = mix_norm_g.shape[0]
    assert depth == 1
    l = 0
    return _layer(x, mem, mix_norm_g[l], w_in[l], gdn_conv_w[l], gdn_a_log[l], gdn_dt_bias[l],
                  gdn_norm_g[l], moba_norm_g[l], rel_bias, w_out[l], xattn_norm_g[l], mem_norm_g[l],
                  w_xq[l], w_xkv[l], w_xo[l], ffn_norm_g[l], w_gate[l], w_up[l], ffn_conv_w[l],
                  ffn_conv_b[l], w_down[l], final_norm_g, last=True)
```

```python
import functools
import math

import jax
import jax.numpy as jnp
import numpy as np
from jax import lax
from jax.experimental import pallas as pl
from jax.experimental.pallas import tpu as pltpu

HEAD_DIM = 128
N_GDN_HEADS = 8
N_MOBA_HEADS = 8
GDN_WIDTH = N_GDN_HEADS * HEAD_DIM
MOBA_WIDTH = N_MOBA_HEADS * HEAD_DIM
GDN_CONV = 4
GDN_CHUNK = 256
MOBA_BLOCK = 256
MOBA_TOPK = 3
REL_BUCKETS = 32
REL_MAX_DIST = 128
N_XATTN_HEADS = 4
FFN_CONV = 3
EPS = 1e-6
NEG = -1e30
SUBLANES = 8
VMEM_LIMIT = 56 * 1024 * 1024

F32 = jnp.float32
BF16 = jnp.bfloat16


def _bdot(a, b):
    return jnp.dot(a.astype(BF16), b.astype(BF16), preferred_element_type=F32)


def _bdot_nt(a, b):
    return lax.dot_general(a.astype(BF16), b.astype(BF16), (((1,), (1,)), ((), ())),
                           preferred_element_type=F32)


def _bdot_tn(a, b):
    return lax.dot_general(a.astype(BF16), b.astype(BF16), (((0,), (0,)), ((), ())),
                           preferred_element_type=F32)


def _fdot(a, b):
    return jnp.dot(a, b, preferred_element_type=F32, precision=lax.Precision.HIGHEST)


def _sigmoid(x):
    return 1.0 / (1.0 + jnp.exp(-x))


def _silu(x):
    return x * _sigmoid(x)


def _rms(x, g):
    return x * lax.rsqrt(jnp.mean(x * x, axis=-1, keepdims=True) + EPS) * g


def _norm_matmul_kernel(x_ref, g_ref, w_ref, o_ref, h_ref):
    @pl.when(pl.program_id(1) == 0)
    def _():
        h_ref[...] = _rms(x_ref[...], g_ref[...]).astype(BF16)

    o_ref[...] = jnp.dot(h_ref[...], w_ref[...], preferred_element_type=F32).astype(o_ref.dtype)


def _norm_matmul(x, g, w, *, tm, tn, out_dtype=F32):
    m, k = x.shape
    n = w.shape[1]
    assert m % tm == 0 and n % tn == 0
    return pl.pallas_call(
        _norm_matmul_kernel,
        grid=(m // tm, n // tn),
        in_specs=[pl.BlockSpec((tm, k), lambda i, j: (i, 0)),
                  pl.BlockSpec((1, k), lambda i, j: (0, 0)),
                  pl.BlockSpec((k, tn), lambda i, j: (0, j))],
        out_specs=pl.BlockSpec((tm, tn), lambda i, j: (i, j)),
        out_shape=jax.ShapeDtypeStruct((m, n), out_dtype),
        scratch_shapes=[pltpu.VMEM((tm, k), BF16)],
        compiler_params=pltpu.CompilerParams(
            dimension_semantics=("parallel", "arbitrary"), vmem_limit_bytes=VMEM_LIMIT),
        name="norm_matmul",
    )(x, g.reshape(1, k), w)


def _unit_lower_inverse(mats, row, col):
    c = mats[0].shape[0]
    eye = (row == col).astype(F32)
    blk = lambda n: (row // n) == (col // n)
    inner = blk(16)
    ds = [jnp.where(inner, a, 0.0) for a in mats]
    ts = [eye - d for d in ds]
    ps = [_bdot(d, d) for d in ds]
    for step in range(3):
        ts = [t + _bdot(t, p) for t, p in zip(ts, ps)]
        if step < 2:
            ps = [_bdot(p, p) for p in ps]
    n = 32
    while n <= c:
        outer = blk(n) if n < c else None
        keep = ~inner if outer is None else (outer & ~inner)
        xs = [_bdot(jnp.where(keep, a, 0.0), t) for a, t in zip(mats, ts)]
        ts = [t - _bdot(t, x) for t, x in zip(ts, xs)]
        inner = outer
        n *= 2
    return ts


def _gdn_kernel(alog_ref, dtb_ref, q_ref, k_ref, v_ref, z_ref, ba_ref, wq_ref, wk_ref, wv_ref,
                ng_ref, o_ref, qp_ref, kp_ref, vp_ref, u_ref, wqd_ref, qk_ref, kw_ref, bc_ref, gl_ref,
                *, group):
    h = pl.program_id(1)
    seq = q_ref.shape[1]
    C = GDN_CHUNK
    D = HEAD_DIM
    pad = SUBLANES
    n_chunks = seq // C

    for src, dst in ((q_ref, qp_ref), (k_ref, kp_ref), (v_ref, vp_ref)):
        dst[0:pad, :] = jnp.zeros((pad, D), F32)
        dst[pad:pad + seq, :] = src[0]

    row = lax.broadcasted_iota(jnp.int32, (C, C), 0)
    col = lax.broadcasted_iota(jnp.int32, (C, C), 1)
    tri_incl = row >= col
    tri_strict = row > col
    lane = lax.broadcasted_iota(jnp.int32, (C, D), 1)
    neg_a = -jnp.exp(jnp.full((C, 1), alog_ref[h], F32))
    dt_bias = dtb_ref[h]
    scale = D ** -0.5

    def conv_silu(xp_ref, w_ref, r0):
        win = xp_ref[pl.ds(r0, C + pad), :]
        w = w_ref[...]
        y = win[pad:pad + C] * w[3:4]
        for j in range(GDN_CONV - 1):
            s = GDN_CONV - 1 - j
            y = y + win[pad - s:pad - s + C] * w[j:j + 1]
        return _silu(y)

    def l2n(x):
        return x * lax.rsqrt(jnp.sum(x * x, axis=-1, keepdims=True) + EPS)

    def prepare(grp, carry):
        cs = [grp * group + i for i in range(group)]
        r0s = [pl.multiple_of(c * C, C) for c in cs]
        pre = []
        for r0 in r0s:
            q = l2n(conv_silu(qp_ref, wq_ref, r0)) * scale
            k = l2n(conv_silu(kp_ref, wk_ref, r0))
            v = conv_silu(vp_ref, wv_ref, r0)
            ba = ba_ref[0, pl.ds(r0, C), :]
            b_col = jnp.sum(jnp.where(lane == h, ba, 0.0), axis=1, keepdims=True)
            a_col = jnp.sum(jnp.where(lane == h + N_GDN_HEADS, ba, 0.0), axis=1, keepdims=True)
            beta = _sigmoid(b_col)
            xs = a_col + dt_bias
            softplus = jnp.maximum(xs, 0.0) + jnp.log1p(jnp.exp(-jnp.abs(xs)))
            pre.append((q, k, v, beta, neg_a * softplus))
        gbs = [_fdot(tri_incl.astype(F32), jnp.broadcast_to(p[4], (C, D))) for p in pre]
        kk_qks = [_bdot_nt(jnp.concatenate([k * beta, q], axis=0), k) for q, k, v, beta, _ in pre]
        reps = C // D
        mats, mids = [], []
        for (q, k, v, beta, _), gb, kk_qk in zip(pre, gbs, kk_qks):
            g_i = jnp.concatenate([gb] * reps, axis=1)
            g_j = jnp.concatenate([gb.T] * reps, axis=0)
            decay = jnp.exp(jnp.where(tri_incl, g_i - g_j, NEG))
            mats.append(jnp.where(tri_strict, kk_qk[:C] * decay, 0.0))
            mids.append((kk_qk[C:] * decay, jnp.exp(gb), gb[C - 1:C, :]))
        ts = _unit_lower_inverse(mats, row, col)
        uws = [_bdot(t, jnp.concatenate([v * beta, k * beta * eg], axis=1))
               for t, (q, k, v, beta, _), (_, eg, _) in zip(ts, pre, mids)]
        transs = [_bdot_tn(k * jnp.exp(g_last - gb), uw)
                  for uw, (q, k, v, beta, _), gb, (_, _, g_last) in zip(uws, pre, gbs, mids)]
        for c, r0, uw, trans, (q, k, v, beta, _), (qk, eg, g_last) in zip(cs, r0s, uws, transs, pre, mids):
            u_ref[pl.ds(r0, C), :] = uw[:, :D]
            wqd_ref[c, 0:C, :] = uw[:, D:].astype(BF16)
            wqd_ref[c, C:2 * C, :] = (q * eg).astype(BF16)
            qk_ref[pl.ds(r0, C), :] = qk.astype(BF16)
            bc_ref[c] = trans[:, :D]
            kw_ref[c] = trans[:, D:].astype(BF16)
            gl_ref[c] = jnp.exp(g_last)
        return carry

    lax.fori_loop(0, n_chunks // group, prepare, 0)

    def scan(c, state):
        r0 = pl.multiple_of(c * C, C)
        sb = state.astype(BF16)
        ws_qs = jnp.dot(wqd_ref[c], sb, preferred_element_type=F32)
        v_new = u_ref[pl.ds(r0, C), :] - ws_qs[:C]
        o = ws_qs[C:] + jnp.dot(qk_ref[pl.ds(r0, C), :], v_new.astype(BF16), preferred_element_type=F32)
        z = z_ref[0, pl.ds(r0, C), :]
        o_ref[0, pl.ds(r0, C), :] = (_rms(o, ng_ref[...]) * _silu(z)).astype(o_ref.dtype)
        return state * gl_ref[c] - jnp.dot(kw_ref[c], sb, preferred_element_type=F32) + bc_ref[c]

    lax.fori_loop(0, n_chunks, scan, jnp.zeros((D, D), F32))


def _gdn(pa, ba, conv_w, a_log, dt_bias, norm_g, *, group=4):
    b, s, _ = pa.shape
    H = N_GDN_HEADS
    C = GDN_CHUNK
    assert s % C == 0
    n_chunks = s // C
    head = lambda off: pl.BlockSpec((1, s, HEAD_DIM), lambda i, j: (i, 0, off + j))
    cw = lambda off: pl.BlockSpec((GDN_CONV, HEAD_DIM), lambda i, j: (0, off + j))
    smem = pl.BlockSpec(memory_space=pltpu.SMEM)
    return pl.pallas_call(
        functools.partial(_gdn_kernel, group=math.gcd(group, n_chunks)),
        grid=(b, H),
        in_specs=[smem, smem, head(0), head(H), head(2 * H), head(3 * H),
                  pl.BlockSpec((1, s, HEAD_DIM), lambda i, j: (i, 0, 0)),
                  cw(0), cw(H), cw(2 * H),
                  pl.BlockSpec((1, HEAD_DIM), lambda i, j: (0, 0))],
        out_specs=pl.BlockSpec((1, s, HEAD_DIM), lambda i, j: (i, 0, j)),
        out_shape=jax.ShapeDtypeStruct((b, s, GDN_WIDTH), F32),
        scratch_shapes=[pltpu.VMEM((s + SUBLANES, HEAD_DIM), F32)] * 3 + [
            pltpu.VMEM((s, HEAD_DIM), F32),
            pltpu.VMEM((n_chunks, 2 * C, HEAD_DIM), BF16),
            pltpu.VMEM((s, C), BF16),
            pltpu.VMEM((n_chunks, HEAD_DIM, HEAD_DIM), BF16),
            pltpu.VMEM((n_chunks, HEAD_DIM, HEAD_DIM), F32),
            pltpu.VMEM((n_chunks, 1, HEAD_DIM), F32)],
        compiler_params=pltpu.CompilerParams(
            dimension_semantics=("parallel", "arbitrary"), vmem_limit_bytes=VMEM_LIMIT),
        name="gdn",
    )(a_log, dt_bias, pa, pa, pa, pa, ba, conv_w, conv_w, conv_w, norm_g.reshape(1, HEAD_DIM))


def _bucket_upper_bounds():
    n = np.arange(0, 4 * REL_MAX_DIST, dtype=np.int64)
    max_exact = REL_BUCKETS // 2
    nf = np.maximum(n, 1).astype(np.float32)
    large = max_exact + (np.log(nf / np.float32(max_exact)) / np.float32(math.log(REL_MAX_DIST / max_exact))
                         * np.float32(REL_BUCKETS - max_exact)).astype(np.int32)
    large = np.minimum(large, REL_BUCKETS - 1)
    bucket = np.where(n < max_exact, n, large)
    assert np.all(np.diff(bucket) >= 0) and bucket[-1] == REL_BUCKETS - 1
    return [int(np.searchsorted(bucket, b, side="right")) for b in range(REL_BUCKETS - 1)]


_BUCKET_UPPER = _bucket_upper_bounds()


def _moba_kernel(rb_ref, q_ref, k_ref, v_ref, ng_ref, o_ref, kb_ref, vt_ref, km_ref, bd_ref, bl_ref):
    h = pl.program_id(0)
    b = pl.program_id(1)
    seq = k_ref.shape[1]
    T = MOBA_BLOCK
    nb = seq // T
    scale = HEAD_DIM ** -0.5
    kk = lax.broadcasted_iota(jnp.int32, (T, T), 0)
    qq = lax.broadcasted_iota(jnp.int32, (T, T), 1)

    @pl.when(b == 0)
    def _():
        def bias_of(n):
            val = jnp.full((T, T), rb_ref[REL_BUCKETS - 1, h], F32)
            for bkt in range(REL_BUCKETS - 2, -1, -1):
                val = jnp.where(n < _BUCKET_UPPER[bkt], rb_ref[bkt, h], val)
            return val
        bd_ref[...] = jnp.where(qq >= kk, bias_of(qq - kk), NEG)
        bl_ref[...] = bias_of(qq - kk + T)

    kb_ref[...] = k_ref[0].astype(BF16)
    for n in range(nb):
        blk_rows = slice(n * T, (n + 1) * T)
        vt_ref[:, blk_rows] = v_ref[0, blk_rows, :].T.astype(BF16)
        km_ref[n:n + 1, :] = jnp.mean(k_ref[0, blk_rows, :], axis=0, keepdims=True)
    far_bias = rb_ref[REL_BUCKETS - 1, h]

    for i in range(nb):
        qf = q_ref[0, i * T:(i + 1) * T, :]
        keep = None
        if i > MOBA_TOPK:
            gate = lax.dot_general(km_ref[0:i, :], qf, (((1,), (1,)), ((), ())),
                                   preferred_element_type=F32, precision=lax.Precision.HIGHEST)
            blk = lax.broadcasted_iota(jnp.int32, (i, T), 0)
            rank = jnp.zeros((i, T), jnp.int32)
            for m in range(i):
                gm = gate[m:m + 1, :]
                rank = rank + jnp.where((gm > gate) | ((gm == gate) & (m < blk)), 1, 0)
            keep = jnp.where(rank < MOBA_TOPK, 0.0, NEG)
        width = (i + 1) * T
        s_all = _bdot_nt(kb_ref[0:width, :], qf * scale)
        parts = []
        for j in range(i + 1):
            sj = s_all[j * T:(j + 1) * T, :]
            if j == i:
                sj = sj + bd_ref[...]
            elif j == i - 1:
                sj = sj + bl_ref[...]
                if keep is not None:
                    sj = sj + keep[j:j + 1, :]
            else:
                sj = sj + (far_bias if keep is None else keep[j:j + 1, :] + far_bias)
            parts.append(sj)
        m_row = jnp.max(functools.reduce(jnp.maximum, parts), axis=0, keepdims=True)
        ps = [jnp.exp(sj - m_row) for sj in parts]
        l_row = jnp.sum(functools.reduce(jnp.add, ps), axis=0, keepdims=True)
        p_all = jnp.concatenate([p.astype(BF16) for p in ps], axis=0)
        o_t = jnp.dot(vt_ref[:, 0:width], p_all, preferred_element_type=F32) / l_row
        o_ref[0, i * T:(i + 1) * T, :] = _rms(o_t.T, ng_ref[...]).astype(o_ref.dtype)


def _moba(pb, rel_bias, norm_g):
    b, s, _ = pb.shape
    H = N_MOBA_HEADS
    T = MOBA_BLOCK
    assert s % T == 0
    nb = s // T
    head = lambda off: pl.BlockSpec((1, s, HEAD_DIM), lambda h, bb: (bb, 0, off + h))
    return pl.pallas_call(
        _moba_kernel,
        grid=(H, b),
        in_specs=[pl.BlockSpec(memory_space=pltpu.SMEM), head(0), head(H), head(2 * H),
                  pl.BlockSpec((1, HEAD_DIM), lambda h, bb: (0, 0))],
        out_specs=pl.BlockSpec((1, s, HEAD_DIM), lambda h, bb: (bb, 0, h)),
        out_shape=jax.ShapeDtypeStruct((b, s, MOBA_WIDTH), F32),
        scratch_shapes=[pltpu.VMEM((s, HEAD_DIM), BF16), pltpu.VMEM((HEAD_DIM, s), BF16),
                        pltpu.VMEM((nb, HEAD_DIM), F32),
                        pltpu.VMEM((T, T), F32), pltpu.VMEM((T, T), F32)],
        compiler_params=pltpu.CompilerParams(
            dimension_semantics=("arbitrary", "arbitrary"), vmem_limit_bytes=VMEM_LIMIT),
        name="moba",
    )(rel_bias, pb, pb, pb, norm_g.reshape(1, HEAD_DIM))


def _out_proj_kernel(x_ref, oa_ref, ob_ref, wa_ref, wb_ref, o_ref):
    o_ref[...] = (x_ref[...] + _bdot(oa_ref[...], wa_ref[...]) + _bdot(ob_ref[...], wb_ref[...]))


def _out_proj(x, oa, ob, wa, wb, *, tm, tn):
    m, d = x.shape
    ka, kb = oa.shape[1], ob.shape[1]
    return pl.pallas_call(
        _out_proj_kernel,
        grid=(m // tm, d // tn),
        in_specs=[pl.BlockSpec((tm, tn), lambda i, j: (i, j)),
                  pl.BlockSpec((tm, ka), lambda i, j: (i, 0)),
                  pl.BlockSpec((tm, kb), lambda i, j: (i, 0)),
                  pl.BlockSpec((ka, tn), lambda i, j: (0, j)),
                  pl.BlockSpec((kb, tn), lambda i, j: (0, j))],
        out_specs=pl.BlockSpec((tm, tn), lambda i, j: (i, j)),
        out_shape=jax.ShapeDtypeStruct((m, d), F32),
        compiler_params=pltpu.CompilerParams(
            dimension_semantics=("parallel", "arbitrary"), vmem_limit_bytes=VMEM_LIMIT),
        name="out_proj",
    )(x, oa, ob, wa, wb)


def _xattn_kernel(x_ref, g_ref, wq_ref, kv_ref, wo_ref, o_ref):
    x = x_ref[0]
    hn = _rms(x, g_ref[...])
    q = _bdot(hn, wq_ref[...])
    kv = kv_ref[0]
    width = N_XATTN_HEADS * HEAD_DIM
    outs = []
    for hd in range(N_XATTN_HEADS):
        sl = slice(hd * HEAD_DIM, (hd + 1) * HEAD_DIM)
        kh = kv[:, sl]
        vh = kv[:, width + hd * HEAD_DIM: width + (hd + 1) * HEAD_DIM]
        s = _bdot_nt(q[:, sl], kh) * HEAD_DIM ** -0.5
        p = jnp.exp(s - jnp.max(s, axis=-1, keepdims=True))
        p = p / jnp.sum(p, axis=-1, keepdims=True)
        outs.append(_bdot(p, vh))
    o = jnp.concatenate(outs, axis=-1)
    o_ref[0] = x + _bdot(o, wo_ref[...])


def _xattn(x, g, wq, kv, wo, *, ts):
    b, s, d = x.shape
    mlen = kv.shape[1]
    width = wq.shape[1]
    return pl.pallas_call(
        _xattn_kernel,
        grid=(b, s // ts),
        in_specs=[pl.BlockSpec((1, ts, d), lambda i, j: (i, j, 0)),
                  pl.BlockSpec((1, d), lambda i, j: (0, 0)),
                  pl.BlockSpec((d, width), lambda i, j: (0, 0)),
                  pl.BlockSpec((1, mlen, 2 * width), lambda i, j: (i, 0, 0)),
                  pl.BlockSpec((width, d), lambda i, j: (0, 0))],
        out_specs=pl.BlockSpec((1, ts, d), lambda i, j: (i, j, 0)),
        out_shape=jax.ShapeDtypeStruct((b, s, d), F32),
        compiler_params=pltpu.CompilerParams(
            dimension_semantics=("parallel", "parallel"), vmem_limit_bytes=VMEM_LIMIT),
        name="xattn",
    )(x, g.reshape(1, d), wq, kv, wo)


def _ffn_kernel(x_ref, halo_ref, g_ref, wg_ref, wu_ref, cw_ref, cb_ref, wd_ref, fg_ref, o_ref,
                h_ref, acc_ref, *, tiles_per_seq):
    i = pl.program_id(0)
    f = pl.program_id(1)
    tm = x_ref.shape[0]
    pad = SUBLANES

    @pl.when(f == 0)
    def _():
        halo = _rms(halo_ref[...], g_ref[...])
        halo = jnp.where(i % tiles_per_seq == 0, 0.0, halo)
        h_ref[0:pad, :] = halo.astype(BF16)
        h_ref[pad:pad + tm, :] = _rms(x_ref[...], g_ref[...]).astype(BF16)
        acc_ref[...] = jnp.zeros_like(acc_ref)

    hx = h_ref[...]
    gp = jnp.dot(hx, wg_ref[...], preferred_element_type=F32)
    up = jnp.dot(hx[pad:], wu_ref[...], preferred_element_type=F32)
    cw = cw_ref[...]
    gate = gp[pad:pad + tm] * cw[2:3] + gp[pad - 1:pad - 1 + tm] * cw[1:2] \
        + gp[pad - 2:pad - 2 + tm] * cw[0:1] + cb_ref[...]
    act = _silu(gate) * up
    acc_ref[...] += jnp.dot(act.astype(BF16), wd_ref[...], preferred_element_type=F32)

    @pl.when(f == pl.num_programs(1) - 1)
    def _():
        o_ref[...] = _rms(x_ref[...] + acc_ref[...], fg_ref[...])


def _ffn(x, g, wg, wu, cw, cb, wd, fg, *, seq, tm, tf):
    m, d = x.shape
    ff = wg.shape[1]
    assert seq % tm == 0 and ff % tf == 0 and tm % SUBLANES == 0
    hb = tm // SUBLANES
    return pl.pallas_call(
        functools.partial(_ffn_kernel, tiles_per_seq=seq // tm),
        grid=(m // tm, ff // tf),
        in_specs=[pl.BlockSpec((tm, d), lambda i, f: (i, 0)),
                  pl.BlockSpec((SUBLANES, d), lambda i, f: (jnp.maximum(i * hb - 1, 0), 0)),
                  pl.BlockSpec((1, d), lambda i, f: (0, 0)),
                  pl.BlockSpec((d, tf), lambda i, f: (0, f)),
                  pl.BlockSpec((d, tf), lambda i, f: (0, f)),
                  pl.BlockSpec((FFN_CONV, tf), lambda i, f: (0, f)),
                  pl.BlockSpec((1, tf), lambda i, f: (0, f)),
                  pl.BlockSpec((tf, d), lambda i, f: (f, 0)),
                  pl.BlockSpec((1, d), lambda i, f: (0, 0))],
        out_specs=pl.BlockSpec((tm, d), lambda i, f: (i, 0)),
        out_shape=jax.ShapeDtypeStruct((m, d), F32),
        scratch_shapes=[pltpu.VMEM((tm + SUBLANES, d), BF16), pltpu.VMEM((tm, d), F32)],
        compiler_params=pltpu.CompilerParams(
            dimension_semantics=("parallel", "arbitrary"), vmem_limit_bytes=VMEM_LIMIT),
        name="ffn",
    )(x, x, g.reshape(1, d), wg, wu, cw, cb.reshape(1, ff), wd, fg.reshape(1, d))


def _layer(x, mem, mix_norm_g, w_in, gdn_conv_w, a_log, dt_bias, gdn_norm_g, moba_norm_g, rel_bias,
           w_out, xattn_norm_g, mem_norm_g, w_xq, w_xkv, w_xo, ffn_norm_g, w_gate, w_up, ffn_conv_w,
           ffn_conv_b, w_down, final_g, *, last):
    b, s, d = x.shape
    m = b * s
    x2 = x.reshape(m, d)
    i1 = 4 * GDN_WIDTH
    i3 = i1 + 2 * N_GDN_HEADS
    w_a = w_in[:, :i1].astype(BF16)
    w_ba = jnp.pad(w_in[:, i1:i3], ((0, 0), (0, HEAD_DIM - 2 * N_GDN_HEADS))).astype(BF16)
    w_b = w_in[:, i3:].astype(BF16)
    tm = min(512, m)
    pa = _norm_matmul(x2, mix_norm_g, w_a, tm=tm, tn=1024).reshape(b, s, i1)
    ba = _norm_matmul(x2, mix_norm_g, w_ba, tm=tm, tn=HEAD_DIM).reshape(b, s, HEAD_DIM)
    pb = _norm_matmul(x2, mix_norm_g, w_b, tm=tm, tn=1024).reshape(b, s, 3 * MOBA_WIDTH)
    o_a = _gdn(pa, ba, gdn_conv_w, a_log, dt_bias, gdn_norm_g)
    o_b = _moba(pb, rel_bias, moba_norm_g)
    x1 = _out_proj(x2, o_a.reshape(m, GDN_WIDTH), o_b.reshape(m, MOBA_WIDTH),
                   w_out[:GDN_WIDTH].astype(BF16), w_out[GDN_WIDTH:].astype(BF16), tm=tm, tn=1024)
    mlen = mem.shape[1]
    kv = _norm_matmul(mem.reshape(b * mlen, d), mem_norm_g, w_xkv.astype(BF16),
                      tm=min(512, b * mlen), tn=1024).reshape(b, mlen, -1)
    x2b = _xattn(x1.reshape(b, s, d), xattn_norm_g, w_xq.astype(BF16), kv, w_xo.astype(BF16),
                 ts=min(512, s))
    assert last, "the final rmsnorm is fused into the last layer's ffn"
    y = _ffn(x2b.reshape(m, d), ffn_norm_g, w_gate.astype(BF16), w_up.astype(BF16), ffn_conv_w,
             ffn_conv_b, w_down.astype(BF16), final_g, seq=s, tm=min(512, s), tf=512)
    return y.reshape(b, s, d)


def kernel(x, mem, mix_norm_g, w_in, gdn_conv_w, gdn_a_log, gdn_dt_bias, gdn_norm_g, moba_norm_g,
           rel_bias, w_out, xattn_norm_g, mem_norm_g, w_xq, w_xkv, w_xo, ffn_norm_g, w_gate, w_up,
           ffn_conv_w, ffn_conv_b, w_down, final_norm_g):
    depth = mix_norm_g.shape[0]
    assert depth == 1
    l = 0
    return _layer(x, mem, mix_norm_g[l], w_in[l], gdn_conv_w[l], gdn_a_log[l], gdn_dt_bias[l],
                  gdn_norm_g[l], moba_norm_g[l], rel_bias, w_out[l], xattn_norm_g[l], mem_norm_g[l],
                  w_xq[l], w_xkv[l], w_xo[l], ffn_norm_g[l], w_gate[l], w_up[l], ffn_conv_w[l],
                  ffn_conv_b[l], w_down[l], final_norm_g, last=True)
```

```python
import functools
import math

import jax
import jax.numpy as jnp
import numpy as np
from jax import lax
from jax.experimental import pallas as pl
from jax.experimental.pallas import tpu as pltpu

HEAD_DIM = 128
N_GDN_HEADS = 8
N_MOBA_HEADS = 8
GDN_WIDTH = N_GDN_HEADS * HEAD_DIM
MOBA_WIDTH = N_MOBA_HEADS * HEAD_DIM
GDN_CONV = 4
GDN_CHUNK = 256
MOBA_BLOCK = 256
MOBA_TOPK = 3
REL_BUCKETS = 32
REL_MAX_DIST = 128
N_XATTN_HEADS = 4
FFN_CONV = 3
EPS = 1e-6
NEG = -1e30
SUBLANES = 8
ROW_SUB = 256
VMEM_LIMIT = 56 * 1024 * 1024

F32 = jnp.float32
BF16 = jnp.bfloat16


def _bdot(a, b):
    return jnp.dot(a.astype(BF16), b.astype(BF16), preferred_element_type=F32)


def _bdot_nt(a, b):
    return lax.dot_general(a.astype(BF16), b.astype(BF16), (((1,), (1,)), ((), ())),
                           preferred_element_type=F32)


def _bdot_tn(a, b):
    return lax.dot_general(a.astype(BF16), b.astype(BF16), (((0,), (0,)), ((), ())),
                           preferred_element_type=F32)


def _fdot(a, b):
    return jnp.dot(a, b, preferred_element_type=F32, precision=lax.Precision.HIGHEST)


def _sigmoid(x):
    return 1.0 / (1.0 + jnp.exp(-x))


def _silu(x):
    return x * _sigmoid(x)


def _rms(x, g):
    return x * lax.rsqrt(jnp.mean(x * x, axis=-1, keepdims=True) + EPS) * g


def _norm_matmul_kernel(x_ref, g_ref, w_ref, o_ref, *, rs):
    for r in range(x_ref.shape[0] // rs):
        rows = slice(r * rs, (r + 1) * rs)
        hn = _rms(x_ref[rows, :], g_ref[...]).astype(BF16)
        o_ref[rows, :] = jnp.dot(hn, w_ref[...], preferred_element_type=F32).astype(o_ref.dtype)


def _norm_matmul(x, g, w, *, tm, tn, out_dtype=F32):
    m, k = x.shape
    n = w.shape[1]
    assert m % tm == 0 and n % tn == 0
    rs = math.gcd(tm, ROW_SUB)
    return pl.pallas_call(
        functools.partial(_norm_matmul_kernel, rs=rs),
        grid=(m // tm, n // tn),
        in_specs=[pl.BlockSpec((tm, k), lambda i, j: (i, 0)),
                  pl.BlockSpec((1, k), lambda i, j: (0, 0)),
                  pl.BlockSpec((k, tn), lambda i, j: (0, j))],
        out_specs=pl.BlockSpec((tm, tn), lambda i, j: (i, j)),
        out_shape=jax.ShapeDtypeStruct((m, n), out_dtype),
        compiler_params=pltpu.CompilerParams(
            dimension_semantics=("parallel", "arbitrary"), vmem_limit_bytes=VMEM_LIMIT),
        name="norm_matmul",
    )(x, g.reshape(1, k), w)


def _unit_lower_inverse(mats, row, col):
    c = mats[0].shape[0]
    eye = (row == col).astype(F32)
    blk = lambda n: (row // n) == (col // n)
    inner = blk(16)
    ds = [jnp.where(inner, a, 0.0) for a in mats]
    ts = [eye - d for d in ds]
    ps = [_bdot(d, d) for d in ds]
    for step in range(3):
        ts = [t + _bdot(t, p) for t, p in zip(ts, ps)]
        if step < 2:
            ps = [_bdot(p, p) for p in ps]
    n = 32
    while n <= c:
        outer = blk(n) if n < c else None
        keep = ~inner if outer is None else (outer & ~inner)
        xs = [_bdot(jnp.where(keep, a, 0.0), t) for a, t in zip(mats, ts)]
        ts = [t - _bdot(t, x) for t, x in zip(ts, xs)]
        inner = outer
        n *= 2
    return ts


def _gdn_kernel(alog_ref, dtb_ref, q_ref, k_ref, v_ref, z_ref, ba_ref, wq_ref, wk_ref, wv_ref,
                ng_ref, o_ref, qp_ref, kp_ref, vp_ref, u_ref, wqd_ref, qk_ref, kw_ref, bc_ref, gl_ref,
                *, group):
    h = pl.program_id(1)
    seq = q_ref.shape[1]
    C = GDN_CHUNK
    D = HEAD_DIM
    pad = SUBLANES
    n_chunks = seq // C

    for src, dst in ((q_ref, qp_ref), (k_ref, kp_ref), (v_ref, vp_ref)):
        dst[0:pad, :] = jnp.zeros((pad, D), F32)
        dst[pad:pad + seq, :] = src[0]

    row = lax.broadcasted_iota(jnp.int32, (C, C), 0)
    col = lax.broadcasted_iota(jnp.int32, (C, C), 1)
    tri_incl = row >= col
    tri_strict = row > col
    lane = lax.broadcasted_iota(jnp.int32, (C, D), 1)
    neg_a = -jnp.exp(jnp.full((C, 1), alog_ref[h], F32))
    dt_bias = dtb_ref[h]
    scale = D ** -0.5

    def conv_silu(xp_ref, w_ref, r0):
        win = xp_ref[pl.ds(r0, C + pad), :]
        w = w_ref[...]
        y = win[pad:pad + C] * w[3:4]
        for j in range(GDN_CONV - 1):
            s = GDN_CONV - 1 - j
            y = y + win[pad - s:pad - s + C] * w[j:j + 1]
        return _silu(y)

    def l2n(x):
        return x * lax.rsqrt(jnp.sum(x * x, axis=-1, keepdims=True) + EPS)

    def prepare(grp, carry):
        cs = [grp * group + i for i in range(group)]
        r0s = [pl.multiple_of(c * C, C) for c in cs]
        pre = []
        for r0 in r0s:
            q = l2n(conv_silu(qp_ref, wq_ref, r0)) * scale
            k = l2n(conv_silu(kp_ref, wk_ref, r0))
            v = conv_silu(vp_ref, wv_ref, r0)
            ba = ba_ref[0, pl.ds(r0, C), :]
            b_col = jnp.sum(jnp.where(lane == h, ba, 0.0), axis=1, keepdims=True)
            a_col = jnp.sum(jnp.where(lane == h + N_GDN_HEADS, ba, 0.0), axis=1, keepdims=True)
            beta = _sigmoid(b_col)
            xs = a_col + dt_bias
            softplus = jnp.maximum(xs, 0.0) + jnp.log1p(jnp.exp(-jnp.abs(xs)))
            pre.append((q, k, v, beta, neg_a * softplus))
        gbs = [_fdot(tri_incl.astype(F32), jnp.broadcast_to(p[4], (C, D))) for p in pre]
        kk_qks = [_bdot_nt(jnp.concatenate([k * beta, q], axis=0), k) for q, k, v, beta, _ in pre]
        reps = C // D
        mats, mids = [], []
        for (q, k, v, beta, _), gb, kk_qk in zip(pre, gbs, kk_qks):
            g_i = jnp.concatenate([gb] * reps, axis=1)
            g_j = jnp.concatenate([gb.T] * reps, axis=0)
            decay = jnp.exp(jnp.where(tri_incl, g_i - g_j, NEG))
            mats.append(jnp.where(tri_strict, kk_qk[:C] * decay, 0.0))
            mids.append((kk_qk[C:] * decay, jnp.exp(gb), gb[C - 1:C, :]))
        ts = _unit_lower_inverse(mats, row, col)
        uws = [_bdot(t, jnp.concatenate([v * beta, k * beta * eg], axis=1))
               for t, (q, k, v, beta, _), (_, eg, _) in zip(ts, pre, mids)]
        transs = [_bdot_tn(k * jnp.exp(g_last - gb), uw)
                  for uw, (q, k, v, beta, _), gb, (_, _, g_last) in zip(uws, pre, gbs, mids)]
        for c, r0, uw, trans, (q, k, v, beta, _), (qk, eg, g_last) in zip(cs, r0s, uws, transs, pre, mids):
            u_ref[pl.ds(r0, C), :] = uw[:, :D]
            wqd_ref[c, 0:C, :] = uw[:, D:].astype(BF16)
            wqd_ref[c, C:2 * C, :] = (q * eg).astype(BF16)
            qk_ref[pl.ds(r0, C), :] = qk.astype(BF16)
            bc_ref[c] = trans[:, :D]
            kw_ref[c] = trans[:, D:].astype(BF16)
            gl_ref[c] = jnp.exp(g_last)
        return carry

    lax.fori_loop(0, n_chunks // group, prepare, 0)

    def scan(c, state):
        r0 = pl.multiple_of(c * C, C)
        sb = state.astype(BF16)
        ws_qs = jnp.dot(wqd_ref[c], sb, preferred_element_type=F32)
        v_new = u_ref[pl.ds(r0, C), :] - ws_qs[:C]
        o = ws_qs[C:] + jnp.dot(qk_ref[pl.ds(r0, C), :], v_new.astype(BF16), preferred_element_type=F32)
        z = z_ref[0, pl.ds(r0, C), :]
        o_ref[0, pl.ds(r0, C), :] = (_rms(o, ng_ref[...]) * _silu(z)).astype(o_ref.dtype)
        return state * gl_ref[c] - jnp.dot(kw_ref[c], sb, preferred_element_type=F32) + bc_ref[c]

    lax.fori_loop(0, n_chunks, scan, jnp.zeros((D, D), F32))


def _gdn(pa, conv_w, a_log, dt_bias, norm_g, *, group=4):
    b, s, _ = pa.shape
    H = N_GDN_HEADS
    C = GDN_CHUNK
    assert s % C == 0
    n_chunks = s // C
    head = lambda off: pl.BlockSpec((1, s, HEAD_DIM), lambda i, j: (i, 0, off + j))
    cw = lambda off: pl.BlockSpec((GDN_CONV, HEAD_DIM), lambda i, j: (0, off + j))
    smem = pl.BlockSpec(memory_space=pltpu.SMEM)
    return pl.pallas_call(
        functools.partial(_gdn_kernel, group=math.gcd(group, n_chunks)),
        grid=(b, H),
        in_specs=[smem, smem, head(0), head(H), head(2 * H), head(3 * H),
                  pl.BlockSpec((1, s, HEAD_DIM), lambda i, j: (i, 0, 4 * H)),
                  cw(0), cw(H), cw(2 * H),
                  pl.BlockSpec((1, HEAD_DIM), lambda i, j: (0, 0))],
        out_specs=pl.BlockSpec((1, s, HEAD_DIM), lambda i, j: (i, 0, j)),
        out_shape=jax.ShapeDtypeStruct((b, s, GDN_WIDTH), BF16),
        scratch_shapes=[pltpu.VMEM((s + SUBLANES, HEAD_DIM), F32)] * 3 + [
            pltpu.VMEM((s, HEAD_DIM), F32),
            pltpu.VMEM((n_chunks, 2 * C, HEAD_DIM), BF16),
            pltpu.VMEM((s, C), BF16),
            pltpu.VMEM((n_chunks, HEAD_DIM, HEAD_DIM), BF16),
            pltpu.VMEM((n_chunks, HEAD_DIM, HEAD_DIM), F32),
            pltpu.VMEM((n_chunks, 1, HEAD_DIM), F32)],
        compiler_params=pltpu.CompilerParams(
            dimension_semantics=("parallel", "arbitrary"), vmem_limit_bytes=VMEM_LIMIT),
        name="gdn",
    )(a_log, dt_bias, pa, pa, pa, pa, pa, conv_w, conv_w, conv_w, norm_g.reshape(1, HEAD_DIM))


def _bucket_upper_bounds():
    n = np.arange(0, 4 * REL_MAX_DIST, dtype=np.int64)
    max_exact = REL_BUCKETS // 2
    nf = np.maximum(n, 1).astype(np.float32)
    large = max_exact + (np.log(nf / np.float32(max_exact)) / np.float32(math.log(REL_MAX_DIST / max_exact))
                         * np.float32(REL_BUCKETS - max_exact)).astype(np.int32)
    large = np.minimum(large, REL_BUCKETS - 1)
    bucket = np.where(n < max_exact, n, large)
    assert np.all(np.diff(bucket) >= 0) and bucket[-1] == REL_BUCKETS - 1
    return [int(np.searchsorted(bucket, b, side="right")) for b in range(REL_BUCKETS - 1)]


_BUCKET_UPPER = _bucket_upper_bounds()


def _moba_kernel(rb_ref, q_ref, k_ref, v_ref, ng_ref, o_ref, kb_ref, vt_ref, km_ref, bd_ref, bl_ref):
    h = pl.program_id(0)
    b = pl.program_id(1)
    seq = k_ref.shape[1]
    T = MOBA_BLOCK
    nb = seq // T
    scale = HEAD_DIM ** -0.5
    kk = lax.broadcasted_iota(jnp.int32, (T, T), 0)
    qq = lax.broadcasted_iota(jnp.int32, (T, T), 1)

    @pl.when(b == 0)
    def _():
        def bias_of(n):
            val = jnp.full((T, T), rb_ref[REL_BUCKETS - 1, h], F32)
            for bkt in range(REL_BUCKETS - 2, -1, -1):
                val = jnp.where(n < _BUCKET_UPPER[bkt], rb_ref[bkt, h], val)
            return val
        bd_ref[...] = jnp.where(qq >= kk, bias_of(qq - kk), NEG)
        bl_ref[...] = bias_of(qq - kk + T)

    kb_ref[...] = k_ref[0].astype(BF16)
    for n in range(nb):
        blk_rows = slice(n * T, (n + 1) * T)
        vt_ref[:, blk_rows] = v_ref[0, blk_rows, :].T.astype(BF16)
        km_ref[n:n + 1, :] = jnp.mean(k_ref[0, blk_rows, :], axis=0, keepdims=True)
    far_bias = rb_ref[REL_BUCKETS - 1, h]

    for i in range(nb):
        qf = q_ref[0, i * T:(i + 1) * T, :]
        keep = None
        if i > MOBA_TOPK:
            gate = lax.dot_general(km_ref[0:i, :], qf, (((1,), (1,)), ((), ())),
                                   preferred_element_type=F32, precision=lax.Precision.HIGHEST)
            blk = lax.broadcasted_iota(jnp.int32, (i, T), 0)
            rank = jnp.zeros((i, T), jnp.int32)
            for m in range(i):
                gm = gate[m:m + 1, :]
                rank = rank + jnp.where((gm > gate) | ((gm == gate) & (m < blk)), 1, 0)
            keep = jnp.where(rank < MOBA_TOPK, 0.0, NEG)
        width = (i + 1) * T
        s_all = _bdot_nt(kb_ref[0:width, :], qf * scale)
        parts = []
        for j in range(i + 1):
            sj = s_all[j * T:(j + 1) * T, :]
            if j == i:
                sj = sj + bd_ref[...]
            elif j == i - 1:
                sj = sj + bl_ref[...]
                if keep is not None:
                    sj = sj + keep[j:j + 1, :]
            else:
                sj = sj + (far_bias if keep is None else keep[j:j + 1, :] + far_bias)
            parts.append(sj)
        m_row = jnp.max(functools.reduce(jnp.maximum, parts), axis=0, keepdims=True)
        ps = [jnp.exp(sj - m_row) for sj in parts]
        l_row = jnp.sum(functools.reduce(jnp.add, ps), axis=0, keepdims=True)
        p_all = jnp.concatenate([p.astype(BF16) for p in ps], axis=0)
        o_t = jnp.dot(vt_ref[:, 0:width], p_all, preferred_element_type=F32) / l_row
        o_ref[0, i * T:(i + 1) * T, :] = _rms(o_t.T, ng_ref[...]).astype(o_ref.dtype)


def _moba(pb, rel_bias, norm_g):
    b, s, _ = pb.shape
    H = N_MOBA_HEADS
    T = MOBA_BLOCK
    assert s % T == 0
    nb = s // T
    head = lambda off: pl.BlockSpec((1, s, HEAD_DIM), lambda h, bb: (bb, 0, off + h))
    return pl.pallas_call(
        _moba_kernel,
        grid=(H, b),
        in_specs=[pl.BlockSpec(memory_space=pltpu.SMEM), head(0), head(H), head(2 * H),
                  pl.BlockSpec((1, HEAD_DIM), lambda h, bb: (0, 0))],
        out_specs=pl.BlockSpec((1, s, HEAD_DIM), lambda h, bb: (bb, 0, h)),
        out_shape=jax.ShapeDtypeStruct((b, s, MOBA_WIDTH), BF16),
        scratch_shapes=[pltpu.VMEM((s, HEAD_DIM), BF16), pltpu.VMEM((HEAD_DIM, s), BF16),
                        pltpu.VMEM((nb, HEAD_DIM), F32),
                        pltpu.VMEM((T, T), F32), pltpu.VMEM((T, T), F32)],
        compiler_params=pltpu.CompilerParams(
            dimension_semantics=("arbitrary", "arbitrary"), vmem_limit_bytes=VMEM_LIMIT),
        name="moba",
    )(rel_bias, pb, pb, pb, norm_g.reshape(1, HEAD_DIM))


def _out_proj_kernel(x_ref, oa_ref, ob_ref, wa_ref, wb_ref, o_ref, *, rs):
    for r in range(x_ref.shape[0] // rs):
        rows = slice(r * rs, (r + 1) * rs)
        o_ref[rows, :] = (x_ref[rows, :]
                          + jnp.dot(oa_ref[rows, :], wa_ref[...], preferred_element_type=F32)
                          + jnp.dot(ob_ref[rows, :], wb_ref[...], preferred_element_type=F32))


def _out_proj(x, oa, ob, wa, wb, *, tm, tn):
    m, d = x.shape
    ka, kb = oa.shape[1], ob.shape[1]
    return pl.pallas_call(
        functools.partial(_out_proj_kernel, rs=math.gcd(tm, ROW_SUB)),
        grid=(m // tm, d // tn),
        in_specs=[pl.BlockSpec((tm, tn), lambda i, j: (i, j)),
                  pl.BlockSpec((tm, ka), lambda i, j: (i, 0)),
                  pl.BlockSpec((tm, kb), lambda i, j: (i, 0)),
                  pl.BlockSpec((ka, tn), lambda i, j: (0, j)),
                  pl.BlockSpec((kb, tn), lambda i, j: (0, j))],
        out_specs=pl.BlockSpec((tm, tn), lambda i, j: (i, j)),
        out_shape=jax.ShapeDtypeStruct((m, d), F32),
        compiler_params=pltpu.CompilerParams(
            dimension_semantics=("parallel", "arbitrary"), vmem_limit_bytes=VMEM_LIMIT),
        name="out_proj",
    )(x, oa, ob, wa, wb)


def _xattn_kernel(x_ref, g_ref, wq_ref, kv_ref, wo_ref, o_ref):
    x = x_ref[0]
    hn = _rms(x, g_ref[...])
    q = _bdot(hn, wq_ref[...])
    kv = kv_ref[0]
    width = N_XATTN_HEADS * HEAD_DIM
    outs = []
    for hd in range(N_XATTN_HEADS):
        sl = slice(hd * HEAD_DIM, (hd + 1) * HEAD_DIM)
        kh = kv[:, sl]
        vh = kv[:, width + hd * HEAD_DIM: width + (hd + 1) * HEAD_DIM]
        s = _bdot_nt(q[:, sl], kh) * HEAD_DIM ** -0.5
        p = jnp.exp(s - jnp.max(s, axis=-1, keepdims=True))
        p = p / jnp.sum(p, axis=-1, keepdims=True)
        outs.append(_bdot(p, vh))
    o = jnp.concatenate(outs, axis=-1)
    o_ref[0] = x + _bdot(o, wo_ref[...])


def _xattn(x, g, wq, kv, wo, *, ts):
    b, s, d = x.shape
    mlen = kv.shape[1]
    width = wq.shape[1]
    return pl.pallas_call(
        _xattn_kernel,
        grid=(b, s // ts),
        in_specs=[pl.BlockSpec((1, ts, d), lambda i, j: (i, j, 0)),
                  pl.BlockSpec((1, d), lambda i, j: (0, 0)),
                  pl.BlockSpec((d, width), lambda i, j: (0, 0)),
                  pl.BlockSpec((1, mlen, 2 * width), lambda i, j: (i, 0, 0)),
                  pl.BlockSpec((width, d), lambda i, j: (0, 0))],
        out_specs=pl.BlockSpec((1, ts, d), lambda i, j: (i, j, 0)),
        out_shape=jax.ShapeDtypeStruct((b, s, d), F32),
        compiler_params=pltpu.CompilerParams(
            dimension_semantics=("parallel", "parallel"), vmem_limit_bytes=VMEM_LIMIT),
        name="xattn",
    )(x, g.reshape(1, d), wq, kv, wo)


def _ffn_kernel(x_ref, halo_ref, g_ref, wg_ref, wu_ref, cw_ref, cb_ref, wd_ref, fg_ref, o_ref,
                h_ref, acc_ref, *, tiles_per_seq):
    i = pl.program_id(0)
    f = pl.program_id(1)
    tm = x_ref.shape[0]
    pad = SUBLANES

    @pl.when(f == 0)
    def _():
        halo = _rms(halo_ref[...], g_ref[...])
        halo = jnp.where(i % tiles_per_seq == 0, 0.0, halo)
        h_ref[0:pad, :] = halo.astype(BF16)
        h_ref[pad:pad + tm, :] = _rms(x_ref[...], g_ref[...]).astype(BF16)
        acc_ref[...] = jnp.zeros_like(acc_ref)

    hx = h_ref[...]
    gp = jnp.dot(hx, wg_ref[...], preferred_element_type=F32)
    up = jnp.dot(hx[pad:], wu_ref[...], preferred_element_type=F32)
    cw = cw_ref[...]
    gate = gp[pad:pad + tm] * cw[2:3] + gp[pad - 1:pad - 1 + tm] * cw[1:2] \
        + gp[pad - 2:pad - 2 + tm] * cw[0:1] + cb_ref[...]
    act = _silu(gate) * up
    acc_ref[...] += jnp.dot(act.astype(BF16), wd_ref[...], preferred_element_type=F32)

    @pl.when(f == pl.num_programs(1) - 1)
    def _():
        o_ref[...] = _rms(x_ref[...] + acc_ref[...], fg_ref[...])


def _ffn(x, g, wg, wu, cw, cb, wd, fg, *, seq, tm, tf):
    m, d = x.shape
    ff = wg.shape[1]
    assert seq % tm == 0 and ff % tf == 0 and tm % SUBLANES == 0
    hb = tm // SUBLANES
    return pl.pallas_call(
        functools.partial(_ffn_kernel, tiles_per_seq=seq // tm),
        grid=(m // tm, ff // tf),
        in_specs=[pl.BlockSpec((tm, d), lambda i, f: (i, 0)),
                  pl.BlockSpec((SUBLANES, d), lambda i, f: (jnp.maximum(i * hb - 1, 0), 0)),
                  pl.BlockSpec((1, d), lambda i, f: (0, 0)),
                  pl.BlockSpec((d, tf), lambda i, f: (0, f)),
                  pl.BlockSpec((d, tf), lambda i, f: (0, f)),
                  pl.BlockSpec((FFN_CONV, tf), lambda i, f: (0, f)),
                  pl.BlockSpec((1, tf), lambda i, f: (0, f)),
                  pl.BlockSpec((tf, d), lambda i, f: (f, 0)),
                  pl.BlockSpec((1, d), lambda i, f: (0, 0))],
        out_specs=pl.BlockSpec((tm, d), lambda i, f: (i, 0)),
        out_shape=jax.ShapeDtypeStruct((m, d), F32),
        scratch_shapes=[pltpu.VMEM((tm + SUBLANES, d), BF16), pltpu.VMEM((tm, d), F32)],
        compiler_params=pltpu.CompilerParams(
            dimension_semantics=("parallel", "arbitrary"), vmem_limit_bytes=VMEM_LIMIT),
        name="ffn",
    )(x, x, g.reshape(1, d), wg, wu, cw, cb.reshape(1, ff), wd, fg.reshape(1, d))


def _layer(x, mem, mix_norm_g, w_in, gdn_conv_w, a_log, dt_bias, gdn_norm_g, moba_norm_g, rel_bias,
           w_out, xattn_norm_g, mem_norm_g, w_xq, w_xkv, w_xo, ffn_norm_g, w_gate, w_up, ffn_conv_w,
           ffn_conv_b, w_down, final_g, *, last):
    b, s, d = x.shape
    m = b * s
    x2 = x.reshape(m, d)
    i1 = 4 * GDN_WIDTH
    i3 = i1 + 2 * N_GDN_HEADS
    w_a = jnp.pad(w_in[:, :i3], ((0, 0), (0, HEAD_DIM - 2 * N_GDN_HEADS))).astype(BF16)
    w_b = w_in[:, i3:].astype(BF16)
    tm = min(1024, m)
    wa_cols = i1 + HEAD_DIM
    pa = _norm_matmul(x2, mix_norm_g, w_a, tm=tm, tn=wa_cols // 3).reshape(b, s, wa_cols)
    pb = _norm_matmul(x2, mix_norm_g, w_b, tm=tm, tn=1024).reshape(b, s, 3 * MOBA_WIDTH)
    o_a = _gdn(pa, gdn_conv_w, a_log, dt_bias, gdn_norm_g)
    o_b = _moba(pb, rel_bias, moba_norm_g)
    x1 = _out_proj(x2, o_a.reshape(m, GDN_WIDTH), o_b.reshape(m, MOBA_WIDTH),
                   w_out[:GDN_WIDTH].astype(BF16), w_out[GDN_WIDTH:].astype(BF16), tm=tm, tn=1024)
    mlen = mem.shape[1]
    kv = _norm_matmul(mem.reshape(b * mlen, d), mem_norm_g, w_xkv.astype(BF16),
                      tm=min(512, b * mlen), tn=1024, out_dtype=BF16).reshape(b, mlen, -1)
    x2b = _xattn(x1.reshape(b, s, d), xattn_norm_g, w_xq.astype(BF16), kv, w_xo.astype(BF16),
                 ts=min(512, s))
    assert last, "the final rmsnorm is fused into the last layer's ffn"
    y = _ffn(x2b.reshape(m, d), ffn_norm_g, w_gate.astype(BF16), w_up.astype(BF16), ffn_conv_w,
             ffn_conv_b, w_down.astype(BF16), final_g, seq=s, tm=min(512, s), tf=512)
    return y.reshape(b, s, d)


def kernel(x, mem, mix_norm_g, w_in, gdn_conv_w, gdn_a_log, gdn_dt_bias, gdn_norm_g, moba_norm_g,
           rel_bias, w_out, xattn_norm_g, mem_norm_g, w_xq, w_xkv, w_xo, ffn_norm_g, w_gate, w_up,
           ffn_conv_w, ffn_conv_b, w_down, final_norm_g):
    depth = mix_norm_g.shape[0]
    assert depth == 1
    l = 0
    return _layer(x, mem, mix_norm_g[l], w_in[l], gdn_conv_w[l], gdn_a_log[l], gdn_dt_bias[l],
                  gdn_norm_g[l], moba_norm_g[l], rel_bias, w_out[l], xattn_norm_g[l], mem_norm_g[l],
                  w_xq[l], w_xkv[l], w_xo[l], ffn_norm_g[l], w_gate[l], w_up[l], ffn_conv_w[l],
                  ffn_conv_b[l], w_down[l], final_norm_g, last=True)
```

```python
import functools
import math

import jax
import jax.numpy as jnp
import numpy as np
from jax import lax
from jax.experimental import pallas as pl
from jax.experimental.pallas import tpu as pltpu

HEAD_DIM = 128
N_GDN_HEADS = 8
N_MOBA_HEADS = 8
GDN_WIDTH = N_GDN_HEADS * HEAD_DIM
MOBA_WIDTH = N_MOBA_HEADS * HEAD_DIM
GDN_CONV = 4
GDN_CHUNK = 256
MOBA_BLOCK = 256
MOBA_TOPK = 3
REL_BUCKETS = 32
REL_MAX_DIST = 128
N_XATTN_HEADS = 4
FFN_CONV = 3
EPS = 1e-6
NEG = -1e30
SUBLANES = 8
BF16_ROWS = 16
ROW_SUB = 256
VMEM_LIMIT = 58 * 1024 * 1024

F32 = jnp.float32
BF16 = jnp.bfloat16


def _bdot(a, b):
    return jnp.dot(a.astype(BF16), b.astype(BF16), preferred_element_type=F32)


def _bdot_nt(a, b):
    return lax.dot_general(a.astype(BF16), b.astype(BF16), (((1,), (1,)), ((), ())),
                           preferred_element_type=F32)


def _bdot_tn(a, b):
    return lax.dot_general(a.astype(BF16), b.astype(BF16), (((0,), (0,)), ((), ())),
                           preferred_element_type=F32)


def _fdot(a, b):
    return jnp.dot(a, b, preferred_element_type=F32, precision=lax.Precision.HIGHEST)


def _sigmoid(x):
    return 1.0 / (1.0 + jnp.exp(-x))


def _silu(x):
    return x * _sigmoid(x)


def _rms(x, g):
    return x * lax.rsqrt(jnp.mean(x * x, axis=-1, keepdims=True) + EPS) * g


def _norm_matmul_kernel(x_ref, g_ref, w_ref, o_ref, *, rs):
    for r in range(x_ref.shape[0] // rs):
        rows = slice(r * rs, (r + 1) * rs)
        hn = _rms(x_ref[rows, :], g_ref[...]).astype(BF16)
        o_ref[rows, :] = jnp.dot(hn, w_ref[...], preferred_element_type=F32).astype(o_ref.dtype)


def _norm_matmul(x, g, w, *, tm, tn, out_dtype=F32):
    m, k = x.shape
    n = w.shape[1]
    assert m % tm == 0 and n % tn == 0
    rs = math.gcd(tm, ROW_SUB)
    return pl.pallas_call(
        functools.partial(_norm_matmul_kernel, rs=rs),
        grid=(m // tm, n // tn),
        in_specs=[pl.BlockSpec((tm, k), lambda i, j: (i, 0)),
                  pl.BlockSpec((1, k), lambda i, j: (0, 0)),
                  pl.BlockSpec((k, tn), lambda i, j: (0, j))],
        out_specs=pl.BlockSpec((tm, tn), lambda i, j: (i, j)),
        out_shape=jax.ShapeDtypeStruct((m, n), out_dtype),
        compiler_params=pltpu.CompilerParams(
            dimension_semantics=("parallel", "arbitrary"), vmem_limit_bytes=VMEM_LIMIT),
        name="norm_matmul",
    )(x, g.reshape(1, k), w)


def _unit_lower_inverse(mats, row, col):
    c = mats[0].shape[0]
    eye = (row == col).astype(F32)
    blk = lambda n: (row // n) == (col // n)
    inner = blk(16)
    ds = [jnp.where(inner, a, 0.0) for a in mats]
    ts = [eye - d for d in ds]
    ps = [_bdot(d, d) for d in ds]
    for step in range(3):
        ts = [t + _bdot(t, p) for t, p in zip(ts, ps)]
        if step < 2:
            ps = [_bdot(p, p) for p in ps]
    n = 32
    while n <= c:
        outer = blk(n) if n < c else None
        keep = ~inner if outer is None else (outer & ~inner)
        xs = [_bdot(jnp.where(keep, a, 0.0), t) for a, t in zip(mats, ts)]
        ts = [t - _bdot(t, x) for t, x in zip(ts, xs)]
        inner = outer
        n *= 2
    return ts


def _gdn_kernel(alog_ref, dtb_ref, q_ref, k_ref, v_ref, z_ref, ba_ref, wq_ref, wk_ref, wv_ref,
                ng_ref, o_ref, qp_ref, kp_ref, vp_ref, u_ref, wqd_ref, qk_ref, kw_ref, bc_ref, gl_ref,
                *, group):
    h = pl.program_id(1)
    seq = q_ref.shape[1]
    C = GDN_CHUNK
    D = HEAD_DIM
    pad = SUBLANES
    n_chunks = seq // C

    for src, dst in ((q_ref, qp_ref), (k_ref, kp_ref), (v_ref, vp_ref)):
        dst[0:pad, :] = jnp.zeros((pad, D), F32)
        dst[pad:pad + seq, :] = src[0]

    row = lax.broadcasted_iota(jnp.int32, (C, C), 0)
    col = lax.broadcasted_iota(jnp.int32, (C, C), 1)
    tri_incl = row >= col
    tri_strict = row > col
    lane = lax.broadcasted_iota(jnp.int32, (C, D), 1)
    neg_a = -jnp.exp(jnp.full((C, 1), alog_ref[h], F32))
    dt_bias = dtb_ref[h]
    scale = D ** -0.5

    def conv_silu(xp_ref, w_ref, r0):
        win = xp_ref[pl.ds(r0, C + pad), :]
        w = w_ref[...]
        y = win[pad:pad + C] * w[3:4]
        for j in range(GDN_CONV - 1):
            s = GDN_CONV - 1 - j
            y = y + win[pad - s:pad - s + C] * w[j:j + 1]
        return _silu(y)

    def l2n(x):
        return x * lax.rsqrt(jnp.sum(x * x, axis=-1, keepdims=True) + EPS)

    def prepare(grp, carry):
        cs = [grp * group + i for i in range(group)]
        r0s = [pl.multiple_of(c * C, C) for c in cs]
        pre = []
        for r0 in r0s:
            q = l2n(conv_silu(qp_ref, wq_ref, r0)) * scale
            k = l2n(conv_silu(kp_ref, wk_ref, r0))
            v = conv_silu(vp_ref, wv_ref, r0)
            ba = ba_ref[0, pl.ds(r0, C), :]
            b_col = jnp.sum(jnp.where(lane == h, ba, 0.0), axis=1, keepdims=True)
            a_col = jnp.sum(jnp.where(lane == h + N_GDN_HEADS, ba, 0.0), axis=1, keepdims=True)
            beta = _sigmoid(b_col)
            xs = a_col + dt_bias
            softplus = jnp.maximum(xs, 0.0) + jnp.log1p(jnp.exp(-jnp.abs(xs)))
            pre.append((q, k, v, beta, neg_a * softplus))
        gbs = [_fdot(tri_incl.astype(F32), jnp.broadcast_to(p[4], (C, D))) for p in pre]
        kk_qks = [_bdot_nt(jnp.concatenate([k * beta, q], axis=0), k) for q, k, v, beta, _ in pre]
        reps = C // D
        mats, mids = [], []
        for (q, k, v, beta, _), gb, kk_qk in zip(pre, gbs, kk_qks):
            g_i = jnp.concatenate([gb] * reps, axis=1)
            g_j = jnp.concatenate([gb.T] * reps, axis=0)
            decay = jnp.exp(jnp.where(tri_incl, g_i - g_j, NEG))
            mats.append(jnp.where(tri_strict, kk_qk[:C] * decay, 0.0))
            mids.append((kk_qk[C:] * decay, jnp.exp(gb), gb[C - 1:C, :]))
        ts = _unit_lower_inverse(mats, row, col)
        uws = [_bdot(t, jnp.concatenate([v * beta, k * beta * eg], axis=1))
               for t, (q, k, v, beta, _), (_, eg, _) in zip(ts, pre, mids)]
        transs = [_bdot_tn(k * jnp.exp(g_last - gb), uw)
                  for uw, (q, k, v, beta, _), gb, (_, _, g_last) in zip(uws, pre, gbs, mids)]
        for c, r0, uw, trans, (q, k, v, beta, _), (qk, eg, g_last) in zip(cs, r0s, uws, transs, pre, mids):
            u_ref[pl.ds(r0, C), :] = uw[:, :D]
            wqd_ref[c, 0:C, :] = uw[:, D:].astype(BF16)
            wqd_ref[c, C:2 * C, :] = (q * eg).astype(BF16)
            qk_ref[pl.ds(r0, C), :] = qk.astype(BF16)
            bc_ref[c] = trans[:, :D]
            kw_ref[c] = trans[:, D:].astype(BF16)
            gl_ref[c] = jnp.exp(g_last)
        return carry

    lax.fori_loop(0, n_chunks // group, prepare, 0)

    def scan(c, state):
        r0 = pl.multiple_of(c * C, C)
        sb = state.astype(BF16)
        ws_qs = jnp.dot(wqd_ref[c], sb, preferred_element_type=F32)
        v_new = u_ref[pl.ds(r0, C), :] - ws_qs[:C]
        o = ws_qs[C:] + jnp.dot(qk_ref[pl.ds(r0, C), :], v_new.astype(BF16), preferred_element_type=F32)
        z = z_ref[0, pl.ds(r0, C), :]
        o_ref[0, pl.ds(r0, C), :] = (_rms(o, ng_ref[...]) * _silu(z)).astype(o_ref.dtype)
        return state * gl_ref[c] - jnp.dot(kw_ref[c], sb, preferred_element_type=F32) + bc_ref[c]

    lax.fori_loop(0, n_chunks, scan, jnp.zeros((D, D), F32))


def _gdn(pa, conv_w, a_log, dt_bias, norm_g, *, group=4):
    b, s, _ = pa.shape
    H = N_GDN_HEADS
    C = GDN_CHUNK
    assert s % C == 0
    n_chunks = s // C
    head = lambda off: pl.BlockSpec((1, s, HEAD_DIM), lambda i, j: (i, 0, off + j))
    cw = lambda off: pl.BlockSpec((GDN_CONV, HEAD_DIM), lambda i, j: (0, off + j))
    smem = pl.BlockSpec(memory_space=pltpu.SMEM)
    return pl.pallas_call(
        functools.partial(_gdn_kernel, group=math.gcd(group, n_chunks)),
        grid=(b, H),
        in_specs=[smem, smem, head(0), head(H), head(2 * H), head(3 * H),
                  pl.BlockSpec((1, s, HEAD_DIM), lambda i, j: (i, 0, 4 * H)),
                  cw(0), cw(H), cw(2 * H),
                  pl.BlockSpec((1, HEAD_DIM), lambda i, j: (0, 0))],
        out_specs=pl.BlockSpec((1, s, HEAD_DIM), lambda i, j: (i, 0, j)),
        out_shape=jax.ShapeDtypeStruct((b, s, GDN_WIDTH), BF16),
        scratch_shapes=[pltpu.VMEM((s + SUBLANES, HEAD_DIM), F32)] * 3 + [
            pltpu.VMEM((s, HEAD_DIM), F32),
            pltpu.VMEM((n_chunks, 2 * C, HEAD_DIM), BF16),
            pltpu.VMEM((s, C), BF16),
            pltpu.VMEM((n_chunks, HEAD_DIM, HEAD_DIM), BF16),
            pltpu.VMEM((n_chunks, HEAD_DIM, HEAD_DIM), F32),
            pltpu.VMEM((n_chunks, 1, HEAD_DIM), F32)],
        compiler_params=pltpu.CompilerParams(
            dimension_semantics=("parallel", "arbitrary"), vmem_limit_bytes=VMEM_LIMIT),
        name="gdn",
    )(a_log, dt_bias, pa, pa, pa, pa, pa, conv_w, conv_w, conv_w, norm_g.reshape(1, HEAD_DIM))


def _bucket_upper_bounds():
    n = np.arange(0, 4 * REL_MAX_DIST, dtype=np.int64)
    max_exact = REL_BUCKETS // 2
    nf = np.maximum(n, 1).astype(np.float32)
    large = max_exact + (np.log(nf / np.float32(max_exact)) / np.float32(math.log(REL_MAX_DIST / max_exact))
                         * np.float32(REL_BUCKETS - max_exact)).astype(np.int32)
    large = np.minimum(large, REL_BUCKETS - 1)
    bucket = np.where(n < max_exact, n, large)
    assert np.all(np.diff(bucket) >= 0) and bucket[-1] == REL_BUCKETS - 1
    return [int(np.searchsorted(bucket, b, side="right")) for b in range(REL_BUCKETS - 1)]


_BUCKET_UPPER = _bucket_upper_bounds()


def _moba_kernel(rb_ref, q_ref, k_ref, v_ref, ng_ref, o_ref, kb_ref, vt_ref, km_ref, bd_ref, bl_ref):
    h = pl.program_id(0)
    b = pl.program_id(1)
    seq = k_ref.shape[1]
    T = MOBA_BLOCK
    nb = seq // T
    scale = HEAD_DIM ** -0.5
    kk = lax.broadcasted_iota(jnp.int32, (T, T), 0)
    qq = lax.broadcasted_iota(jnp.int32, (T, T), 1)

    @pl.when(b == 0)
    def _():
        def bias_of(n):
            val = jnp.full((T, T), rb_ref[REL_BUCKETS - 1, h], F32)
            for bkt in range(REL_BUCKETS - 2, -1, -1):
                val = jnp.where(n < _BUCKET_UPPER[bkt], rb_ref[bkt, h], val)
            return val
        bd_ref[...] = jnp.where(qq >= kk, bias_of(qq - kk), NEG)
        bl_ref[...] = bias_of(qq - kk + T)

    kb_ref[...] = k_ref[0].astype(BF16)
    for n in range(nb):
        blk_rows = slice(n * T, (n + 1) * T)
        vt_ref[:, blk_rows] = v_ref[0, blk_rows, :].T.astype(BF16)
        km_ref[n:n + 1, :] = jnp.mean(k_ref[0, blk_rows, :], axis=0, keepdims=True)
    far_bias = rb_ref[REL_BUCKETS - 1, h]

    for i in range(nb):
        qf = q_ref[0, i * T:(i + 1) * T, :]
        keep = None
        if i > MOBA_TOPK:
            gate = lax.dot_general(km_ref[0:i, :], qf, (((1,), (1,)), ((), ())),
                                   preferred_element_type=F32, precision=lax.Precision.HIGHEST)
            blk = lax.broadcasted_iota(jnp.int32, (i, T), 0)
            rank = jnp.zeros((i, T), jnp.int32)
            for m in range(i):
                gm = gate[m:m + 1, :]
                rank = rank + jnp.where((gm > gate) | ((gm == gate) & (m < blk)), 1, 0)
            keep = jnp.where(rank < MOBA_TOPK, 0.0, NEG)
        width = (i + 1) * T
        s_all = _bdot_nt(kb_ref[0:width, :], qf * scale)
        parts = []
        for j in range(i + 1):
            sj = s_all[j * T:(j + 1) * T, :]
            if j == i:
                sj = sj + bd_ref[...]
            elif j == i - 1:
                sj = sj + bl_ref[...]
                if keep is not None:
                    sj = sj + keep[j:j + 1, :]
            else:
                sj = sj + (far_bias if keep is None else keep[j:j + 1, :] + far_bias)
            parts.append(sj)
        m_row = jnp.max(functools.reduce(jnp.maximum, parts), axis=0, keepdims=True)
        ps = [jnp.exp(sj - m_row) for sj in parts]
        l_row = jnp.sum(functools.reduce(jnp.add, ps), axis=0, keepdims=True)
        p_all = jnp.concatenate([p.astype(BF16) for p in ps], axis=0)
        o_t = jnp.dot(vt_ref[:, 0:width], p_all, preferred_element_type=F32) / l_row
        o_ref[0, i * T:(i + 1) * T, :] = _rms(o_t.T, ng_ref[...]).astype(o_ref.dtype)


def _moba(pb, rel_bias, norm_g):
    b, s, _ = pb.shape
    H = N_MOBA_HEADS
    T = MOBA_BLOCK
    assert s % T == 0
    nb = s // T
    head = lambda off: pl.BlockSpec((1, s, HEAD_DIM), lambda h, bb: (bb, 0, off + h))
    return pl.pallas_call(
        _moba_kernel,
        grid=(H, b),
        in_specs=[pl.BlockSpec(memory_space=pltpu.SMEM), head(0), head(H), head(2 * H),
                  pl.BlockSpec((1, HEAD_DIM), lambda h, bb: (0, 0))],
        out_specs=pl.BlockSpec((1, s, HEAD_DIM), lambda h, bb: (bb, 0, h)),
        out_shape=jax.ShapeDtypeStruct((b, s, MOBA_WIDTH), BF16),
        scratch_shapes=[pltpu.VMEM((s, HEAD_DIM), BF16), pltpu.VMEM((HEAD_DIM, s), BF16),
                        pltpu.VMEM((nb, HEAD_DIM), F32),
                        pltpu.VMEM((T, T), F32), pltpu.VMEM((T, T), F32)],
        compiler_params=pltpu.CompilerParams(
            dimension_semantics=("arbitrary", "arbitrary"), vmem_limit_bytes=VMEM_LIMIT),
        name="moba",
    )(rel_bias, pb, pb, pb, norm_g.reshape(1, HEAD_DIM))


def _out_proj_kernel(x_ref, oa_ref, ob_ref, wa_ref, wb_ref, o_ref, *, rs):
    for r in range(x_ref.shape[0] // rs):
        rows = slice(r * rs, (r + 1) * rs)
        o_ref[rows, :] = (x_ref[rows, :]
                          + jnp.dot(oa_ref[rows, :], wa_ref[...], preferred_element_type=F32)
                          + jnp.dot(ob_ref[rows, :], wb_ref[...], preferred_element_type=F32))


def _out_proj(x, oa, ob, wa, wb, *, tm, tn):
    m, d = x.shape
    ka, kb = oa.shape[1], ob.shape[1]
    return pl.pallas_call(
        functools.partial(_out_proj_kernel, rs=math.gcd(tm, ROW_SUB)),
        grid=(m // tm, d // tn),
        in_specs=[pl.BlockSpec((tm, tn), lambda i, j: (i, j)),
                  pl.BlockSpec((tm, ka), lambda i, j: (i, 0)),
                  pl.BlockSpec((tm, kb), lambda i, j: (i, 0)),
                  pl.BlockSpec((ka, tn), lambda i, j: (0, j)),
                  pl.BlockSpec((kb, tn), lambda i, j: (0, j))],
        out_specs=pl.BlockSpec((tm, tn), lambda i, j: (i, j)),
        out_shape=jax.ShapeDtypeStruct((m, d), F32),
        compiler_params=pltpu.CompilerParams(
            dimension_semantics=("parallel", "arbitrary"), vmem_limit_bytes=VMEM_LIMIT),
        name="out_proj",
    )(x, oa, ob, wa, wb)


def _xattn_kernel(x_ref, g_ref, wq_ref, kv_ref, wo_ref, ng_ref, o_ref, hn_ref):
    x = x_ref[0]
    hn = _rms(x, g_ref[...])
    q = _bdot(hn, wq_ref[...])
    kv = kv_ref[0]
    width = N_XATTN_HEADS * HEAD_DIM
    outs = []
    for hd in range(N_XATTN_HEADS):
        sl = slice(hd * HEAD_DIM, (hd + 1) * HEAD_DIM)
        kh = kv[:, sl]
        vh = kv[:, width + hd * HEAD_DIM: width + (hd + 1) * HEAD_DIM]
        s = _bdot_nt(q[:, sl], kh) * HEAD_DIM ** -0.5
        p = jnp.exp(s - jnp.max(s, axis=-1, keepdims=True))
        p = p / jnp.sum(p, axis=-1, keepdims=True)
        outs.append(_bdot(p, vh))
    o = jnp.concatenate(outs, axis=-1)
    y = x + _bdot(o, wo_ref[...])
    o_ref[0] = y
    hn_ref[0] = _rms(y, ng_ref[...]).astype(BF16)


def _xattn(x, g, wq, kv, wo, next_g, *, ts):
    b, s, d = x.shape
    mlen = kv.shape[1]
    width = wq.shape[1]
    tile = pl.BlockSpec((1, ts, d), lambda i, j: (i, j, 0))
    vec = pl.BlockSpec((1, d), lambda i, j: (0, 0))
    return pl.pallas_call(
        _xattn_kernel,
        grid=(b, s // ts),
        in_specs=[tile, vec,
                  pl.BlockSpec((d, width), lambda i, j: (0, 0)),
                  pl.BlockSpec((1, mlen, 2 * width), lambda i, j: (i, 0, 0)),
                  pl.BlockSpec((width, d), lambda i, j: (0, 0)),
                  vec],
        out_specs=[tile, tile],
        out_shape=[jax.ShapeDtypeStruct((b, s, d), F32), jax.ShapeDtypeStruct((b, s, d), BF16)],
        compiler_params=pltpu.CompilerParams(
            dimension_semantics=("parallel", "parallel"), vmem_limit_bytes=VMEM_LIMIT),
        name="xattn",
    )(x, g.reshape(1, d), wq, kv, wo, next_g.reshape(1, d))


def _ffn_kernel(x_hbm, h_ref, halo_ref, wg_ref, wu_ref, cw_ref, cb_ref, wd_ref, fg_ref, o_ref,
                hx_ref, xres_ref, sem, *, tiles_per_seq, rs):
    i = pl.program_id(0)
    f = pl.program_id(1)
    tm = h_ref.shape[0]
    pad = BF16_ROWS

    def x_copy():
        return pltpu.make_async_copy(x_hbm.at[pl.ds(i * tm, tm), :], xres_ref, sem)

    @pl.when(f == 0)
    def _():
        x_copy().start()
        halo = halo_ref[...]
        hx_ref[0:pad, :] = jnp.where(i % tiles_per_seq == 0, jnp.zeros_like(halo), halo)
        hx_ref[pad:pad + tm, :] = h_ref[...]
        o_ref[...] = jnp.zeros_like(o_ref)

    cw = cw_ref[...]
    keep = SUBLANES
    tail = None
    for r in range(tm // rs):
        rows = slice(r * rs, (r + 1) * rs)
        hrows = slice(pad + r * rs, pad + (r + 1) * rs)
        if r == 0:
            gp = jnp.dot(hx_ref[0:pad + rs, :], wg_ref[...], preferred_element_type=F32)[pad - keep:]
        else:
            gp = jnp.concatenate(
                [tail, jnp.dot(hx_ref[hrows, :], wg_ref[...], preferred_element_type=F32)], axis=0)
        tail = gp[rs:rs + keep]
        up = jnp.dot(hx_ref[hrows, :], wu_ref[...], preferred_element_type=F32)
        gate = gp[keep:keep + rs] * cw[2:3] + gp[keep - 1:keep - 1 + rs] * cw[1:2] \
            + gp[keep - 2:keep - 2 + rs] * cw[0:1] + cb_ref[...]
        act = _silu(gate) * up
        o_ref[rows, :] += jnp.dot(act.astype(BF16), wd_ref[...], preferred_element_type=F32)

    @pl.when(f == pl.num_programs(1) - 1)
    def _():
        x_copy().wait()
        def finish(r, carry):
            rows = pl.ds(pl.multiple_of(r * HEAD_DIM, HEAD_DIM), HEAD_DIM)
            o_ref[rows, :] = _rms(xres_ref[rows, :] + o_ref[rows, :], fg_ref[...])
            return carry

        lax.fori_loop(0, tm // HEAD_DIM, finish, 0)


def _ffn(x, h, wg, wu, cw, cb, wd, fg, *, seq, tm, tf):
    m, d = x.shape
    ff = wg.shape[1]
    assert seq % tm == 0 and ff % tf == 0 and tm % BF16_ROWS == 0
    hb = tm // BF16_ROWS
    return pl.pallas_call(
        functools.partial(_ffn_kernel, tiles_per_seq=seq // tm, rs=tm),
        grid=(m // tm, ff // tf),
        in_specs=[pl.BlockSpec(memory_space=pl.ANY),
                  pl.BlockSpec((tm, d), lambda i, f: (i, 0)),
                  pl.BlockSpec((BF16_ROWS, d), lambda i, f: (jnp.maximum(i * hb - 1, 0), 0)),
                  pl.BlockSpec((d, tf), lambda i, f: (0, f)),
                  pl.BlockSpec((d, tf), lambda i, f: (0, f)),
                  pl.BlockSpec((FFN_CONV, tf), lambda i, f: (0, f)),
                  pl.BlockSpec((1, tf), lambda i, f: (0, f)),
                  pl.BlockSpec((tf, d), lambda i, f: (f, 0)),
                  pl.BlockSpec((1, d), lambda i, f: (0, 0))],
        out_specs=pl.BlockSpec((tm, d), lambda i, f: (i, 0)),
        out_shape=jax.ShapeDtypeStruct((m, d), F32),
        scratch_shapes=[pltpu.VMEM((tm + BF16_ROWS, d), BF16), pltpu.VMEM((tm, d), F32),
                        pltpu.SemaphoreType.DMA(())],
        compiler_params=pltpu.CompilerParams(
            dimension_semantics=("arbitrary", "arbitrary"), vmem_limit_bytes=VMEM_LIMIT),
        name="ffn",
    )(x, h, h, wg, wu, cw, cb.reshape(1, ff), wd, fg.reshape(1, d))


def _layer(x, mem, mix_norm_g, w_in, gdn_conv_w, a_log, dt_bias, gdn_norm_g, moba_norm_g, rel_bias,
           w_out, xattn_norm_g, mem_norm_g, w_xq, w_xkv, w_xo, ffn_norm_g, w_gate, w_up, ffn_conv_w,
           ffn_conv_b, w_down, final_g, *, last):
    b, s, d = x.shape
    m = b * s
    x2 = x.reshape(m, d)
    i1 = 4 * GDN_WIDTH
    i3 = i1 + 2 * N_GDN_HEADS
    w_a = jnp.pad(w_in[:, :i3], ((0, 0), (0, HEAD_DIM - 2 * N_GDN_HEADS))).astype(BF16)
    w_b = w_in[:, i3:].astype(BF16)
    tm = min(1024, m)
    wa_cols = i1 + HEAD_DIM
    pa = _norm_matmul(x2, mix_norm_g, w_a, tm=tm, tn=wa_cols // 3).reshape(b, s, wa_cols)
    pb = _norm_matmul(x2, mix_norm_g, w_b, tm=tm, tn=1024).reshape(b, s, 3 * MOBA_WIDTH)
    o_a = _gdn(pa, gdn_conv_w, a_log, dt_bias, gdn_norm_g)
    o_b = _moba(pb, rel_bias, moba_norm_g)
    x1 = _out_proj(x2, o_a.reshape(m, GDN_WIDTH), o_b.reshape(m, MOBA_WIDTH),
                   w_out[:GDN_WIDTH].astype(BF16), w_out[GDN_WIDTH:].astype(BF16), tm=tm, tn=1024)
    mlen = mem.shape[1]
    kv = _norm_matmul(mem.reshape(b * mlen, d), mem_norm_g, w_xkv.astype(BF16),
                      tm=min(512, b * mlen), tn=1024, out_dtype=BF16).reshape(b, mlen, -1)
    x2b, h2 = _xattn(x1.reshape(b, s, d), xattn_norm_g, w_xq.astype(BF16), kv, w_xo.astype(BF16),
                     ffn_norm_g, ts=min(512, s))
    assert last, "the final rmsnorm is fused into the last layer's ffn"
    y = _ffn(x2b.reshape(m, d), h2.reshape(m, d), w_gate.astype(BF16), w_up.astype(BF16), ffn_conv_w,
             ffn_conv_b, w_down.astype(BF16), final_g, seq=s, tm=min(1024, s), tf=512)
    return y.reshape(b, s, d)


def kernel(x, mem, mix_norm_g, w_in, gdn_conv_w, gdn_a_log, gdn_dt_bias, gdn_norm_g, moba_norm_g,
           rel_bias, w_out, xattn_norm_g, mem_norm_g, w_xq, w_xkv, w_xo, ffn_norm_g, w_gate, w_up,
           ffn_conv_w, ffn_conv_b, w_down, final_norm_g):
    depth = mix_norm_g.shape[0]
    assert depth == 1
    l = 0
    return _layer(x, mem, mix_norm_g[l], w_in[l], gdn_conv_w[l], gdn_a_log[l], gdn_dt_bias[l],
                  gdn_norm_g[l], moba_norm_g[l], rel_bias, w_out[l], xattn_norm_g[l], mem_norm_g[l],
                  w_xq[l], w_xkv[l], w_xo[l], ffn_norm_g[l], w_gate[l], w_up[l], ffn_conv_w[l],
                  ffn_conv_b[l], w_down[l], final_norm_g, last=True)
```

```python
import functools
import math

import jax
import jax.numpy as jnp
import numpy as np
from jax import lax
from jax.experimental import pallas as pl
from jax.experimental.pallas import tpu as pltpu

HEAD_DIM = 128
N_GDN_HEADS = 8
N_MOBA_HEADS = 8
GDN_WIDTH = N_GDN_HEADS * HEAD_DIM
MOBA_WIDTH = N_MOBA_HEADS * HEAD_DIM
GDN_CONV = 4
GDN_CHUNK = 256
MOBA_BLOCK = 256
MOBA_TOPK = 3
REL_BUCKETS = 32
REL_MAX_DIST = 128
N_XATTN_HEADS = 4
FFN_CONV = 3
EPS = 1e-6
NEG = -1e30
SUBLANES = 8
BF16_ROWS = 16
ROW_SUB = 256
VMEM_LIMIT = 58 * 1024 * 1024

F32 = jnp.float32
BF16 = jnp.bfloat16


def _bdot(a, b):
    return jnp.dot(a.astype(BF16), b.astype(BF16), preferred_element_type=F32)


def _bdot_nt(a, b):
    return lax.dot_general(a.astype(BF16), b.astype(BF16), (((1,), (1,)), ((), ())),
                           preferred_element_type=F32)


def _bdot_tn(a, b):
    return lax.dot_general(a.astype(BF16), b.astype(BF16), (((0,), (0,)), ((), ())),
                           preferred_element_type=F32)


def _fdot(a, b):
    return jnp.dot(a, b, preferred_element_type=F32, precision=lax.Precision.HIGHEST)


def _sigmoid(x):
    return 1.0 / (1.0 + jnp.exp(-x))


def _silu(x):
    return x * _sigmoid(x)


def _rms(x, g):
    return x * lax.rsqrt(jnp.mean(x * x, axis=-1, keepdims=True) + EPS) * g


def _norm_matmul_kernel(x_ref, g_ref, w_ref, o_ref, *, rs):
    for r in range(x_ref.shape[0] // rs):
        rows = slice(r * rs, (r + 1) * rs)
        hn = _rms(x_ref[rows, :], g_ref[...]).astype(BF16)
        o_ref[rows, :] = jnp.dot(hn, w_ref[...], preferred_element_type=F32).astype(o_ref.dtype)


def _norm_matmul(x, g, w, *, tm, tn, out_dtype=F32):
    m, k = x.shape
    n = w.shape[1]
    assert m % tm == 0 and n % tn == 0
    rs = math.gcd(tm, ROW_SUB)
    return pl.pallas_call(
        functools.partial(_norm_matmul_kernel, rs=rs),
        grid=(m // tm, n // tn),
        in_specs=[pl.BlockSpec((tm, k), lambda i, j: (i, 0)),
                  pl.BlockSpec((1, k), lambda i, j: (0, 0)),
                  pl.BlockSpec((k, tn), lambda i, j: (0, j))],
        out_specs=pl.BlockSpec((tm, tn), lambda i, j: (i, j)),
        out_shape=jax.ShapeDtypeStruct((m, n), out_dtype),
        compiler_params=pltpu.CompilerParams(
            dimension_semantics=("parallel", "arbitrary"), vmem_limit_bytes=VMEM_LIMIT),
        name="norm_matmul",
    )(x, g.reshape(1, k), w)


def _unit_lower_inverse(mats, row, col):
    c = mats[0].shape[0]
    eye = (row == col).astype(F32)
    blk = lambda n: (row // n) == (col // n)
    inner = blk(16)
    ds = [jnp.where(inner, a, 0.0) for a in mats]
    ts = [eye - d for d in ds]
    ps = [_bdot(d, d) for d in ds]
    for step in range(3):
        ts = [t + _bdot(t, p) for t, p in zip(ts, ps)]
        if step < 2:
            ps = [_bdot(p, p) for p in ps]
    n = 32
    while n <= c:
        outer = blk(n) if n < c else None
        keep = ~inner if outer is None else (outer & ~inner)
        xs = [_bdot(jnp.where(keep, a, 0.0), t) for a, t in zip(mats, ts)]
        ts = [t - _bdot(t, x) for t, x in zip(ts, xs)]
        inner = outer
        n *= 2
    return ts


def _gdn_kernel(alog_ref, dtb_ref, q_ref, k_ref, v_ref, z_ref, bat_ref, wq_ref, wk_ref, wv_ref,
                ng_ref, o_ref, qp_ref, kp_ref, vp_ref, u_ref, wqd_ref, qk_ref, kw_ref, bc_ref, gl_ref,
                sb_ref, *, group):
    h = pl.program_id(1)
    seq = q_ref.shape[1]
    C = GDN_CHUNK
    D = HEAD_DIM
    pad = SUBLANES
    n_chunks = seq // C

    for src, dst in ((q_ref, qp_ref), (k_ref, kp_ref), (v_ref, vp_ref)):
        dst[0:pad, :] = jnp.zeros((pad, D), F32)
        dst[pad:pad + seq, :] = src[0]

    row = lax.broadcasted_iota(jnp.int32, (C, C), 0)
    col = lax.broadcasted_iota(jnp.int32, (C, C), 1)
    tri_incl = row >= col
    tri_strict = row > col
    lane = lax.broadcasted_iota(jnp.int32, (SUBLANES, C), 1)
    neg_a = -jnp.exp(jnp.full((1, C), alog_ref[h], F32))
    dt_bias = dtb_ref[h]
    scale = D ** -0.5

    def conv_silu(xp_ref, w_ref, r0):
        win = xp_ref[pl.ds(r0, C + pad), :]
        w = w_ref[...]
        y = win[pad:pad + C] * w[3:4]
        for j in range(GDN_CONV - 1):
            s = GDN_CONV - 1 - j
            y = y + win[pad - s:pad - s + C] * w[j:j + 1]
        return _silu(y)

    def l2n(x):
        return x * lax.rsqrt(jnp.sum(x * x, axis=-1, keepdims=True) + EPS)

    def cumsum_lanes(x):
        x = jnp.broadcast_to(x, (SUBLANES, C))
        s = 1
        while s < C:
            x = x + jnp.where(lane >= s, pltpu.roll(x, s, axis=1), 0.0)
            s *= 2
        return x[0:1, :]

    reps = C // D

    def rows_to_cols(x):
        return jnp.concatenate(
            [jnp.broadcast_to(x[:, n * D:(n + 1) * D], (D, D)).T for n in range(reps)], axis=0)

    def prepare(grp, carry):
        cs = [grp * group + i for i in range(group)]
        r0s = [pl.multiple_of(c * C, C) for c in cs]
        pre, gbs, g_rows = [], [], []
        for c, r0 in zip(cs, r0s):
            q = l2n(conv_silu(qp_ref, wq_ref, r0)) * scale
            k = l2n(conv_silu(kp_ref, wk_ref, r0))
            v = conv_silu(vp_ref, wv_ref, r0)
            b_row = bat_ref[0, h, pl.ds(c, 1), :]
            xs = bat_ref[0, h + N_GDN_HEADS, pl.ds(c, 1), :] + dt_bias
            softplus = jnp.maximum(xs, 0.0) + jnp.log1p(jnp.exp(-jnp.abs(xs)))
            g_row = cumsum_lanes(neg_a * softplus)
            pre.append((q, k, v, rows_to_cols(_sigmoid(b_row)), None))
            g_rows.append(g_row)
            gbs.append(rows_to_cols(g_row))
        kk_qks = [_bdot_nt(jnp.concatenate([k * beta, q], axis=0), k) for q, k, v, beta, _ in pre]
        mats, mids = [], []
        for (q, k, v, beta, _), gb, g_row, kk_qk in zip(pre, gbs, g_rows, kk_qks):
            g_i = jnp.concatenate([gb] * reps, axis=1)
            g_j = jnp.broadcast_to(g_row, (C, C))
            decay = jnp.exp(jnp.where(tri_incl, g_i - g_j, NEG))
            mats.append(jnp.where(tri_strict, kk_qk[:C] * decay, 0.0))
            mids.append((kk_qk[C:] * decay, jnp.exp(gb), gb[C - 1:C, :]))
        ts = _unit_lower_inverse(mats, row, col)
        uws = [_bdot(t, jnp.concatenate([v * beta, k * beta * eg], axis=1))
               for t, (q, k, v, beta, _), (_, eg, _) in zip(ts, pre, mids)]
        transs = [_bdot_tn(k * jnp.exp(g_last - gb), uw)
                  for uw, (q, k, v, beta, _), gb, (_, _, g_last) in zip(uws, pre, gbs, mids)]
        for c, r0, uw, trans, (q, k, v, beta, _), (qk, eg, g_last) in zip(cs, r0s, uws, transs, pre, mids):
            u_ref[pl.ds(r0, C), :] = uw[:, :D]
            wqd_ref[c, 0:C, :] = uw[:, D:].astype(BF16)
            wqd_ref[c, C:2 * C, :] = (q * eg).astype(BF16)
            qk_ref[pl.ds(r0, C), :] = qk.astype(BF16)
            bc_ref[c] = trans[:, :D]
            kw_ref[c] = trans[:, D:].astype(BF16)
            gl_ref[c] = jnp.exp(g_last)
        return carry

    lax.fori_loop(0, n_chunks // group, prepare, 0)

    def chain(c, state):
        sb = state.astype(BF16)
        sb_ref[c] = sb
        return state * gl_ref[c] - jnp.dot(kw_ref[c], sb, preferred_element_type=F32) + bc_ref[c]

    lax.fori_loop(0, n_chunks, chain, jnp.zeros((D, D), F32))

    def outputs(grp, carry):
        cs = [grp * group + i for i in range(group)]
        r0s = [pl.multiple_of(c * C, C) for c in cs]
        ws_qs = [jnp.dot(wqd_ref[c], sb_ref[c], preferred_element_type=F32) for c in cs]
        v_news = [(u_ref[pl.ds(r0, C), :] - wq[:C]).astype(BF16) for r0, wq in zip(r0s, ws_qs)]
        os = [wq[C:] + jnp.dot(qk_ref[pl.ds(r0, C), :], vn, preferred_element_type=F32)
              for r0, wq, vn in zip(r0s, ws_qs, v_news)]
        for r0, o in zip(r0s, os):
            z = z_ref[0, pl.ds(r0, C), :]
            o_ref[0, pl.ds(r0, C), :] = (_rms(o, ng_ref[...]) * _silu(z)).astype(o_ref.dtype)
        return carry

    lax.fori_loop(0, n_chunks // group, outputs, 0)


def _gdn(pa, conv_w, a_log, dt_bias, norm_g, *, group=4):
    b, s, _ = pa.shape
    H = N_GDN_HEADS
    C = GDN_CHUNK
    assert s % C == 0
    n_chunks = s // C
    bat = pa[:, :, 4 * GDN_WIDTH:4 * GDN_WIDTH + 2 * H].transpose(0, 2, 1).reshape(b, 2 * H, n_chunks, C)
    head = lambda off: pl.BlockSpec((1, s, HEAD_DIM), lambda i, j: (i, 0, off + j))
    cw = lambda off: pl.BlockSpec((GDN_CONV, HEAD_DIM), lambda i, j: (0, off + j))
    smem = pl.BlockSpec(memory_space=pltpu.SMEM)
    return pl.pallas_call(
        functools.partial(_gdn_kernel, group=math.gcd(group, n_chunks)),
        grid=(b, H),
        in_specs=[smem, smem, head(0), head(H), head(2 * H), head(3 * H),
                  pl.BlockSpec((1, 2 * H, n_chunks, C), lambda i, j: (i, 0, 0, 0)),
                  cw(0), cw(H), cw(2 * H),
                  pl.BlockSpec((1, HEAD_DIM), lambda i, j: (0, 0))],
        out_specs=pl.BlockSpec((1, s, HEAD_DIM), lambda i, j: (i, 0, j)),
        out_shape=jax.ShapeDtypeStruct((b, s, GDN_WIDTH), BF16),
        scratch_shapes=[pltpu.VMEM((s + SUBLANES, HEAD_DIM), F32)] * 3 + [
            pltpu.VMEM((s, HEAD_DIM), F32),
            pltpu.VMEM((n_chunks, 2 * C, HEAD_DIM), BF16),
            pltpu.VMEM((s, C), BF16),
            pltpu.VMEM((n_chunks, HEAD_DIM, HEAD_DIM), BF16),
            pltpu.VMEM((n_chunks, HEAD_DIM, HEAD_DIM), F32),
            pltpu.VMEM((n_chunks, 1, HEAD_DIM), F32),
            pltpu.VMEM((n_chunks, HEAD_DIM, HEAD_DIM), BF16)],
        compiler_params=pltpu.CompilerParams(
            dimension_semantics=("parallel", "arbitrary"), vmem_limit_bytes=VMEM_LIMIT),
        name="gdn",
    )(a_log, dt_bias, pa, pa, pa, pa, bat, conv_w, conv_w, conv_w, norm_g.reshape(1, HEAD_DIM))


def _bucket_upper_bounds():
    n = np.arange(0, 4 * REL_MAX_DIST, dtype=np.int64)
    max_exact = REL_BUCKETS // 2
    nf = np.maximum(n, 1).astype(np.float32)
    large = max_exact + (np.log(nf / np.float32(max_exact)) / np.float32(math.log(REL_MAX_DIST / max_exact))
                         * np.float32(REL_BUCKETS - max_exact)).astype(np.int32)
    large = np.minimum(large, REL_BUCKETS - 1)
    bucket = np.where(n < max_exact, n, large)
    assert np.all(np.diff(bucket) >= 0) and bucket[-1] == REL_BUCKETS - 1
    return [int(np.searchsorted(bucket, b, side="right")) for b in range(REL_BUCKETS - 1)]


_BUCKET_UPPER = _bucket_upper_bounds()


def _moba_kernel(rb_ref, q_ref, k_ref, v_ref, ng_ref, o_ref, kb_ref, vt_ref, km_ref, bd_ref, bl_ref):
    h = pl.program_id(0)
    b = pl.program_id(1)
    seq = k_ref.shape[1]
    T = MOBA_BLOCK
    nb = seq // T
    scale = HEAD_DIM ** -0.5
    kk = lax.broadcasted_iota(jnp.int32, (T, T), 0)
    qq = lax.broadcasted_iota(jnp.int32, (T, T), 1)

    @pl.when(b == 0)
    def _():
        def bias_of(n):
            val = jnp.full((T, T), rb_ref[REL_BUCKETS - 1, h], F32)
            for bkt in range(REL_BUCKETS - 2, -1, -1):
                val = jnp.where(n < _BUCKET_UPPER[bkt], rb_ref[bkt, h], val)
            return val
        bd_ref[...] = jnp.where(qq >= kk, bias_of(qq - kk), NEG)
        bl_ref[...] = bias_of(qq - kk + T)

    kb_ref[...] = k_ref[0].astype(BF16)
    for n in range(nb):
        blk_rows = slice(n * T, (n + 1) * T)
        vt_ref[:, blk_rows] = v_ref[0, blk_rows, :].T.astype(BF16)
        km_ref[n:n + 1, :] = jnp.mean(k_ref[0, blk_rows, :], axis=0, keepdims=True)
    far_bias = rb_ref[REL_BUCKETS - 1, h]

    for i in range(nb):
        qf = q_ref[0, i * T:(i + 1) * T, :]
        keep = None
        if i > MOBA_TOPK:
            gate = lax.dot_general(km_ref[0:i, :], qf, (((1,), (1,)), ((), ())),
                                   preferred_element_type=F32, precision=lax.Precision.HIGHEST)
            blk = lax.broadcasted_iota(jnp.int32, (i, T), 0)
            rank = jnp.zeros((i, T), jnp.int32)
            for m in range(i):
                gm = gate[m:m + 1, :]
                rank = rank + jnp.where((gm > gate) | ((gm == gate) & (m < blk)), 1, 0)
            keep = jnp.where(rank < MOBA_TOPK, 0.0, NEG)
        width = (i + 1) * T
        s_all = _bdot_nt(kb_ref[0:width, :], qf * scale)
        parts = []
        for j in range(i + 1):
            sj = s_all[j * T:(j + 1) * T, :]
            if j == i:
                sj = sj + bd_ref[...]
            elif j == i - 1:
                sj = sj + bl_ref[...]
                if keep is not None:
                    sj = sj + keep[j:j + 1, :]
            else:
                sj = sj + (far_bias if keep is None else keep[j:j + 1, :] + far_bias)
            parts.append(sj)
        m_row = jnp.max(functools.reduce(jnp.maximum, parts), axis=0, keepdims=True)
        ps = [jnp.exp(sj - m_row) for sj in parts]
        l_row = jnp.sum(functools.reduce(jnp.add, ps), axis=0, keepdims=True)
        p_all = jnp.concatenate([p.astype(BF16) for p in ps], axis=0)
        o_t = jnp.dot(vt_ref[:, 0:width], p_all, preferred_element_type=F32) / l_row
        o_ref[0, i * T:(i + 1) * T, :] = _rms(o_t.T, ng_ref[...]).astype(o_ref.dtype)


def _moba(pb, rel_bias, norm_g):
    b, s, _ = pb.shape
    H = N_MOBA_HEADS
    T = MOBA_BLOCK
    assert s % T == 0
    nb = s // T
    head = lambda off: pl.BlockSpec((1, s, HEAD_DIM), lambda h, bb: (bb, 0, off + h))
    return pl.pallas_call(
        _moba_kernel,
        grid=(H, b),
        in_specs=[pl.BlockSpec(memory_space=pltpu.SMEM), head(0), head(H), head(2 * H),
                  pl.BlockSpec((1, HEAD_DIM), lambda h, bb: (0, 0))],
        out_specs=pl.BlockSpec((1, s, HEAD_DIM), lambda h, bb: (bb, 0, h)),
        out_shape=jax.ShapeDtypeStruct((b, s, MOBA_WIDTH), BF16),
        scratch_shapes=[pltpu.VMEM((s, HEAD_DIM), BF16), pltpu.VMEM((HEAD_DIM, s), BF16),
                        pltpu.VMEM((nb, HEAD_DIM), F32),
                        pltpu.VMEM((T, T), F32), pltpu.VMEM((T, T), F32)],
        compiler_params=pltpu.CompilerParams(
            dimension_semantics=("arbitrary", "arbitrary"), vmem_limit_bytes=VMEM_LIMIT),
        name="moba",
    )(rel_bias, pb, pb, pb, norm_g.reshape(1, HEAD_DIM))


def _out_proj_kernel(x_ref, oa_ref, ob_ref, wa_ref, wb_ref, o_ref, *, rs):
    for r in range(x_ref.shape[0] // rs):
        rows = slice(r * rs, (r + 1) * rs)
        o_ref[rows, :] = (x_ref[rows, :]
                          + jnp.dot(oa_ref[rows, :], wa_ref[...], preferred_element_type=F32)
                          + jnp.dot(ob_ref[rows, :], wb_ref[...], preferred_element_type=F32))


def _out_proj(x, oa, ob, wa, wb, *, tm, tn):
    m, d = x.shape
    ka, kb = oa.shape[1], ob.shape[1]
    return pl.pallas_call(
        functools.partial(_out_proj_kernel, rs=math.gcd(tm, ROW_SUB)),
        grid=(m // tm, d // tn),
        in_specs=[pl.BlockSpec((tm, tn), lambda i, j: (i, j)),
                  pl.BlockSpec((tm, ka), lambda i, j: (i, 0)),
                  pl.BlockSpec((tm, kb), lambda i, j: (i, 0)),
                  pl.BlockSpec((ka, tn), lambda i, j: (0, j)),
                  pl.BlockSpec((kb, tn), lambda i, j: (0, j))],
        out_specs=pl.BlockSpec((tm, tn), lambda i, j: (i, j)),
        out_shape=jax.ShapeDtypeStruct((m, d), F32),
        compiler_params=pltpu.CompilerParams(
            dimension_semantics=("parallel", "arbitrary"), vmem_limit_bytes=VMEM_LIMIT),
        name="out_proj",
    )(x, oa, ob, wa, wb)


def _xattn_kernel(x_ref, g_ref, wq_ref, kv_ref, wo_ref, ng_ref, o_ref, hn_ref):
    x = x_ref[0]
    hn = _rms(x, g_ref[...])
    q = _bdot(hn, wq_ref[...])
    kv = kv_ref[0]
    width = N_XATTN_HEADS * HEAD_DIM
    outs = []
    for hd in range(N_XATTN_HEADS):
        sl = slice(hd * HEAD_DIM, (hd + 1) * HEAD_DIM)
        kh = kv[:, sl]
        vh = kv[:, width + hd * HEAD_DIM: width + (hd + 1) * HEAD_DIM]
        s = _bdot_nt(q[:, sl], kh) * HEAD_DIM ** -0.5
        p = jnp.exp(s - jnp.max(s, axis=-1, keepdims=True))
        p = p / jnp.sum(p, axis=-1, keepdims=True)
        outs.append(_bdot(p, vh))
    o = jnp.concatenate(outs, axis=-1)
    y = x + _bdot(o, wo_ref[...])
    o_ref[0] = y
    hn_ref[0] = _rms(y, ng_ref[...]).astype(BF16)


def _xattn(x, g, wq, kv, wo, next_g, *, ts):
    b, s, d = x.shape
    mlen = kv.shape[1]
    width = wq.shape[1]
    tile = pl.BlockSpec((1, ts, d), lambda i, j: (i, j, 0))
    vec = pl.BlockSpec((1, d), lambda i, j: (0, 0))
    return pl.pallas_call(
        _xattn_kernel,
        grid=(b, s // ts),
        in_specs=[tile, vec,
                  pl.BlockSpec((d, width), lambda i, j: (0, 0)),
                  pl.BlockSpec((1, mlen, 2 * width), lambda i, j: (i, 0, 0)),
                  pl.BlockSpec((width, d), lambda i, j: (0, 0)),
                  vec],
        out_specs=[tile, tile],
        out_shape=[jax.ShapeDtypeStruct((b, s, d), F32), jax.ShapeDtypeStruct((b, s, d), BF16)],
        compiler_params=pltpu.CompilerParams(
            dimension_semantics=("parallel", "parallel"), vmem_limit_bytes=VMEM_LIMIT),
        name="xattn",
    )(x, g.reshape(1, d), wq, kv, wo, next_g.reshape(1, d))


def _ffn_kernel(x_hbm, h_ref, halo_ref, wg_ref, wu_ref, cw_ref, cb_ref, wd_ref, fg_ref, o_ref,
                hx_ref, xres_ref, sem, *, tiles_per_seq, rs):
    i = pl.program_id(0)
    f = pl.program_id(1)
    tm = h_ref.shape[0]
    pad = BF16_ROWS

    def x_copy():
        return pltpu.make_async_copy(x_hbm.at[pl.ds(i * tm, tm), :], xres_ref, sem)

    @pl.when(f == 0)
    def _():
        x_copy().start()
        halo = halo_ref[...]
        hx_ref[0:pad, :] = jnp.where(i % tiles_per_seq == 0, jnp.zeros_like(halo), halo)
        hx_ref[pad:pad + tm, :] = h_ref[...]
        o_ref[...] = jnp.zeros_like(o_ref)

    cw = cw_ref[...]
    keep = SUBLANES
    tail = None
    for r in range(tm // rs):
        rows = slice(r * rs, (r + 1) * rs)
        hrows = slice(pad + r * rs, pad + (r + 1) * rs)
        if r == 0:
            gp = jnp.dot(hx_ref[0:pad + rs, :], wg_ref[...], preferred_element_type=F32)[pad - keep:]
        else:
            gp = jnp.concatenate(
                [tail, jnp.dot(hx_ref[hrows, :], wg_ref[...], preferred_element_type=F32)], axis=0)
        tail = gp[rs:rs + keep]
        up = jnp.dot(hx_ref[hrows, :], wu_ref[...], preferred_element_type=F32)
        gate = gp[keep:keep + rs] * cw[2:3] + gp[keep - 1:keep - 1 + rs] * cw[1:2] \
            + gp[keep - 2:keep - 2 + rs] * cw[0:1] + cb_ref[...]
        act = _silu(gate) * up
        o_ref[rows, :] += jnp.dot(act.astype(BF16), wd_ref[...], preferred_element_type=F32)

    @pl.when(f == pl.num_programs(1) - 1)
    def _():
        x_copy().wait()
        def finish(r, carry):
            rows = pl.ds(pl.multiple_of(r * HEAD_DIM, HEAD_DIM), HEAD_DIM)
            o_ref[rows, :] = _rms(xres_ref[rows, :] + o_ref[rows, :], fg_ref[...])
            return carry

        lax.fori_loop(0, tm // HEAD_DIM, finish, 0)


def _ffn(x, h, wg, wu, cw, cb, wd, fg, *, seq, tm, tf):
    m, d = x.shape
    ff = wg.shape[1]
    assert seq % tm == 0 and ff % tf == 0 and tm % BF16_ROWS == 0
    hb = tm // BF16_ROWS
    return pl.pallas_call(
        functools.partial(_ffn_kernel, tiles_per_seq=seq // tm, rs=tm),
        grid=(m // tm, ff // tf),
        in_specs=[pl.BlockSpec(memory_space=pl.ANY),
                  pl.BlockSpec((tm, d), lambda i, f: (i, 0)),
                  pl.BlockSpec((BF16_ROWS, d), lambda i, f: (jnp.maximum(i * hb - 1, 0), 0)),
                  pl.BlockSpec((d, tf), lambda i, f: (0, f)),
                  pl.BlockSpec((d, tf), lambda i, f: (0, f)),
                  pl.BlockSpec((FFN_CONV, tf), lambda i, f: (0, f)),
                  pl.BlockSpec((1, tf), lambda i, f: (0, f)),
                  pl.BlockSpec((tf, d), lambda i, f: (f, 0)),
                  pl.BlockSpec((1, d), lambda i, f: (0, 0))],
        out_specs=pl.BlockSpec((tm, d), lambda i, f: (i, 0)),
        out_shape=jax.ShapeDtypeStruct((m, d), F32),
        scratch_shapes=[pltpu.VMEM((tm + BF16_ROWS, d), BF16), pltpu.VMEM((tm, d), F32),
                        pltpu.SemaphoreType.DMA(())],
        compiler_params=pltpu.CompilerParams(
            dimension_semantics=("arbitrary", "arbitrary"), vmem_limit_bytes=VMEM_LIMIT),
        name="ffn",
    )(x, h, h, wg, wu, cw, cb.reshape(1, ff), wd, fg.reshape(1, d))


def _layer(x, mem, mix_norm_g, w_in, gdn_conv_w, a_log, dt_bias, gdn_norm_g, moba_norm_g, rel_bias,
           w_out, xattn_norm_g, mem_norm_g, w_xq, w_xkv, w_xo, ffn_norm_g, w_gate, w_up, ffn_conv_w,
           ffn_conv_b, w_down, final_g, *, last):
    b, s, d = x.shape
    m = b * s
    x2 = x.reshape(m, d)
    i1 = 4 * GDN_WIDTH
    i3 = i1 + 2 * N_GDN_HEADS
    w_a = jnp.pad(w_in[:, :i3], ((0, 0), (0, HEAD_DIM - 2 * N_GDN_HEADS))).astype(BF16)
    w_b = w_in[:, i3:].astype(BF16)
    tm = min(1024, m)
    wa_cols = i1 + HEAD_DIM
    pa = _norm_matmul(x2, mix_norm_g, w_a, tm=tm, tn=wa_cols // 3).reshape(b, s, wa_cols)
    pb = _norm_matmul(x2, mix_norm_g, w_b, tm=tm, tn=1024).reshape(b, s, 3 * MOBA_WIDTH)
    o_a = _gdn(pa, gdn_conv_w, a_log, dt_bias, gdn_norm_g)
    o_b = _moba(pb, rel_bias, moba_norm_g)
    x1 = _out_proj(x2, o_a.reshape(m, GDN_WIDTH), o_b.reshape(m, MOBA_WIDTH),
                   w_out[:GDN_WIDTH].astype(BF16), w_out[GDN_WIDTH:].astype(BF16), tm=tm, tn=1024)
    mlen = mem.shape[1]
    kv = _norm_matmul(mem.reshape(b * mlen, d), mem_norm_g, w_xkv.astype(BF16),
                      tm=min(512, b * mlen), tn=1024, out_dtype=BF16).reshape(b, mlen, -1)
    x2b, h2 = _xattn(x1.reshape(b, s, d), xattn_norm_g, w_xq.astype(BF16), kv, w_xo.astype(BF16),
                     ffn_norm_g, ts=min(512, s))
    assert last, "the final rmsnorm is fused into the last layer's ffn"
    y = _ffn(x2b.reshape(m, d), h2.reshape(m, d), w_gate.astype(BF16), w_up.astype(BF16), ffn_conv_w,
             ffn_conv_b, w_down.astype(BF16), final_g, seq=s, tm=min(1024, s), tf=512)
    return y.reshape(b, s, d)


def kernel(x, mem, mix_norm_g, w_in, gdn_conv_w, gdn_a_log, gdn_dt_bias, gdn_norm_g, moba_norm_g,
           rel_bias, w_out, xattn_norm_g, mem_norm_g, w_xq, w_xkv, w_xo, ffn_norm_g, w_gate, w_up,
           ffn_conv_w, ffn_conv_b, w_down, final_norm_g):
    depth = mix_norm_g.shape[0]
    assert depth == 1
    l = 0
    return _layer(x, mem, mix_norm_g[l], w_in[l], gdn_conv_w[l], gdn_a_log[l], gdn_dt_bias[l],
                  gdn_norm_g[l], moba_norm_g[l], rel_bias, w_out[l], xattn_norm_g[l], mem_norm_g[l],
                  w_xq[l], w_xkv[l], w_xo[l], ffn_norm_g[l], w_gate[l], w_up[l], ffn_conv_w[l],
                  ffn_conv_b[l], w_down[l], final_norm_g, last=True)
```

```python
import functools
import math

import jax
import jax.numpy as jnp
import numpy as np
from jax import lax
from jax.experimental import pallas as pl
from jax.experimental.pallas import tpu as pltpu

HEAD_DIM = 128
N_GDN_HEADS = 8
N_MOBA_HEADS = 8
GDN_WIDTH = N_GDN_HEADS * HEAD_DIM
MOBA_WIDTH = N_MOBA_HEADS * HEAD_DIM
GDN_CONV = 4
GDN_CHUNK = 256
MOBA_BLOCK = 256
MOBA_TOPK = 3
MOBA_TILES_IN_FLIGHT = 4
REL_BUCKETS = 32
REL_MAX_DIST = 128
N_XATTN_HEADS = 4
FFN_CONV = 3
EPS = 1e-6
NEG = -1e30
LOG2E = math.log2(math.e)
SUBLANES = 8
BF16_ROWS = 16
ROW_SUB = 256
VMEM_LIMIT = 58 * 1024 * 1024

F32 = jnp.float32
BF16 = jnp.bfloat16


def _bdot(a, b):
    return jnp.dot(a.astype(BF16), b.astype(BF16), preferred_element_type=F32)


def _bdot_nt(a, b):
    return lax.dot_general(a.astype(BF16), b.astype(BF16), (((1,), (1,)), ((), ())),
                           preferred_element_type=F32)


def _bdot_tn(a, b):
    return lax.dot_general(a.astype(BF16), b.astype(BF16), (((0,), (0,)), ((), ())),
                           preferred_element_type=F32)


def _fdot(a, b):
    return jnp.dot(a, b, preferred_element_type=F32, precision=lax.Precision.HIGHEST)


def _sigmoid(x):
    return 1.0 / (1.0 + jnp.exp(-x))


def _silu(x):
    return x * _sigmoid(x)


def _rms(x, g):
    return x * lax.rsqrt(jnp.mean(x * x, axis=-1, keepdims=True) + EPS) * g


def _norm_matmul_kernel(x_ref, g_ref, w_ref, o_ref, *, rs):
    for r in range(x_ref.shape[0] // rs):
        rows = slice(r * rs, (r + 1) * rs)
        hn = _rms(x_ref[rows, :], g_ref[...]).astype(BF16)
        o_ref[rows, :] = jnp.dot(hn, w_ref[...], preferred_element_type=F32).astype(o_ref.dtype)


def _norm_matmul(x, g, w, *, tm, tn, out_dtype=F32):
    m, k = x.shape
    n = w.shape[1]
    assert m % tm == 0 and n % tn == 0
    rs = math.gcd(tm, ROW_SUB)
    return pl.pallas_call(
        functools.partial(_norm_matmul_kernel, rs=rs),
        grid=(m // tm, n // tn),
        in_specs=[pl.BlockSpec((tm, k), lambda i, j: (i, 0)),
                  pl.BlockSpec((1, k), lambda i, j: (0, 0)),
                  pl.BlockSpec((k, tn), lambda i, j: (0, j))],
        out_specs=pl.BlockSpec((tm, tn), lambda i, j: (i, j)),
        out_shape=jax.ShapeDtypeStruct((m, n), out_dtype),
        compiler_params=pltpu.CompilerParams(
            dimension_semantics=("parallel", "arbitrary"), vmem_limit_bytes=VMEM_LIMIT),
        name="norm_matmul",
    )(x, g.reshape(1, k), w)


def _unit_lower_inverse(mats, row, col):
    c = mats[0].shape[0]
    eye = (row == col).astype(F32)
    blk = lambda n: (row // n) == (col // n)
    inner = blk(16)
    ds = [jnp.where(inner, a, 0.0) for a in mats]
    ts = [eye - d for d in ds]
    ps = [_bdot(d, d) for d in ds]
    for step in range(3):
        ts = [t + _bdot(t, p) for t, p in zip(ts, ps)]
        if step < 2:
            ps = [_bdot(p, p) for p in ps]
    n = 32
    while n <= c:
        outer = blk(n) if n < c else None
        keep = ~inner if outer is None else (outer & ~inner)
        xs = [_bdot(jnp.where(keep, a, 0.0), t) for a, t in zip(mats, ts)]
        ts = [t - _bdot(t, x) for t, x in zip(ts, xs)]
        inner = outer
        n *= 2
    return ts


def _gdn_kernel(alog_ref, dtb_ref, q_ref, k_ref, v_ref, z_ref, bat_ref, wq_ref, wk_ref, wv_ref,
                ng_ref, o_ref, qp_ref, kp_ref, vp_ref, u_ref, wqd_ref, qk_ref, kw_ref, bc_ref, gl_ref,
                sb_ref, *, group):
    h = pl.program_id(1)
    seq = q_ref.shape[1]
    C = GDN_CHUNK
    D = HEAD_DIM
    pad = SUBLANES
    n_chunks = seq // C

    for src, dst in ((q_ref, qp_ref), (k_ref, kp_ref), (v_ref, vp_ref)):
        dst[0:pad, :] = jnp.zeros((pad, D), F32)
        dst[pad:pad + seq, :] = src[0]

    row = lax.broadcasted_iota(jnp.int32, (C, C), 0)
    col = lax.broadcasted_iota(jnp.int32, (C, C), 1)
    tri_incl = row >= col
    tri_strict = row > col
    lane = lax.broadcasted_iota(jnp.int32, (SUBLANES, C), 1)
    neg_a = -jnp.exp(jnp.full((1, C), alog_ref[h], F32))
    dt_bias = dtb_ref[h]
    scale = D ** -0.5

    def conv_silu(xp_ref, w_ref, r0):
        win = xp_ref[pl.ds(r0, C + pad), :]
        w = w_ref[...]
        y = win[pad:pad + C] * w[3:4]
        for j in range(GDN_CONV - 1):
            s = GDN_CONV - 1 - j
            y = y + win[pad - s:pad - s + C] * w[j:j + 1]
        return _silu(y)

    def l2n(x):
        return x * lax.rsqrt(jnp.sum(x * x, axis=-1, keepdims=True) + EPS)

    def cumsum_lanes(x):
        x = jnp.broadcast_to(x, (SUBLANES, C))
        s = 1
        while s < C:
            x = x + jnp.where(lane >= s, pltpu.roll(x, s, axis=1), 0.0)
            s *= 2
        return x[0:1, :]

    reps = C // D

    def rows_to_cols(x):
        return jnp.concatenate(
            [jnp.broadcast_to(x[:, n * D:(n + 1) * D], (D, D)).T for n in range(reps)], axis=0)

    def prepare(grp, carry):
        cs = [grp * group + i for i in range(group)]
        r0s = [pl.multiple_of(c * C, C) for c in cs]
        pre, gbs, g_rows = [], [], []
        for c, r0 in zip(cs, r0s):
            q = l2n(conv_silu(qp_ref, wq_ref, r0)) * scale
            k = l2n(conv_silu(kp_ref, wk_ref, r0))
            v = conv_silu(vp_ref, wv_ref, r0)
            b_row = bat_ref[0, h, pl.ds(c, 1), :]
            xs = bat_ref[0, h + N_GDN_HEADS, pl.ds(c, 1), :] + dt_bias
            softplus = jnp.maximum(xs, 0.0) + jnp.log1p(jnp.exp(-jnp.abs(xs)))
            g_row = cumsum_lanes(neg_a * softplus)
            pre.append((q, k, v, rows_to_cols(_sigmoid(b_row)), None))
            g_rows.append(g_row)
            gbs.append(rows_to_cols(g_row))
        kk_qks = [_bdot_nt(jnp.concatenate([k * beta, q], axis=0), k) for q, k, v, beta, _ in pre]
        mats, mids = [], []
        for (q, k, v, beta, _), gb, g_row, kk_qk in zip(pre, gbs, g_rows, kk_qks):
            g_i = jnp.concatenate([gb] * reps, axis=1)
            g_j = jnp.broadcast_to(g_row, (C, C))
            decay = jnp.exp(jnp.where(tri_incl, g_i - g_j, NEG))
            mats.append(jnp.where(tri_strict, kk_qk[:C] * decay, 0.0))
            mids.append((kk_qk[C:] * decay, jnp.exp(gb), gb[C - 1:C, :]))
        ts = _unit_lower_inverse(mats, row, col)
        uws = [_bdot(t, jnp.concatenate([v * beta, k * beta * eg], axis=1))
               for t, (q, k, v, beta, _), (_, eg, _) in zip(ts, pre, mids)]
        transs = [_bdot_tn(k * jnp.exp(g_last - gb), uw)
                  for uw, (q, k, v, beta, _), gb, (_, _, g_last) in zip(uws, pre, gbs, mids)]
        for c, r0, uw, trans, (q, k, v, beta, _), (qk, eg, g_last) in zip(cs, r0s, uws, transs, pre, mids):
            u_ref[pl.ds(r0, C), :] = uw[:, :D]
            wqd_ref[c, 0:C, :] = uw[:, D:].astype(BF16)
            wqd_ref[c, C:2 * C, :] = (q * eg).astype(BF16)
            qk_ref[pl.ds(r0, C), :] = qk.astype(BF16)
            bc_ref[c] = trans[:, :D]
            kw_ref[c] = trans[:, D:].astype(BF16)
            gl_ref[c] = jnp.exp(g_last)
        return carry

    lax.fori_loop(0, n_chunks // group, prepare, 0)

    def chain(c, state):
        sb = state.astype(BF16)
        sb_ref[c] = sb
        return state * gl_ref[c] - jnp.dot(kw_ref[c], sb, preferred_element_type=F32) + bc_ref[c]

    lax.fori_loop(0, n_chunks, chain, jnp.zeros((D, D), F32))

    def outputs(grp, carry):
        cs = [grp * group + i for i in range(group)]
        r0s = [pl.multiple_of(c * C, C) for c in cs]
        ws_qs = [jnp.dot(wqd_ref[c], sb_ref[c], preferred_element_type=F32) for c in cs]
        v_news = [(u_ref[pl.ds(r0, C), :] - wq[:C]).astype(BF16) for r0, wq in zip(r0s, ws_qs)]
        os = [wq[C:] + jnp.dot(qk_ref[pl.ds(r0, C), :], vn, preferred_element_type=F32)
              for r0, wq, vn in zip(r0s, ws_qs, v_news)]
        for r0, o in zip(r0s, os):
            z = z_ref[0, pl.ds(r0, C), :]
            o_ref[0, pl.ds(r0, C), :] = (_rms(o, ng_ref[...]) * _silu(z)).astype(o_ref.dtype)
        return carry

    lax.fori_loop(0, n_chunks // group, outputs, 0)


def _gdn(pa, conv_w, a_log, dt_bias, norm_g, *, group=4):
    b, s, _ = pa.shape
    H = N_GDN_HEADS
    C = GDN_CHUNK
    assert s % C == 0
    n_chunks = s // C
    bat = pa[:, :, 4 * GDN_WIDTH:4 * GDN_WIDTH + 2 * H].transpose(0, 2, 1).reshape(b, 2 * H, n_chunks, C)
    head = lambda off: pl.BlockSpec((1, s, HEAD_DIM), lambda i, j: (i, 0, off + j))
    cw = lambda off: pl.BlockSpec((GDN_CONV, HEAD_DIM), lambda i, j: (0, off + j))
    smem = pl.BlockSpec(memory_space=pltpu.SMEM)
    return pl.pallas_call(
        functools.partial(_gdn_kernel, group=math.gcd(group, n_chunks)),
        grid=(b, H),
        in_specs=[smem, smem, head(0), head(H), head(2 * H), head(3 * H),
                  pl.BlockSpec((1, 2 * H, n_chunks, C), lambda i, j: (i, 0, 0, 0)),
                  cw(0), cw(H), cw(2 * H),
                  pl.BlockSpec((1, HEAD_DIM), lambda i, j: (0, 0))],
        out_specs=pl.BlockSpec((1, s, HEAD_DIM), lambda i, j: (i, 0, j)),
        out_shape=jax.ShapeDtypeStruct((b, s, GDN_WIDTH), BF16),
        scratch_shapes=[pltpu.VMEM((s + SUBLANES, HEAD_DIM), F32)] * 3 + [
            pltpu.VMEM((s, HEAD_DIM), F32),
            pltpu.VMEM((n_chunks, 2 * C, HEAD_DIM), BF16),
            pltpu.VMEM((s, C), BF16),
            pltpu.VMEM((n_chunks, HEAD_DIM, HEAD_DIM), BF16),
            pltpu.VMEM((n_chunks, HEAD_DIM, HEAD_DIM), F32),
            pltpu.VMEM((n_chunks, 1, HEAD_DIM), F32),
            pltpu.VMEM((n_chunks, HEAD_DIM, HEAD_DIM), BF16)],
        compiler_params=pltpu.CompilerParams(
            dimension_semantics=("parallel", "arbitrary"), vmem_limit_bytes=VMEM_LIMIT),
        name="gdn",
    )(a_log, dt_bias, pa, pa, pa, pa, bat, conv_w, conv_w, conv_w, norm_g.reshape(1, HEAD_DIM))


def _bucket_upper_bounds():
    n = np.arange(0, 4 * REL_MAX_DIST, dtype=np.int64)
    max_exact = REL_BUCKETS // 2
    nf = np.maximum(n, 1).astype(np.float32)
    large = max_exact + (np.log(nf / np.float32(max_exact)) / np.float32(math.log(REL_MAX_DIST / max_exact))
                         * np.float32(REL_BUCKETS - max_exact)).astype(np.int32)
    large = np.minimum(large, REL_BUCKETS - 1)
    bucket = np.where(n < max_exact, n, large)
    assert np.all(np.diff(bucket) >= 0) and bucket[-1] == REL_BUCKETS - 1
    return [int(np.searchsorted(bucket, b, side="right")) for b in range(REL_BUCKETS - 1)]


_BUCKET_UPPER = _bucket_upper_bounds()


def _moba_kernel(rb_ref, q_ref, k_ref, v_ref, ng_ref, o_ref, kb_ref, vt_ref, km_ref, bd_ref, bl_ref):
    h = pl.program_id(0)
    b = pl.program_id(1)
    seq = k_ref.shape[1]
    T = MOBA_BLOCK
    nb = seq // T
    scale = HEAD_DIM ** -0.5 * LOG2E
    kk = lax.broadcasted_iota(jnp.int32, (T, T), 0)
    qq = lax.broadcasted_iota(jnp.int32, (T, T), 1)

    @pl.when(b == 0)
    def _():
        def bias_of(n):
            val = jnp.full((T, T), rb_ref[REL_BUCKETS - 1, h], F32)
            for bkt in range(REL_BUCKETS - 2, -1, -1):
                val = jnp.where(n < _BUCKET_UPPER[bkt], rb_ref[bkt, h], val)
            return val
        bd_ref[...] = jnp.where(qq >= kk, bias_of(qq - kk) * LOG2E, NEG)
        bl_ref[...] = bias_of(qq - kk + T) * LOG2E

    kb_ref[...] = k_ref[0].astype(BF16)
    for n in range(nb):
        blk_rows = slice(n * T, (n + 1) * T)
        vt_ref[:, blk_rows] = v_ref[0, blk_rows, :].T.astype(BF16)
        km_ref[n:n + 1, :] = jnp.mean(k_ref[0, blk_rows, :], axis=0, keepdims=True)
    far_bias = rb_ref[REL_BUCKETS - 1, h] * LOG2E

    def additive_mask(i, qf):
        if i <= MOBA_TOPK:
            return None
        gate = lax.dot_general(km_ref[0:i, :], qf, (((1,), (1,)), ((), ())),
                               preferred_element_type=F32, precision=lax.Precision.HIGHEST)
        blk = lax.broadcasted_iota(jnp.int32, (i, T), 0)
        rank = jnp.zeros((i, T), jnp.int32)
        for m in range(i):
            gm = gate[m:m + 1, :]
            rank = rank + jnp.where((gm > gate) | ((gm == gate) & (m < blk)), 1, 0)
        return jnp.where(rank < MOBA_TOPK, 0.0, NEG)

    def biased(i, s_all, keep):
        parts = []
        for j in range(i + 1):
            sj = s_all[j * T:(j + 1) * T, :]
            if j == i:
                sj = sj + bd_ref[...]
            elif j == i - 1:
                sj = sj + bl_ref[...]
                if keep is not None:
                    sj = sj + keep[j:j + 1, :]
            else:
                sj = sj + (far_bias if keep is None else keep[j:j + 1, :] + far_bias)
            parts.append(sj)
        return parts

    order = [t for pair in zip(range(nb // 2), range(nb - 1, nb // 2 - 1, -1)) for t in pair]
    if nb % 2:
        order.append(nb // 2)
    for g0 in range(0, nb, MOBA_TILES_IN_FLIGHT):
        tiles = order[g0:g0 + MOBA_TILES_IN_FLIGHT]
        qfs = [q_ref[0, i * T:(i + 1) * T, :] for i in tiles]
        keeps = [additive_mask(i, qf) for i, qf in zip(tiles, qfs)]
        s_alls = [_bdot_nt(kb_ref[0:(i + 1) * T, :], qf * scale) for i, qf in zip(tiles, qfs)]
        partss = [biased(i, s_all, keep) for i, s_all, keep in zip(tiles, s_alls, keeps)]
        m_rows = [jnp.max(functools.reduce(jnp.maximum, parts), axis=0, keepdims=True) for parts in partss]
        pss = [[jnp.exp2(sj - m_row) for sj in parts] for parts, m_row in zip(partss, m_rows)]
        l_rows = [jnp.sum(functools.reduce(jnp.add, ps), axis=0, keepdims=True) for ps in pss]
        p_alls = [jnp.concatenate([p.astype(BF16) for p in ps], axis=0) for ps in pss]
        o_ts = [jnp.dot(vt_ref[:, 0:(i + 1) * T], p_all, preferred_element_type=F32) / l_row
                for i, p_all, l_row in zip(tiles, p_alls, l_rows)]
        for i, o_t in zip(tiles, o_ts):
            o_ref[0, i * T:(i + 1) * T, :] = _rms(o_t.T, ng_ref[...]).astype(o_ref.dtype)


def _moba(pb, rel_bias, norm_g):
    b, s, _ = pb.shape
    H = N_MOBA_HEADS
    T = MOBA_BLOCK
    assert s % T == 0
    nb = s // T
    head = lambda off: pl.BlockSpec((1, s, HEAD_DIM), lambda h, bb: (bb, 0, off + h))
    return pl.pallas_call(
        _moba_kernel,
        grid=(H, b),
        in_specs=[pl.BlockSpec(memory_space=pltpu.SMEM), head(0), head(H), head(2 * H),
                  pl.BlockSpec((1, HEAD_DIM), lambda h, bb: (0, 0))],
        out_specs=pl.BlockSpec((1, s, HEAD_DIM), lambda h, bb: (bb, 0, h)),
        out_shape=jax.ShapeDtypeStruct((b, s, MOBA_WIDTH), BF16),
        scratch_shapes=[pltpu.VMEM((s, HEAD_DIM), BF16), pltpu.VMEM((HEAD_DIM, s), BF16),
                        pltpu.VMEM((nb, HEAD_DIM), F32),
                        pltpu.VMEM((T, T), F32), pltpu.VMEM((T, T), F32)],
        compiler_params=pltpu.CompilerParams(
            dimension_semantics=("arbitrary", "arbitrary"), vmem_limit_bytes=VMEM_LIMIT),
        name="moba",
    )(rel_bias, pb, pb, pb, norm_g.reshape(1, HEAD_DIM))


def _out_proj_kernel(x_ref, oa_ref, ob_ref, wa_ref, wb_ref, o_ref, *, rs):
    for r in range(x_ref.shape[0] // rs):
        rows = slice(r * rs, (r + 1) * rs)
        o_ref[rows, :] = (x_ref[rows, :]
                          + jnp.dot(oa_ref[rows, :], wa_ref[...], preferred_element_type=F32)
                          + jnp.dot(ob_ref[rows, :], wb_ref[...], preferred_element_type=F32))


def _out_proj(x, oa, ob, wa, wb, *, tm, tn):
    m, d = x.shape
    ka, kb = oa.shape[1], ob.shape[1]
    return pl.pallas_call(
        functools.partial(_out_proj_kernel, rs=math.gcd(tm, ROW_SUB)),
        grid=(m // tm, d // tn),
        in_specs=[pl.BlockSpec((tm, tn), lambda i, j: (i, j)),
                  pl.BlockSpec((tm, ka), lambda i, j: (i, 0)),
                  pl.BlockSpec((tm, kb), lambda i, j: (i, 0)),
                  pl.BlockSpec((ka, tn), lambda i, j: (0, j)),
                  pl.BlockSpec((kb, tn), lambda i, j: (0, j))],
        out_specs=pl.BlockSpec((tm, tn), lambda i, j: (i, j)),
        out_shape=jax.ShapeDtypeStruct((m, d), F32),
        compiler_params=pltpu.CompilerParams(
            dimension_semantics=("parallel", "arbitrary"), vmem_limit_bytes=VMEM_LIMIT),
        name="out_proj",
    )(x, oa, ob, wa, wb)


def _xattn_kernel(x_ref, g_ref, wq_ref, kv_ref, wo_ref, ng_ref, o_ref, hn_ref):
    x = x_ref[0]
    hn = _rms(x, g_ref[...])
    q = _bdot(hn, wq_ref[...])
    kv = kv_ref[0]
    width = N_XATTN_HEADS * HEAD_DIM
    outs = []
    for hd in range(N_XATTN_HEADS):
        sl = slice(hd * HEAD_DIM, (hd + 1) * HEAD_DIM)
        kh = kv[:, sl]
        vh = kv[:, width + hd * HEAD_DIM: width + (hd + 1) * HEAD_DIM]
        s = _bdot_nt(q[:, sl], kh) * HEAD_DIM ** -0.5
        p = jnp.exp(s - jnp.max(s, axis=-1, keepdims=True))
        p = p / jnp.sum(p, axis=-1, keepdims=True)
        outs.append(_bdot(p, vh))
    o = jnp.concatenate(outs, axis=-1)
    y = x + _bdot(o, wo_ref[...])
    o_ref[0] = y
    hn_ref[0] = _rms(y, ng_ref[...]).astype(BF16)


def _xattn(x, g, wq, kv, wo, next_g, *, ts):
    b, s, d = x.shape
    mlen = kv.shape[1]
    width = wq.shape[1]
    tile = pl.BlockSpec((1, ts, d), lambda i, j: (i, j, 0))
    vec = pl.BlockSpec((1, d), lambda i, j: (0, 0))
    return pl.pallas_call(
        _xattn_kernel,
        grid=(b, s // ts),
        in_specs=[tile, vec,
                  pl.BlockSpec((d, width), lambda i, j: (0, 0)),
                  pl.BlockSpec((1, mlen, 2 * width), lambda i, j: (i, 0, 0)),
                  pl.BlockSpec((width, d), lambda i, j: (0, 0)),
                  vec],
        out_specs=[tile, tile],
        out_shape=[jax.ShapeDtypeStruct((b, s, d), F32), jax.ShapeDtypeStruct((b, s, d), BF16)],
        compiler_params=pltpu.CompilerParams(
            dimension_semantics=("parallel", "parallel"), vmem_limit_bytes=VMEM_LIMIT),
        name="xattn",
    )(x, g.reshape(1, d), wq, kv, wo, next_g.reshape(1, d))


def _ffn_kernel(x_hbm, h_ref, halo_ref, wg_ref, wu_ref, cw_ref, cb_ref, wd_ref, fg_ref, o_ref,
                hx_ref, xres_ref, sem, *, tiles_per_seq, rs):
    i = pl.program_id(0)
    f = pl.program_id(1)
    tm = h_ref.shape[0]
    pad = BF16_ROWS

    def x_copy():
        return pltpu.make_async_copy(x_hbm.at[pl.ds(i * tm, tm), :], xres_ref, sem)

    @pl.when(f == 0)
    def _():
        x_copy().start()
        halo = halo_ref[...]
        hx_ref[0:pad, :] = jnp.where(i % tiles_per_seq == 0, jnp.zeros_like(halo), halo)
        hx_ref[pad:pad + tm, :] = h_ref[...]
        o_ref[...] = jnp.zeros_like(o_ref)

    cw = cw_ref[...]
    keep = SUBLANES
    tail = None
    for r in range(tm // rs):
        rows = slice(r * rs, (r + 1) * rs)
        hrows = slice(pad + r * rs, pad + (r + 1) * rs)
        if r == 0:
            gp = jnp.dot(hx_ref[0:pad + rs, :], wg_ref[...], preferred_element_type=F32)[pad - keep:]
        else:
            gp = jnp.concatenate(
                [tail, jnp.dot(hx_ref[hrows, :], wg_ref[...], preferred_element_type=F32)], axis=0)
        tail = gp[rs:rs + keep]
        up = jnp.dot(hx_ref[hrows, :], wu_ref[...], preferred_element_type=F32)
        gate = gp[keep:keep + rs] * cw[2:3] + gp[keep - 1:keep - 1 + rs] * cw[1:2] \
            + gp[keep - 2:keep - 2 + rs] * cw[0:1] + cb_ref[...]
        act = _silu(gate) * up
        o_ref[rows, :] += jnp.dot(act.astype(BF16), wd_ref[...], preferred_element_type=F32)

    @pl.when(f == pl.num_programs(1) - 1)
    def _():
        x_copy().wait()
        def finish(r, carry):
            rows = pl.ds(pl.multiple_of(r * HEAD_DIM, HEAD_DIM), HEAD_DIM)
            o_ref[rows, :] = _rms(xres_ref[rows, :] + o_ref[rows, :], fg_ref[...])
            return carry

        lax.fori_loop(0, tm // HEAD_DIM, finish, 0)


def _ffn(x, h, wg, wu, cw, cb, wd, fg, *, seq, tm, tf):
    m, d = x.shape
    ff = wg.shape[1]
    assert seq % tm == 0 and ff % tf == 0 and tm % BF16_ROWS == 0
    hb = tm // BF16_ROWS
    return pl.pallas_call(
        functools.partial(_ffn_kernel, tiles_per_seq=seq // tm, rs=tm),
        grid=(m // tm, ff // tf),
        in_specs=[pl.BlockSpec(memory_space=pl.ANY),
                  pl.BlockSpec((tm, d), lambda i, f: (i, 0)),
                  pl.BlockSpec((BF16_ROWS, d), lambda i, f: (jnp.maximum(i * hb - 1, 0), 0)),
                  pl.BlockSpec((d, tf), lambda i, f: (0, f)),
                  pl.BlockSpec((d, tf), lambda i, f: (0, f)),
                  pl.BlockSpec((FFN_CONV, tf), lambda i, f: (0, f)),
                  pl.BlockSpec((1, tf), lambda i, f: (0, f)),
                  pl.BlockSpec((tf, d), lambda i, f: (f, 0)),
                  pl.BlockSpec((1, d), lambda i, f: (0, 0))],
        out_specs=pl.BlockSpec((tm, d), lambda i, f: (i, 0)),
        out_shape=jax.ShapeDtypeStruct((m, d), F32),
        scratch_shapes=[pltpu.VMEM((tm + BF16_ROWS, d), BF16), pltpu.VMEM((tm, d), F32),
                        pltpu.SemaphoreType.DMA(())],
        compiler_params=pltpu.CompilerParams(
            dimension_semantics=("arbitrary", "arbitrary"), vmem_limit_bytes=VMEM_LIMIT),
        name="ffn",
    )(x, h, h, wg, wu, cw, cb.reshape(1, ff), wd, fg.reshape(1, d))


def _layer(x, mem, mix_norm_g, w_in, gdn_conv_w, a_log, dt_bias, gdn_norm_g, moba_norm_g, rel_bias,
           w_out, xattn_norm_g, mem_norm_g, w_xq, w_xkv, w_xo, ffn_norm_g, w_gate, w_up, ffn_conv_w,
           ffn_conv_b, w_down, final_g, *, last):
    b, s, d = x.shape
    m = b * s
    x2 = x.reshape(m, d)
    i1 = 4 * GDN_WIDTH
    i3 = i1 + 2 * N_GDN_HEADS
    w_a = jnp.pad(w_in[:, :i3], ((0, 0), (0, HEAD_DIM - 2 * N_GDN_HEADS))).astype(BF16)
    w_b = w_in[:, i3:].astype(BF16)
    tm = min(1024, m)
    wa_cols = i1 + HEAD_DIM
    pa = _norm_matmul(x2, mix_norm_g, w_a, tm=tm, tn=wa_cols // 3).reshape(b, s, wa_cols)
    pb = _norm_matmul(x2, mix_norm_g, w_b, tm=tm, tn=1024).reshape(b, s, 3 * MOBA_WIDTH)
    o_a = _gdn(pa, gdn_conv_w, a_log, dt_bias, gdn_norm_g)
    o_b = _moba(pb, rel_bias, moba_norm_g)
    x1 = _out_proj(x2, o_a.reshape(m, GDN_WIDTH), o_b.reshape(m, MOBA_WIDTH),
                   w_out[:GDN_WIDTH].astype(BF16), w_out[GDN_WIDTH:].astype(BF16), tm=tm, tn=1024)
    mlen = mem.shape[1]
    kv = _norm_matmul(mem.reshape(b * mlen, d), mem_norm_g, w_xkv.astype(BF16),
                      tm=min(512, b * mlen), tn=1024, out_dtype=BF16).reshape(b, mlen, -1)
    x2b, h2 = _xattn(x1.reshape(b, s, d), xattn_norm_g, w_xq.astype(BF16), kv, w_xo.astype(BF16),
                     ffn_norm_g, ts=min(512, s))
    assert last, "the final rmsnorm is fused into the last layer's ffn"
    y = _ffn(x2b.reshape(m, d), h2.reshape(m, d), w_gate.astype(BF16), w_up.astype(BF16), ffn_conv_w,
             ffn_conv_b, w_down.astype(BF16), final_g, seq=s, tm=min(1024, s), tf=512)
    return y.reshape(b, s, d)


def kernel(x, mem, mix_norm_g, w_in, gdn_conv_w, gdn_a_log, gdn_dt_bias, gdn_norm_g, moba_norm_g,
           rel_bias, w_out, xattn_norm_g, mem_norm_g, w_xq, w_xkv, w_xo, ffn_norm_g, w_gate, w_up,
           ffn_conv_w, ffn_conv_b, w_down, final_norm_g):
    depth = mix_norm_g.shape[0]
    assert depth == 1
    l = 0
    return _layer(x, mem, mix_norm_g[l], w_in[l], gdn_conv_w[l], gdn_a_log[l], gdn_dt_bias[l],
                  gdn_norm_g[l], moba_norm_g[l], rel_bias, w_out[l], xattn_norm_g[l], mem_norm_g[l],
                  w_xq[l], w_xkv[l], w_xo[l], ffn_norm_g[l], w_gate[l], w_up[l], ffn_conv_w[l],
                  ffn_conv_b[l], w_down[l], final_norm_g, last=True)
```

```python
import functools
import math

import jax
import jax.numpy as jnp
import numpy as np
from jax import lax
from jax.experimental import pallas as pl
from jax.experimental.pallas import tpu as pltpu

HEAD_DIM = 128
N_GDN_HEADS = 8
N_MOBA_HEADS = 8
GDN_WIDTH = N_GDN_HEADS * HEAD_DIM
MOBA_WIDTH = N_MOBA_HEADS * HEAD_DIM
GDN_CONV = 4
GDN_CHUNK = 256
MOBA_BLOCK = 256
MOBA_TOPK = 3
MOBA_TILES_IN_FLIGHT = 4
REL_BUCKETS = 32
REL_MAX_DIST = 128
N_XATTN_HEADS = 4
FFN_CONV = 3
EPS = 1e-6
NEG = -1e30
LOG2E = math.log2(math.e)
SUBLANES = 8
BF16_ROWS = 16
ROW_SUB = 256
VMEM_LIMIT = 58 * 1024 * 1024

F32 = jnp.float32
BF16 = jnp.bfloat16


def _bdot(a, b):
    return jnp.dot(a.astype(BF16), b.astype(BF16), preferred_element_type=F32)


def _bdot_nt(a, b):
    return lax.dot_general(a.astype(BF16), b.astype(BF16), (((1,), (1,)), ((), ())),
                           preferred_element_type=F32)


def _bdot_tn(a, b):
    return lax.dot_general(a.astype(BF16), b.astype(BF16), (((0,), (0,)), ((), ())),
                           preferred_element_type=F32)


def _fdot(a, b):
    return jnp.dot(a, b, preferred_element_type=F32, precision=lax.Precision.HIGHEST)


def _sigmoid(x):
    return 1.0 / (1.0 + jnp.exp(-x))


def _silu(x):
    return x * _sigmoid(x)


def _rms(x, g):
    return x * lax.rsqrt(jnp.mean(x * x, axis=-1, keepdims=True) + EPS) * g


def _norm_matmul_kernel(x_ref, g_ref, w_ref, o_ref, *, rs):
    for r in range(x_ref.shape[0] // rs):
        rows = slice(r * rs, (r + 1) * rs)
        hn = _rms(x_ref[rows, :], g_ref[...]).astype(BF16)
        o_ref[rows, :] = jnp.dot(hn, w_ref[...], preferred_element_type=F32).astype(o_ref.dtype)


def _norm_matmul(x, g, w, *, tm, tn, n=None, out_dtype=F32):
    m, k = x.shape
    n = w.shape[1] if n is None else n
    assert m % tm == 0 and n % tn == 0 and n <= w.shape[1]
    rs = math.gcd(tm, ROW_SUB)
    return pl.pallas_call(
        functools.partial(_norm_matmul_kernel, rs=rs),
        grid=(m // tm, n // tn),
        in_specs=[pl.BlockSpec((tm, k), lambda i, j: (i, 0)),
                  pl.BlockSpec((1, k), lambda i, j: (0, 0)),
                  pl.BlockSpec((k, tn), lambda i, j: (0, j))],
        out_specs=pl.BlockSpec((tm, tn), lambda i, j: (i, j)),
        out_shape=jax.ShapeDtypeStruct((m, n), out_dtype),
        compiler_params=pltpu.CompilerParams(
            dimension_semantics=("parallel", "arbitrary"), vmem_limit_bytes=VMEM_LIMIT),
        name="norm_matmul",
    )(x, g.reshape(1, k), w)


def _unit_lower_inverse(mats, row, col):
    c = mats[0].shape[0]
    eye = (row == col).astype(F32)
    blk = lambda n: (row // n) == (col // n)
    inner = blk(16)
    ds = [jnp.where(inner, a, 0.0) for a in mats]
    ts = [eye - d for d in ds]
    ps = [_bdot(d, d) for d in ds]
    for step in range(3):
        ts = [t + _bdot(t, p) for t, p in zip(ts, ps)]
        if step < 2:
            ps = [_bdot(p, p) for p in ps]
    n = 32
    while n <= c:
        outer = blk(n) if n < c else None
        keep = ~inner if outer is None else (outer & ~inner)
        xs = [_bdot(jnp.where(keep, a, 0.0), t) for a, t in zip(mats, ts)]
        ts = [t - _bdot(t, x) for t, x in zip(ts, xs)]
        inner = outer
        n *= 2
    return ts


def _gdn_kernel(alog_ref, dtb_ref, q_ref, k_ref, v_ref, z_ref, bat_ref, wq_ref, wk_ref, wv_ref,
                ng_ref, o_ref, qp_ref, kp_ref, vp_ref, u_ref, wqd_ref, qk_ref, kw_ref, bc_ref, gl_ref,
                sb_ref, *, group):
    h = pl.program_id(1)
    seq = q_ref.shape[1]
    C = GDN_CHUNK
    D = HEAD_DIM
    pad = SUBLANES
    n_chunks = seq // C

    for src, dst in ((q_ref, qp_ref), (k_ref, kp_ref), (v_ref, vp_ref)):
        dst[0:pad, :] = jnp.zeros((pad, D), F32)
        dst[pad:pad + seq, :] = src[0]

    row = lax.broadcasted_iota(jnp.int32, (C, C), 0)
    col = lax.broadcasted_iota(jnp.int32, (C, C), 1)
    tri_incl = row >= col
    tri_strict = row > col
    lane = lax.broadcasted_iota(jnp.int32, (SUBLANES, C), 1)
    neg_a = -jnp.exp(jnp.full((1, C), alog_ref[h], F32))
    dt_bias = dtb_ref[h]
    scale = D ** -0.5

    def conv_silu(xp_ref, w_ref, r0):
        win = xp_ref[pl.ds(r0, C + pad), :]
        w = w_ref[...]
        y = win[pad:pad + C] * w[3:4]
        for j in range(GDN_CONV - 1):
            s = GDN_CONV - 1 - j
            y = y + win[pad - s:pad - s + C] * w[j:j + 1]
        return _silu(y)

    def l2n(x):
        return x * lax.rsqrt(jnp.sum(x * x, axis=-1, keepdims=True) + EPS)

    def cumsum_lanes(x):
        x = jnp.broadcast_to(x, (SUBLANES, C))
        s = 1
        while s < C:
            x = x + jnp.where(lane >= s, pltpu.roll(x, s, axis=1), 0.0)
            s *= 2
        return x[0:1, :]

    reps = C // D

    def rows_to_cols(x):
        return jnp.concatenate(
            [jnp.broadcast_to(x[:, n * D:(n + 1) * D], (D, D)).T for n in range(reps)], axis=0)

    def prepare(grp, carry):
        cs = [grp * group + i for i in range(group)]
        r0s = [pl.multiple_of(c * C, C) for c in cs]
        pre, gbs, g_rows = [], [], []
        for c, r0 in zip(cs, r0s):
            q = l2n(conv_silu(qp_ref, wq_ref, r0)) * scale
            k = l2n(conv_silu(kp_ref, wk_ref, r0))
            v = conv_silu(vp_ref, wv_ref, r0)
            b_row = bat_ref[0, h, pl.ds(c, 1), :]
            xs = bat_ref[0, h + N_GDN_HEADS, pl.ds(c, 1), :] + dt_bias
            softplus = jnp.maximum(xs, 0.0) + jnp.log1p(jnp.exp(-jnp.abs(xs)))
            g_row = cumsum_lanes(neg_a * softplus)
            pre.append((q, k, v, rows_to_cols(_sigmoid(b_row)), None))
            g_rows.append(g_row)
            gbs.append(rows_to_cols(g_row))
        kk_qks = [_bdot_nt(jnp.concatenate([k * beta, q], axis=0), k) for q, k, v, beta, _ in pre]
        mats, mids = [], []
        for (q, k, v, beta, _), gb, g_row, kk_qk in zip(pre, gbs, g_rows, kk_qks):
            g_i = jnp.concatenate([gb] * reps, axis=1)
            g_j = jnp.broadcast_to(g_row, (C, C))
            decay = jnp.exp(jnp.where(tri_incl, g_i - g_j, NEG))
            mats.append(jnp.where(tri_strict, kk_qk[:C] * decay, 0.0))
            mids.append((kk_qk[C:] * decay, jnp.exp(gb), gb[C - 1:C, :]))
        ts = _unit_lower_inverse(mats, row, col)
        uws = [_bdot(t, jnp.concatenate([v * beta, k * beta * eg], axis=1))
               for t, (q, k, v, beta, _), (_, eg, _) in zip(ts, pre, mids)]
        transs = [_bdot_tn(k * jnp.exp(g_last - gb), uw)
                  for uw, (q, k, v, beta, _), gb, (_, _, g_last) in zip(uws, pre, gbs, mids)]
        for c, r0, uw, trans, (q, k, v, beta, _), (qk, eg, g_last) in zip(cs, r0s, uws, transs, pre, mids):
            u_ref[pl.ds(r0, C), :] = uw[:, :D]
            wqd_ref[c, 0:C, :] = uw[:, D:].astype(BF16)
            wqd_ref[c, C:2 * C, :] = (q * eg).astype(BF16)
            qk_ref[pl.ds(r0, C), :] = qk.astype(BF16)
            bc_ref[c] = trans[:, :D]
            kw_ref[c] = trans[:, D:].astype(BF16)
            gl_ref[c] = jnp.exp(g_last)
        return carry

    lax.fori_loop(0, n_chunks // group, prepare, 0)

    def chain(c, state):
        sb = state.astype(BF16)
        sb_ref[c] = sb
        return state * gl_ref[c] - jnp.dot(kw_ref[c], sb, preferred_element_type=F32) + bc_ref[c]

    lax.fori_loop(0, n_chunks, chain, jnp.zeros((D, D), F32))

    def outputs(grp, carry):
        cs = [grp * group + i for i in range(group)]
        r0s = [pl.multiple_of(c * C, C) for c in cs]
        ws_qs = [jnp.dot(wqd_ref[c], sb_ref[c], preferred_element_type=F32) for c in cs]
        v_news = [(u_ref[pl.ds(r0, C), :] - wq[:C]).astype(BF16) for r0, wq in zip(r0s, ws_qs)]
        os = [wq[C:] + jnp.dot(qk_ref[pl.ds(r0, C), :], vn, preferred_element_type=F32)
              for r0, wq, vn in zip(r0s, ws_qs, v_news)]
        for r0, o in zip(r0s, os):
            z = z_ref[0, pl.ds(r0, C), :]
            o_ref[0, pl.ds(r0, C), :] = (_rms(o, ng_ref[...]) * _silu(z)).astype(o_ref.dtype)
        return carry

    lax.fori_loop(0, n_chunks // group, outputs, 0)


def _gdn(pa, conv_w, a_log, dt_bias, norm_g, *, group=4):
    b, s, _ = pa.shape
    H = N_GDN_HEADS
    C = GDN_CHUNK
    assert s % C == 0
    n_chunks = s // C
    bat = pa[:, :, 4 * GDN_WIDTH:4 * GDN_WIDTH + 2 * H].transpose(0, 2, 1).reshape(b, 2 * H, n_chunks, C)
    head = lambda off: pl.BlockSpec((1, s, HEAD_DIM), lambda i, j: (i, 0, off + j))
    cw = lambda off: pl.BlockSpec((GDN_CONV, HEAD_DIM), lambda i, j: (0, off + j))
    smem = pl.BlockSpec(memory_space=pltpu.SMEM)
    return pl.pallas_call(
        functools.partial(_gdn_kernel, group=math.gcd(group, n_chunks)),
        grid=(b, H),
        in_specs=[smem, smem, head(0), head(H), head(2 * H), head(3 * H),
                  pl.BlockSpec((1, 2 * H, n_chunks, C), lambda i, j: (i, 0, 0, 0)),
                  cw(0), cw(H), cw(2 * H),
                  pl.BlockSpec((1, HEAD_DIM), lambda i, j: (0, 0))],
        out_specs=pl.BlockSpec((1, s, HEAD_DIM), lambda i, j: (i, 0, j)),
        out_shape=jax.ShapeDtypeStruct((b, s, GDN_WIDTH), BF16),
        scratch_shapes=[pltpu.VMEM((s + SUBLANES, HEAD_DIM), F32)] * 3 + [
            pltpu.VMEM((s, HEAD_DIM), F32),
            pltpu.VMEM((n_chunks, 2 * C, HEAD_DIM), BF16),
            pltpu.VMEM((s, C), BF16),
            pltpu.VMEM((n_chunks, HEAD_DIM, HEAD_DIM), BF16),
            pltpu.VMEM((n_chunks, HEAD_DIM, HEAD_DIM), F32),
            pltpu.VMEM((n_chunks, 1, HEAD_DIM), F32),
            pltpu.VMEM((n_chunks, HEAD_DIM, HEAD_DIM), BF16)],
        compiler_params=pltpu.CompilerParams(
            dimension_semantics=("parallel", "arbitrary"), vmem_limit_bytes=VMEM_LIMIT),
        name="gdn",
    )(a_log, dt_bias, pa, pa, pa, pa, bat, conv_w, conv_w, conv_w, norm_g.reshape(1, HEAD_DIM))


def _bucket_upper_bounds():
    n = np.arange(0, 4 * REL_MAX_DIST, dtype=np.int64)
    max_exact = REL_BUCKETS // 2
    nf = np.maximum(n, 1).astype(np.float32)
    large = max_exact + (np.log(nf / np.float32(max_exact)) / np.float32(math.log(REL_MAX_DIST / max_exact))
                         * np.float32(REL_BUCKETS - max_exact)).astype(np.int32)
    large = np.minimum(large, REL_BUCKETS - 1)
    bucket = np.where(n < max_exact, n, large)
    assert np.all(np.diff(bucket) >= 0) and bucket[-1] == REL_BUCKETS - 1
    return [int(np.searchsorted(bucket, b, side="right")) for b in range(REL_BUCKETS - 1)]


_BUCKET_UPPER = _bucket_upper_bounds()


def _moba_kernel(rb_ref, q_ref, k_ref, v_ref, ng_ref, o_ref, kb_ref, vt_ref, km_ref, bd_ref, bl_ref):
    h = pl.program_id(0)
    b = pl.program_id(1)
    seq = k_ref.shape[1]
    T = MOBA_BLOCK
    nb = seq // T
    scale = HEAD_DIM ** -0.5 * LOG2E
    kk = lax.broadcasted_iota(jnp.int32, (T, T), 0)
    qq = lax.broadcasted_iota(jnp.int32, (T, T), 1)

    @pl.when(b == 0)
    def _():
        def bias_of(n):
            val = jnp.full((T, T), rb_ref[REL_BUCKETS - 1, h], F32)
            for bkt in range(REL_BUCKETS - 2, -1, -1):
                val = jnp.where(n < _BUCKET_UPPER[bkt], rb_ref[bkt, h], val)
            return val
        bd_ref[...] = jnp.where(qq >= kk, bias_of(qq - kk) * LOG2E, NEG)
        bl_ref[...] = bias_of(qq - kk + T) * LOG2E

    kb_ref[...] = k_ref[0].astype(BF16)
    for n in range(nb):
        blk_rows = slice(n * T, (n + 1) * T)
        vt_ref[:, blk_rows] = v_ref[0, blk_rows, :].T.astype(BF16)
        km_ref[n:n + 1, :] = jnp.mean(k_ref[0, blk_rows, :], axis=0, keepdims=True)
    far_bias = rb_ref[REL_BUCKETS - 1, h] * LOG2E

    def additive_mask(i, qf):
        if i <= MOBA_TOPK:
            return None
        gate = lax.dot_general(km_ref[0:i, :], qf, (((1,), (1,)), ((), ())),
                               preferred_element_type=F32, precision=lax.Precision.HIGHEST)
        blk = lax.broadcasted_iota(jnp.int32, (i, T), 0)
        rank = jnp.zeros((i, T), jnp.int32)
        for m in range(i):
            gm = gate[m:m + 1, :]
            rank = rank + jnp.where((gm > gate) | ((gm == gate) & (m < blk)), 1, 0)
        return jnp.where(rank < MOBA_TOPK, 0.0, NEG)

    def biased(i, s_all, keep):
        parts = []
        for j in range(i + 1):
            sj = s_all[j * T:(j + 1) * T, :]
            if j == i:
                sj = sj + bd_ref[...]
            elif j == i - 1:
                sj = sj + bl_ref[...]
                if keep is not None:
                    sj = sj + keep[j:j + 1, :]
            else:
                sj = sj + (far_bias if keep is None else keep[j:j + 1, :] + far_bias)
            parts.append(sj)
        return parts

    order = [t for pair in zip(range(nb // 2), range(nb - 1, nb // 2 - 1, -1)) for t in pair]
    if nb % 2:
        order.append(nb // 2)
    for g0 in range(0, nb, MOBA_TILES_IN_FLIGHT):
        tiles = order[g0:g0 + MOBA_TILES_IN_FLIGHT]
        qfs = [q_ref[0, i * T:(i + 1) * T, :] for i in tiles]
        keeps = [additive_mask(i, qf) for i, qf in zip(tiles, qfs)]
        s_alls = [_bdot_nt(kb_ref[0:(i + 1) * T, :], qf * scale) for i, qf in zip(tiles, qfs)]
        partss = [biased(i, s_all, keep) for i, s_all, keep in zip(tiles, s_alls, keeps)]
        m_rows = [jnp.max(functools.reduce(jnp.maximum, parts), axis=0, keepdims=True) for parts in partss]
        pss = [[jnp.exp2(sj - m_row) for sj in parts] for parts, m_row in zip(partss, m_rows)]
        l_rows = [jnp.sum(functools.reduce(jnp.add, ps), axis=0, keepdims=True) for ps in pss]
        p_alls = [jnp.concatenate([p.astype(BF16) for p in ps], axis=0) for ps in pss]
        o_ts = [jnp.dot(vt_ref[:, 0:(i + 1) * T], p_all, preferred_element_type=F32) / l_row
                for i, p_all, l_row in zip(tiles, p_alls, l_rows)]
        for i, o_t in zip(tiles, o_ts):
            o_ref[0, i * T:(i + 1) * T, :] = _rms(o_t.T, ng_ref[...]).astype(o_ref.dtype)


def _moba(pb, rel_bias, norm_g):
    b, s, _ = pb.shape
    H = N_MOBA_HEADS
    T = MOBA_BLOCK
    assert s % T == 0
    nb = s // T
    head = lambda off: pl.BlockSpec((1, s, HEAD_DIM), lambda h, bb: (bb, 0, off + h))
    return pl.pallas_call(
        _moba_kernel,
        grid=(H, b),
        in_specs=[pl.BlockSpec(memory_space=pltpu.SMEM), head(0), head(H), head(2 * H),
                  pl.BlockSpec((1, HEAD_DIM), lambda h, bb: (0, 0))],
        out_specs=pl.BlockSpec((1, s, HEAD_DIM), lambda h, bb: (bb, 0, h)),
        out_shape=jax.ShapeDtypeStruct((b, s, MOBA_WIDTH), BF16),
        scratch_shapes=[pltpu.VMEM((s, HEAD_DIM), BF16), pltpu.VMEM((HEAD_DIM, s), BF16),
                        pltpu.VMEM((nb, HEAD_DIM), F32),
                        pltpu.VMEM((T, T), F32), pltpu.VMEM((T, T), F32)],
        compiler_params=pltpu.CompilerParams(
            dimension_semantics=("arbitrary", "arbitrary"), vmem_limit_bytes=VMEM_LIMIT),
        name="moba",
    )(rel_bias, pb, pb, pb, norm_g.reshape(1, HEAD_DIM))


def _out_proj_kernel(x_ref, oa_ref, ob_ref, wa_ref, wb_ref, o_ref, *, rs):
    for r in range(x_ref.shape[0] // rs):
        rows = slice(r * rs, (r + 1) * rs)
        o_ref[rows, :] = (x_ref[rows, :]
                          + jnp.dot(oa_ref[rows, :], wa_ref[...], preferred_element_type=F32)
                          + jnp.dot(ob_ref[rows, :], wb_ref[...], preferred_element_type=F32))


def _out_proj(x, oa, ob, wa, wb, *, tm, tn):
    m, d = x.shape
    ka, kb = oa.shape[1], ob.shape[1]
    return pl.pallas_call(
        functools.partial(_out_proj_kernel, rs=math.gcd(tm, ROW_SUB)),
        grid=(m // tm, d // tn),
        in_specs=[pl.BlockSpec((tm, tn), lambda i, j: (i, j)),
                  pl.BlockSpec((tm, ka), lambda i, j: (i, 0)),
                  pl.BlockSpec((tm, kb), lambda i, j: (i, 0)),
                  pl.BlockSpec((ka, tn), lambda i, j: (0, j)),
                  pl.BlockSpec((kb, tn), lambda i, j: (0, j))],
        out_specs=pl.BlockSpec((tm, tn), lambda i, j: (i, j)),
        out_shape=jax.ShapeDtypeStruct((m, d), F32),
        compiler_params=pltpu.CompilerParams(
            dimension_semantics=("parallel", "arbitrary"), vmem_limit_bytes=VMEM_LIMIT),
        name="out_proj",
    )(x, oa, ob, wa, wb)


def _xattn_kernel(x_ref, g_ref, wq_ref, kv_ref, wo_ref, ng_ref, o_ref, hn_ref, *, rs):
    width = N_XATTN_HEADS * HEAD_DIM
    scale = HEAD_DIM ** -0.5 * LOG2E
    blocks = [slice(r * rs, (r + 1) * rs) for r in range(x_ref.shape[1] // rs)]
    qs = [_bdot(_rms(x_ref[0, rows, :], g_ref[...]), wq_ref[...]) * scale for rows in blocks]
    outs = [[] for _ in blocks]
    for hd in range(N_XATTN_HEADS):
        sl = slice(hd * HEAD_DIM, (hd + 1) * HEAD_DIM)
        ss = [_bdot_nt(q[:, sl], kv_ref[0, :, sl]) for q in qs]
        ps = [jnp.exp2(s - jnp.max(s, axis=-1, keepdims=True)) for s in ss]
        for out, p in zip(outs, ps):
            l = jnp.sum(p, axis=-1, keepdims=True)
            out.append(_bdot(p, kv_ref[0, :, width + hd * HEAD_DIM: width + (hd + 1) * HEAD_DIM]) / l)
    ys = [x_ref[0, rows, :] + _bdot(jnp.concatenate(out, axis=-1), wo_ref[...])
          for rows, out in zip(blocks, outs)]
    for rows, y in zip(blocks, ys):
        o_ref[0, rows, :] = y
        hn_ref[0, rows, :] = _rms(y, ng_ref[...]).astype(BF16)


def _xattn(x, g, wq, kv, wo, next_g, *, ts):
    b, s, d = x.shape
    mlen = kv.shape[1]
    width = wq.shape[1]
    tile = pl.BlockSpec((1, ts, d), lambda i, j: (i, j, 0))
    vec = pl.BlockSpec((1, d), lambda i, j: (0, 0))
    return pl.pallas_call(
        functools.partial(_xattn_kernel, rs=math.gcd(ts, ROW_SUB)),
        grid=(b, s // ts),
        in_specs=[tile, vec,
                  pl.BlockSpec((d, width), lambda i, j: (0, 0)),
                  pl.BlockSpec((1, mlen, 2 * width), lambda i, j: (i, 0, 0)),
                  pl.BlockSpec((width, d), lambda i, j: (0, 0)),
                  vec],
        out_specs=[tile, tile],
        out_shape=[jax.ShapeDtypeStruct((b, s, d), F32), jax.ShapeDtypeStruct((b, s, d), BF16)],
        compiler_params=pltpu.CompilerParams(
            dimension_semantics=("parallel", "parallel"), vmem_limit_bytes=VMEM_LIMIT),
        name="xattn",
    )(x, g.reshape(1, d), wq, kv, wo, next_g.reshape(1, d))


def _ffn_kernel(x_hbm, h_ref, halo_ref, wg_ref, wu_ref, cw_ref, cb_ref, wd_ref, fg_ref, o_ref,
                hx_ref, xres_ref, sem, *, tiles_per_seq, rs):
    i = pl.program_id(0)
    f = pl.program_id(1)
    tm = h_ref.shape[0]
    pad = BF16_ROWS

    def x_copy():
        return pltpu.make_async_copy(x_hbm.at[pl.ds(i * tm, tm), :], xres_ref, sem)

    @pl.when(f == 0)
    def _():
        x_copy().start()
        halo = halo_ref[...]
        hx_ref[0:pad, :] = jnp.where(i % tiles_per_seq == 0, jnp.zeros_like(halo), halo)
        hx_ref[pad:pad + tm, :] = h_ref[...]
        o_ref[...] = jnp.zeros_like(o_ref)

    cw = cw_ref[...]
    keep = SUBLANES
    tail = None
    for r in range(tm // rs):
        rows = slice(r * rs, (r + 1) * rs)
        hrows = slice(pad + r * rs, pad + (r + 1) * rs)
        if r == 0:
            gp = jnp.dot(hx_ref[0:pad + rs, :], wg_ref[...], preferred_element_type=F32)[pad - keep:]
        else:
            gp = jnp.concatenate(
                [tail, jnp.dot(hx_ref[hrows, :], wg_ref[...], preferred_element_type=F32)], axis=0)
        tail = gp[rs:rs + keep]
        up = jnp.dot(hx_ref[hrows, :], wu_ref[...], preferred_element_type=F32)
        gate = gp[keep:keep + rs] * cw[2:3] + gp[keep - 1:keep - 1 + rs] * cw[1:2] \
            + gp[keep - 2:keep - 2 + rs] * cw[0:1] + cb_ref[...]
        act = _silu(gate) * up
        o_ref[rows, :] += jnp.dot(act.astype(BF16), wd_ref[...], preferred_element_type=F32)

    @pl.when(f == pl.num_programs(1) - 1)
    def _():
        x_copy().wait()
        def finish(r, carry):
            rows = pl.ds(pl.multiple_of(r * HEAD_DIM, HEAD_DIM), HEAD_DIM)
            o_ref[rows, :] = _rms(xres_ref[rows, :] + o_ref[rows, :], fg_ref[...])
            return carry

        lax.fori_loop(0, tm // HEAD_DIM, finish, 0)


def _ffn(x, h, wg, wu, cw, cb, wd, fg, *, seq, tm, tf):
    m, d = x.shape
    ff = wg.shape[1]
    assert seq % tm == 0 and ff % tf == 0 and tm % BF16_ROWS == 0
    hb = tm // BF16_ROWS
    return pl.pallas_call(
        functools.partial(_ffn_kernel, tiles_per_seq=seq // tm, rs=tm),
        grid=(m // tm, ff // tf),
        in_specs=[pl.BlockSpec(memory_space=pl.ANY),
                  pl.BlockSpec((tm, d), lambda i, f: (i, 0)),
                  pl.BlockSpec((BF16_ROWS, d), lambda i, f: (jnp.maximum(i * hb - 1, 0), 0)),
                  pl.BlockSpec((d, tf), lambda i, f: (0, f)),
                  pl.BlockSpec((d, tf), lambda i, f: (0, f)),
                  pl.BlockSpec((FFN_CONV, tf), lambda i, f: (0, f)),
                  pl.BlockSpec((1, tf), lambda i, f: (0, f)),
                  pl.BlockSpec((tf, d), lambda i, f: (f, 0)),
                  pl.BlockSpec((1, d), lambda i, f: (0, 0))],
        out_specs=pl.BlockSpec((tm, d), lambda i, f: (i, 0)),
        out_shape=jax.ShapeDtypeStruct((m, d), F32),
        scratch_shapes=[pltpu.VMEM((tm + BF16_ROWS, d), BF16), pltpu.VMEM((tm, d), F32),
                        pltpu.SemaphoreType.DMA(())],
        compiler_params=pltpu.CompilerParams(
            dimension_semantics=("arbitrary", "arbitrary"), vmem_limit_bytes=VMEM_LIMIT),
        name="ffn",
    )(x, h, h, wg, wu, cw, cb.reshape(1, ff), wd, fg.reshape(1, d))


def _layer(x, mem, mix_norm_g, w_in, gdn_conv_w, a_log, dt_bias, gdn_norm_g, moba_norm_g, rel_bias,
           w_out, xattn_norm_g, mem_norm_g, w_xq, w_xkv, w_xo, ffn_norm_g, w_gate, w_up, ffn_conv_w,
           ffn_conv_b, w_down, final_g, *, last):
    b, s, d = x.shape
    m = b * s
    x2 = x.reshape(m, d)
    i1 = 4 * GDN_WIDTH
    i3 = i1 + 2 * N_GDN_HEADS
    w_all = w_in.astype(BF16)
    w_b = w_all[:, i3:]
    tm = min(1024, m)
    wa_cols = i1 + HEAD_DIM
    pa = _norm_matmul(x2, mix_norm_g, w_all, tm=tm, tn=wa_cols // 3, n=wa_cols).reshape(b, s, wa_cols)
    pb = _norm_matmul(x2, mix_norm_g, w_b, tm=tm, tn=1024).reshape(b, s, 3 * MOBA_WIDTH)
    o_a = _gdn(pa, gdn_conv_w, a_log, dt_bias, gdn_norm_g)
    o_b = _moba(pb, rel_bias, moba_norm_g)
    x1 = _out_proj(x2, o_a.reshape(m, GDN_WIDTH), o_b.reshape(m, MOBA_WIDTH),
                   w_out[:GDN_WIDTH].astype(BF16), w_out[GDN_WIDTH:].astype(BF16), tm=tm, tn=1024)
    mlen = mem.shape[1]
    kv = _norm_matmul(mem.reshape(b * mlen, d), mem_norm_g, w_xkv.astype(BF16),
                      tm=min(512, b * mlen), tn=1024, out_dtype=BF16).reshape(b, mlen, -1)
    x2b, h2 = _xattn(x1.reshape(b, s, d), xattn_norm_g, w_xq.astype(BF16), kv, w_xo.astype(BF16),
                     ffn_norm_g, ts=min(512, s))
    assert last, "the final rmsnorm is fused into the last layer's ffn"
    y = _ffn(x2b.reshape(m, d), h2.reshape(m, d), w_gate.astype(BF16), w_up.astype(BF16), ffn_conv_w,
             ffn_conv_b, w_down.astype(BF16), final_g, seq=s, tm=min(1024, s), tf=512)
    return y.reshape(b, s, d)


def kernel(x, mem, mix_norm_g, w_in, gdn_conv_w, gdn_a_log, gdn_dt_bias, gdn_norm_g, moba_norm_g,
           rel_bias, w_out, xattn_norm_g, mem_norm_g, w_xq, w_xkv, w_xo, ffn_norm_g, w_gate, w_up,
           ffn_conv_w, ffn_conv_b, w_down, final_norm_g):
    depth = mix_norm_g.shape[0]
    assert depth == 1
    l = 0
    return _layer(x, mem, mix_norm_g[l], w_in[l], gdn_conv_w[l], gdn_a_log[l], gdn_dt_bias[l],
                  gdn_norm_g[l], moba_norm_g[l], rel_bias, w_out[l], xattn_norm_g[l], mem_norm_g[l],
                  w_xq[l], w_xkv[l], w_xo[l], ffn_norm_g[l], w_gate[l], w_up[l], ffn_conv_w[l],
                  ffn_conv_b[l], w_down[l], final_norm_g, last=True)
```

```python
import functools
import math

import jax
import jax.numpy as jnp
import numpy as np
from jax import lax
from jax.experimental import pallas as pl
from jax.experimental.pallas import tpu as pltpu

HEAD_DIM = 128
N_GDN_HEADS = 8
N_MOBA_HEADS = 8
GDN_WIDTH = N_GDN_HEADS * HEAD_DIM
MOBA_WIDTH = N_MOBA_HEADS * HEAD_DIM
GDN_CONV = 4
GDN_CHUNK = 256
MOBA_BLOCK = 256
MOBA_TOPK = 3
MOBA_TILES_IN_FLIGHT = 4
REL_BUCKETS = 32
REL_MAX_DIST = 128
N_XATTN_HEADS = 4
FFN_CONV = 3
EPS = 1e-6
NEG = -1e30
LOG2E = math.log2(math.e)
SUBLANES = 8
BF16_ROWS = 16
ROW_SUB = 256
VMEM_LIMIT = 58 * 1024 * 1024

F32 = jnp.float32
BF16 = jnp.bfloat16


def _bdot(a, b):
    return jnp.dot(a.astype(BF16), b.astype(BF16), preferred_element_type=F32)


def _bdot_nt(a, b):
    return lax.dot_general(a.astype(BF16), b.astype(BF16), (((1,), (1,)), ((), ())),
                           preferred_element_type=F32)


def _bdot_tn(a, b):
    return lax.dot_general(a.astype(BF16), b.astype(BF16), (((0,), (0,)), ((), ())),
                           preferred_element_type=F32)


def _fdot(a, b):
    return jnp.dot(a, b, preferred_element_type=F32, precision=lax.Precision.HIGHEST)


def _sigmoid(x):
    return 1.0 / (1.0 + jnp.exp(-x))


def _silu(x):
    return x * _sigmoid(x)


def _rms(x, g):
    return x * lax.rsqrt(jnp.mean(x * x, axis=-1, keepdims=True) + EPS) * g


def _norm_matmul_kernel(x_ref, g_ref, w_ref, o_ref, *, rs):
    for r in range(x_ref.shape[0] // rs):
        rows = slice(r * rs, (r + 1) * rs)
        hn = _rms(x_ref[rows, :], g_ref[...]).astype(BF16)
        o_ref[rows, :] = jnp.dot(hn, w_ref[...], preferred_element_type=F32).astype(o_ref.dtype)


def _norm_matmul(x, g, w, *, tm, tn, n=None, out_dtype=F32):
    m, k = x.shape
    n = w.shape[1] if n is None else n
    assert m % tm == 0 and n % tn == 0 and n <= w.shape[1]
    rs = math.gcd(tm, ROW_SUB)
    return pl.pallas_call(
        functools.partial(_norm_matmul_kernel, rs=rs),
        grid=(m // tm, n // tn),
        in_specs=[pl.BlockSpec((tm, k), lambda i, j: (i, 0)),
                  pl.BlockSpec((1, k), lambda i, j: (0, 0)),
                  pl.BlockSpec((k, tn), lambda i, j: (0, j))],
        out_specs=pl.BlockSpec((tm, tn), lambda i, j: (i, j)),
        out_shape=jax.ShapeDtypeStruct((m, n), out_dtype),
        compiler_params=pltpu.CompilerParams(
            dimension_semantics=("parallel", "arbitrary"), vmem_limit_bytes=VMEM_LIMIT),
        name="norm_matmul",
    )(x, g.reshape(1, k), w)


def _unit_lower_inverse(mats, row, col):
    c = mats[0].shape[0]
    eye = (row == col).astype(F32)
    blk = lambda n: (row // n) == (col // n)
    inner = blk(16)
    ds = [jnp.where(inner, a, 0.0) for a in mats]
    ts = [eye - d for d in ds]
    ps = [_bdot(d, d) for d in ds]
    for step in range(3):
        ts = [t + _bdot(t, p) for t, p in zip(ts, ps)]
        if step < 2:
            ps = [_bdot(p, p) for p in ps]
    n = 32
    while n <= c:
        outer = blk(n) if n < c else None
        keep = ~inner if outer is None else (outer & ~inner)
        xs = [_bdot(jnp.where(keep, a, 0.0), t) for a, t in zip(mats, ts)]
        ts = [t - _bdot(t, x) for t, x in zip(ts, xs)]
        inner = outer
        n *= 2
    return ts


def _gdn_kernel(alog_ref, dtb_ref, q_ref, k_ref, v_ref, z_ref, bat_ref, wq_ref, wk_ref, wv_ref,
                ng_ref, o_ref, qp_ref, kp_ref, vp_ref, u_ref, wqd_ref, qk_ref, kw_ref, bc_ref, gl_ref,
                sb_ref, *, group):
    h = pl.program_id(1)
    seq = q_ref.shape[1]
    C = GDN_CHUNK
    D = HEAD_DIM
    pad = SUBLANES
    n_chunks = seq // C

    for src, dst in ((q_ref, qp_ref), (k_ref, kp_ref), (v_ref, vp_ref)):
        dst[0:pad, :] = jnp.zeros((pad, D), F32)
        dst[pad:pad + seq, :] = src[0]

    row = lax.broadcasted_iota(jnp.int32, (C, C), 0)
    col = lax.broadcasted_iota(jnp.int32, (C, C), 1)
    tri_incl = row >= col
    tri_strict = row > col
    lane = lax.broadcasted_iota(jnp.int32, (SUBLANES, C), 1)
    neg_a = -jnp.exp(jnp.full((1, C), alog_ref[h], F32))
    dt_bias = dtb_ref[h]
    scale = D ** -0.5

    def conv_silu(xp_ref, w_ref, r0):
        win = xp_ref[pl.ds(r0, C + pad), :]
        w = w_ref[...]
        y = win[pad:pad + C] * w[3:4]
        for j in range(GDN_CONV - 1):
            s = GDN_CONV - 1 - j
            y = y + win[pad - s:pad - s + C] * w[j:j + 1]
        return _silu(y)

    def l2n(x):
        return x * lax.rsqrt(jnp.sum(x * x, axis=-1, keepdims=True) + EPS)

    def cumsum_lanes(x):
        x = jnp.broadcast_to(x, (SUBLANES, C))
        s = 1
        while s < C:
            x = x + jnp.where(lane >= s, pltpu.roll(x, s, axis=1), 0.0)
            s *= 2
        return x[0:1, :]

    reps = C // D

    def rows_to_cols(x):
        return jnp.concatenate(
            [jnp.broadcast_to(x[:, n * D:(n + 1) * D], (D, D)).T for n in range(reps)], axis=0)

    def prepare(grp, carry):
        cs = [grp * group + i for i in range(group)]
        r0s = [pl.multiple_of(c * C, C) for c in cs]
        pre, gbs, g_rows = [], [], []
        for c, r0 in zip(cs, r0s):
            q = l2n(conv_silu(qp_ref, wq_ref, r0)) * scale
            k = l2n(conv_silu(kp_ref, wk_ref, r0))
            v = conv_silu(vp_ref, wv_ref, r0)
            b_row = bat_ref[0, h, pl.ds(c, 1), :]
            xs = bat_ref[0, h + N_GDN_HEADS, pl.ds(c, 1), :] + dt_bias
            softplus = jnp.maximum(xs, 0.0) + jnp.log1p(jnp.exp(-jnp.abs(xs)))
            g_row = cumsum_lanes(neg_a * softplus)
            pre.append((q, k, v, rows_to_cols(_sigmoid(b_row)), None))
            g_rows.append(g_row)
            gbs.append(rows_to_cols(g_row))
        kk_qks = [_bdot_nt(jnp.concatenate([k * beta, q], axis=0), k) for q, k, v, beta, _ in pre]
        mats, mids = [], []
        for (q, k, v, beta, _), gb, g_row, kk_qk in zip(pre, gbs, g_rows, kk_qks):
            g_i = jnp.concatenate([gb] * reps, axis=1)
            g_j = jnp.broadcast_to(g_row, (C, C))
            decay = jnp.exp(jnp.where(tri_incl, g_i - g_j, NEG))
            mats.append(jnp.where(tri_strict, kk_qk[:C] * decay, 0.0))
            mids.append((kk_qk[C:] * decay, jnp.exp(gb), gb[C - 1:C, :]))
        ts = _unit_lower_inverse(mats, row, col)
        uws = [_bdot(t, jnp.concatenate([v * beta, k * beta * eg], axis=1))
               for t, (q, k, v, beta, _), (_, eg, _) in zip(ts, pre, mids)]
        transs = [_bdot_tn(k * jnp.exp(g_last - gb), uw)
                  for uw, (q, k, v, beta, _), gb, (_, _, g_last) in zip(uws, pre, gbs, mids)]
        for c, r0, uw, trans, (q, k, v, beta, _), (qk, eg, g_last) in zip(cs, r0s, uws, transs, pre, mids):
            u_ref[pl.ds(r0, C), :] = uw[:, :D]
            wqd_ref[c, 0:C, :] = uw[:, D:].astype(BF16)
            wqd_ref[c, C:2 * C, :] = (q * eg).astype(BF16)
            qk_ref[pl.ds(r0, C), :] = qk.astype(BF16)
            bc_ref[c] = trans[:, :D]
            kw_ref[c] = trans[:, D:].astype(BF16)
            gl_ref[c] = jnp.exp(g_last)
        return carry

    lax.fori_loop(0, n_chunks // group, prepare, 0)

    def chain(c, state):
        sb = state.astype(BF16)
        sb_ref[c] = sb
        return state * gl_ref[c] - jnp.dot(kw_ref[c], sb, preferred_element_type=F32) + bc_ref[c]

    lax.fori_loop(0, n_chunks, chain, jnp.zeros((D, D), F32))

    def outputs(grp, carry):
        cs = [grp * group + i for i in range(group)]
        r0s = [pl.multiple_of(c * C, C) for c in cs]
        ws_qs = [jnp.dot(wqd_ref[c], sb_ref[c], preferred_element_type=F32) for c in cs]
        v_news = [(u_ref[pl.ds(r0, C), :] - wq[:C]).astype(BF16) for r0, wq in zip(r0s, ws_qs)]
        os = [wq[C:] + jnp.dot(qk_ref[pl.ds(r0, C), :], vn, preferred_element_type=F32)
              for r0, wq, vn in zip(r0s, ws_qs, v_news)]
        for r0, o in zip(r0s, os):
            z = z_ref[0, pl.ds(r0, C), :]
            o_ref[0, pl.ds(r0, C), :] = (_rms(o, ng_ref[...]) * _silu(z)).astype(o_ref.dtype)
        return carry

    lax.fori_loop(0, n_chunks // group, outputs, 0)


def _gdn(pa, conv_w, a_log, dt_bias, norm_g, *, group=4):
    b, s, _ = pa.shape
    H = N_GDN_HEADS
    C = GDN_CHUNK
    assert s % C == 0
    n_chunks = s // C
    bat = pa[:, :, 4 * GDN_WIDTH:4 * GDN_WIDTH + 2 * H].transpose(0, 2, 1).reshape(b, 2 * H, n_chunks, C)
    head = lambda off: pl.BlockSpec((1, s, HEAD_DIM), lambda i, j: (i, 0, off + j))
    cw = lambda off: pl.BlockSpec((GDN_CONV, HEAD_DIM), lambda i, j: (0, off + j))
    smem = pl.BlockSpec(memory_space=pltpu.SMEM)
    return pl.pallas_call(
        functools.partial(_gdn_kernel, group=math.gcd(group, n_chunks)),
        grid=(b, H),
        in_specs=[smem, smem, head(0), head(H), head(2 * H), head(3 * H),
                  pl.BlockSpec((1, 2 * H, n_chunks, C), lambda i, j: (i, 0, 0, 0)),
                  cw(0), cw(H), cw(2 * H),
                  pl.BlockSpec((1, HEAD_DIM), lambda i, j: (0, 0))],
        out_specs=pl.BlockSpec((1, s, HEAD_DIM), lambda i, j: (i, 0, j)),
        out_shape=jax.ShapeDtypeStruct((b, s, GDN_WIDTH), BF16),
        scratch_shapes=[pltpu.VMEM((s + SUBLANES, HEAD_DIM), F32)] * 3 + [
            pltpu.VMEM((s, HEAD_DIM), F32),
            pltpu.VMEM((n_chunks, 2 * C, HEAD_DIM), BF16),
            pltpu.VMEM((s, C), BF16),
            pltpu.VMEM((n_chunks, HEAD_DIM, HEAD_DIM), BF16),
            pltpu.VMEM((n_chunks, HEAD_DIM, HEAD_DIM), F32),
            pltpu.VMEM((n_chunks, 1, HEAD_DIM), F32),
            pltpu.VMEM((n_chunks, HEAD_DIM, HEAD_DIM), BF16)],
        compiler_params=pltpu.CompilerParams(
            dimension_semantics=("parallel", "arbitrary"), vmem_limit_bytes=VMEM_LIMIT),
        name="gdn",
    )(a_log, dt_bias, pa, pa, pa, pa, bat, conv_w, conv_w, conv_w, norm_g.reshape(1, HEAD_DIM))


def _bucket_upper_bounds():
    n = np.arange(0, 4 * REL_MAX_DIST, dtype=np.int64)
    max_exact = REL_BUCKETS // 2
    nf = np.maximum(n, 1).astype(np.float32)
    large = max_exact + (np.log(nf / np.float32(max_exact)) / np.float32(math.log(REL_MAX_DIST / max_exact))
                         * np.float32(REL_BUCKETS - max_exact)).astype(np.int32)
    large = np.minimum(large, REL_BUCKETS - 1)
    bucket = np.where(n < max_exact, n, large)
    assert np.all(np.diff(bucket) >= 0) and bucket[-1] == REL_BUCKETS - 1
    return [int(np.searchsorted(bucket, b, side="right")) for b in range(REL_BUCKETS - 1)]


_BUCKET_UPPER = _bucket_upper_bounds()


def _moba_kernel(rb_ref, q_ref, k_ref, v_ref, ng_ref, o_ref, kb_ref, vt_ref, km_ref, bd_ref, bl_ref):
    h = pl.program_id(0)
    b = pl.program_id(1)
    seq = k_ref.shape[1]
    T = MOBA_BLOCK
    nb = seq // T
    scale = HEAD_DIM ** -0.5 * LOG2E
    kk = lax.broadcasted_iota(jnp.int32, (T, T), 0)
    qq = lax.broadcasted_iota(jnp.int32, (T, T), 1)

    @pl.when(b == 0)
    def _():
        def bias_of(n):
            val = jnp.full((T, T), rb_ref[REL_BUCKETS - 1, h], F32)
            for bkt in range(REL_BUCKETS - 2, -1, -1):
                val = jnp.where(n < _BUCKET_UPPER[bkt], rb_ref[bkt, h], val)
            return val
        bd_ref[...] = jnp.where(qq >= kk, bias_of(qq - kk) * LOG2E, NEG)
        bl_ref[...] = bias_of(qq - kk + T) * LOG2E

    kb_ref[...] = k_ref[0].astype(BF16)
    for n in range(nb):
        blk_rows = slice(n * T, (n + 1) * T)
        vt_ref[:, blk_rows] = v_ref[0, blk_rows, :].T.astype(BF16)
        km_ref[n:n + 1, :] = jnp.mean(k_ref[0, blk_rows, :], axis=0, keepdims=True)
    far_bias = rb_ref[REL_BUCKETS - 1, h] * LOG2E

    def additive_mask(i, qf):
        if i <= MOBA_TOPK:
            return None
        gate = lax.dot_general(km_ref[0:i, :], qf, (((1,), (1,)), ((), ())),
                               preferred_element_type=F32, precision=lax.Precision.HIGHEST)
        blk = lax.broadcasted_iota(jnp.int32, (i, T), 0)
        rank = jnp.zeros((i, T), jnp.int32)
        for m in range(i):
            gm = gate[m:m + 1, :]
            rank = rank + jnp.where((gm > gate) | ((gm == gate) & (m < blk)), 1, 0)
        return jnp.where(rank < MOBA_TOPK, 0.0, NEG)

    def biased(i, s_all, keep):
        parts = []
        for j in range(i + 1):
            sj = s_all[j * T:(j + 1) * T, :]
            if j == i:
                sj = sj + bd_ref[...]
            elif j == i - 1:
                sj = sj + bl_ref[...]
                if keep is not None:
                    sj = sj + keep[j:j + 1, :]
            else:
                sj = sj + (far_bias if keep is None else keep[j:j + 1, :] + far_bias)
            parts.append(sj)
        return parts

    order = [t for pair in zip(range(nb // 2), range(nb - 1, nb // 2 - 1, -1)) for t in pair]
    if nb % 2:
        order.append(nb // 2)
    for g0 in range(0, nb, MOBA_TILES_IN_FLIGHT):
        tiles = order[g0:g0 + MOBA_TILES_IN_FLIGHT]
        qfs = [q_ref[0, i * T:(i + 1) * T, :] for i in tiles]
        keeps = [additive_mask(i, qf) for i, qf in zip(tiles, qfs)]
        s_alls = [_bdot_nt(kb_ref[0:(i + 1) * T, :], qf * scale) for i, qf in zip(tiles, qfs)]
        partss = [biased(i, s_all, keep) for i, s_all, keep in zip(tiles, s_alls, keeps)]
        m_rows = [jnp.max(functools.reduce(jnp.maximum, parts), axis=0, keepdims=True) for parts in partss]
        pss = [[jnp.exp2(sj - m_row) for sj in parts] for parts, m_row in zip(partss, m_rows)]
        l_rows = [jnp.sum(functools.reduce(jnp.add, ps), axis=0, keepdims=True) for ps in pss]
        p_alls = [jnp.concatenate([p.astype(BF16) for p in ps], axis=0) for ps in pss]
        o_ts = [jnp.dot(vt_ref[:, 0:(i + 1) * T], p_all, preferred_element_type=F32) / l_row
                for i, p_all, l_row in zip(tiles, p_alls, l_rows)]
        for i, o_t in zip(tiles, o_ts):
            o_ref[0, i * T:(i + 1) * T, :] = _rms(o_t.T, ng_ref[...]).astype(o_ref.dtype)


def _moba(pb, rel_bias, norm_g):
    b, s, _ = pb.shape
    H = N_MOBA_HEADS
    T = MOBA_BLOCK
    assert s % T == 0
    nb = s // T
    head = lambda off: pl.BlockSpec((1, s, HEAD_DIM), lambda h, bb: (bb, 0, off + h))
    return pl.pallas_call(
        _moba_kernel,
        grid=(H, b),
        in_specs=[pl.BlockSpec(memory_space=pltpu.SMEM), head(0), head(H), head(2 * H),
                  pl.BlockSpec((1, HEAD_DIM), lambda h, bb: (0, 0))],
        out_specs=pl.BlockSpec((1, s, HEAD_DIM), lambda h, bb: (bb, 0, h)),
        out_shape=jax.ShapeDtypeStruct((b, s, MOBA_WIDTH), BF16),
        scratch_shapes=[pltpu.VMEM((s, HEAD_DIM), BF16), pltpu.VMEM((HEAD_DIM, s), BF16),
                        pltpu.VMEM((nb, HEAD_DIM), F32),
                        pltpu.VMEM((T, T), F32), pltpu.VMEM((T, T), F32)],
        compiler_params=pltpu.CompilerParams(
            dimension_semantics=("arbitrary", "arbitrary"), vmem_limit_bytes=VMEM_LIMIT),
        name="moba",
    )(rel_bias, pb, pb, pb, norm_g.reshape(1, HEAD_DIM))


def _xattn_kernel(x_ref, oa_ref, ob_ref, wa_ref, wb_ref, g_ref, wq_ref, kv_ref, wo_ref, ng_ref,
                  o_ref, hn_ref, *, rs):
    width = N_XATTN_HEADS * HEAD_DIM
    scale = HEAD_DIM ** -0.5 * LOG2E
    blocks = [slice(r * rs, (r + 1) * rs) for r in range(x_ref.shape[1] // rs)]
    x1s = [x_ref[0, rows, :]
           + jnp.dot(oa_ref[0, rows, :], wa_ref[...], preferred_element_type=F32)
           + jnp.dot(ob_ref[0, rows, :], wb_ref[...], preferred_element_type=F32) for rows in blocks]
    qs = [_bdot(_rms(x1, g_ref[...]), wq_ref[...]) * scale for x1 in x1s]
    outs = [[] for _ in blocks]
    for hd in range(N_XATTN_HEADS):
        sl = slice(hd * HEAD_DIM, (hd + 1) * HEAD_DIM)
        ss = [_bdot_nt(q[:, sl], kv_ref[0, :, sl]) for q in qs]
        ps = [jnp.exp2(s - jnp.max(s, axis=-1, keepdims=True)) for s in ss]
        for out, p in zip(outs, ps):
            l = jnp.sum(p, axis=-1, keepdims=True)
            out.append(_bdot(p, kv_ref[0, :, width + hd * HEAD_DIM: width + (hd + 1) * HEAD_DIM]) / l)
    ys = [x1 + _bdot(jnp.concatenate(out, axis=-1), wo_ref[...]) for x1, out in zip(x1s, outs)]
    for rows, y in zip(blocks, ys):
        o_ref[0, rows, :] = y
        hn_ref[0, rows, :] = _rms(y, ng_ref[...]).astype(BF16)


def _xattn(x, oa, ob, wa, wb, g, wq, kv, wo, next_g, *, ts):
    b, s, d = x.shape
    mlen = kv.shape[1]
    width = wq.shape[1]
    ka, kb = oa.shape[2], ob.shape[2]
    tile = pl.BlockSpec((1, ts, d), lambda i, j: (i, j, 0))
    vec = pl.BlockSpec((1, d), lambda i, j: (0, 0))
    whole = lambda r, c: pl.BlockSpec((r, c), lambda i, j: (0, 0))
    return pl.pallas_call(
        functools.partial(_xattn_kernel, rs=math.gcd(ts, ROW_SUB)),
        grid=(b, s // ts),
        in_specs=[tile,
                  pl.BlockSpec((1, ts, ka), lambda i, j: (i, j, 0)),
                  pl.BlockSpec((1, ts, kb), lambda i, j: (i, j, 0)),
                  whole(ka, d), whole(kb, d), vec, whole(d, width),
                  pl.BlockSpec((1, mlen, 2 * width), lambda i, j: (i, 0, 0)),
                  whole(width, d), vec],
        out_specs=[tile, tile],
        out_shape=[jax.ShapeDtypeStruct((b, s, d), F32), jax.ShapeDtypeStruct((b, s, d), BF16)],
        compiler_params=pltpu.CompilerParams(
            dimension_semantics=("parallel", "parallel"), vmem_limit_bytes=VMEM_LIMIT),
        name="xattn",
    )(x, oa, ob, wa, wb, g.reshape(1, d), wq, kv, wo, next_g.reshape(1, d))


def _ffn_kernel(x_hbm, h_ref, halo_ref, wg_ref, wu_ref, cw_ref, cb_ref, wd_ref, fg_ref, o_ref,
                hx_ref, xres_ref, sem, *, tiles_per_seq, rs):
    i = pl.program_id(0)
    f = pl.program_id(1)
    tm = h_ref.shape[0]
    pad = BF16_ROWS

    def x_copy():
        return pltpu.make_async_copy(x_hbm.at[pl.ds(i * tm, tm), :], xres_ref, sem)

    @pl.when(f == 0)
    def _():
        x_copy().start()
        halo = halo_ref[...]
        hx_ref[0:pad, :] = jnp.where(i % tiles_per_seq == 0, jnp.zeros_like(halo), halo)
        hx_ref[pad:pad + tm, :] = h_ref[...]
        o_ref[...] = jnp.zeros_like(o_ref)

    cw = cw_ref[...]
    keep = SUBLANES
    tail = None
    for r in range(tm // rs):
        rows = slice(r * rs, (r + 1) * rs)
        hrows = slice(pad + r * rs, pad + (r + 1) * rs)
        if r == 0:
            gp = jnp.dot(hx_ref[0:pad + rs, :], wg_ref[...], preferred_element_type=F32)[pad - keep:]
        else:
            gp = jnp.concatenate(
                [tail, jnp.dot(hx_ref[hrows, :], wg_ref[...], preferred_element_type=F32)], axis=0)
        tail = gp[rs:rs + keep]
        up = jnp.dot(hx_ref[hrows, :], wu_ref[...], preferred_element_type=F32)
        gate = gp[keep:keep + rs] * cw[2:3] + gp[keep - 1:keep - 1 + rs] * cw[1:2] \
            + gp[keep - 2:keep - 2 + rs] * cw[0:1] + cb_ref[...]
        act = _silu(gate) * up
        o_ref[rows, :] += jnp.dot(act.astype(BF16), wd_ref[...], preferred_element_type=F32)

    @pl.when(f == pl.num_programs(1) - 1)
    def _():
        x_copy().wait()
        def finish(r, carry):
            rows = pl.ds(pl.multiple_of(r * HEAD_DIM, HEAD_DIM), HEAD_DIM)
            o_ref[rows, :] = _rms(xres_ref[rows, :] + o_ref[rows, :], fg_ref[...])
            return carry

        lax.fori_loop(0, tm // HEAD_DIM, finish, 0)


def _ffn(x, h, wg, wu, cw, cb, wd, fg, *, seq, tm, tf):
    m, d = x.shape
    ff = wg.shape[1]
    assert seq % tm == 0 and ff % tf == 0 and tm % BF16_ROWS == 0
    hb = tm // BF16_ROWS
    return pl.pallas_call(
        functools.partial(_ffn_kernel, tiles_per_seq=seq // tm, rs=tm),
        grid=(m // tm, ff // tf),
        in_specs=[pl.BlockSpec(memory_space=pl.ANY),
                  pl.BlockSpec((tm, d), lambda i, f: (i, 0)),
                  pl.BlockSpec((BF16_ROWS, d), lambda i, f: (jnp.maximum(i * hb - 1, 0), 0)),
                  pl.BlockSpec((d, tf), lambda i, f: (0, f)),
                  pl.BlockSpec((d, tf), lambda i, f: (0, f)),
                  pl.BlockSpec((FFN_CONV, tf), lambda i, f: (0, f)),
                  pl.BlockSpec((1, tf), lambda i, f: (0, f)),
                  pl.BlockSpec((tf, d), lambda i, f: (f, 0)),
                  pl.BlockSpec((1, d), lambda i, f: (0, 0))],
        out_specs=pl.BlockSpec((tm, d), lambda i, f: (i, 0)),
        out_shape=jax.ShapeDtypeStruct((m, d), F32),
        scratch_shapes=[pltpu.VMEM((tm + BF16_ROWS, d), BF16), pltpu.VMEM((tm, d), F32),
                        pltpu.SemaphoreType.DMA(())],
        compiler_params=pltpu.CompilerParams(
            dimension_semantics=("arbitrary", "arbitrary"), vmem_limit_bytes=VMEM_LIMIT),
        name="ffn",
    )(x, h, h, wg, wu, cw, cb.reshape(1, ff), wd, fg.reshape(1, d))


def _layer(x, mem, mix_norm_g, w_in, gdn_conv_w, a_log, dt_bias, gdn_norm_g, moba_norm_g, rel_bias,
           w_out, xattn_norm_g, mem_norm_g, w_xq, w_xkv, w_xo, ffn_norm_g, w_gate, w_up, ffn_conv_w,
           ffn_conv_b, w_down, final_g, *, last):
    b, s, d = x.shape
    m = b * s
    x2 = x.reshape(m, d)
    i1 = 4 * GDN_WIDTH
    i3 = i1 + 2 * N_GDN_HEADS
    w_all = w_in.astype(BF16)
    w_b = w_all[:, i3:]
    tm = min(1024, m)
    wa_cols = i1 + HEAD_DIM
    pa = _norm_matmul(x2, mix_norm_g, w_all, tm=tm, tn=wa_cols // 3, n=wa_cols).reshape(b, s, wa_cols)
    pb = _norm_matmul(x2, mix_norm_g, w_b, tm=tm, tn=1024).reshape(b, s, 3 * MOBA_WIDTH)
    o_a = _gdn(pa, gdn_conv_w, a_log, dt_bias, gdn_norm_g)
    o_b = _moba(pb, rel_bias, moba_norm_g)
    mlen = mem.shape[1]
    kv = _norm_matmul(mem.reshape(b * mlen, d), mem_norm_g, w_xkv.astype(BF16),
                      tm=min(512, b * mlen), tn=1024, out_dtype=BF16).reshape(b, mlen, -1)
    x2b, h2 = _xattn(x, o_a, o_b, w_out[:GDN_WIDTH].astype(BF16), w_out[GDN_WIDTH:].astype(BF16),
                     xattn_norm_g, w_xq.astype(BF16), kv, w_xo.astype(BF16), ffn_norm_g, ts=min(512, s))
    assert last, "the final rmsnorm is fused into the last layer's ffn"
    y = _ffn(x2b.reshape(m, d), h2.reshape(m, d), w_gate.astype(BF16), w_up.astype(BF16), ffn_conv_w,
             ffn_conv_b, w_down.astype(BF16), final_g, seq=s, tm=min(1024, s), tf=512)
    return y.reshape(b, s, d)


def kernel(x, mem, mix_norm_g, w_in, gdn_conv_w, gdn_a_log, gdn_dt_bias, gdn_norm_g, moba_norm_g,
           rel_bias, w_out, xattn_norm_g, mem_norm_g, w_xq, w_xkv, w_xo, ffn_norm_g, w_gate, w_up,
           ffn_conv_w, ffn_conv_b, w_down, final_norm_g):
    depth = mix_norm_g.shape[0]
    assert depth == 1
    l = 0
    return _layer(x, mem, mix_norm_g[l], w_in[l], gdn_conv_w[l], gdn_a_log[l], gdn_dt_bias[l],
                  gdn_norm_g[l], moba_norm_g[l], rel_bias, w_out[l], xattn_norm_g[l], mem_norm_g[l],
                  w_xq[l], w_xkv[l], w_xo[l], ffn_norm_g[l], w_gate[l], w_up[l], ffn_conv_w[l],
                  ffn_conv_b[l], w_down[l], final_norm_g, last=True)
```

```python
import functools
import math

import jax
import jax.numpy as jnp
import numpy as np
from jax import lax
from jax.experimental import pallas as pl
from jax.experimental.pallas import tpu as pltpu

HEAD_DIM = 128
N_GDN_HEADS = 8
N_MOBA_HEADS = 8
GDN_WIDTH = N_GDN_HEADS * HEAD_DIM
MOBA_WIDTH = N_MOBA_HEADS * HEAD_DIM
GDN_CONV = 4
GDN_CHUNK = 256
GDN_HALF_ROWS_FROM = 64
MOBA_BLOCK = 256
MOBA_TOPK = 3
MOBA_TILES_IN_FLIGHT = 4
REL_BUCKETS = 32
REL_MAX_DIST = 128
N_XATTN_HEADS = 4
FFN_CONV = 3
EPS = 1e-6
NEG = -1e30
LOG2E = math.log2(math.e)
SUBLANES = 8
BF16_ROWS = 16
ROW_SUB = 256
VMEM_LIMIT = 58 * 1024 * 1024

F32 = jnp.float32
BF16 = jnp.bfloat16


def _bdot(a, b):
    return jnp.dot(a.astype(BF16), b.astype(BF16), preferred_element_type=F32)


def _bdot_nt(a, b):
    return lax.dot_general(a.astype(BF16), b.astype(BF16), (((1,), (1,)), ((), ())),
                           preferred_element_type=F32)


def _bdot_tn(a, b):
    return lax.dot_general(a.astype(BF16), b.astype(BF16), (((0,), (0,)), ((), ())),
                           preferred_element_type=F32)


def _fdot(a, b):
    return jnp.dot(a, b, preferred_element_type=F32, precision=lax.Precision.HIGHEST)


def _sigmoid(x):
    return 1.0 / (1.0 + jnp.exp(-x))


def _silu(x):
    return x * _sigmoid(x)


def _rms(x, g):
    return x * lax.rsqrt(jnp.mean(x * x, axis=-1, keepdims=True) + EPS) * g


def _norm_matmul_kernel(x_ref, g_ref, w_ref, o_ref, *, rs):
    for r in range(x_ref.shape[0] // rs):
        rows = slice(r * rs, (r + 1) * rs)
        hn = _rms(x_ref[rows, :], g_ref[...]).astype(BF16)
        o_ref[rows, :] = jnp.dot(hn, w_ref[...], preferred_element_type=F32).astype(o_ref.dtype)


def _norm_matmul(x, g, w, *, tm, tn, n=None, out_dtype=F32):
    m, k = x.shape
    n = w.shape[1] if n is None else n
    assert m % tm == 0 and n % tn == 0 and n <= w.shape[1]
    rs = math.gcd(tm, ROW_SUB)
    return pl.pallas_call(
        functools.partial(_norm_matmul_kernel, rs=rs),
        grid=(m // tm, n // tn),
        in_specs=[pl.BlockSpec((tm, k), lambda i, j: (i, 0)),
                  pl.BlockSpec((1, k), lambda i, j: (0, 0)),
                  pl.BlockSpec((k, tn), lambda i, j: (0, j))],
        out_specs=pl.BlockSpec((tm, tn), lambda i, j: (i, j)),
        out_shape=jax.ShapeDtypeStruct((m, n), out_dtype),
        compiler_params=pltpu.CompilerParams(
            dimension_semantics=("parallel", "arbitrary"), vmem_limit_bytes=VMEM_LIMIT),
        name="norm_matmul",
    )(x, g.reshape(1, k), w)


def _unit_lower_inverse(mats, row, col):
    c = mats[0].shape[0]
    eye = (row == col).astype(F32)
    blk = lambda n: (row // n) == (col // n)
    inner = blk(16)
    ds = [jnp.where(inner, a, 0.0) for a in mats]
    ts = [eye - d for d in ds]
    ps = [_bdot(d, d) for d in ds]
    for step in range(3):
        ts = [t + _bdot(t, p) for t, p in zip(ts, ps)]
        if step < 2:
            ps = [_bdot(p, p) for p in ps]
    half_row = lax.broadcasted_iota(jnp.int32, (c // 2, c), 0)
    half_col = lax.broadcasted_iota(jnp.int32, (c // 2, c), 1)
    n = 32
    while n <= c:
        h = n // 2
        if n < GDN_HALF_ROWS_FROM:
            outer = blk(n) if n < c else None
            keep = ~inner if outer is None else (outer & ~inner)
            xs = [_bdot(jnp.where(keep, a, 0.0), t) for a, t in zip(mats, ts)]
            ts = [t - _bdot(t, x) for t, x in zip(ts, xs)]
        else:
            pairs = range(c // n)
            lower = lambda m: jnp.concatenate([m[(2 * k + 1) * h:(2 * k + 2) * h] for k in pairs], axis=0)
            left = (half_col // h) == 2 * (half_row // h)
            ys = [_bdot(jnp.where(left, lower(a), 0.0), t) for a, t in zip(mats, ts)]
            zero = jnp.zeros((h, c), F32)
            spread = lambda y: jnp.concatenate(
                [part for k in pairs for part in (zero, y[k * h:(k + 1) * h])], axis=0)
            zs = [_bdot(lower(t), spread(y)) for t, y in zip(ts, ys)]
            ts = [jnp.concatenate([part for k in pairs for part in
                                   (t[2 * k * h:(2 * k + 1) * h],
                                    t[(2 * k + 1) * h:(2 * k + 2) * h] - z[k * h:(k + 1) * h])], axis=0)
                  for t, z in zip(ts, zs)]
        inner = blk(n) if n < c else None
        n *= 2
    return ts


def _gdn_kernel(alog_ref, dtb_ref, q_ref, k_ref, v_ref, z_ref, bat_ref, wq_ref, wk_ref, wv_ref,
                ng_ref, o_ref, qp_ref, kp_ref, vp_ref, u_ref, wqd_ref, qk_ref, kw_ref, bc_ref, gl_ref,
                sb_ref, *, group):
    h = pl.program_id(1)
    seq = q_ref.shape[1]
    C = GDN_CHUNK
    D = HEAD_DIM
    pad = SUBLANES
    n_chunks = seq // C

    for src, dst in ((q_ref, qp_ref), (k_ref, kp_ref), (v_ref, vp_ref)):
        dst[0:pad, :] = jnp.zeros((pad, D), F32)
        dst[pad:pad + seq, :] = src[0]

    row = lax.broadcasted_iota(jnp.int32, (C, C), 0)
    col = lax.broadcasted_iota(jnp.int32, (C, C), 1)
    tri_incl = row >= col
    tri_strict = row > col
    lane = lax.broadcasted_iota(jnp.int32, (SUBLANES, C), 1)
    neg_a = -jnp.exp(jnp.full((1, C), alog_ref[h], F32))
    dt_bias = dtb_ref[h]
    scale = D ** -0.5

    def conv_silu(xp_ref, w_ref, r0):
        win = xp_ref[pl.ds(r0, C + pad), :]
        w = w_ref[...]
        y = win[pad:pad + C] * w[3:4]
        for j in range(GDN_CONV - 1):
            s = GDN_CONV - 1 - j
            y = y + win[pad - s:pad - s + C] * w[j:j + 1]
        return _silu(y)

    def l2n(x):
        return x * lax.rsqrt(jnp.sum(x * x, axis=-1, keepdims=True) + EPS)

    def cumsum_lanes(x):
        x = jnp.broadcast_to(x, (SUBLANES, C))
        s = 1
        while s < C:
            x = x + jnp.where(lane >= s, pltpu.roll(x, s, axis=1), 0.0)
            s *= 2
        return x[0:1, :]

    reps = C // D

    def rows_to_cols(x):
        return jnp.concatenate(
            [jnp.broadcast_to(x[:, n * D:(n + 1) * D], (D, D)).T for n in range(reps)], axis=0)

    def prepare(grp, carry):
        cs = [grp * group + i for i in range(group)]
        r0s = [pl.multiple_of(c * C, C) for c in cs]
        pre, gbs, g_rows = [], [], []
        for c, r0 in zip(cs, r0s):
            q = l2n(conv_silu(qp_ref, wq_ref, r0)) * scale
            k = l2n(conv_silu(kp_ref, wk_ref, r0))
            v = conv_silu(vp_ref, wv_ref, r0)
            b_row = bat_ref[0, h, pl.ds(c, 1), :]
            xs = bat_ref[0, h + N_GDN_HEADS, pl.ds(c, 1), :] + dt_bias
            softplus = jnp.maximum(xs, 0.0) + jnp.log1p(jnp.exp(-jnp.abs(xs)))
            g_row = cumsum_lanes(neg_a * softplus)
            pre.append((q, k, v, rows_to_cols(_sigmoid(b_row)), None))
            g_rows.append(g_row)
            gbs.append(rows_to_cols(g_row))
        kk_qks = [_bdot_nt(jnp.concatenate([k * beta, q], axis=0), k) for q, k, v, beta, _ in pre]
        mats, mids = [], []
        for (q, k, v, beta, _), gb, g_row, kk_qk in zip(pre, gbs, g_rows, kk_qks):
            g_i = jnp.concatenate([gb] * reps, axis=1)
            g_j = jnp.broadcast_to(g_row, (C, C))
            decay = jnp.exp(jnp.where(tri_incl, g_i - g_j, NEG))
            mats.append(jnp.where(tri_strict, kk_qk[:C] * decay, 0.0))
            mids.append((kk_qk[C:] * decay, jnp.exp(gb), gb[C - 1:C, :]))
        ts = _unit_lower_inverse(mats, row, col)
        uws = [_bdot(t, jnp.concatenate([v * beta, k * beta * eg], axis=1))
               for t, (q, k, v, beta, _), (_, eg, _) in zip(ts, pre, mids)]
        transs = [_bdot_tn(k * jnp.exp(g_last - gb), uw)
                  for uw, (q, k, v, beta, _), gb, (_, _, g_last) in zip(uws, pre, gbs, mids)]
        for c, r0, uw, trans, (q, k, v, beta, _), (qk, eg, g_last) in zip(cs, r0s, uws, transs, pre, mids):
            u_ref[pl.ds(r0, C), :] = uw[:, :D]
            wqd_ref[c, 0:C, :] = uw[:, D:].astype(BF16)
            wqd_ref[c, C:2 * C, :] = (q * eg).astype(BF16)
            qk_ref[pl.ds(r0, C), :] = qk.astype(BF16)
            bc_ref[c] = trans[:, :D]
            kw_ref[c] = trans[:, D:].astype(BF16)
            gl_ref[c] = jnp.exp(g_last)
        return carry

    lax.fori_loop(0, n_chunks // group, prepare, 0)

    def chain(c, state):
        sb = state.astype(BF16)
        sb_ref[c] = sb
        return state * gl_ref[c] - jnp.dot(kw_ref[c], sb, preferred_element_type=F32) + bc_ref[c]

    lax.fori_loop(0, n_chunks, chain, jnp.zeros((D, D), F32))

    def outputs(grp, carry):
        cs = [grp * group + i for i in range(group)]
        r0s = [pl.multiple_of(c * C, C) for c in cs]
        ws_qs = [jnp.dot(wqd_ref[c], sb_ref[c], preferred_element_type=F32) for c in cs]
        v_news = [(u_ref[pl.ds(r0, C), :] - wq[:C]).astype(BF16) for r0, wq in zip(r0s, ws_qs)]
        os = [wq[C:] + jnp.dot(qk_ref[pl.ds(r0, C), :], vn, preferred_element_type=F32)
              for r0, wq, vn in zip(r0s, ws_qs, v_news)]
        for r0, o in zip(r0s, os):
            z = z_ref[0, pl.ds(r0, C), :]
            o_ref[0, pl.ds(r0, C), :] = (_rms(o, ng_ref[...]) * _silu(z)).astype(o_ref.dtype)
        return carry

    lax.fori_loop(0, n_chunks // group, outputs, 0)


def _gdn(pa, conv_w, a_log, dt_bias, norm_g, *, group=8):
    b, s, _ = pa.shape
    H = N_GDN_HEADS
    C = GDN_CHUNK
    assert s % C == 0
    n_chunks = s // C
    bat = pa[:, :, 4 * GDN_WIDTH:4 * GDN_WIDTH + 2 * H].transpose(0, 2, 1).reshape(b, 2 * H, n_chunks, C)
    head = lambda off: pl.BlockSpec((1, s, HEAD_DIM), lambda i, j: (i, 0, off + j))
    cw = lambda off: pl.BlockSpec((GDN_CONV, HEAD_DIM), lambda i, j: (0, off + j))
    smem = pl.BlockSpec(memory_space=pltpu.SMEM)
    return pl.pallas_call(
        functools.partial(_gdn_kernel, group=math.gcd(group, n_chunks)),
        grid=(b, H),
        in_specs=[smem, smem, head(0), head(H), head(2 * H), head(3 * H),
                  pl.BlockSpec((1, 2 * H, n_chunks, C), lambda i, j: (i, 0, 0, 0)),
                  cw(0), cw(H), cw(2 * H),
                  pl.BlockSpec((1, HEAD_DIM), lambda i, j: (0, 0))],
        out_specs=pl.BlockSpec((1, s, HEAD_DIM), lambda i, j: (i, 0, j)),
        out_shape=jax.ShapeDtypeStruct((b, s, GDN_WIDTH), BF16),
        scratch_shapes=[pltpu.VMEM((s + SUBLANES, HEAD_DIM), F32)] * 3 + [
            pltpu.VMEM((s, HEAD_DIM), F32),
            pltpu.VMEM((n_chunks, 2 * C, HEAD_DIM), BF16),
            pltpu.VMEM((s, C), BF16),
            pltpu.VMEM((n_chunks, HEAD_DIM, HEAD_DIM), BF16),
            pltpu.VMEM((n_chunks, HEAD_DIM, HEAD_DIM), F32),
            pltpu.VMEM((n_chunks, 1, HEAD_DIM), F32),
            pltpu.VMEM((n_chunks, HEAD_DIM, HEAD_DIM), BF16)],
        compiler_params=pltpu.CompilerParams(
            dimension_semantics=("parallel", "arbitrary"), vmem_limit_bytes=VMEM_LIMIT),
        name="gdn",
    )(a_log, dt_bias, pa, pa, pa, pa, bat, conv_w, conv_w, conv_w, norm_g.reshape(1, HEAD_DIM))


def _bucket_upper_bounds():
    n = np.arange(0, 4 * REL_MAX_DIST, dtype=np.int64)
    max_exact = REL_BUCKETS // 2
    nf = np.maximum(n, 1).astype(np.float32)
    large = max_exact + (np.log(nf / np.float32(max_exact)) / np.float32(math.log(REL_MAX_DIST / max_exact))
                         * np.float32(REL_BUCKETS - max_exact)).astype(np.int32)
    large = np.minimum(large, REL_BUCKETS - 1)
    bucket = np.where(n < max_exact, n, large)
    assert np.all(np.diff(bucket) >= 0) and bucket[-1] == REL_BUCKETS - 1
    return [int(np.searchsorted(bucket, b, side="right")) for b in range(REL_BUCKETS - 1)]


_BUCKET_UPPER = _bucket_upper_bounds()


def _moba_kernel(rb_ref, q_ref, k_ref, v_ref, ng_ref, o_ref, kb_ref, vt_ref, km_ref, bd_ref, bl_ref):
    h = pl.program_id(0)
    b = pl.program_id(1)
    seq = k_ref.shape[1]
    T = MOBA_BLOCK
    nb = seq // T
    scale = HEAD_DIM ** -0.5 * LOG2E
    kk = lax.broadcasted_iota(jnp.int32, (T, T), 0)
    qq = lax.broadcasted_iota(jnp.int32, (T, T), 1)

    @pl.when(b == 0)
    def _():
        def bias_of(n):
            val = jnp.full((T, T), rb_ref[REL_BUCKETS - 1, h], F32)
            for bkt in range(REL_BUCKETS - 2, -1, -1):
                val = jnp.where(n < _BUCKET_UPPER[bkt], rb_ref[bkt, h], val)
            return val
        bd_ref[...] = jnp.where(qq >= kk, bias_of(qq - kk) * LOG2E, NEG)
        bl_ref[...] = bias_of(qq - kk + T) * LOG2E

    kb_ref[...] = k_ref[0].astype(BF16)
    for n in range(nb):
        blk_rows = slice(n * T, (n + 1) * T)
        vt_ref[:, blk_rows] = v_ref[0, blk_rows, :].astype(F32).T.astype(BF16)
        km_ref[n:n + 1, :] = jnp.mean(k_ref[0, blk_rows, :].astype(F32), axis=0, keepdims=True)
    far_bias = rb_ref[REL_BUCKETS - 1, h] * LOG2E

    def additive_mask(i, qf):
        if i <= MOBA_TOPK:
            return None
        gate = lax.dot_general(km_ref[0:i, :], qf, (((1,), (1,)), ((), ())),
                               preferred_element_type=F32, precision=lax.Precision.HIGHEST)
        blk = lax.broadcasted_iota(jnp.int32, (i, T), 0)
        rank = jnp.zeros((i, T), jnp.int32)
        for m in range(i):
            gm = gate[m:m + 1, :]
            rank = rank + jnp.where((gm > gate) | ((gm == gate) & (m < blk)), 1, 0)
        return jnp.where(rank < MOBA_TOPK, 0.0, NEG)

    def biased(i, s_all, keep):
        parts = []
        for j in range(i + 1):
            sj = s_all[j * T:(j + 1) * T, :]
            if j == i:
                sj = sj + bd_ref[...]
            elif j == i - 1:
                sj = sj + bl_ref[...]
                if keep is not None:
                    sj = sj + keep[j:j + 1, :]
            else:
                sj = sj + (far_bias if keep is None else keep[j:j + 1, :] + far_bias)
            parts.append(sj)
        return parts

    order = [t for pair in zip(range(nb // 2), range(nb - 1, nb // 2 - 1, -1)) for t in pair]
    if nb % 2:
        order.append(nb // 2)
    for g0 in range(0, nb, MOBA_TILES_IN_FLIGHT):
        tiles = order[g0:g0 + MOBA_TILES_IN_FLIGHT]
        qfs = [q_ref[0, i * T:(i + 1) * T, :].astype(F32) for i in tiles]
        keeps = [additive_mask(i, qf) for i, qf in zip(tiles, qfs)]
        s_alls = [_bdot_nt(kb_ref[0:(i + 1) * T, :], qf * scale) for i, qf in zip(tiles, qfs)]
        partss = [biased(i, s_all, keep) for i, s_all, keep in zip(tiles, s_alls, keeps)]
        m_rows = [jnp.max(functools.reduce(jnp.maximum, parts), axis=0, keepdims=True) for parts in partss]
        pss = [[jnp.exp2(sj - m_row) for sj in parts] for parts, m_row in zip(partss, m_rows)]
        l_rows = [jnp.sum(functools.reduce(jnp.add, ps), axis=0, keepdims=True) for ps in pss]
        p_alls = [jnp.concatenate([p.astype(BF16) for p in ps], axis=0) for ps in pss]
        o_ts = [jnp.dot(vt_ref[:, 0:(i + 1) * T], p_all, preferred_element_type=F32) / l_row
                for i, p_all, l_row in zip(tiles, p_alls, l_rows)]
        for i, o_t in zip(tiles, o_ts):
            o_ref[0, i * T:(i + 1) * T, :] = _rms(o_t.T, ng_ref[...]).astype(o_ref.dtype)


def _moba(pb, rel_bias, norm_g):
    b, s, _ = pb.shape
    H = N_MOBA_HEADS
    T = MOBA_BLOCK
    assert s % T == 0
    nb = s // T
    head = lambda off: pl.BlockSpec((1, s, HEAD_DIM), lambda h, bb: (bb, 0, off + h))
    return pl.pallas_call(
        _moba_kernel,
        grid=(H, b),
        in_specs=[pl.BlockSpec(memory_space=pltpu.SMEM), head(0), head(H), head(2 * H),
                  pl.BlockSpec((1, HEAD_DIM), lambda h, bb: (0, 0))],
        out_specs=pl.BlockSpec((1, s, HEAD_DIM), lambda h, bb: (bb, 0, h)),
        out_shape=jax.ShapeDtypeStruct((b, s, MOBA_WIDTH), BF16),
        scratch_shapes=[pltpu.VMEM((s, HEAD_DIM), BF16), pltpu.VMEM((HEAD_DIM, s), BF16),
                        pltpu.VMEM((nb, HEAD_DIM), F32),
                        pltpu.VMEM((T, T), F32), pltpu.VMEM((T, T), F32)],
        compiler_params=pltpu.CompilerParams(
            dimension_semantics=("arbitrary", "arbitrary"), vmem_limit_bytes=VMEM_LIMIT),
        name="moba",
    )(rel_bias, pb, pb, pb, norm_g.reshape(1, HEAD_DIM))


def _xattn_kernel(x_ref, oa_ref, ob_ref, wa_ref, wb_ref, g_ref, wq_ref, kv_ref, wo_ref, ng_ref,
                  o_ref, hn_ref, *, rs):
    width = N_XATTN_HEADS * HEAD_DIM
    scale = HEAD_DIM ** -0.5 * LOG2E
    blocks = [slice(r * rs, (r + 1) * rs) for r in range(x_ref.shape[1] // rs)]
    x1s = [x_ref[0, rows, :]
           + jnp.dot(oa_ref[0, rows, :], wa_ref[...], preferred_element_type=F32)
           + jnp.dot(ob_ref[0, rows, :], wb_ref[...], preferred_element_type=F32) for rows in blocks]
    qs = [_bdot(_rms(x1, g_ref[...]), wq_ref[...]) * scale for x1 in x1s]
    outs = [[] for _ in blocks]
    for hd in range(N_XATTN_HEADS):
        sl = slice(hd * HEAD_DIM, (hd + 1) * HEAD_DIM)
        ss = [_bdot_nt(q[:, sl], kv_ref[0, :, sl]) for q in qs]
        ps = [jnp.exp2(s - jnp.max(s, axis=-1, keepdims=True)) for s in ss]
        for out, p in zip(outs, ps):
            l = jnp.sum(p, axis=-1, keepdims=True)
            out.append(_bdot(p, kv_ref[0, :, width + hd * HEAD_DIM: width + (hd + 1) * HEAD_DIM]) / l)
    ys = [x1 + _bdot(jnp.concatenate(out, axis=-1), wo_ref[...]) for x1, out in zip(x1s, outs)]
    for rows, y in zip(blocks, ys):
        o_ref[0, rows, :] = y
        hn_ref[0, rows, :] = _rms(y, ng_ref[...]).astype(BF16)


def _xattn(x, oa, ob, wa, wb, g, wq, kv, wo, next_g, *, ts):
    b, s, d = x.shape
    mlen = kv.shape[1]
    width = wq.shape[1]
    ka, kb = oa.shape[2], ob.shape[2]
    tile = pl.BlockSpec((1, ts, d), lambda i, j: (i, j, 0))
    vec = pl.BlockSpec((1, d), lambda i, j: (0, 0))
    whole = lambda r, c: pl.BlockSpec((r, c), lambda i, j: (0, 0))
    return pl.pallas_call(
        functools.partial(_xattn_kernel, rs=math.gcd(ts, ROW_SUB)),
        grid=(b, s // ts),
        in_specs=[tile,
                  pl.BlockSpec((1, ts, ka), lambda i, j: (i, j, 0)),
                  pl.BlockSpec((1, ts, kb), lambda i, j: (i, j, 0)),
                  whole(ka, d), whole(kb, d), vec, whole(d, width),
                  pl.BlockSpec((1, mlen, 2 * width), lambda i, j: (i, 0, 0)),
                  whole(width, d), vec],
        out_specs=[tile, tile],
        out_shape=[jax.ShapeDtypeStruct((b, s, d), F32), jax.ShapeDtypeStruct((b, s, d), BF16)],
        compiler_params=pltpu.CompilerParams(
            dimension_semantics=("parallel", "parallel"), vmem_limit_bytes=VMEM_LIMIT),
        name="xattn",
    )(x, oa, ob, wa, wb, g.reshape(1, d), wq, kv, wo, next_g.reshape(1, d))


def _ffn_kernel(x_hbm, h_ref, halo_ref, wg_ref, wu_ref, cw_ref, cb_ref, wd_ref, fg_ref, o_ref,
                hx_ref, xres_ref, sem, *, tiles_per_seq, rs):
    i = pl.program_id(0)
    f = pl.program_id(1)
    tm = h_ref.shape[0]
    pad = BF16_ROWS

    def x_copy():
        return pltpu.make_async_copy(x_hbm.at[pl.ds(i * tm, tm), :], xres_ref, sem)

    @pl.when(f == 0)
    def _():
        x_copy().start()
        halo = halo_ref[...]
        hx_ref[0:pad, :] = jnp.where(i % tiles_per_seq == 0, jnp.zeros_like(halo), halo)
        hx_ref[pad:pad + tm, :] = h_ref[...]
        o_ref[...] = jnp.zeros_like(o_ref)

    cw = cw_ref[...]
    keep = SUBLANES
    tail = None
    for r in range(tm // rs):
        rows = slice(r * rs, (r + 1) * rs)
        hrows = slice(pad + r * rs, pad + (r + 1) * rs)
        if r == 0:
            gp = jnp.dot(hx_ref[0:pad + rs, :], wg_ref[...], preferred_element_type=F32)[pad - keep:]
        else:
            gp = jnp.concatenate(
                [tail, jnp.dot(hx_ref[hrows, :], wg_ref[...], preferred_element_type=F32)], axis=0)
        tail = gp[rs:rs + keep]
        up = jnp.dot(hx_ref[hrows, :], wu_ref[...], preferred_element_type=F32)
        gate = gp[keep:keep + rs] * cw[2:3] + gp[keep - 1:keep - 1 + rs] * cw[1:2] \
            + gp[keep - 2:keep - 2 + rs] * cw[0:1] + cb_ref[...]
        act = _silu(gate) * up
        o_ref[rows, :] += jnp.dot(act.astype(BF16), wd_ref[...], preferred_element_type=F32)

    @pl.when(f == pl.num_programs(1) - 1)
    def _():
        x_copy().wait()
        def finish(r, carry):
            rows = pl.ds(pl.multiple_of(r * HEAD_DIM, HEAD_DIM), HEAD_DIM)
            o_ref[rows, :] = _rms(xres_ref[rows, :] + o_ref[rows, :], fg_ref[...])
            return carry

        lax.fori_loop(0, tm // HEAD_DIM, finish, 0)


def _ffn(x, h, wg, wu, cw, cb, wd, fg, *, seq, tm, tf):
    m, d = x.shape
    ff = wg.shape[1]
    assert seq % tm == 0 and ff % tf == 0 and tm % BF16_ROWS == 0
    hb = tm // BF16_ROWS
    return pl.pallas_call(
        functools.partial(_ffn_kernel, tiles_per_seq=seq // tm, rs=tm),
        grid=(m // tm, ff // tf),
        in_specs=[pl.BlockSpec(memory_space=pl.ANY),
                  pl.BlockSpec((tm, d), lambda i, f: (i, 0)),
                  pl.BlockSpec((BF16_ROWS, d), lambda i, f: (jnp.maximum(i * hb - 1, 0), 0)),
                  pl.BlockSpec((d, tf), lambda i, f: (0, f)),
                  pl.BlockSpec((d, tf), lambda i, f: (0, f)),
                  pl.BlockSpec((FFN_CONV, tf), lambda i, f: (0, f)),
                  pl.BlockSpec((1, tf), lambda i, f: (0, f)),
                  pl.BlockSpec((tf, d), lambda i, f: (f, 0)),
                  pl.BlockSpec((1, d), lambda i, f: (0, 0))],
        out_specs=pl.BlockSpec((tm, d), lambda i, f: (i, 0)),
        out_shape=jax.ShapeDtypeStruct((m, d), F32),
        scratch_shapes=[pltpu.VMEM((tm + BF16_ROWS, d), BF16), pltpu.VMEM((tm, d), F32),
                        pltpu.SemaphoreType.DMA(())],
        compiler_params=pltpu.CompilerParams(
            dimension_semantics=("arbitrary", "arbitrary"), vmem_limit_bytes=VMEM_LIMIT),
        name="ffn",
    )(x, h, h, wg, wu, cw, cb.reshape(1, ff), wd, fg.reshape(1, d))


def _layer(x, mem, mix_norm_g, w_in, gdn_conv_w, a_log, dt_bias, gdn_norm_g, moba_norm_g, rel_bias,
           w_out, xattn_norm_g, mem_norm_g, w_xq, w_xkv, w_xo, ffn_norm_g, w_gate, w_up, ffn_conv_w,
           ffn_conv_b, w_down, final_g, *, last):
    b, s, d = x.shape
    m = b * s
    x2 = x.reshape(m, d)
    i1 = 4 * GDN_WIDTH
    i3 = i1 + 2 * N_GDN_HEADS
    w_all = w_in.astype(BF16)
    w_b = w_all[:, i3:]
    tm = min(1024, m)
    wa_cols = i1 + HEAD_DIM
    pa = _norm_matmul(x2, mix_norm_g, w_all, tm=tm, tn=wa_cols // 3, n=wa_cols).reshape(b, s, wa_cols)
    pb = _norm_matmul(x2, mix_norm_g, w_b, tm=tm, tn=1024, out_dtype=BF16).reshape(b, s, 3 * MOBA_WIDTH)
    o_a = _gdn(pa, gdn_conv_w, a_log, dt_bias, gdn_norm_g)
    o_b = _moba(pb, rel_bias, moba_norm_g)
    mlen = mem.shape[1]
    kv = _norm_matmul(mem.reshape(b * mlen, d), mem_norm_g, w_xkv.astype(BF16),
                      tm=min(512, b * mlen), tn=1024, out_dtype=BF16).reshape(b, mlen, -1)
    x2b, h2 = _xattn(x, o_a, o_b, w_out[:GDN_WIDTH].astype(BF16), w_out[GDN_WIDTH:].astype(BF16),
                     xattn_norm_g, w_xq.astype(BF16), kv, w_xo.astype(BF16), ffn_norm_g, ts=min(512, s))
    assert last, "the final rmsnorm is fused into the last layer's ffn"
    y = _ffn(x2b.reshape(m, d), h2.reshape(m, d), w_gate.astype(BF16), w_up.astype(BF16), ffn_conv_w,
             ffn_conv_b, w_down.astype(BF16), final_g, seq=s, tm=min(1024, s), tf=512)
    return y.reshape(b, s, d)


def kernel(x, mem, mix_norm_g, w_in, gdn_conv_w, gdn_a_log, gdn_dt_bias, gdn_norm_g, moba_norm_g,
           rel_bias, w_out, xattn_norm_g, mem_norm_g, w_xq, w_xkv, w_xo, ffn_norm_g, w_gate, w_up,
           ffn_conv_w, ffn_conv_b, w_down, final_norm_g):
    depth = mix_norm_g.shape[0]
    assert depth == 1
    l = 0
    return _layer(x, mem, mix_norm_g[l], w_in[l], gdn_conv_w[l], gdn_a_log[l], gdn_dt_bias[l],
                  gdn_norm_g[l], moba_norm_g[l], rel_bias, w_out[l], xattn_norm_g[l], mem_norm_g[l],
                  w_xq[l], w_xkv[l], w_xo[l], ffn_norm_g[l], w_gate[l], w_up[l], ffn_conv_w[l],
                  ffn_conv_b[l], w_down[l], final_norm_g, last=True)
```

```python
import functools
import math

import jax
import jax.numpy as jnp
import numpy as np
from jax import lax
from jax.experimental import pallas as pl
from jax.experimental.pallas import tpu as pltpu

HEAD_DIM = 128
N_GDN_HEADS = 8
N_MOBA_HEADS = 8
GDN_WIDTH = N_GDN_HEADS * HEAD_DIM
MOBA_WIDTH = N_MOBA_HEADS * HEAD_DIM
GDN_CONV = 4
GDN_CHUNK = 256
GDN_HALF_ROWS_FROM = 64
MOBA_BLOCK = 256
MOBA_TOPK = 3
MOBA_TILES_IN_FLIGHT = 4
REL_BUCKETS = 32
REL_MAX_DIST = 128
N_XATTN_HEADS = 4
FFN_CONV = 3
EPS = 1e-6
NEG = -1e30
LOG2E = math.log2(math.e)
SUBLANES = 8
BF16_ROWS = 16
ROW_SUB = 256
VMEM_LIMIT = 58 * 1024 * 1024

F32 = jnp.float32
BF16 = jnp.bfloat16


def _bdot(a, b):
    return jnp.dot(a.astype(BF16), b.astype(BF16), preferred_element_type=F32)


def _bdot_nt(a, b):
    return lax.dot_general(a.astype(BF16), b.astype(BF16), (((1,), (1,)), ((), ())),
                           preferred_element_type=F32)


def _bdot_tn(a, b):
    return lax.dot_general(a.astype(BF16), b.astype(BF16), (((0,), (0,)), ((), ())),
                           preferred_element_type=F32)


def _fdot(a, b):
    return jnp.dot(a, b, preferred_element_type=F32, precision=lax.Precision.HIGHEST)


def _sigmoid(x):
    return 1.0 / (1.0 + jnp.exp(-x))


def _silu(x):
    return x * _sigmoid(x)


def _rms(x, g):
    return x * lax.rsqrt(jnp.mean(x * x, axis=-1, keepdims=True) + EPS) * g


def _norm_matmul_kernel(x_ref, g_ref, w_ref, o_ref, *, rs):
    for r in range(x_ref.shape[0] // rs):
        rows = slice(r * rs, (r + 1) * rs)
        hn = _rms(x_ref[rows, :], g_ref[...]).astype(BF16)
        o_ref[rows, :] = jnp.dot(hn, w_ref[...], preferred_element_type=F32).astype(o_ref.dtype)


def _norm_matmul(x, g, w, *, tm, tn, n=None, out_dtype=F32):
    m, k = x.shape
    n = w.shape[1] if n is None else n
    assert m % tm == 0 and n % tn == 0 and n <= w.shape[1]
    rs = math.gcd(tm, ROW_SUB)
    return pl.pallas_call(
        functools.partial(_norm_matmul_kernel, rs=rs),
        grid=(m // tm, n // tn),
        in_specs=[pl.BlockSpec((tm, k), lambda i, j: (i, 0)),
                  pl.BlockSpec((1, k), lambda i, j: (0, 0)),
                  pl.BlockSpec((k, tn), lambda i, j: (0, j))],
        out_specs=pl.BlockSpec((tm, tn), lambda i, j: (i, j)),
        out_shape=jax.ShapeDtypeStruct((m, n), out_dtype),
        compiler_params=pltpu.CompilerParams(
            dimension_semantics=("parallel", "arbitrary"), vmem_limit_bytes=VMEM_LIMIT),
        name="norm_matmul",
    )(x, g.reshape(1, k), w)


def _unit_lower_inverse(mats, row, col):
    c = mats[0].shape[0]
    eye = (row == col).astype(F32)
    blk = lambda n: (row // n) == (col // n)
    inner = blk(16)
    ds = [jnp.where(inner, a, 0.0) for a in mats]
    ts = [eye - d for d in ds]
    ps = [_bdot(d, d) for d in ds]
    for step in range(3):
        ts = [t + _bdot(t, p) for t, p in zip(ts, ps)]
        if step < 2:
            ps = [_bdot(p, p) for p in ps]
    half_row = lax.broadcasted_iota(jnp.int32, (c // 2, c), 0)
    half_col = lax.broadcasted_iota(jnp.int32, (c // 2, c), 1)
    n = 32
    while n <= c:
        h = n // 2
        if n < GDN_HALF_ROWS_FROM:
            outer = blk(n) if n < c else None
            keep = ~inner if outer is None else (outer & ~inner)
            xs = [_bdot(jnp.where(keep, a, 0.0), t) for a, t in zip(mats, ts)]
            ts = [t - _bdot(t, x) for t, x in zip(ts, xs)]
        else:
            pairs = range(c // n)
            lower = lambda m: jnp.concatenate([m[(2 * k + 1) * h:(2 * k + 2) * h] for k in pairs], axis=0)
            left = (half_col // h) == 2 * (half_row // h)
            ys = [_bdot(jnp.where(left, lower(a), 0.0), t) for a, t in zip(mats, ts)]
            zero = jnp.zeros((h, c), F32)
            spread = lambda y: jnp.concatenate(
                [part for k in pairs for part in (zero, y[k * h:(k + 1) * h])], axis=0)
            zs = [_bdot(lower(t), spread(y)) for t, y in zip(ts, ys)]
            ts = [jnp.concatenate([part for k in pairs for part in
                                   (t[2 * k * h:(2 * k + 1) * h],
                                    t[(2 * k + 1) * h:(2 * k + 2) * h] - z[k * h:(k + 1) * h])], axis=0)
                  for t, z in zip(ts, zs)]
        inner = blk(n) if n < c else None
        n *= 2
    return ts


def _gdn_kernel(alog_ref, dtb_ref, q_ref, k_ref, v_ref, z_ref, bat_ref, wq_ref, wk_ref, wv_ref,
                ng_ref, o_ref, u_ref, wqd_ref, qk_ref, kw_ref, bc_ref, gl_ref, sb_ref, *, group):
    h = pl.program_id(1)
    seq = q_ref.shape[1]
    C = GDN_CHUNK
    D = HEAD_DIM
    pad = SUBLANES
    n_chunks = seq // C

    static_chunks = n_chunks == group
    row = lax.broadcasted_iota(jnp.int32, (C, C), 0)
    col = lax.broadcasted_iota(jnp.int32, (C, C), 1)
    tri_incl = row >= col
    tri_strict = row > col
    lane = lax.broadcasted_iota(jnp.int32, (SUBLANES, C), 1)
    neg_a = -jnp.exp(jnp.full((1, C), alog_ref[h], F32))
    dt_bias = dtb_ref[h]
    scale = D ** -0.5

    def conv_silu(x_ref, w_ref, c, r0):
        if static_chunks:
            win = (jnp.concatenate([jnp.zeros((pad, D), F32), x_ref[0, 0:C, :]], axis=0) if c == 0
                   else x_ref[0, r0 - pad:r0 + C, :])
        else:
            halo = x_ref[0, pl.ds(jnp.maximum(r0 - pad, 0), pad), :]
            win = jnp.concatenate([jnp.where(c > 0, halo, 0.0), x_ref[0, pl.ds(r0, C), :]], axis=0)
        w = w_ref[...]
        y = win[pad:pad + C] * w[3:4]
        for j in range(GDN_CONV - 1):
            s = GDN_CONV - 1 - j
            y = y + win[pad - s:pad - s + C] * w[j:j + 1]
        return _silu(y)

    def l2n(x):
        return x * lax.rsqrt(jnp.sum(x * x, axis=-1, keepdims=True) + EPS)

    def cumsum_lanes(x):
        x = jnp.broadcast_to(x, (SUBLANES, C))
        s = 1
        while s < C:
            x = x + jnp.where(lane >= s, pltpu.roll(x, s, axis=1), 0.0)
            s *= 2
        return x[0:1, :]

    reps = C // D

    def rows_to_cols(x):
        return jnp.concatenate(
            [jnp.broadcast_to(x[:, n * D:(n + 1) * D], (D, D)).T for n in range(reps)], axis=0)

    def chunk_ids(grp):
        cs = [grp * group + i for i in range(group)]
        return cs, [c * C if static_chunks else pl.multiple_of(c * C, C) for c in cs]

    def for_each_group(body):
        if static_chunks:
            body(0, 0)
        else:
            lax.fori_loop(0, n_chunks // group, body, 0)

    def prepare(grp, carry):
        cs, r0s = chunk_ids(grp)
        pre, gbs, g_rows = [], [], []
        for c, r0 in zip(cs, r0s):
            q = l2n(conv_silu(q_ref, wq_ref, c, r0)) * scale
            k = l2n(conv_silu(k_ref, wk_ref, c, r0))
            v = conv_silu(v_ref, wv_ref, c, r0)
            b_row = bat_ref[0, h, pl.ds(c, 1), :]
            xs = bat_ref[0, h + N_GDN_HEADS, pl.ds(c, 1), :] + dt_bias
            softplus = jnp.maximum(xs, 0.0) + jnp.log1p(jnp.exp(-jnp.abs(xs)))
            g_row = cumsum_lanes(neg_a * softplus)
            pre.append((q, k, v, rows_to_cols(_sigmoid(b_row)), None))
            g_rows.append(g_row)
            gbs.append(rows_to_cols(g_row))
        kk_qks = [_bdot_nt(jnp.concatenate([k * beta, q], axis=0), k) for q, k, v, beta, _ in pre]
        mats, mids = [], []
        for (q, k, v, beta, _), gb, g_row, kk_qk in zip(pre, gbs, g_rows, kk_qks):
            g_i = jnp.concatenate([gb] * reps, axis=1)
            g_j = jnp.broadcast_to(g_row, (C, C))
            decay = jnp.exp(jnp.where(tri_incl, g_i - g_j, NEG))
            mats.append(jnp.where(tri_strict, kk_qk[:C] * decay, 0.0))
            mids.append((kk_qk[C:] * decay, jnp.exp(gb), gb[C - 1:C, :]))
        ts = _unit_lower_inverse(mats, row, col)
        uws = [_bdot(t, jnp.concatenate([v * beta, k * beta * eg], axis=1))
               for t, (q, k, v, beta, _), (_, eg, _) in zip(ts, pre, mids)]
        transs = [_bdot_tn(k * jnp.exp(g_last - gb), uw)
                  for uw, (q, k, v, beta, _), gb, (_, _, g_last) in zip(uws, pre, gbs, mids)]
        for c, r0, uw, trans, (q, k, v, beta, _), (qk, eg, g_last) in zip(cs, r0s, uws, transs, pre, mids):
            u_ref[pl.ds(r0, C), :] = uw[:, :D]
            wqd_ref[c, 0:C, :] = uw[:, D:].astype(BF16)
            wqd_ref[c, C:2 * C, :] = (q * eg).astype(BF16)
            qk_ref[pl.ds(r0, C), :] = qk.astype(BF16)
            bc_ref[c] = trans[:, :D]
            kw_ref[c] = trans[:, D:].astype(BF16)
            gl_ref[c] = jnp.exp(g_last)
        return carry

    for_each_group(prepare)

    def chain(c, state):
        sb = state.astype(BF16)
        sb_ref[c] = sb
        return state * gl_ref[c] - jnp.dot(kw_ref[c], sb, preferred_element_type=F32) + bc_ref[c]

    lax.fori_loop(0, n_chunks, chain, jnp.zeros((D, D), F32))

    def outputs(grp, carry):
        cs, r0s = chunk_ids(grp)
        ws_qs = [jnp.dot(wqd_ref[c], sb_ref[c], preferred_element_type=F32) for c in cs]
        v_news = [(u_ref[pl.ds(r0, C), :] - wq[:C]).astype(BF16) for r0, wq in zip(r0s, ws_qs)]
        os = [wq[C:] + jnp.dot(qk_ref[pl.ds(r0, C), :], vn, preferred_element_type=F32)
              for r0, wq, vn in zip(r0s, ws_qs, v_news)]
        for r0, o in zip(r0s, os):
            z = z_ref[0, pl.ds(r0, C), :]
            o_ref[0, pl.ds(r0, C), :] = (_rms(o, ng_ref[...]) * _silu(z)).astype(o_ref.dtype)
        return carry

    for_each_group(outputs)


def _gdn(pa, conv_w, a_log, dt_bias, norm_g, *, group=8):
    b, s, _ = pa.shape
    H = N_GDN_HEADS
    C = GDN_CHUNK
    assert s % C == 0
    n_chunks = s // C
    bat = pa[:, :, 4 * GDN_WIDTH:4 * GDN_WIDTH + 2 * H].transpose(0, 2, 1).reshape(b, 2 * H, n_chunks, C)
    head = lambda off: pl.BlockSpec((1, s, HEAD_DIM), lambda i, j: (i, 0, off + j))
    cw = lambda off: pl.BlockSpec((GDN_CONV, HEAD_DIM), lambda i, j: (0, off + j))
    smem = pl.BlockSpec(memory_space=pltpu.SMEM)
    return pl.pallas_call(
        functools.partial(_gdn_kernel, group=math.gcd(group, n_chunks)),
        grid=(b, H),
        in_specs=[smem, smem, head(0), head(H), head(2 * H), head(3 * H),
                  pl.BlockSpec((1, 2 * H, n_chunks, C), lambda i, j: (i, 0, 0, 0)),
                  cw(0), cw(H), cw(2 * H),
                  pl.BlockSpec((1, HEAD_DIM), lambda i, j: (0, 0))],
        out_specs=pl.BlockSpec((1, s, HEAD_DIM), lambda i, j: (i, 0, j)),
        out_shape=jax.ShapeDtypeStruct((b, s, GDN_WIDTH), BF16),
        scratch_shapes=[
            pltpu.VMEM((s, HEAD_DIM), F32),
            pltpu.VMEM((n_chunks, 2 * C, HEAD_DIM), BF16),
            pltpu.VMEM((s, C), BF16),
            pltpu.VMEM((n_chunks, HEAD_DIM, HEAD_DIM), BF16),
            pltpu.VMEM((n_chunks, HEAD_DIM, HEAD_DIM), F32),
            pltpu.VMEM((n_chunks, 1, HEAD_DIM), F32),
            pltpu.VMEM((n_chunks, HEAD_DIM, HEAD_DIM), BF16)],
        compiler_params=pltpu.CompilerParams(
            dimension_semantics=("parallel", "arbitrary"), vmem_limit_bytes=VMEM_LIMIT),
        name="gdn",
    )(a_log, dt_bias, pa, pa, pa, pa, bat, conv_w, conv_w, conv_w, norm_g.reshape(1, HEAD_DIM))


def _bucket_upper_bounds():
    n = np.arange(0, 4 * REL_MAX_DIST, dtype=np.int64)
    max_exact = REL_BUCKETS // 2
    nf = np.maximum(n, 1).astype(np.float32)
    large = max_exact + (np.log(nf / np.float32(max_exact)) / np.float32(math.log(REL_MAX_DIST / max_exact))
                         * np.float32(REL_BUCKETS - max_exact)).astype(np.int32)
    large = np.minimum(large, REL_BUCKETS - 1)
    bucket = np.where(n < max_exact, n, large)
    assert np.all(np.diff(bucket) >= 0) and bucket[-1] == REL_BUCKETS - 1
    return [int(np.searchsorted(bucket, b, side="right")) for b in range(REL_BUCKETS - 1)]


_BUCKET_UPPER = _bucket_upper_bounds()


def _moba_kernel(rb_ref, q_ref, k_ref, v_ref, ng_ref, o_ref, vt_ref, km_ref, bd_ref, bl_ref):
    h = pl.program_id(0)
    b = pl.program_id(1)
    seq = k_ref.shape[1]
    T = MOBA_BLOCK
    nb = seq // T
    scale = HEAD_DIM ** -0.5 * LOG2E
    kk = lax.broadcasted_iota(jnp.int32, (T, T), 0)
    qq = lax.broadcasted_iota(jnp.int32, (T, T), 1)

    @pl.when(b == 0)
    def _():
        def bias_of(n):
            val = jnp.full((T, T), rb_ref[REL_BUCKETS - 1, h], F32)
            for bkt in range(REL_BUCKETS - 2, -1, -1):
                val = jnp.where(n < _BUCKET_UPPER[bkt], rb_ref[bkt, h], val)
            return val
        bd_ref[...] = jnp.where(qq >= kk, bias_of(qq - kk) * LOG2E, NEG)
        bl_ref[...] = bias_of(qq - kk + T) * LOG2E

    for n in range(nb):
        blk_rows = slice(n * T, (n + 1) * T)
        vt_ref[:, blk_rows] = v_ref[0, blk_rows, :].astype(F32).T.astype(BF16)
        km_ref[n:n + 1, :] = jnp.mean(k_ref[0, blk_rows, :].astype(F32), axis=0, keepdims=True)
    far_bias = rb_ref[REL_BUCKETS - 1, h] * LOG2E

    def additive_mask(i, qf):
        if i <= MOBA_TOPK:
            return None
        gate = lax.dot_general(km_ref[0:i, :], qf, (((1,), (1,)), ((), ())),
                               preferred_element_type=F32, precision=lax.Precision.HIGHEST)
        blk = lax.broadcasted_iota(jnp.int32, (i, T), 0)
        rank = jnp.zeros((i, T), jnp.int32)
        for m in range(i):
            gm = gate[m:m + 1, :]
            rank = rank + jnp.where((gm > gate) | ((gm == gate) & (m < blk)), 1, 0)
        return jnp.where(rank < MOBA_TOPK, 0.0, NEG)

    def biased(i, s_all, keep):
        parts = []
        for j in range(i + 1):
            sj = s_all[j * T:(j + 1) * T, :]
            if j == i:
                sj = sj + bd_ref[...]
            elif j == i - 1:
                sj = sj + bl_ref[...]
                if keep is not None:
                    sj = sj + keep[j:j + 1, :]
            else:
                sj = sj + (far_bias if keep is None else keep[j:j + 1, :] + far_bias)
            parts.append(sj)
        return parts

    order = [t for pair in zip(range(nb // 2), range(nb - 1, nb // 2 - 1, -1)) for t in pair]
    if nb % 2:
        order.append(nb // 2)
    for g0 in range(0, nb, MOBA_TILES_IN_FLIGHT):
        tiles = order[g0:g0 + MOBA_TILES_IN_FLIGHT]
        qfs = [q_ref[0, i * T:(i + 1) * T, :].astype(F32) for i in tiles]
        keeps = [additive_mask(i, qf) for i, qf in zip(tiles, qfs)]
        s_alls = [_bdot_nt(k_ref[0, 0:(i + 1) * T, :], qf * scale) for i, qf in zip(tiles, qfs)]
        partss = [biased(i, s_all, keep) for i, s_all, keep in zip(tiles, s_alls, keeps)]
        m_rows = [jnp.max(functools.reduce(jnp.maximum, parts), axis=0, keepdims=True) for parts in partss]
        pss = [[jnp.exp2(sj - m_row) for sj in parts] for parts, m_row in zip(partss, m_rows)]
        l_rows = [jnp.sum(functools.reduce(jnp.add, ps), axis=0, keepdims=True) for ps in pss]
        p_alls = [jnp.concatenate([p.astype(BF16) for p in ps], axis=0) for ps in pss]
        o_ts = [jnp.dot(vt_ref[:, 0:(i + 1) * T], p_all, preferred_element_type=F32) / l_row
                for i, p_all, l_row in zip(tiles, p_alls, l_rows)]
        for i, o_t in zip(tiles, o_ts):
            o_ref[0, i * T:(i + 1) * T, :] = _rms(o_t.T, ng_ref[...]).astype(o_ref.dtype)


def _moba(pb, rel_bias, norm_g):
    b, s, _ = pb.shape
    H = N_MOBA_HEADS
    T = MOBA_BLOCK
    assert s % T == 0
    nb = s // T
    head = lambda off: pl.BlockSpec((1, s, HEAD_DIM), lambda h, bb: (bb, 0, off + h))
    return pl.pallas_call(
        _moba_kernel,
        grid=(H, b),
        in_specs=[pl.BlockSpec(memory_space=pltpu.SMEM), head(0), head(H), head(2 * H),
                  pl.BlockSpec((1, HEAD_DIM), lambda h, bb: (0, 0))],
        out_specs=pl.BlockSpec((1, s, HEAD_DIM), lambda h, bb: (bb, 0, h)),
        out_shape=jax.ShapeDtypeStruct((b, s, MOBA_WIDTH), BF16),
        scratch_shapes=[pltpu.VMEM((HEAD_DIM, s), BF16),
                        pltpu.VMEM((nb, HEAD_DIM), F32),
                        pltpu.VMEM((T, T), F32), pltpu.VMEM((T, T), F32)],
        compiler_params=pltpu.CompilerParams(
            dimension_semantics=("arbitrary", "arbitrary"), vmem_limit_bytes=VMEM_LIMIT),
        name="moba",
    )(rel_bias, pb, pb, pb, norm_g.reshape(1, HEAD_DIM))


def _xattn_kernel(x_ref, oa_ref, ob_ref, wa_ref, wb_ref, g_ref, wq_ref, kv_ref, wo_ref, ng_ref,
                  o_ref, hn_ref, *, rs):
    width = N_XATTN_HEADS * HEAD_DIM
    scale = HEAD_DIM ** -0.5 * LOG2E
    blocks = [slice(r * rs, (r + 1) * rs) for r in range(x_ref.shape[1] // rs)]
    x1s = [x_ref[0, rows, :]
           + jnp.dot(oa_ref[0, rows, :], wa_ref[...], preferred_element_type=F32)
           + jnp.dot(ob_ref[0, rows, :], wb_ref[...], preferred_element_type=F32) for rows in blocks]
    qs = [_bdot(_rms(x1, g_ref[...]), wq_ref[...]) * scale for x1 in x1s]
    outs = [[] for _ in blocks]
    for hd in range(N_XATTN_HEADS):
        sl = slice(hd * HEAD_DIM, (hd + 1) * HEAD_DIM)
        ss = [_bdot_nt(q[:, sl], kv_ref[0, :, sl]) for q in qs]
        ps = [jnp.exp2(s - jnp.max(s, axis=-1, keepdims=True)) for s in ss]
        for out, p in zip(outs, ps):
            l = jnp.sum(p, axis=-1, keepdims=True)
            out.append(_bdot(p, kv_ref[0, :, width + hd * HEAD_DIM: width + (hd + 1) * HEAD_DIM]) / l)
    ys = [x1 + _bdot(jnp.concatenate(out, axis=-1), wo_ref[...]) for x1, out in zip(x1s, outs)]
    for rows, y in zip(blocks, ys):
        o_ref[0, rows, :] = y
        hn_ref[0, rows, :] = _rms(y, ng_ref[...]).astype(BF16)


def _xattn(x, oa, ob, wa, wb, g, wq, kv, wo, next_g, *, ts):
    b, s, d = x.shape
    mlen = kv.shape[1]
    width = wq.shape[1]
    ka, kb = oa.shape[2], ob.shape[2]
    tile = pl.BlockSpec((1, ts, d), lambda i, j: (i, j, 0))
    vec = pl.BlockSpec((1, d), lambda i, j: (0, 0))
    whole = lambda r, c: pl.BlockSpec((r, c), lambda i, j: (0, 0))
    return pl.pallas_call(
        functools.partial(_xattn_kernel, rs=math.gcd(ts, ROW_SUB)),
        grid=(b, s // ts),
        in_specs=[tile,
                  pl.BlockSpec((1, ts, ka), lambda i, j: (i, j, 0)),
                  pl.BlockSpec((1, ts, kb), lambda i, j: (i, j, 0)),
                  whole(ka, d), whole(kb, d), vec, whole(d, width),
                  pl.BlockSpec((1, mlen, 2 * width), lambda i, j: (i, 0, 0)),
                  whole(width, d), vec],
        out_specs=[tile, tile],
        out_shape=[jax.ShapeDtypeStruct((b, s, d), F32), jax.ShapeDtypeStruct((b, s, d), BF16)],
        compiler_params=pltpu.CompilerParams(
            dimension_semantics=("parallel", "parallel"), vmem_limit_bytes=VMEM_LIMIT),
        name="xattn",
    )(x, oa, ob, wa, wb, g.reshape(1, d), wq, kv, wo, next_g.reshape(1, d))


def _ffn_kernel(x_hbm, h_ref, halo_ref, wg_ref, wu_ref, cw_ref, cb_ref, wd_ref, fg_ref, o_ref,
                hx_ref, xres_ref, sem, *, tiles_per_seq, rs):
    i = pl.program_id(0)
    f = pl.program_id(1)
    tm = h_ref.shape[0]
    pad = BF16_ROWS

    def x_copy():
        return pltpu.make_async_copy(x_hbm.at[pl.ds(i * tm, tm), :], xres_ref, sem)

    @pl.when(f == 0)
    def _():
        x_copy().start()
        halo = halo_ref[...]
        hx_ref[0:pad, :] = jnp.where(i % tiles_per_seq == 0, jnp.zeros_like(halo), halo)
        hx_ref[pad:pad + tm, :] = h_ref[...]
        o_ref[...] = jnp.zeros_like(o_ref)

    cw = cw_ref[...]
    keep = SUBLANES
    tail = None
    for r in range(tm // rs):
        rows = slice(r * rs, (r + 1) * rs)
        hrows = slice(pad + r * rs, pad + (r + 1) * rs)
        if r == 0:
            gp = jnp.dot(hx_ref[0:pad + rs, :], wg_ref[...], preferred_element_type=F32)[pad - keep:]
        else:
            gp = jnp.concatenate(
                [tail, jnp.dot(hx_ref[hrows, :], wg_ref[...], preferred_element_type=F32)], axis=0)
        tail = gp[rs:rs + keep]
        up = jnp.dot(hx_ref[hrows, :], wu_ref[...], preferred_element_type=F32)
        gate = gp[keep:keep + rs] * cw[2:3] + gp[keep - 1:keep - 1 + rs] * cw[1:2] \
            + gp[keep - 2:keep - 2 + rs] * cw[0:1] + cb_ref[...]
        act = _silu(gate) * up
        o_ref[rows, :] += jnp.dot(act.astype(BF16), wd_ref[...], preferred_element_type=F32)

    @pl.when(f == pl.num_programs(1) - 1)
    def _():
        x_copy().wait()
        def finish(r, carry):
            rows = pl.ds(pl.multiple_of(r * HEAD_DIM, HEAD_DIM), HEAD_DIM)
            o_ref[rows, :] = _rms(xres_ref[rows, :] + o_ref[rows, :], fg_ref[...])
            return carry

        lax.fori_loop(0, tm // HEAD_DIM, finish, 0)


def _ffn(x, h, wg, wu, cw, cb, wd, fg, *, seq, tm, tf):
    m, d = x.shape
    ff = wg.shape[1]
    assert seq % tm == 0 and ff % tf == 0 and tm % BF16_ROWS == 0
    hb = tm // BF16_ROWS
    return pl.pallas_call(
        functools.partial(_ffn_kernel, tiles_per_seq=seq // tm, rs=tm),
        grid=(m // tm, ff // tf),
        in_specs=[pl.BlockSpec(memory_space=pl.ANY),
                  pl.BlockSpec((tm, d), lambda i, f: (i, 0)),
                  pl.BlockSpec((BF16_ROWS, d), lambda i, f: (jnp.maximum(i * hb - 1, 0), 0)),
                  pl.BlockSpec((d, tf), lambda i, f: (0, f)),
                  pl.BlockSpec((d, tf), lambda i, f: (0, f)),
                  pl.BlockSpec((FFN_CONV, tf), lambda i, f: (0, f)),
                  pl.BlockSpec((1, tf), lambda i, f: (0, f)),
                  pl.BlockSpec((tf, d), lambda i, f: (f, 0)),
                  pl.BlockSpec((1, d), lambda i, f: (0, 0))],
        out_specs=pl.BlockSpec((tm, d), lambda i, f: (i, 0)),
        out_shape=jax.ShapeDtypeStruct((m, d), F32),
        scratch_shapes=[pltpu.VMEM((tm + BF16_ROWS, d), BF16), pltpu.VMEM((tm, d), F32),
                        pltpu.SemaphoreType.DMA(())],
        compiler_params=pltpu.CompilerParams(
            dimension_semantics=("arbitrary", "arbitrary"), vmem_limit_bytes=VMEM_LIMIT),
        name="ffn",
    )(x, h, h, wg, wu, cw, cb.reshape(1, ff), wd, fg.reshape(1, d))


def _layer(x, mem, mix_norm_g, w_in, gdn_conv_w, a_log, dt_bias, gdn_norm_g, moba_norm_g, rel_bias,
           w_out, xattn_norm_g, mem_norm_g, w_xq, w_xkv, w_xo, ffn_norm_g, w_gate, w_up, ffn_conv_w,
           ffn_conv_b, w_down, final_g, *, last):
    b, s, d = x.shape
    m = b * s
    x2 = x.reshape(m, d)
    i1 = 4 * GDN_WIDTH
    i3 = i1 + 2 * N_GDN_HEADS
    w_all = w_in.astype(BF16)
    w_b = w_all[:, i3:]
    tm = min(1024, m)
    wa_cols = i1 + HEAD_DIM
    pa = _norm_matmul(x2, mix_norm_g, w_all, tm=tm, tn=wa_cols // 3, n=wa_cols).reshape(b, s, wa_cols)
    pb = _norm_matmul(x2, mix_norm_g, w_b, tm=tm, tn=1024, out_dtype=BF16).reshape(b, s, 3 * MOBA_WIDTH)
    o_a = _gdn(pa, gdn_conv_w, a_log, dt_bias, gdn_norm_g)
    o_b = _moba(pb, rel_bias, moba_norm_g)
    mlen = mem.shape[1]
    kv = _norm_matmul(mem.reshape(b * mlen, d), mem_norm_g, w_xkv.astype(BF16),
                      tm=min(512, b * mlen), tn=1024, out_dtype=BF16).reshape(b, mlen, -1)
    x2b, h2 = _xattn(x, o_a, o_b, w_out[:GDN_WIDTH].astype(BF16), w_out[GDN_WIDTH:].astype(BF16),
                     xattn_norm_g, w_xq.astype(BF16), kv, w_xo.astype(BF16), ffn_norm_g, ts=min(512, s))
    assert last, "the final rmsnorm is fused into the last layer's ffn"
    y = _ffn(x2b.reshape(m, d), h2.reshape(m, d), w_gate.astype(BF16), w_up.astype(BF16), ffn_conv_w,
             ffn_conv_b, w_down.astype(BF16), final_g, seq=s, tm=min(1024, s), tf=512)
    return y.reshape(b, s, d)


def kernel(x, mem, mix_norm_g, w_in, gdn_conv_w, gdn_a_log, gdn_dt_bias, gdn_norm_g, moba_norm_g,
           rel_bias, w_out, xattn_norm_g, mem_norm_g, w_xq, w_xkv, w_xo, ffn_norm_g, w_gate, w_up,
           ffn_conv_w, ffn_conv_b, w_down, final_norm_g):
    depth = mix_norm_g.shape[0]
    assert depth == 1
    l = 0
    return _layer(x, mem, mix_norm_g[l], w_in[l], gdn_conv_w[l], gdn_a_log[l], gdn_dt_bias[l],
                  gdn_norm_g[l], moba_norm_g[l], rel_bias, w_out[l], xattn_norm_g[l], mem_norm_g[l],
                  w_xq[l], w_xkv[l], w_xo[l], ffn_norm_g[l], w_gate[l], w_up[l], ffn_conv_w[l],
                  ffn_conv_b[l], w_down[l], final_norm_g, last=True)
```

```python
import functools
import math

import jax
import jax.numpy as jnp
import numpy as np
from jax import lax
from jax.experimental import pallas as pl
from jax.experimental.pallas import tpu as pltpu

HEAD_DIM = 128
N_GDN_HEADS = 8
N_MOBA_HEADS = 8
GDN_WIDTH = N_GDN_HEADS * HEAD_DIM
MOBA_WIDTH = N_MOBA_HEADS * HEAD_DIM
GDN_CONV = 4
GDN_CHUNK = 256
GDN_HALF_ROWS_FROM = 64
MOBA_BLOCK = 256
MOBA_TOPK = 3
MOBA_TILES_IN_FLIGHT = 4
REL_BUCKETS = 32
REL_MAX_DIST = 128
N_XATTN_HEADS = 4
FFN_CONV = 3
EPS = 1e-6
NEG = -1e30
LOG2E = math.log2(math.e)
SUBLANES = 8
BF16_ROWS = 16
ROW_SUB = 256
VMEM_LIMIT = 58 * 1024 * 1024

F32 = jnp.float32
BF16 = jnp.bfloat16


def _bdot(a, b):
    return jnp.dot(a.astype(BF16), b.astype(BF16), preferred_element_type=F32)


def _bdot_nt(a, b):
    return lax.dot_general(a.astype(BF16), b.astype(BF16), (((1,), (1,)), ((), ())),
                           preferred_element_type=F32)


def _bdot_tn(a, b):
    return lax.dot_general(a.astype(BF16), b.astype(BF16), (((0,), (0,)), ((), ())),
                           preferred_element_type=F32)


def _fdot(a, b):
    return jnp.dot(a, b, preferred_element_type=F32, precision=lax.Precision.HIGHEST)


def _sigmoid(x):
    return 1.0 / (1.0 + jnp.exp(-x))


def _silu(x):
    return x * _sigmoid(x)


def _rms(x, g):
    return x * lax.rsqrt(jnp.mean(x * x, axis=-1, keepdims=True) + EPS) * g


def _norm_matmul_kernel(x_ref, g_ref, w_ref, o_ref, *, rs):
    for r in range(x_ref.shape[0] // rs):
        rows = slice(r * rs, (r + 1) * rs)
        hn = _rms(x_ref[rows, :], g_ref[...]).astype(BF16)
        o_ref[rows, :] = jnp.dot(hn, w_ref[...], preferred_element_type=F32).astype(o_ref.dtype)


def _norm_matmul(x, g, w, *, tm, tn, n=None, out_dtype=F32):
    m, k = x.shape
    n = w.shape[1] if n is None else n
    assert m % tm == 0 and n % tn == 0 and n <= w.shape[1]
    rs = math.gcd(tm, ROW_SUB)
    return pl.pallas_call(
        functools.partial(_norm_matmul_kernel, rs=rs),
        grid=(m // tm, n // tn),
        in_specs=[pl.BlockSpec((tm, k), lambda i, j: (i, 0)),
                  pl.BlockSpec((1, k), lambda i, j: (0, 0)),
                  pl.BlockSpec((k, tn), lambda i, j: (0, j))],
        out_specs=pl.BlockSpec((tm, tn), lambda i, j: (i, j)),
        out_shape=jax.ShapeDtypeStruct((m, n), out_dtype),
        compiler_params=pltpu.CompilerParams(
            dimension_semantics=("parallel", "arbitrary"), vmem_limit_bytes=VMEM_LIMIT),
        name="norm_matmul",
    )(x, g.reshape(1, k), w)


def _unit_lower_inverse(mats, row, col):
    c = mats[0].shape[0]
    eye = (row == col).astype(F32)
    blk = lambda n: (row // n) == (col // n)
    inner = blk(16)
    ds = [jnp.where(inner, a, 0.0) for a in mats]
    ts = [eye - d for d in ds]
    ps = [_bdot(d, d) for d in ds]
    for step in range(3):
        ts = [t + _bdot(t, p) for t, p in zip(ts, ps)]
        if step < 2:
            ps = [_bdot(p, p) for p in ps]
    half_row = lax.broadcasted_iota(jnp.int32, (c // 2, c), 0)
    half_col = lax.broadcasted_iota(jnp.int32, (c // 2, c), 1)
    n = 32
    while n <= c:
        h = n // 2
        if n < GDN_HALF_ROWS_FROM:
            outer = blk(n) if n < c else None
            keep = ~inner if outer is None else (outer & ~inner)
            xs = [_bdot(jnp.where(keep, a, 0.0), t) for a, t in zip(mats, ts)]
            ts = [t - _bdot(t, x) for t, x in zip(ts, xs)]
        else:
            pairs = range(c // n)
            lower = lambda m: jnp.concatenate([m[(2 * k + 1) * h:(2 * k + 2) * h] for k in pairs], axis=0)
            left = (half_col // h) == 2 * (half_row // h)
            ys = [_bdot(jnp.where(left, lower(a), 0.0), t) for a, t in zip(mats, ts)]
            zero = jnp.zeros((h, c), F32)
            spread = lambda y: jnp.concatenate(
                [part for k in pairs for part in (zero, y[k * h:(k + 1) * h])], axis=0)
            zs = [_bdot(lower(t), spread(y)) for t, y in zip(ts, ys)]
            ts = [jnp.concatenate([part for k in pairs for part in
                                   (t[2 * k * h:(2 * k + 1) * h],
                                    t[(2 * k + 1) * h:(2 * k + 2) * h] - z[k * h:(k + 1) * h])], axis=0)
                  for t, z in zip(ts, zs)]
        inner = blk(n) if n < c else None
        n *= 2
    return ts


def _gdn_kernel(alog_ref, dtb_ref, q_ref, k_ref, v_ref, z_ref, bat_ref, wq_ref, wk_ref, wv_ref,
                ng_ref, o_ref, u_ref, wqd_ref, qk_ref, kw_ref, bc_ref, gl_ref, sb_ref, *, group):
    h = pl.program_id(1)
    seq = q_ref.shape[1]
    C = GDN_CHUNK
    D = HEAD_DIM
    pad = SUBLANES
    n_chunks = seq // C

    static_chunks = n_chunks == group
    row = lax.broadcasted_iota(jnp.int32, (C, C), 0)
    col = lax.broadcasted_iota(jnp.int32, (C, C), 1)
    tri_incl = row >= col
    tri_strict = row > col
    lane = lax.broadcasted_iota(jnp.int32, (SUBLANES, C), 1)
    neg_a = -jnp.exp(jnp.full((1, C), alog_ref[h], F32))
    dt_bias = dtb_ref[h]
    scale = D ** -0.5

    def conv_silu(x_ref, w_ref, c, r0):
        if static_chunks:
            win = (jnp.concatenate([jnp.zeros((pad, D), F32), x_ref[0, 0:C, :]], axis=0) if c == 0
                   else x_ref[0, r0 - pad:r0 + C, :])
        else:
            halo = x_ref[0, pl.ds(jnp.maximum(r0 - pad, 0), pad), :]
            win = jnp.concatenate([jnp.where(c > 0, halo, 0.0), x_ref[0, pl.ds(r0, C), :]], axis=0)
        w = w_ref[...]
        y = win[pad:pad + C] * w[3:4]
        for j in range(GDN_CONV - 1):
            s = GDN_CONV - 1 - j
            y = y + win[pad - s:pad - s + C] * w[j:j + 1]
        return _silu(y)

    def l2n(x):
        return x * lax.rsqrt(jnp.sum(x * x, axis=-1, keepdims=True) + EPS)

    def cumsum_lanes(x):
        x = jnp.broadcast_to(x, (SUBLANES, C))
        s = 1
        while s < C:
            x = x + jnp.where(lane >= s, pltpu.roll(x, s, axis=1), 0.0)
            s *= 2
        return x[0:1, :]

    reps = C // D

    def rows_to_cols(x):
        return jnp.concatenate(
            [jnp.broadcast_to(x[:, n * D:(n + 1) * D], (D, D)).T for n in range(reps)], axis=0)

    def chunk_ids(grp):
        cs = [grp * group + i for i in range(group)]
        return cs, [c * C if static_chunks else pl.multiple_of(c * C, C) for c in cs]

    def for_each_group(body):
        if static_chunks:
            body(0, 0)
        else:
            lax.fori_loop(0, n_chunks // group, body, 0)

    def prepare(grp, carry):
        cs, r0s = chunk_ids(grp)
        pre, gbs, g_rows = [], [], []
        for c, r0 in zip(cs, r0s):
            q = l2n(conv_silu(q_ref, wq_ref, c, r0)) * scale
            k = l2n(conv_silu(k_ref, wk_ref, c, r0))
            v = conv_silu(v_ref, wv_ref, c, r0)
            b_row = bat_ref[0, h, pl.ds(c, 1), :]
            xs = bat_ref[0, h + N_GDN_HEADS, pl.ds(c, 1), :] + dt_bias
            softplus = jnp.maximum(xs, 0.0) + jnp.log1p(jnp.exp(-jnp.abs(xs)))
            g_row = cumsum_lanes(neg_a * softplus)
            pre.append((q, k, v, rows_to_cols(_sigmoid(b_row)), None))
            g_rows.append(g_row)
            gbs.append(rows_to_cols(g_row))
        kk_qks = [_bdot_nt(jnp.concatenate([k * beta, q], axis=0), k) for q, k, v, beta, _ in pre]
        mats, mids = [], []
        for (q, k, v, beta, _), gb, g_row, kk_qk in zip(pre, gbs, g_rows, kk_qks):
            g_i = jnp.concatenate([gb] * reps, axis=1)
            g_j = jnp.broadcast_to(g_row, (C, C))
            decay = jnp.exp(jnp.where(tri_incl, g_i - g_j, NEG))
            mats.append(jnp.where(tri_strict, kk_qk[:C] * decay, 0.0))
            mids.append((kk_qk[C:] * decay, jnp.exp(gb), gb[C - 1:C, :]))
        ts = _unit_lower_inverse(mats, row, col)
        uws = [_bdot(t, jnp.concatenate([v * beta, k * beta * eg], axis=1))
               for t, (q, k, v, beta, _), (_, eg, _) in zip(ts, pre, mids)]
        transs = [_bdot_tn(k * jnp.exp(g_last - gb), uw)
                  for uw, (q, k, v, beta, _), gb, (_, _, g_last) in zip(uws, pre, gbs, mids)]
        for c, r0, uw, trans, (q, k, v, beta, _), (qk, eg, g_last) in zip(cs, r0s, uws, transs, pre, mids):
            u_ref[pl.ds(r0, C), :] = uw[:, :D]
            wqd_ref[c, 0:C, :] = uw[:, D:].astype(BF16)
            wqd_ref[c, C:2 * C, :] = (q * eg).astype(BF16)
            qk_ref[pl.ds(r0, C), :] = qk.astype(BF16)
            bc_ref[c] = trans[:, :D]
            kw_ref[c] = trans[:, D:].astype(BF16)
            gl_ref[c] = jnp.exp(g_last)
        return carry

    for_each_group(prepare)

    def chain(c, state):
        sb = state.astype(BF16)
        sb_ref[c] = sb
        return state * gl_ref[c] - jnp.dot(kw_ref[c], sb, preferred_element_type=F32) + bc_ref[c]

    lax.fori_loop(0, n_chunks, chain, jnp.zeros((D, D), F32))

    def outputs(grp, carry):
        cs, r0s = chunk_ids(grp)
        ws_qs = [jnp.dot(wqd_ref[c], sb_ref[c], preferred_element_type=F32) for c in cs]
        v_news = [(u_ref[pl.ds(r0, C), :] - wq[:C]).astype(BF16) for r0, wq in zip(r0s, ws_qs)]
        os = [wq[C:] + jnp.dot(qk_ref[pl.ds(r0, C), :], vn, preferred_element_type=F32)
              for r0, wq, vn in zip(r0s, ws_qs, v_news)]
        for r0, o in zip(r0s, os):
            z = z_ref[0, pl.ds(r0, C), :]
            o_ref[0, pl.ds(r0, C), :] = (_rms(o, ng_ref[...]) * _silu(z)).astype(o_ref.dtype)
        return carry

    for_each_group(outputs)


def _gdn(pa, conv_w, a_log, dt_bias, norm_g, *, group=8):
    b, s, _ = pa.shape
    H = N_GDN_HEADS
    C = GDN_CHUNK
    assert s % C == 0
    n_chunks = s // C
    bat = pa[:, :, 4 * GDN_WIDTH:4 * GDN_WIDTH + 2 * H].transpose(0, 2, 1).reshape(b, 2 * H, n_chunks, C)
    head = lambda off: pl.BlockSpec((1, s, HEAD_DIM), lambda i, j: (i, 0, off + j))
    cw = lambda off: pl.BlockSpec((GDN_CONV, HEAD_DIM), lambda i, j: (0, off + j))
    smem = pl.BlockSpec(memory_space=pltpu.SMEM)
    return pl.pallas_call(
        functools.partial(_gdn_kernel, group=math.gcd(group, n_chunks)),
        grid=(b, H),
        in_specs=[smem, smem, head(0), head(H), head(2 * H), head(3 * H),
                  pl.BlockSpec((1, 2 * H, n_chunks, C), lambda i, j: (i, 0, 0, 0)),
                  cw(0), cw(H), cw(2 * H),
                  pl.BlockSpec((1, HEAD_DIM), lambda i, j: (0, 0))],
        out_specs=pl.BlockSpec((1, s, HEAD_DIM), lambda i, j: (i, 0, j)),
        out_shape=jax.ShapeDtypeStruct((b, s, GDN_WIDTH), BF16),
        scratch_shapes=[
            pltpu.VMEM((s, HEAD_DIM), F32),
            pltpu.VMEM((n_chunks, 2 * C, HEAD_DIM), BF16),
            pltpu.VMEM((s, C), BF16),
            pltpu.VMEM((n_chunks, HEAD_DIM, HEAD_DIM), BF16),
            pltpu.VMEM((n_chunks, HEAD_DIM, HEAD_DIM), F32),
            pltpu.VMEM((n_chunks, 1, HEAD_DIM), F32),
            pltpu.VMEM((n_chunks, HEAD_DIM, HEAD_DIM), BF16)],
        compiler_params=pltpu.CompilerParams(
            dimension_semantics=("parallel", "arbitrary"), vmem_limit_bytes=VMEM_LIMIT),
        name="gdn",
    )(a_log, dt_bias, pa, pa, pa, pa, bat, conv_w, conv_w, conv_w, norm_g.reshape(1, HEAD_DIM))


def _bucket_upper_bounds():
    n = np.arange(0, 4 * REL_MAX_DIST, dtype=np.int64)
    max_exact = REL_BUCKETS // 2
    nf = np.maximum(n, 1).astype(np.float32)
    large = max_exact + (np.log(nf / np.float32(max_exact)) / np.float32(math.log(REL_MAX_DIST / max_exact))
                         * np.float32(REL_BUCKETS - max_exact)).astype(np.int32)
    large = np.minimum(large, REL_BUCKETS - 1)
    bucket = np.where(n < max_exact, n, large)
    assert np.all(np.diff(bucket) >= 0) and bucket[-1] == REL_BUCKETS - 1
    return [int(np.searchsorted(bucket, b, side="right")) for b in range(REL_BUCKETS - 1)]


_BUCKET_UPPER = _bucket_upper_bounds()


def _moba_kernel(rb_ref, q_ref, k_ref, v_ref, ng_ref, o_ref, vt_ref, km_ref, bd_ref, bl_ref):
    h = pl.program_id(0)
    b = pl.program_id(1)
    seq = k_ref.shape[1]
    T = MOBA_BLOCK
    nb = seq // T
    scale = HEAD_DIM ** -0.5 * LOG2E
    kk = lax.broadcasted_iota(jnp.int32, (T, T), 0)
    qq = lax.broadcasted_iota(jnp.int32, (T, T), 1)

    @pl.when(b == 0)
    def _():
        def bias_of(n):
            val = jnp.full((T, T), rb_ref[REL_BUCKETS - 1, h], F32)
            for bkt in range(REL_BUCKETS - 2, -1, -1):
                val = jnp.where(n < _BUCKET_UPPER[bkt], rb_ref[bkt, h], val)
            return val
        bd_ref[...] = jnp.where(qq >= kk, bias_of(qq - kk) * LOG2E, NEG)
        bl_ref[...] = bias_of(qq - kk + T) * LOG2E

    for n in range(nb):
        blk_rows = slice(n * T, (n + 1) * T)
        vt_ref[0:HEAD_DIM, blk_rows] = v_ref[0, blk_rows, :].astype(F32).T.astype(BF16)
        km_ref[n:n + 1, :] = jnp.mean(k_ref[0, blk_rows, :].astype(F32), axis=0, keepdims=True)
    vt_ref[HEAD_DIM:HEAD_DIM + BF16_ROWS, :] = jnp.ones((BF16_ROWS, seq), BF16)
    far_bias = rb_ref[REL_BUCKETS - 1, h] * LOG2E

    def additive_mask(i, qf):
        if i <= MOBA_TOPK:
            return None
        gate = lax.dot_general(km_ref[0:i, :], qf, (((1,), (1,)), ((), ())),
                               preferred_element_type=F32, precision=lax.Precision.HIGHEST)
        blk = lax.broadcasted_iota(jnp.int32, (i, T), 0)
        rank = jnp.zeros((i, T), jnp.int32)
        for m in range(i):
            gm = gate[m:m + 1, :]
            rank = rank + jnp.where((gm > gate) | ((gm == gate) & (m < blk)), 1, 0)
        return jnp.where(rank < MOBA_TOPK, 0.0, NEG)

    def biased(i, s_all, keep):
        parts = []
        for j in range(i + 1):
            sj = s_all[j * T:(j + 1) * T, :]
            if j == i:
                sj = sj + bd_ref[...]
            elif j == i - 1:
                sj = sj + bl_ref[...]
                if keep is not None:
                    sj = sj + keep[j:j + 1, :]
            else:
                sj = sj + (far_bias if keep is None else keep[j:j + 1, :] + far_bias)
            parts.append(sj)
        return parts

    order = [t for pair in zip(range(nb // 2), range(nb - 1, nb // 2 - 1, -1)) for t in pair]
    if nb % 2:
        order.append(nb // 2)
    for g0 in range(0, nb, MOBA_TILES_IN_FLIGHT):
        tiles = order[g0:g0 + MOBA_TILES_IN_FLIGHT]
        qfs = [q_ref[0, i * T:(i + 1) * T, :].astype(F32) for i in tiles]
        keeps = [additive_mask(i, qf) for i, qf in zip(tiles, qfs)]
        s_alls = [_bdot_nt(k_ref[0, 0:(i + 1) * T, :], qf * scale) for i, qf in zip(tiles, qfs)]
        partss = [biased(i, s_all, keep) for i, s_all, keep in zip(tiles, s_alls, keeps)]
        m_rows = [jnp.max(functools.reduce(jnp.maximum, parts), axis=0, keepdims=True) for parts in partss]
        p_alls = [jnp.concatenate([jnp.exp2((sj - m_row).astype(BF16)) for sj in parts], axis=0)
                  for parts, m_row in zip(partss, m_rows)]
        o_augs = [jnp.dot(vt_ref[:, 0:(i + 1) * T], p_all, preferred_element_type=F32)
                  for i, p_all in zip(tiles, p_alls)]
        o_ts = [o[0:HEAD_DIM] / o[HEAD_DIM:HEAD_DIM + 1] for o in o_augs]
        for i, o_t in zip(tiles, o_ts):
            o_ref[0, i * T:(i + 1) * T, :] = _rms(o_t.T, ng_ref[...]).astype(o_ref.dtype)


def _moba(pb, rel_bias, norm_g):
    b, s, _ = pb.shape
    H = N_MOBA_HEADS
    T = MOBA_BLOCK
    assert s % T == 0
    nb = s // T
    head = lambda off: pl.BlockSpec((1, s, HEAD_DIM), lambda h, bb: (bb, 0, off + h))
    return pl.pallas_call(
        _moba_kernel,
        grid=(H, b),
        in_specs=[pl.BlockSpec(memory_space=pltpu.SMEM), head(0), head(H), head(2 * H),
                  pl.BlockSpec((1, HEAD_DIM), lambda h, bb: (0, 0))],
        out_specs=pl.BlockSpec((1, s, HEAD_DIM), lambda h, bb: (bb, 0, h)),
        out_shape=jax.ShapeDtypeStruct((b, s, MOBA_WIDTH), BF16),
        scratch_shapes=[pltpu.VMEM((HEAD_DIM + BF16_ROWS, s), BF16),
                        pltpu.VMEM((nb, HEAD_DIM), F32),
                        pltpu.VMEM((T, T), F32), pltpu.VMEM((T, T), F32)],
        compiler_params=pltpu.CompilerParams(
            dimension_semantics=("arbitrary", "arbitrary"), vmem_limit_bytes=VMEM_LIMIT),
        name="moba",
    )(rel_bias, pb, pb, pb, norm_g.reshape(1, HEAD_DIM))


def _xattn_kernel(x_ref, oa_ref, ob_ref, wa_ref, wb_ref, g_ref, wq_ref, kv_ref, wo_ref, ng_ref,
                  o_ref, hn_ref, *, rs):
    width = N_XATTN_HEADS * HEAD_DIM
    scale = HEAD_DIM ** -0.5 * LOG2E
    blocks = [slice(r * rs, (r + 1) * rs) for r in range(x_ref.shape[1] // rs)]
    x1s = [x_ref[0, rows, :]
           + jnp.dot(oa_ref[0, rows, :], wa_ref[...], preferred_element_type=F32)
           + jnp.dot(ob_ref[0, rows, :], wb_ref[...], preferred_element_type=F32) for rows in blocks]
    qs = [_bdot(_rms(x1, g_ref[...]), wq_ref[...]) * scale for x1 in x1s]
    outs = [[] for _ in blocks]
    for hd in range(N_XATTN_HEADS):
        sl = slice(hd * HEAD_DIM, (hd + 1) * HEAD_DIM)
        ss = [_bdot_nt(q[:, sl], kv_ref[0, :, sl]) for q in qs]
        ps = [jnp.exp2(s - jnp.max(s, axis=-1, keepdims=True)) for s in ss]
        for out, p in zip(outs, ps):
            l = jnp.sum(p, axis=-1, keepdims=True)
            out.append(_bdot(p, kv_ref[0, :, width + hd * HEAD_DIM: width + (hd + 1) * HEAD_DIM]) / l)
    ys = [x1 + _bdot(jnp.concatenate(out, axis=-1), wo_ref[...]) for x1, out in zip(x1s, outs)]
    for rows, y in zip(blocks, ys):
        o_ref[0, rows, :] = y
        hn_ref[0, rows, :] = _rms(y, ng_ref[...]).astype(BF16)


def _xattn(x, oa, ob, wa, wb, g, wq, kv, wo, next_g, *, ts):
    b, s, d = x.shape
    mlen = kv.shape[1]
    width = wq.shape[1]
    ka, kb = oa.shape[2], ob.shape[2]
    tile = pl.BlockSpec((1, ts, d), lambda i, j: (i, j, 0))
    vec = pl.BlockSpec((1, d), lambda i, j: (0, 0))
    whole = lambda r, c: pl.BlockSpec((r, c), lambda i, j: (0, 0))
    return pl.pallas_call(
        functools.partial(_xattn_kernel, rs=math.gcd(ts, ROW_SUB)),
        grid=(b, s // ts),
        in_specs=[tile,
                  pl.BlockSpec((1, ts, ka), lambda i, j: (i, j, 0)),
                  pl.BlockSpec((1, ts, kb), lambda i, j: (i, j, 0)),
                  whole(ka, d), whole(kb, d), vec, whole(d, width),
                  pl.BlockSpec((1, mlen, 2 * width), lambda i, j: (i, 0, 0)),
                  whole(width, d), vec],
        out_specs=[tile, tile],
        out_shape=[jax.ShapeDtypeStruct((b, s, d), F32), jax.ShapeDtypeStruct((b, s, d), BF16)],
        compiler_params=pltpu.CompilerParams(
            dimension_semantics=("parallel", "parallel"), vmem_limit_bytes=VMEM_LIMIT),
        name="xattn",
    )(x, oa, ob, wa, wb, g.reshape(1, d), wq, kv, wo, next_g.reshape(1, d))


def _ffn_kernel(x_hbm, h_ref, halo_ref, wg_ref, wu_ref, cw_ref, cb_ref, wd_ref, fg_ref, o_ref,
                hx_ref, xres_ref, sem, *, tiles_per_seq, rs):
    i = pl.program_id(0)
    f = pl.program_id(1)
    tm = h_ref.shape[0]
    pad = BF16_ROWS

    def x_copy():
        return pltpu.make_async_copy(x_hbm.at[pl.ds(i * tm, tm), :], xres_ref, sem)

    @pl.when(f == 0)
    def _():
        x_copy().start()
        halo = halo_ref[...]
        hx_ref[0:pad, :] = jnp.where(i % tiles_per_seq == 0, jnp.zeros_like(halo), halo)
        hx_ref[pad:pad + tm, :] = h_ref[...]
        o_ref[...] = jnp.zeros_like(o_ref)

    cw = cw_ref[...]
    keep = SUBLANES
    tail = None
    for r in range(tm // rs):
        rows = slice(r * rs, (r + 1) * rs)
        hrows = slice(pad + r * rs, pad + (r + 1) * rs)
        if r == 0:
            gp = jnp.dot(hx_ref[0:pad + rs, :], wg_ref[...], preferred_element_type=F32)[pad - keep:]
        else:
            gp = jnp.concatenate(
                [tail, jnp.dot(hx_ref[hrows, :], wg_ref[...], preferred_element_type=F32)], axis=0)
        tail = gp[rs:rs + keep]
        up = jnp.dot(hx_ref[hrows, :], wu_ref[...], preferred_element_type=F32)
        gate = gp[keep:keep + rs] * cw[2:3] + gp[keep - 1:keep - 1 + rs] * cw[1:2] \
            + gp[keep - 2:keep - 2 + rs] * cw[0:1] + cb_ref[...]
        act = _silu(gate) * up
        o_ref[rows, :] += jnp.dot(act.astype(BF16), wd_ref[...], preferred_element_type=F32)

    @pl.when(f == pl.num_programs(1) - 1)
    def _():
        x_copy().wait()
        def finish(r, carry):
            rows = pl.ds(pl.multiple_of(r * HEAD_DIM, HEAD_DIM), HEAD_DIM)
            o_ref[rows, :] = _rms(xres_ref[rows, :] + o_ref[rows, :], fg_ref[...])
            return carry

        lax.fori_loop(0, tm // HEAD_DIM, finish, 0)


def _ffn(x, h, wg, wu, cw, cb, wd, fg, *, seq, tm, tf):
    m, d = x.shape
    ff = wg.shape[1]
    assert seq % tm == 0 and ff % tf == 0 and tm % BF16_ROWS == 0
    hb = tm // BF16_ROWS
    return pl.pallas_call(
        functools.partial(_ffn_kernel, tiles_per_seq=seq // tm, rs=tm),
        grid=(m // tm, ff // tf),
        in_specs=[pl.BlockSpec(memory_space=pl.ANY),
                  pl.BlockSpec((tm, d), lambda i, f: (i, 0)),
                  pl.BlockSpec((BF16_ROWS, d), lambda i, f: (jnp.maximum(i * hb - 1, 0), 0)),
                  pl.BlockSpec((d, tf), lambda i, f: (0, f)),
                  pl.BlockSpec((d, tf), lambda i, f: (0, f)),
                  pl.BlockSpec((FFN_CONV, tf), lambda i, f: (0, f)),
                  pl.BlockSpec((1, tf), lambda i, f: (0, f)),
                  pl.BlockSpec((tf, d), lambda i, f: (f, 0)),
                  pl.BlockSpec((1, d), lambda i, f: (0, 0))],
        out_specs=pl.BlockSpec((tm, d), lambda i, f: (i, 0)),
        out_shape=jax.ShapeDtypeStruct((m, d), F32),
        scratch_shapes=[pltpu.VMEM((tm + BF16_ROWS, d), BF16), pltpu.VMEM((tm, d), F32),
                        pltpu.SemaphoreType.DMA(())],
        compiler_params=pltpu.CompilerParams(
            dimension_semantics=("arbitrary", "arbitrary"), vmem_limit_bytes=VMEM_LIMIT),
        name="ffn",
    )(x, h, h, wg, wu, cw, cb.reshape(1, ff), wd, fg.reshape(1, d))


def _layer(x, mem, mix_norm_g, w_in, gdn_conv_w, a_log, dt_bias, gdn_norm_g, moba_norm_g, rel_bias,
           w_out, xattn_norm_g, mem_norm_g, w_xq, w_xkv, w_xo, ffn_norm_g, w_gate, w_up, ffn_conv_w,
           ffn_conv_b, w_down, final_g, *, last):
    b, s, d = x.shape
    m = b * s
    x2 = x.reshape(m, d)
    i1 = 4 * GDN_WIDTH
    i3 = i1 + 2 * N_GDN_HEADS
    w_all = w_in.astype(BF16)
    w_b = w_all[:, i3:]
    tm = min(1024, m)
    wa_cols = i1 + HEAD_DIM
    pa = _norm_matmul(x2, mix_norm_g, w_all, tm=tm, tn=wa_cols // 3, n=wa_cols).reshape(b, s, wa_cols)
    pb = _norm_matmul(x2, mix_norm_g, w_b, tm=tm, tn=1024, out_dtype=BF16).reshape(b, s, 3 * MOBA_WIDTH)
    o_a = _gdn(pa, gdn_conv_w, a_log, dt_bias, gdn_norm_g)
    o_b = _moba(pb, rel_bias, moba_norm_g)
    mlen = mem.shape[1]
    kv = _norm_matmul(mem.reshape(b * mlen, d), mem_norm_g, w_xkv.astype(BF16),
                      tm=min(512, b * mlen), tn=1024, out_dtype=BF16).reshape(b, mlen, -1)
    x2b, h2 = _xattn(x, o_a, o_b, w_out[:GDN_WIDTH].astype(BF16), w_out[GDN_WIDTH:].astype(BF16),
                     xattn_norm_g, w_xq.astype(BF16), kv, w_xo.astype(BF16), ffn_norm_g, ts=min(512, s))
    assert last, "the final rmsnorm is fused into the last layer's ffn"
    y = _ffn(x2b.reshape(m, d), h2.reshape(m, d), w_gate.astype(BF16), w_up.astype(BF16), ffn_conv_w,
             ffn_conv_b, w_down.astype(BF16), final_g, seq=s, tm=min(1024, s), tf=512)
    return y.reshape(b, s, d)


def kernel(x, mem, mix_norm_g, w_in, gdn_conv_w, gdn_a_log, gdn_dt_bias, gdn_norm_g, moba_norm_g,
           rel_bias, w_out, xattn_norm_g, mem_norm_g, w_xq, w_xkv, w_xo, ffn_norm_g, w_gate, w_up,
           ffn_conv_w, ffn_conv_b, w_down, final_norm_g):
    depth = mix_norm_g.shape[0]
    assert depth == 1
    l = 0
    return _layer(x, mem, mix_norm_g[l], w_in[l], gdn_conv_w[l], gdn_a_log[l], gdn_dt_bias[l],
                  gdn_norm_g[l], moba_norm_g[l], rel_bias, w_out[l], xattn_norm_g[l], mem_norm_g[l],
                  w_xq[l], w_xkv[l], w_xo[l], ffn_norm_g[l], w_gate[l], w_up[l], ffn_conv_w[l],
                  ffn_conv_b[l], w_down[l], final_norm_g, last=True)
```

```python
import functools
import math

import jax
import jax.numpy as jnp
import numpy as np
from jax import lax
from jax.experimental import pallas as pl
from jax.experimental.pallas import tpu as pltpu

HEAD_DIM = 128
N_GDN_HEADS = 8
N_MOBA_HEADS = 8
GDN_WIDTH = N_GDN_HEADS * HEAD_DIM
MOBA_WIDTH = N_MOBA_HEADS * HEAD_DIM
GDN_CONV = 4
GDN_CHUNK = 256
GDN_FILL_PER_STEP = 2
GDN_HALF_ROWS_FROM = 64
MOBA_BLOCK = 256
MOBA_TOPK = 3
MOBA_TILES_IN_FLIGHT = 4
REL_BUCKETS = 32
REL_MAX_DIST = 128
N_XATTN_HEADS = 4
FFN_CONV = 3
EPS = 1e-6
NEG = -1e30
LOG2E = math.log2(math.e)
SUBLANES = 8
BF16_ROWS = 16
ROW_SUB = 256
VMEM_LIMIT = 58 * 1024 * 1024

F32 = jnp.float32
BF16 = jnp.bfloat16


def _bdot(a, b):
    return jnp.dot(a.astype(BF16), b.astype(BF16), preferred_element_type=F32)


def _bdot_nt(a, b):
    return lax.dot_general(a.astype(BF16), b.astype(BF16), (((1,), (1,)), ((), ())),
                           preferred_element_type=F32)


def _bdot_tn(a, b):
    return lax.dot_general(a.astype(BF16), b.astype(BF16), (((0,), (0,)), ((), ())),
                           preferred_element_type=F32)


def _sigmoid(x):
    return 1.0 / (1.0 + jnp.exp(-x))


def _silu(x):
    return x * _sigmoid(x)


def _rms(x, g):
    return x * lax.rsqrt(jnp.mean(x * x, axis=-1, keepdims=True) + EPS) * g


def _norm_matmul_kernel(x_ref, g_ref, w_ref, o_ref, *, rs):
    for r in range(x_ref.shape[0] // rs):
        rows = slice(r * rs, (r + 1) * rs)
        hn = _rms(x_ref[rows, :], g_ref[...]).astype(BF16)
        o_ref[rows, :] = jnp.dot(hn, w_ref[...], preferred_element_type=F32).astype(o_ref.dtype)


def _norm_matmul(x, g, w, *, tm, tn, n=None, out_dtype=F32):
    m, k = x.shape
    n = w.shape[1] if n is None else n
    assert m % tm == 0 and n % tn == 0 and n <= w.shape[1]
    rs = math.gcd(tm, ROW_SUB)
    return pl.pallas_call(
        functools.partial(_norm_matmul_kernel, rs=rs),
        grid=(m // tm, n // tn),
        in_specs=[pl.BlockSpec((tm, k), lambda i, j: (i, 0)),
                  pl.BlockSpec((1, k), lambda i, j: (0, 0)),
                  pl.BlockSpec((k, tn), lambda i, j: (0, j))],
        out_specs=pl.BlockSpec((tm, tn), lambda i, j: (i, j)),
        out_shape=jax.ShapeDtypeStruct((m, n), out_dtype),
        compiler_params=pltpu.CompilerParams(
            dimension_semantics=("parallel", "arbitrary"), vmem_limit_bytes=VMEM_LIMIT),
        name="norm_matmul",
    )(x, g.reshape(1, k), w)


def _unit_lower_inverse(mats, row, col, fill=lambda: None):
    c = mats[0].shape[0]
    eye = (row == col).astype(F32)
    blk = lambda n: (row // n) == (col // n)
    inner = blk(16)
    ds = [jnp.where(inner, a, 0.0) for a in mats]
    ts = [eye - d for d in ds]
    ps = [_bdot(d, d) for d in ds]
    fill()
    for step in range(3):
        ts = [t + _bdot(t, p) for t, p in zip(ts, ps)]
        fill()
        if step < 2:
            ps = [_bdot(p, p) for p in ps]
            fill()
    half_row = lax.broadcasted_iota(jnp.int32, (c // 2, c), 0)
    half_col = lax.broadcasted_iota(jnp.int32, (c // 2, c), 1)
    n = 32
    while n <= c:
        h = n // 2
        if n < GDN_HALF_ROWS_FROM:
            outer = blk(n) if n < c else None
            keep = ~inner if outer is None else (outer & ~inner)
            xs = [_bdot(jnp.where(keep, a, 0.0), t) for a, t in zip(mats, ts)]
            fill()
            ts = [t - _bdot(t, x) for t, x in zip(ts, xs)]
            fill()
        else:
            pairs = range(c // n)
            lower = lambda m: jnp.concatenate([m[(2 * k + 1) * h:(2 * k + 2) * h] for k in pairs], axis=0)
            left = (half_col // h) == 2 * (half_row // h)
            ys = [_bdot(jnp.where(left, lower(a), 0.0), t) for a, t in zip(mats, ts)]
            fill()
            zero = jnp.zeros((h, c), F32)
            spread = lambda y: jnp.concatenate(
                [part for k in pairs for part in (zero, y[k * h:(k + 1) * h])], axis=0)
            zs = [_bdot(lower(t), spread(y)) for t, y in zip(ts, ys)]
            ts = [jnp.concatenate([part for k in pairs for part in
                                   (t[2 * k * h:(2 * k + 1) * h],
                                    t[(2 * k + 1) * h:(2 * k + 2) * h] - z[k * h:(k + 1) * h])], axis=0)
                  for t, z in zip(ts, zs)]
            fill()
        inner = blk(n) if n < c else None
        n *= 2
    return ts


def _gdn_body(h, alog_ref, dtb_ref, q_ref, k_ref, v_ref, z_ref, bat_ref, wq_ref, wk_ref, wv_ref,
              ng_ref, o_ref, u_ref, wqd_ref, qk_ref, kw_ref, bc_ref, gl_ref, sb_ref, *, group,
              fill=lambda n=1: None):
    seq = q_ref.shape[1]
    C = GDN_CHUNK
    D = HEAD_DIM
    pad = SUBLANES
    n_chunks = seq // C

    static_chunks = n_chunks == group
    row = lax.broadcasted_iota(jnp.int32, (C, C), 0)
    col = lax.broadcasted_iota(jnp.int32, (C, C), 1)
    tri_incl = row >= col
    tri_strict = row > col
    lane = lax.broadcasted_iota(jnp.int32, (SUBLANES, C), 1)
    neg_a = -jnp.exp(jnp.full((1, C), alog_ref[h], F32))
    dt_bias = dtb_ref[h]
    scale = D ** -0.5

    def conv_silu(x_ref, w_ref, c, r0):
        if static_chunks:
            win = (jnp.concatenate([jnp.zeros((pad, D), F32), x_ref[0, 0:C, :]], axis=0) if c == 0
                   else x_ref[0, r0 - pad:r0 + C, :])
        else:
            halo = x_ref[0, pl.ds(jnp.maximum(r0 - pad, 0), pad), :]
            win = jnp.concatenate([jnp.where(c > 0, halo, 0.0), x_ref[0, pl.ds(r0, C), :]], axis=0)
        w = w_ref[...]
        y = win[pad:pad + C] * w[3:4]
        for j in range(GDN_CONV - 1):
            s = GDN_CONV - 1 - j
            y = y + win[pad - s:pad - s + C] * w[j:j + 1]
        return _silu(y)

    def l2n(x):
        return x * lax.rsqrt(jnp.sum(x * x, axis=-1, keepdims=True) + EPS)

    def cumsum_lanes(x):
        x = jnp.broadcast_to(x, (SUBLANES, C))
        s = 1
        while s < C:
            x = x + jnp.where(lane >= s, pltpu.roll(x, s, axis=1), 0.0)
            s *= 2
        return x[0:1, :]

    reps = C // D

    def rows_to_cols(x):
        return jnp.concatenate(
            [jnp.broadcast_to(x[:, n * D:(n + 1) * D], (D, D)).T for n in range(reps)], axis=0)

    def chunk_ids(grp):
        cs = [grp * group + i for i in range(group)]
        return cs, [c * C if static_chunks else pl.multiple_of(c * C, C) for c in cs]

    def for_each_group(body):
        if static_chunks:
            body(0, 0)
        else:
            lax.fori_loop(0, n_chunks // group, body, 0)

    def prepare(grp, carry):
        cs, r0s = chunk_ids(grp)
        pre, gbs, g_rows = [], [], []
        for c, r0 in zip(cs, r0s):
            q = l2n(conv_silu(q_ref, wq_ref, c, r0)) * scale
            k = l2n(conv_silu(k_ref, wk_ref, c, r0))
            v = conv_silu(v_ref, wv_ref, c, r0)
            b_row = bat_ref[0, h, pl.ds(c, 1), :]
            xs = bat_ref[0, h + N_GDN_HEADS, pl.ds(c, 1), :] + dt_bias
            softplus = jnp.maximum(xs, 0.0) + jnp.log1p(jnp.exp(-jnp.abs(xs)))
            g_row = cumsum_lanes(neg_a * softplus)
            pre.append((q, k, v, rows_to_cols(_sigmoid(b_row)), None))
            g_rows.append(g_row)
            gbs.append(rows_to_cols(g_row))
            fill()
        kk_qks = [_bdot_nt(jnp.concatenate([k * beta, q], axis=0), k) for q, k, v, beta, _ in pre]
        mats, mids = [], []
        for (q, k, v, beta, _), gb, g_row, kk_qk in zip(pre, gbs, g_rows, kk_qks):
            g_i = jnp.concatenate([gb] * reps, axis=1)
            g_j = jnp.broadcast_to(g_row, (C, C))
            decay = jnp.exp(jnp.where(tri_incl, g_i - g_j, NEG))
            mats.append(jnp.where(tri_strict, kk_qk[:C] * decay, 0.0))
            mids.append((kk_qk[C:] * decay, jnp.exp(gb), gb[C - 1:C, :]))
        ts = _unit_lower_inverse(mats, row, col, fill=fill)
        uws = [_bdot(t, jnp.concatenate([v * beta, k * beta * eg], axis=1))
               for t, (q, k, v, beta, _), (_, eg, _) in zip(ts, pre, mids)]
        transs = [_bdot_tn(k * jnp.exp(g_last - gb), uw)
                  for uw, (q, k, v, beta, _), gb, (_, _, g_last) in zip(uws, pre, gbs, mids)]
        for c, r0, uw, trans, (q, k, v, beta, _), (qk, eg, g_last) in zip(cs, r0s, uws, transs, pre, mids):
            u_ref[pl.ds(r0, C), :] = uw[:, :D]
            wqd_ref[c, 0:C, :] = uw[:, D:].astype(BF16)
            wqd_ref[c, C:2 * C, :] = (q * eg).astype(BF16)
            qk_ref[pl.ds(r0, C), :] = qk.astype(BF16)
            bc_ref[c] = trans[:, :D]
            kw_ref[c] = trans[:, D:].astype(BF16)
            gl_ref[c] = jnp.exp(g_last)
        return carry

    for_each_group(prepare)

    def chain(c, state):
        sb = state.astype(BF16)
        sb_ref[c] = sb
        return state * gl_ref[c] - jnp.dot(kw_ref[c], sb, preferred_element_type=F32) + bc_ref[c]

    if static_chunks:
        state = jnp.zeros((D, D), F32)
        for c in range(n_chunks):
            state = chain(c, state)
            fill(GDN_FILL_PER_STEP)
    else:
        lax.fori_loop(0, n_chunks, chain, jnp.zeros((D, D), F32))

    def outputs(grp, carry):
        cs, r0s = chunk_ids(grp)
        ws_qs = [jnp.dot(wqd_ref[c], sb_ref[c], preferred_element_type=F32) for c in cs]
        v_news = [(u_ref[pl.ds(r0, C), :] - wq[:C]).astype(BF16) for r0, wq in zip(r0s, ws_qs)]
        os = [wq[C:] + jnp.dot(qk_ref[pl.ds(r0, C), :], vn, preferred_element_type=F32)
              for r0, wq, vn in zip(r0s, ws_qs, v_news)]
        for r0, o in zip(r0s, os):
            z = z_ref[0, pl.ds(r0, C), :]
            o_ref[0, pl.ds(r0, C), :] = (_rms(o, ng_ref[...]) * _silu(z)).astype(o_ref.dtype)
        return carry

    for_each_group(outputs)


def _bucket_upper_bounds():
    n = np.arange(0, 4 * REL_MAX_DIST, dtype=np.int64)
    max_exact = REL_BUCKETS // 2
    nf = np.maximum(n, 1).astype(np.float32)
    large = max_exact + (np.log(nf / np.float32(max_exact)) / np.float32(math.log(REL_MAX_DIST / max_exact))
                         * np.float32(REL_BUCKETS - max_exact)).astype(np.int32)
    large = np.minimum(large, REL_BUCKETS - 1)
    bucket = np.where(n < max_exact, n, large)
    assert np.all(np.diff(bucket) >= 0) and bucket[-1] == REL_BUCKETS - 1
    return [int(np.searchsorted(bucket, b, side="right")) for b in range(REL_BUCKETS - 1)]


_BUCKET_UPPER = _bucket_upper_bounds()


def _moba_steps(h, b, rb_ref, q_ref, k_ref, v_ref, ng_ref, o_ref, vt_ref, km_ref, bd_ref, bl_ref):
    seq = k_ref.shape[1]
    T = MOBA_BLOCK
    nb = seq // T
    scale = HEAD_DIM ** -0.5 * LOG2E
    kk = lax.broadcasted_iota(jnp.int32, (T, T), 0)
    qq = lax.broadcasted_iota(jnp.int32, (T, T), 1)

    @pl.when(b == 0)
    def _():
        def bias_of(n):
            val = jnp.full((T, T), rb_ref[REL_BUCKETS - 1, h], F32)
            for bkt in range(REL_BUCKETS - 2, -1, -1):
                val = jnp.where(n < _BUCKET_UPPER[bkt], rb_ref[bkt, h], val)
            return val
        bd_ref[...] = jnp.where(qq >= kk, bias_of(qq - kk) * LOG2E, NEG)
        bl_ref[...] = bias_of(qq - kk + T) * LOG2E

    for n in range(nb):
        blk_rows = slice(n * T, (n + 1) * T)
        vt_ref[0:HEAD_DIM, blk_rows] = v_ref[0, blk_rows, :].astype(F32).T.astype(BF16)
        km_ref[n:n + 1, :] = jnp.mean(k_ref[0, blk_rows, :].astype(F32), axis=0, keepdims=True)
    vt_ref[HEAD_DIM:HEAD_DIM + BF16_ROWS, :] = jnp.ones((BF16_ROWS, seq), BF16)
    far_bias = rb_ref[REL_BUCKETS - 1, h] * LOG2E

    def additive_mask(i, qf):
        if i <= MOBA_TOPK:
            return None
        gate = lax.dot_general(km_ref[0:i, :], qf, (((1,), (1,)), ((), ())),
                               preferred_element_type=F32, precision=lax.Precision.HIGHEST)
        blk = lax.broadcasted_iota(jnp.int32, (i, T), 0)
        rank = jnp.zeros((i, T), jnp.int32)
        for m in range(i):
            gm = gate[m:m + 1, :]
            rank = rank + jnp.where((gm > gate) | ((gm == gate) & (m < blk)), 1, 0)
        return jnp.where(rank < MOBA_TOPK, 0.0, NEG)

    def biased(i, s_all, keep):
        parts = []
        for j in range(i + 1):
            sj = s_all[j * T:(j + 1) * T, :]
            if j == i:
                sj = sj + bd_ref[...]
            elif j == i - 1:
                sj = sj + bl_ref[...]
                if keep is not None:
                    sj = sj + keep[j:j + 1, :]
            else:
                sj = sj + (far_bias if keep is None else keep[j:j + 1, :] + far_bias)
            parts.append(sj)
        return parts

    order = [t for pair in zip(range(nb // 2), range(nb - 1, nb // 2 - 1, -1)) for t in pair]
    if nb % 2:
        order.append(nb // 2)
    for g0 in range(0, nb, MOBA_TILES_IN_FLIGHT):
        tiles = order[g0:g0 + MOBA_TILES_IN_FLIGHT]
        qfs = [q_ref[0, i * T:(i + 1) * T, :].astype(F32) for i in tiles]
        keeps, s_alls, partss, m_rows, p_alls, o_augs = [], [], [], [], [], []
        for i, qf in zip(tiles, qfs):
            keeps.append(additive_mask(i, qf))
            s_alls.append(_bdot_nt(k_ref[0, 0:(i + 1) * T, :], qf * scale))
            yield
        for i, s_all, keep in zip(tiles, s_alls, keeps):
            partss.append(biased(i, s_all, keep))
            m_rows.append(jnp.max(functools.reduce(jnp.maximum, partss[-1]), axis=0, keepdims=True))
            yield
        for parts, m_row in zip(partss, m_rows):
            p_alls.append(jnp.concatenate([jnp.exp2((sj - m_row).astype(BF16)) for sj in parts], axis=0))
            yield
        for i, p_all in zip(tiles, p_alls):
            o_augs.append(jnp.dot(vt_ref[:, 0:(i + 1) * T], p_all, preferred_element_type=F32))
            yield
        for i, o in zip(tiles, o_augs):
            o_t = o[0:HEAD_DIM] / o[HEAD_DIM:HEAD_DIM + 1]
            o_ref[0, i * T:(i + 1) * T, :] = _rms(o_t.T, ng_ref[...]).astype(o_ref.dtype)
            yield


def _mixer_kernel(alog_ref, dtb_ref, rb_ref, q_ref, k_ref, v_ref, z_ref, bat_ref, wq_ref, wk_ref, wv_ref,
                  gng_ref, mq_ref, mk_ref, mv_ref, mng_ref, oa_ref, ob_ref,
                  u_ref, wqd_ref, qk_ref, kw_ref, bc_ref, gl_ref, sb_ref, vt_ref, km_ref, bd_ref, bl_ref,
                  *, group):
    h = pl.program_id(0)
    b = pl.program_id(1)
    moba = _moba_steps(h, b, rb_ref, mq_ref, mk_ref, mv_ref, mng_ref, ob_ref, vt_ref, km_ref, bd_ref, bl_ref)

    def fill(n=1):
        for _ in range(n):
            next(moba, None)

    _gdn_body(h, alog_ref, dtb_ref, q_ref, k_ref, v_ref, z_ref, bat_ref, wq_ref, wk_ref, wv_ref,
              gng_ref, oa_ref, u_ref, wqd_ref, qk_ref, kw_ref, bc_ref, gl_ref, sb_ref, group=group, fill=fill)
    for _ in moba:
        pass


def _mixer(pa, pb, conv_w, a_log, dt_bias, gdn_norm_g, rel_bias, moba_norm_g, *, group=8):
    b, s, _ = pa.shape
    H = N_GDN_HEADS
    assert N_MOBA_HEADS == H
    C, T = GDN_CHUNK, MOBA_BLOCK
    assert s % C == 0 and s % T == 0
    n_chunks, nb = s // C, s // T
    bat = pa[:, :, 4 * GDN_WIDTH:4 * GDN_WIDTH + 2 * H].transpose(0, 2, 1).reshape(b, 2 * H, n_chunks, C)
    head = lambda off: pl.BlockSpec((1, s, HEAD_DIM), lambda h, i: (i, 0, off + h))
    cw = lambda off: pl.BlockSpec((GDN_CONV, HEAD_DIM), lambda h, i: (0, off + h))
    vec = pl.BlockSpec((1, HEAD_DIM), lambda h, i: (0, 0))
    smem = pl.BlockSpec(memory_space=pltpu.SMEM)
    out = pl.BlockSpec((1, s, HEAD_DIM), lambda h, i: (i, 0, h))
    return pl.pallas_call(
        functools.partial(_mixer_kernel, group=math.gcd(group, n_chunks)),
        grid=(H, b),
        in_specs=[smem, smem, smem, head(0), head(H), head(2 * H), head(3 * H),
                  pl.BlockSpec((1, 2 * H, n_chunks, C), lambda h, i: (i, 0, 0, 0)),
                  cw(0), cw(H), cw(2 * H), vec,
                  head(0), head(H), head(2 * H), vec],
        out_specs=[out, out],
        out_shape=[jax.ShapeDtypeStruct((b, s, GDN_WIDTH), BF16), jax.ShapeDtypeStruct((b, s, MOBA_WIDTH), BF16)],
        scratch_shapes=[
            pltpu.VMEM((s, HEAD_DIM), F32),
            pltpu.VMEM((n_chunks, 2 * C, HEAD_DIM), BF16),
            pltpu.VMEM((s, C), BF16),
            pltpu.VMEM((n_chunks, HEAD_DIM, HEAD_DIM), BF16),
            pltpu.VMEM((n_chunks, HEAD_DIM, HEAD_DIM), F32),
            pltpu.VMEM((n_chunks, 1, HEAD_DIM), F32),
            pltpu.VMEM((n_chunks, HEAD_DIM, HEAD_DIM), BF16),
            pltpu.VMEM((HEAD_DIM + BF16_ROWS, s), BF16),
            pltpu.VMEM((nb, HEAD_DIM), F32),
            pltpu.VMEM((T, T), F32), pltpu.VMEM((T, T), F32)],
        compiler_params=pltpu.CompilerParams(
            dimension_semantics=("arbitrary", "arbitrary"), vmem_limit_bytes=VMEM_LIMIT),
        name="mixer",
    )(a_log, dt_bias, rel_bias, pa, pa, pa, pa, bat, conv_w, conv_w, conv_w, gdn_norm_g.reshape(1, HEAD_DIM),
      pb, pb, pb, moba_norm_g.reshape(1, HEAD_DIM))


def _xattn_kernel(x_ref, oa_ref, ob_ref, wa_ref, wb_ref, g_ref, wq_ref, kv_ref, wo_ref, ng_ref,
                  o_ref, hn_ref, *, rs):
    width = N_XATTN_HEADS * HEAD_DIM
    scale = HEAD_DIM ** -0.5 * LOG2E
    blocks = [slice(r * rs, (r + 1) * rs) for r in range(x_ref.shape[1] // rs)]
    x1s = [x_ref[0, rows, :]
           + jnp.dot(oa_ref[0, rows, :], wa_ref[...], preferred_element_type=F32)
           + jnp.dot(ob_ref[0, rows, :], wb_ref[...], preferred_element_type=F32) for rows in blocks]
    qs = [_bdot(_rms(x1, g_ref[...]), wq_ref[...]) * scale for x1 in x1s]
    outs = [[] for _ in blocks]
    for hd in range(N_XATTN_HEADS):
        sl = slice(hd * HEAD_DIM, (hd + 1) * HEAD_DIM)
        ss = [_bdot_nt(q[:, sl], kv_ref[0, :, sl]) for q in qs]
        ps = [jnp.exp2(s - jnp.max(s, axis=-1, keepdims=True)) for s in ss]
        for out, p in zip(outs, ps):
            l = jnp.sum(p, axis=-1, keepdims=True)
            out.append(_bdot(p, kv_ref[0, :, width + hd * HEAD_DIM: width + (hd + 1) * HEAD_DIM]) / l)
    ys = [x1 + _bdot(jnp.concatenate(out, axis=-1), wo_ref[...]) for x1, out in zip(x1s, outs)]
    for rows, y in zip(blocks, ys):
        o_ref[0, rows, :] = y
        hn_ref[0, rows, :] = _rms(y, ng_ref[...]).astype(BF16)


def _xattn(x, oa, ob, wa, wb, g, wq, kv, wo, next_g, *, ts):
    b, s, d = x.shape
    mlen = kv.shape[1]
    width = wq.shape[1]
    ka, kb = oa.shape[2], ob.shape[2]
    tile = pl.BlockSpec((1, ts, d), lambda i, j: (i, j, 0))
    vec = pl.BlockSpec((1, d), lambda i, j: (0, 0))
    whole = lambda r, c: pl.BlockSpec((r, c), lambda i, j: (0, 0))
    return pl.pallas_call(
        functools.partial(_xattn_kernel, rs=math.gcd(ts, ROW_SUB)),
        grid=(b, s // ts),
        in_specs=[tile,
                  pl.BlockSpec((1, ts, ka), lambda i, j: (i, j, 0)),
                  pl.BlockSpec((1, ts, kb), lambda i, j: (i, j, 0)),
                  whole(ka, d), whole(kb, d), vec, whole(d, width),
                  pl.BlockSpec((1, mlen, 2 * width), lambda i, j: (i, 0, 0)),
                  whole(width, d), vec],
        out_specs=[tile, tile],
        out_shape=[jax.ShapeDtypeStruct((b, s, d), F32), jax.ShapeDtypeStruct((b, s, d), BF16)],
        compiler_params=pltpu.CompilerParams(
            dimension_semantics=("parallel", "parallel"), vmem_limit_bytes=VMEM_LIMIT),
        name="xattn",
    )(x, oa, ob, wa, wb, g.reshape(1, d), wq, kv, wo, next_g.reshape(1, d))


def _ffn_kernel(x_hbm, h_ref, halo_ref, wg_ref, wu_ref, cw_ref, cb_ref, wd_ref, fg_ref, o_ref,
                hx_ref, xres_ref, sem, *, tiles_per_seq, rs):
    i = pl.program_id(0)
    f = pl.program_id(1)
    tm = h_ref.shape[0]
    pad = BF16_ROWS

    def x_copy():
        return pltpu.make_async_copy(x_hbm.at[pl.ds(i * tm, tm), :], xres_ref, sem)

    @pl.when(f == 0)
    def _():
        x_copy().start()
        halo = halo_ref[...]
        hx_ref[0:pad, :] = jnp.where(i % tiles_per_seq == 0, jnp.zeros_like(halo), halo)
        hx_ref[pad:pad + tm, :] = h_ref[...]
        o_ref[...] = jnp.zeros_like(o_ref)

    cw = cw_ref[...]
    keep = SUBLANES
    tail = None
    for r in range(tm // rs):
        rows = slice(r * rs, (r + 1) * rs)
        hrows = slice(pad + r * rs, pad + (r + 1) * rs)
        if r == 0:
            gp = jnp.dot(hx_ref[0:pad + rs, :], wg_ref[...], preferred_element_type=F32)[pad - keep:]
        else:
            gp = jnp.concatenate(
                [tail, jnp.dot(hx_ref[hrows, :], wg_ref[...], preferred_element_type=F32)], axis=0)
        tail = gp[rs:rs + keep]
        up = jnp.dot(hx_ref[hrows, :], wu_ref[...], preferred_element_type=F32)
        gate = gp[keep:keep + rs] * cw[2:3] + gp[keep - 1:keep - 1 + rs] * cw[1:2] \
            + gp[keep - 2:keep - 2 + rs] * cw[0:1] + cb_ref[...]
        act = _silu(gate) * up
        o_ref[rows, :] += jnp.dot(act.astype(BF16), wd_ref[...], preferred_element_type=F32)

    @pl.when(f == pl.num_programs(1) - 1)
    def _():
        x_copy().wait()

        def finish(r, carry):
            rows = pl.ds(pl.multiple_of(r * HEAD_DIM, HEAD_DIM), HEAD_DIM)
            o_ref[rows, :] = _rms(xres_ref[rows, :] + o_ref[rows, :], fg_ref[...])
            return carry

        lax.fori_loop(0, tm // HEAD_DIM, finish, 0)


def _ffn(x, h, wg, wu, cw, cb, wd, fg, *, seq, tm, tf):
    m, d = x.shape
    ff = wg.shape[1]
    assert seq % tm == 0 and ff % tf == 0 and tm % BF16_ROWS == 0
    hb = tm // BF16_ROWS
    return pl.pallas_call(
        functools.partial(_ffn_kernel, tiles_per_seq=seq // tm, rs=tm),
        grid=(m // tm, ff // tf),
        in_specs=[pl.BlockSpec(memory_space=pl.ANY),
                  pl.BlockSpec((tm, d), lambda i, f: (i, 0)),
                  pl.BlockSpec((BF16_ROWS, d), lambda i, f: (jnp.maximum(i * hb - 1, 0), 0)),
                  pl.BlockSpec((d, tf), lambda i, f: (0, f)),
                  pl.BlockSpec((d, tf), lambda i, f: (0, f)),
                  pl.BlockSpec((FFN_CONV, tf), lambda i, f: (0, f)),
                  pl.BlockSpec((1, tf), lambda i, f: (0, f)),
                  pl.BlockSpec((tf, d), lambda i, f: (f, 0)),
                  pl.BlockSpec((1, d), lambda i, f: (0, 0))],
        out_specs=pl.BlockSpec((tm, d), lambda i, f: (i, 0)),
        out_shape=jax.ShapeDtypeStruct((m, d), F32),
        scratch_shapes=[pltpu.VMEM((tm + BF16_ROWS, d), BF16), pltpu.VMEM((tm, d), F32),
                        pltpu.SemaphoreType.DMA(())],
        compiler_params=pltpu.CompilerParams(
            dimension_semantics=("arbitrary", "arbitrary"), vmem_limit_bytes=VMEM_LIMIT),
        name="ffn",
    )(x, h, h, wg, wu, cw, cb.reshape(1, ff), wd, fg.reshape(1, d))


def _layer(x, mem, mix_norm_g, w_in, gdn_conv_w, a_log, dt_bias, gdn_norm_g, moba_norm_g, rel_bias,
           w_out, xattn_norm_g, mem_norm_g, w_xq, w_xkv, w_xo, ffn_norm_g, w_gate, w_up, ffn_conv_w,
           ffn_conv_b, w_down, final_g, *, last):
    b, s, d = x.shape
    m = b * s
    x2 = x.reshape(m, d)
    i1 = 4 * GDN_WIDTH
    i3 = i1 + 2 * N_GDN_HEADS
    w_all = w_in.astype(BF16)
    w_b = w_all[:, i3:]
    tm = min(1024, m)
    wa_cols = i1 + HEAD_DIM
    pa = _norm_matmul(x2, mix_norm_g, w_all, tm=tm, tn=wa_cols // 3, n=wa_cols).reshape(b, s, wa_cols)
    pb = _norm_matmul(x2, mix_norm_g, w_b, tm=tm, tn=1024, out_dtype=BF16).reshape(b, s, 3 * MOBA_WIDTH)
    o_a, o_b = _mixer(pa, pb, gdn_conv_w, a_log, dt_bias, gdn_norm_g, rel_bias, moba_norm_g)
    mlen = mem.shape[1]
    kv = _norm_matmul(mem.reshape(b * mlen, d), mem_norm_g, w_xkv.astype(BF16),
                      tm=min(512, b * mlen), tn=1024, out_dtype=BF16).reshape(b, mlen, -1)
    x2b, h2 = _xattn(x, o_a, o_b, w_out[:GDN_WIDTH].astype(BF16), w_out[GDN_WIDTH:].astype(BF16),
                     xattn_norm_g, w_xq.astype(BF16), kv, w_xo.astype(BF16), ffn_norm_g, ts=min(512, s))
    assert last, "the final rmsnorm is fused into the last layer's ffn"
    y = _ffn(x2b.reshape(m, d), h2.reshape(m, d), w_gate.astype(BF16), w_up.astype(BF16), ffn_conv_w,
             ffn_conv_b, w_down.astype(BF16), final_g, seq=s, tm=min(1024, s), tf=512)
    return y.reshape(b, s, d)


def kernel(x, mem, mix_norm_g, w_in, gdn_conv_w, gdn_a_log, gdn_dt_bias, gdn_norm_g, moba_norm_g,
           rel_bias, w_out, xattn_norm_g, mem_norm_g, w_xq, w_xkv, w_xo, ffn_norm_g, w_gate, w_up,
           ffn_conv_w, ffn_conv_b, w_down, final_norm_g):
    depth = mix_norm_g.shape[0]
    assert depth == 1
    l = 0
    return _layer(x, mem, mix_norm_g[l], w_in[l], gdn_conv_w[l], gdn_a_log[l], gdn_dt_bias[l],
                  gdn_norm_g[l], moba_norm_g[l], rel_bias, w_out[l], xattn_norm_g[l], mem_norm_g[l],
                  w_xq[l], w_xkv[l], w_xo[l], ffn_norm_g[l], w_gate[l], w_up[l], ffn_conv_w[l],
                  ffn_conv_b[l], w_down[l], final_norm_g, last=True)
```

```python
import functools
import math

import jax
import jax.numpy as jnp
import numpy as np
from jax import lax
from jax.experimental import pallas as pl
from jax.experimental.pallas import tpu as pltpu

HEAD_DIM = 128
N_GDN_HEADS = 8
N_MOBA_HEADS = 8
GDN_WIDTH = N_GDN_HEADS * HEAD_DIM
MOBA_WIDTH = N_MOBA_HEADS * HEAD_DIM
GDN_CONV = 4
GDN_CHUNK = 256
GDN_FILL_PER_STEP = 2
GDN_HALF_ROWS_FROM = 64
MOBA_BLOCK = 256
MOBA_TOPK = 3
MOBA_TILES_IN_FLIGHT = 8
REL_BUCKETS = 32
REL_MAX_DIST = 128
N_XATTN_HEADS = 4
FFN_CONV = 3
EPS = 1e-6
NEG = -1e30
LOG2E = math.log2(math.e)
SUBLANES = 8
BF16_ROWS = 16
ROW_SUB = 256
VMEM_LIMIT = 58 * 1024 * 1024

F32 = jnp.float32
BF16 = jnp.bfloat16


def _bdot(a, b):
    return jnp.dot(a.astype(BF16), b.astype(BF16), preferred_element_type=F32)


def _bdot_nt(a, b):
    return lax.dot_general(a.astype(BF16), b.astype(BF16), (((1,), (1,)), ((), ())),
                           preferred_element_type=F32)


def _bdot_tn(a, b):
    return lax.dot_general(a.astype(BF16), b.astype(BF16), (((0,), (0,)), ((), ())),
                           preferred_element_type=F32)


def _sigmoid(x):
    return 1.0 / (1.0 + jnp.exp(-x))


def _silu(x):
    return x * _sigmoid(x)


def _rms(x, g):
    return x * lax.rsqrt(jnp.mean(x * x, axis=-1, keepdims=True) + EPS) * g


def _norm_matmul_kernel(x_ref, g_ref, w_ref, o_ref, *, rs):
    for r in range(x_ref.shape[0] // rs):
        rows = slice(r * rs, (r + 1) * rs)
        hn = _rms(x_ref[rows, :], g_ref[...]).astype(BF16)
        o_ref[rows, :] = jnp.dot(hn, w_ref[...], preferred_element_type=F32).astype(o_ref.dtype)


def _norm_matmul(x, g, w, *, tm, tn, n=None, out_dtype=F32):
    m, k = x.shape
    n = w.shape[1] if n is None else n
    assert m % tm == 0 and n % tn == 0 and n <= w.shape[1]
    rs = math.gcd(tm, ROW_SUB)
    return pl.pallas_call(
        functools.partial(_norm_matmul_kernel, rs=rs),
        grid=(m // tm, n // tn),
        in_specs=[pl.BlockSpec((tm, k), lambda i, j: (i, 0)),
                  pl.BlockSpec((1, k), lambda i, j: (0, 0)),
                  pl.BlockSpec((k, tn), lambda i, j: (0, j))],
        out_specs=pl.BlockSpec((tm, tn), lambda i, j: (i, j)),
        out_shape=jax.ShapeDtypeStruct((m, n), out_dtype),
        compiler_params=pltpu.CompilerParams(
            dimension_semantics=("parallel", "arbitrary"), vmem_limit_bytes=VMEM_LIMIT),
        name="norm_matmul",
    )(x, g.reshape(1, k), w)


def _unit_lower_inverse(mats, row, col, fill=lambda: None):
    c = mats[0].shape[0]
    eye = (row == col).astype(F32)
    blk = lambda n: (row // n) == (col // n)
    inner = blk(16)
    ds = [jnp.where(inner, a, 0.0) for a in mats]
    ts = [eye - d for d in ds]
    ps = [_bdot(d, d) for d in ds]
    fill()
    for step in range(3):
        ts = [t + _bdot(t, p) for t, p in zip(ts, ps)]
        fill()
        if step < 2:
            ps = [_bdot(p, p) for p in ps]
            fill()
    half_row = lax.broadcasted_iota(jnp.int32, (c // 2, c), 0)
    half_col = lax.broadcasted_iota(jnp.int32, (c // 2, c), 1)
    n = 32
    while n <= c:
        h = n // 2
        if n < GDN_HALF_ROWS_FROM:
            outer = blk(n) if n < c else None
            keep = ~inner if outer is None else (outer & ~inner)
            xs = [_bdot(jnp.where(keep, a, 0.0), t) for a, t in zip(mats, ts)]
            fill()
            ts = [t - _bdot(t, x) for t, x in zip(ts, xs)]
            fill()
        else:
            pairs = range(c // n)
            lower = lambda m: jnp.concatenate([m[(2 * k + 1) * h:(2 * k + 2) * h] for k in pairs], axis=0)
            left = (half_col // h) == 2 * (half_row // h)
            ys = [_bdot(jnp.where(left, lower(a), 0.0), t) for a, t in zip(mats, ts)]
            fill()
            zero = jnp.zeros((h, c), F32)
            spread = lambda y: jnp.concatenate(
                [part for k in pairs for part in (zero, y[k * h:(k + 1) * h])], axis=0)
            zs = [_bdot(lower(t), spread(y)) for t, y in zip(ts, ys)]
            ts = [jnp.concatenate([part for k in pairs for part in
                                   (t[2 * k * h:(2 * k + 1) * h],
                                    t[(2 * k + 1) * h:(2 * k + 2) * h] - z[k * h:(k + 1) * h])], axis=0)
                  for t, z in zip(ts, zs)]
            fill()
        inner = blk(n) if n < c else None
        n *= 2
    return ts


def _gdn_body(h, alog_ref, dtb_ref, q_ref, k_ref, v_ref, z_ref, bat_ref, wq_ref, wk_ref, wv_ref,
              ng_ref, o_ref, u_ref, wqd_ref, qk_ref, kw_ref, bc_ref, gl_ref, sb_ref, *, group,
              fill=lambda n=1: None):
    seq = q_ref.shape[1]
    C = GDN_CHUNK
    D = HEAD_DIM
    pad = SUBLANES
    n_chunks = seq // C

    static_chunks = n_chunks == group
    row = lax.broadcasted_iota(jnp.int32, (C, C), 0)
    col = lax.broadcasted_iota(jnp.int32, (C, C), 1)
    tri_incl = row >= col
    tri_strict = row > col
    lane = lax.broadcasted_iota(jnp.int32, (SUBLANES, C), 1)
    neg_a = -jnp.exp(jnp.full((1, C), alog_ref[h], F32))
    dt_bias = dtb_ref[h]
    scale = D ** -0.5

    def conv_silu(x_ref, w_ref, c, r0):
        if static_chunks:
            win = (jnp.concatenate([jnp.zeros((pad, D), F32), x_ref[0, 0:C, :]], axis=0) if c == 0
                   else x_ref[0, r0 - pad:r0 + C, :])
        else:
            halo = x_ref[0, pl.ds(jnp.maximum(r0 - pad, 0), pad), :]
            win = jnp.concatenate([jnp.where(c > 0, halo, 0.0), x_ref[0, pl.ds(r0, C), :]], axis=0)
        w = w_ref[...]
        prev = pltpu.roll(win, 1, axis=0)
        near = win * w[3:4] + prev * w[2:3]
        far = pltpu.roll(win * w[1:2] + prev * w[0:1], 2, axis=0)
        return _silu(near[pad:pad + C] + far[pad:pad + C])

    def l2n(x):
        return x * lax.rsqrt(jnp.sum(x * x, axis=-1, keepdims=True) + EPS)

    def cumsum_lanes(x):
        x = jnp.broadcast_to(x, (SUBLANES, C))
        s = 1
        while s < C:
            x = x + jnp.where(lane >= s, pltpu.roll(x, s, axis=1), 0.0)
            s *= 2
        return x[0:1, :]

    reps = C // D

    def rows_to_cols(x):
        return jnp.concatenate(
            [jnp.broadcast_to(x[:, n * D:(n + 1) * D], (D, D)).T for n in range(reps)], axis=0)

    def chunk_ids(grp):
        cs = [grp * group + i for i in range(group)]
        return cs, [c * C if static_chunks else pl.multiple_of(c * C, C) for c in cs]

    def for_each_group(body):
        if static_chunks:
            body(0, 0)
        else:
            lax.fori_loop(0, n_chunks // group, body, 0)

    def prepare(grp, carry):
        cs, r0s = chunk_ids(grp)
        pre, gbs, g_rows = [], [], []
        for c, r0 in zip(cs, r0s):
            q = l2n(conv_silu(q_ref, wq_ref, c, r0)) * scale
            k = l2n(conv_silu(k_ref, wk_ref, c, r0))
            v = conv_silu(v_ref, wv_ref, c, r0)
            b_row = bat_ref[0, h, pl.ds(c, 1), :]
            xs = bat_ref[0, h + N_GDN_HEADS, pl.ds(c, 1), :] + dt_bias
            softplus = jnp.maximum(xs, 0.0) + jnp.log1p(jnp.exp(-jnp.abs(xs)))
            g_row = cumsum_lanes(neg_a * softplus)
            pre.append((q, k, v, rows_to_cols(_sigmoid(b_row)), None))
            g_rows.append(g_row)
            gbs.append(rows_to_cols(g_row))
            fill()
        kk_qks = [_bdot_nt(jnp.concatenate([k * beta, q], axis=0), k) for q, k, v, beta, _ in pre]
        mats, mids = [], []
        for (q, k, v, beta, _), gb, g_row, kk_qk in zip(pre, gbs, g_rows, kk_qks):
            g_i = jnp.concatenate([gb] * reps, axis=1)
            g_j = jnp.broadcast_to(g_row, (C, C))
            decay = jnp.exp(jnp.where(tri_incl, g_i - g_j, NEG))
            mats.append(jnp.where(tri_strict, kk_qk[:C] * decay, 0.0))
            mids.append((kk_qk[C:] * decay, jnp.exp(gb), gb[C - 1:C, :]))
        ts = _unit_lower_inverse(mats, row, col, fill=fill)
        uws = [_bdot(t, jnp.concatenate([v * beta, k * beta * eg], axis=1))
               for t, (q, k, v, beta, _), (_, eg, _) in zip(ts, pre, mids)]
        transs = [_bdot_tn(k * jnp.exp(g_last - gb), uw)
                  for uw, (q, k, v, beta, _), gb, (_, _, g_last) in zip(uws, pre, gbs, mids)]
        for c, r0, uw, trans, (q, k, v, beta, _), (qk, eg, g_last) in zip(cs, r0s, uws, transs, pre, mids):
            u_ref[pl.ds(r0, C), :] = uw[:, :D]
            wqd_ref[c, 0:C, :] = uw[:, D:].astype(BF16)
            wqd_ref[c, C:2 * C, :] = (q * eg).astype(BF16)
            qk_ref[pl.ds(r0, C), :] = qk.astype(BF16)
            bc_ref[c] = trans[:, :D]
            kw_ref[c] = trans[:, D:].astype(BF16)
            gl_ref[c] = jnp.exp(g_last)
        return carry

    for_each_group(prepare)

    def chain(c, state):
        sb = state.astype(BF16)
        sb_ref[c] = sb
        return state * gl_ref[c] - jnp.dot(kw_ref[c], sb, preferred_element_type=F32) + bc_ref[c]

    if static_chunks:
        state = jnp.zeros((D, D), F32)
        for c in range(n_chunks):
            state = chain(c, state)
            fill(GDN_FILL_PER_STEP)
    else:
        lax.fori_loop(0, n_chunks, chain, jnp.zeros((D, D), F32))

    def outputs(grp, carry):
        cs, r0s = chunk_ids(grp)
        ws_qs = [jnp.dot(wqd_ref[c], sb_ref[c], preferred_element_type=F32) for c in cs]
        v_news = [(u_ref[pl.ds(r0, C), :] - wq[:C]).astype(BF16) for r0, wq in zip(r0s, ws_qs)]
        os = [wq[C:] + jnp.dot(qk_ref[pl.ds(r0, C), :], vn, preferred_element_type=F32)
              for r0, wq, vn in zip(r0s, ws_qs, v_news)]
        for r0, o in zip(r0s, os):
            z = z_ref[0, pl.ds(r0, C), :]
            o_ref[0, pl.ds(r0, C), :] = (_rms(o, ng_ref[...]) * _silu(z)).astype(o_ref.dtype)
        return carry

    for_each_group(outputs)


def _bucket_upper_bounds():
    n = np.arange(0, 4 * REL_MAX_DIST, dtype=np.int64)
    max_exact = REL_BUCKETS // 2
    nf = np.maximum(n, 1).astype(np.float32)
    large = max_exact + (np.log(nf / np.float32(max_exact)) / np.float32(math.log(REL_MAX_DIST / max_exact))
                         * np.float32(REL_BUCKETS - max_exact)).astype(np.int32)
    large = np.minimum(large, REL_BUCKETS - 1)
    bucket = np.where(n < max_exact, n, large)
    assert np.all(np.diff(bucket) >= 0) and bucket[-1] == REL_BUCKETS - 1
    return [int(np.searchsorted(bucket, b, side="right")) for b in range(REL_BUCKETS - 1)]


_BUCKET_UPPER = _bucket_upper_bounds()


def _moba_steps(h, b, rb_ref, q_ref, k_ref, v_ref, ng_ref, o_ref, vt_ref, km_ref, bd_ref, bl_ref):
    seq = k_ref.shape[1]
    T = MOBA_BLOCK
    nb = seq // T
    scale = HEAD_DIM ** -0.5 * LOG2E
    kk = lax.broadcasted_iota(jnp.int32, (T, T), 0)
    qq = lax.broadcasted_iota(jnp.int32, (T, T), 1)

    @pl.when(b == 0)
    def _():
        def bias_of(n):
            val = jnp.full((T, T), rb_ref[REL_BUCKETS - 1, h], F32)
            for bkt in range(REL_BUCKETS - 2, -1, -1):
                val = jnp.where(n < _BUCKET_UPPER[bkt], rb_ref[bkt, h], val)
            return val
        bd_ref[...] = jnp.where(qq >= kk, bias_of(qq - kk) * LOG2E, NEG)
        bl_ref[...] = bias_of(qq - kk + T) * LOG2E

    for n in range(nb):
        blk_rows = slice(n * T, (n + 1) * T)
        vt_ref[0:HEAD_DIM, blk_rows] = v_ref[0, blk_rows, :].astype(F32).T.astype(BF16)
        km_ref[n:n + 1, :] = jnp.mean(k_ref[0, blk_rows, :].astype(F32), axis=0, keepdims=True)
    vt_ref[HEAD_DIM:HEAD_DIM + BF16_ROWS, :] = jnp.ones((BF16_ROWS, seq), BF16)
    far_bias = rb_ref[REL_BUCKETS - 1, h] * LOG2E

    def additive_mask(i, qf):
        if i <= MOBA_TOPK:
            return None
        gate = lax.dot_general(km_ref[0:i, :], qf, (((1,), (1,)), ((), ())),
                               preferred_element_type=F32, precision=lax.Precision.HIGHEST)
        blk = lax.broadcasted_iota(jnp.int32, (i, T), 0)
        rank = jnp.zeros((i, T), jnp.int32)
        for m in range(i):
            gm = gate[m:m + 1, :]
            rank = rank + jnp.where((gm > gate) | ((gm == gate) & (m < blk)), 1, 0)
        return jnp.where(rank < MOBA_TOPK, 0.0, NEG)

    def biased(i, s_all, keep):
        parts, maxes = [], []
        for j in range(i + 1):
            sj = s_all[j * T:(j + 1) * T, :]
            if j >= i - 1:
                sj = sj + (bd_ref[...] if j == i else bl_ref[...])
                row = keep[j:j + 1, :] if (keep is not None and j < i) else 0.0
            else:
                row = far_bias if keep is None else keep[j:j + 1, :] + far_bias
            parts.append((sj, row))
            maxes.append(jnp.max(sj, axis=0, keepdims=True) + row)
        return parts, functools.reduce(jnp.maximum, maxes)

    order = [t for pair in zip(range(nb // 2), range(nb - 1, nb // 2 - 1, -1)) for t in pair]
    if nb % 2:
        order.append(nb // 2)
    for g0 in range(0, nb, MOBA_TILES_IN_FLIGHT):
        tiles = order[g0:g0 + MOBA_TILES_IN_FLIGHT]
        qfs = [q_ref[0, i * T:(i + 1) * T, :].astype(F32) for i in tiles]
        keeps, s_alls, partss, m_rows, p_alls, o_augs = [], [], [], [], [], []
        for i, qf in zip(tiles, qfs):
            keeps.append(additive_mask(i, qf))
            s_alls.append(_bdot_nt(k_ref[0, 0:(i + 1) * T, :], qf * scale))
            yield
        for i, s_all, keep in zip(tiles, s_alls, keeps):
            parts, m_row = biased(i, s_all, keep)
            partss.append(parts)
            m_rows.append(m_row)
            yield
        for parts, m_row in zip(partss, m_rows):
            p_alls.append(jnp.concatenate(
                [jnp.exp2((sj + (row - m_row)).astype(BF16)) for sj, row in parts], axis=0))
            yield
        for i, p_all in zip(tiles, p_alls):
            o_augs.append(jnp.dot(vt_ref[:, 0:(i + 1) * T], p_all, preferred_element_type=F32))
            yield
        for i, o in zip(tiles, o_augs):
            o_t = o[0:HEAD_DIM] / o[HEAD_DIM:HEAD_DIM + 1]
            o_ref[0, i * T:(i + 1) * T, :] = _rms(o_t.T, ng_ref[...]).astype(o_ref.dtype)
            yield


def _mixer_kernel(alog_ref, dtb_ref, rb_ref, q_ref, k_ref, v_ref, z_ref, bat_ref, wq_ref, wk_ref, wv_ref,
                  gng_ref, mq_ref, mk_ref, mv_ref, mng_ref, oa_ref, ob_ref,
                  u_ref, wqd_ref, qk_ref, kw_ref, bc_ref, gl_ref, sb_ref, vt_ref, km_ref, bd_ref, bl_ref,
                  *, group):
    h = pl.program_id(0)
    b = pl.program_id(1)
    moba = _moba_steps(h, b, rb_ref, mq_ref, mk_ref, mv_ref, mng_ref, ob_ref, vt_ref, km_ref, bd_ref, bl_ref)

    def fill(n=1):
        for _ in range(n):
            next(moba, None)

    _gdn_body(h, alog_ref, dtb_ref, q_ref, k_ref, v_ref, z_ref, bat_ref, wq_ref, wk_ref, wv_ref,
              gng_ref, oa_ref, u_ref, wqd_ref, qk_ref, kw_ref, bc_ref, gl_ref, sb_ref, group=group, fill=fill)
    for _ in moba:
        pass


def _mixer(pa, pb, conv_w, a_log, dt_bias, gdn_norm_g, rel_bias, moba_norm_g, *, group=8):
    b, s, _ = pa.shape
    H = N_GDN_HEADS
    assert N_MOBA_HEADS == H
    C, T = GDN_CHUNK, MOBA_BLOCK
    assert s % C == 0 and s % T == 0
    n_chunks, nb = s // C, s // T
    bat = pa[:, :, 4 * GDN_WIDTH:4 * GDN_WIDTH + 2 * H].transpose(0, 2, 1).reshape(b, 2 * H, n_chunks, C)
    head = lambda off: pl.BlockSpec((1, s, HEAD_DIM), lambda h, i: (i, 0, off + h))
    cw = lambda off: pl.BlockSpec((GDN_CONV, HEAD_DIM), lambda h, i: (0, off + h))
    vec = pl.BlockSpec((1, HEAD_DIM), lambda h, i: (0, 0))
    smem = pl.BlockSpec(memory_space=pltpu.SMEM)
    out = pl.BlockSpec((1, s, HEAD_DIM), lambda h, i: (i, 0, h))
    return pl.pallas_call(
        functools.partial(_mixer_kernel, group=math.gcd(group, n_chunks)),
        grid=(H, b),
        in_specs=[smem, smem, smem, head(0), head(H), head(2 * H), head(3 * H),
                  pl.BlockSpec((1, 2 * H, n_chunks, C), lambda h, i: (i, 0, 0, 0)),
                  cw(0), cw(H), cw(2 * H), vec,
                  head(0), head(H), head(2 * H), vec],
        out_specs=[out, out],
        out_shape=[jax.ShapeDtypeStruct((b, s, GDN_WIDTH), BF16), jax.ShapeDtypeStruct((b, s, MOBA_WIDTH), BF16)],
        scratch_shapes=[
            pltpu.VMEM((s, HEAD_DIM), F32),
            pltpu.VMEM((n_chunks, 2 * C, HEAD_DIM), BF16),
            pltpu.VMEM((s, C), BF16),
            pltpu.VMEM((n_chunks, HEAD_DIM, HEAD_DIM), BF16),
            pltpu.VMEM((n_chunks, HEAD_DIM, HEAD_DIM), F32),
            pltpu.VMEM((n_chunks, 1, HEAD_DIM), F32),
            pltpu.VMEM((n_chunks, HEAD_DIM, HEAD_DIM), BF16),
            pltpu.VMEM((HEAD_DIM + BF16_ROWS, s), BF16),
            pltpu.VMEM((nb, HEAD_DIM), F32),
            pltpu.VMEM((T, T), F32), pltpu.VMEM((T, T), F32)],
        compiler_params=pltpu.CompilerParams(
            dimension_semantics=("arbitrary", "arbitrary"), vmem_limit_bytes=VMEM_LIMIT),
        name="mixer",
    )(a_log, dt_bias, rel_bias, pa, pa, pa, pa, bat, conv_w, conv_w, conv_w, gdn_norm_g.reshape(1, HEAD_DIM),
      pb, pb, pb, moba_norm_g.reshape(1, HEAD_DIM))


def _xattn_kernel(x_ref, oa_ref, ob_ref, wa_ref, wb_ref, g_ref, wq_ref, kv_ref, wo_ref, ng_ref,
                  o_ref, hn_ref, *, rs):
    width = N_XATTN_HEADS * HEAD_DIM
    scale = HEAD_DIM ** -0.5 * LOG2E
    blocks = [slice(r * rs, (r + 1) * rs) for r in range(x_ref.shape[1] // rs)]
    x1s = [x_ref[0, rows, :]
           + jnp.dot(oa_ref[0, rows, :], wa_ref[...], preferred_element_type=F32)
           + jnp.dot(ob_ref[0, rows, :], wb_ref[...], preferred_element_type=F32) for rows in blocks]
    qs = [_bdot(_rms(x1, g_ref[...]), wq_ref[...]) * scale for x1 in x1s]
    outs = [[] for _ in blocks]
    for hd in range(N_XATTN_HEADS):
        sl = slice(hd * HEAD_DIM, (hd + 1) * HEAD_DIM)
        ss = [_bdot_nt(q[:, sl], kv_ref[0, :, sl]) for q in qs]
        ps = [jnp.exp2(s - jnp.max(s, axis=-1, keepdims=True)) for s in ss]
        for out, p in zip(outs, ps):
            l = jnp.sum(p, axis=-1, keepdims=True)
            out.append(_bdot(p, kv_ref[0, :, width + hd * HEAD_DIM: width + (hd + 1) * HEAD_DIM]) / l)
    ys = [x1 + _bdot(jnp.concatenate(out, axis=-1), wo_ref[...]) for x1, out in zip(x1s, outs)]
    for rows, y in zip(blocks, ys):
        o_ref[0, rows, :] = y
        hn_ref[0, rows, :] = _rms(y, ng_ref[...]).astype(BF16)


def _xattn(x, oa, ob, wa, wb, g, wq, kv, wo, next_g, *, ts):
    b, s, d = x.shape
    mlen = kv.shape[1]
    width = wq.shape[1]
    ka, kb = oa.shape[2], ob.shape[2]
    tile = pl.BlockSpec((1, ts, d), lambda i, j: (i, j, 0))
    vec = pl.BlockSpec((1, d), lambda i, j: (0, 0))
    whole = lambda r, c: pl.BlockSpec((r, c), lambda i, j: (0, 0))
    return pl.pallas_call(
        functools.partial(_xattn_kernel, rs=math.gcd(ts, ROW_SUB)),
        grid=(b, s // ts),
        in_specs=[tile,
                  pl.BlockSpec((1, ts, ka), lambda i, j: (i, j, 0)),
                  pl.BlockSpec((1, ts, kb), lambda i, j: (i, j, 0)),
                  whole(ka, d), whole(kb, d), vec, whole(d, width),
                  pl.BlockSpec((1, mlen, 2 * width), lambda i, j: (i, 0, 0)),
                  whole(width, d), vec],
        out_specs=[tile, tile],
        out_shape=[jax.ShapeDtypeStruct((b, s, d), F32), jax.ShapeDtypeStruct((b, s, d), BF16)],
        compiler_params=pltpu.CompilerParams(
            dimension_semantics=("parallel", "parallel"), vmem_limit_bytes=VMEM_LIMIT),
        name="xattn",
    )(x, oa, ob, wa, wb, g.reshape(1, d), wq, kv, wo, next_g.reshape(1, d))


def _ffn_kernel(x_hbm, h_ref, halo_ref, wg_ref, wu_ref, cw_ref, cb_ref, wd_ref, fg_ref, o_ref,
                hx_ref, xres_ref, sem, *, tiles_per_seq, rs):
    i = pl.program_id(0)
    f = pl.program_id(1)
    tm = h_ref.shape[0]
    pad = BF16_ROWS

    def x_copy():
        return pltpu.make_async_copy(x_hbm.at[pl.ds(i * tm, tm), :], xres_ref, sem)

    @pl.when(f == 0)
    def _():
        x_copy().start()
        halo = halo_ref[...]
        hx_ref[0:pad, :] = jnp.where(i % tiles_per_seq == 0, jnp.zeros_like(halo), halo)
        hx_ref[pad:pad + tm, :] = h_ref[...]
        o_ref[...] = jnp.zeros_like(o_ref)

    cw = cw_ref[...]
    keep = SUBLANES
    tail = None
    for r in range(tm // rs):
        rows = slice(r * rs, (r + 1) * rs)
        hrows = slice(pad + r * rs, pad + (r + 1) * rs)
        if r == 0:
            gp = jnp.dot(hx_ref[0:pad + rs, :], wg_ref[...], preferred_element_type=F32)[pad - keep:]
        else:
            gp = jnp.concatenate(
                [tail, jnp.dot(hx_ref[hrows, :], wg_ref[...], preferred_element_type=F32)], axis=0)
        tail = gp[rs:rs + keep]
        up = jnp.dot(hx_ref[hrows, :], wu_ref[...], preferred_element_type=F32)
        gate = gp[keep:keep + rs] * cw[2:3] + gp[keep - 1:keep - 1 + rs] * cw[1:2] \
            + gp[keep - 2:keep - 2 + rs] * cw[0:1] + cb_ref[...]
        act = _silu(gate) * up
        o_ref[rows, :] += jnp.dot(act.astype(BF16), wd_ref[...], preferred_element_type=F32)

    @pl.when(f == pl.num_programs(1) - 1)
    def _():
        x_copy().wait()

        def finish(r, carry):
            rows = pl.ds(pl.multiple_of(r * HEAD_DIM, HEAD_DIM), HEAD_DIM)
            o_ref[rows, :] = _rms(xres_ref[rows, :] + o_ref[rows, :], fg_ref[...])
            return carry

        lax.fori_loop(0, tm // HEAD_DIM, finish, 0)


def _ffn(x, h, wg, wu, cw, cb, wd, fg, *, seq, tm, tf):
    m, d = x.shape
    ff = wg.shape[1]
    assert seq % tm == 0 and ff % tf == 0 and tm % BF16_ROWS == 0
    hb = tm // BF16_ROWS
    return pl.pallas_call(
        functools.partial(_ffn_kernel, tiles_per_seq=seq // tm, rs=tm),
        grid=(m // tm, ff // tf),
        in_specs=[pl.BlockSpec(memory_space=pl.ANY),
                  pl.BlockSpec((tm, d), lambda i, f: (i, 0)),
                  pl.BlockSpec((BF16_ROWS, d), lambda i, f: (jnp.maximum(i * hb - 1, 0), 0)),
                  pl.BlockSpec((d, tf), lambda i, f: (0, f)),
                  pl.BlockSpec((d, tf), lambda i, f: (0, f)),
                  pl.BlockSpec((FFN_CONV, tf), lambda i, f: (0, f)),
                  pl.BlockSpec((1, tf), lambda i, f: (0, f)),
                  pl.BlockSpec((tf, d), lambda i, f: (f, 0)),
                  pl.BlockSpec((1, d), lambda i, f: (0, 0))],
        out_specs=pl.BlockSpec((tm, d), lambda i, f: (i, 0)),
        out_shape=jax.ShapeDtypeStruct((m, d), F32),
        scratch_shapes=[pltpu.VMEM((tm + BF16_ROWS, d), BF16), pltpu.VMEM((tm, d), F32),
                        pltpu.SemaphoreType.DMA(())],
        compiler_params=pltpu.CompilerParams(
            dimension_semantics=("arbitrary", "arbitrary"), vmem_limit_bytes=VMEM_LIMIT),
        name="ffn",
    )(x, h, h, wg, wu, cw, cb.reshape(1, ff), wd, fg.reshape(1, d))


def _layer(x, mem, mix_norm_g, w_in, gdn_conv_w, a_log, dt_bias, gdn_norm_g, moba_norm_g, rel_bias,
           w_out, xattn_norm_g, mem_norm_g, w_xq, w_xkv, w_xo, ffn_norm_g, w_gate, w_up, ffn_conv_w,
           ffn_conv_b, w_down, final_g, *, last):
    b, s, d = x.shape
    m = b * s
    x2 = x.reshape(m, d)
    i1 = 4 * GDN_WIDTH
    i3 = i1 + 2 * N_GDN_HEADS
    w_all = w_in.astype(BF16)
    w_b = w_all[:, i3:]
    tm = min(1024, m)
    wa_cols = i1 + HEAD_DIM
    pa = _norm_matmul(x2, mix_norm_g, w_all, tm=tm, tn=wa_cols // 3, n=wa_cols).reshape(b, s, wa_cols)
    pb = _norm_matmul(x2, mix_norm_g, w_b, tm=tm, tn=1024, out_dtype=BF16).reshape(b, s, 3 * MOBA_WIDTH)
    o_a, o_b = _mixer(pa, pb, gdn_conv_w, a_log, dt_bias, gdn_norm_g, rel_bias, moba_norm_g)
    mlen = mem.shape[1]
    kv = _norm_matmul(mem.reshape(b * mlen, d), mem_norm_g, w_xkv.astype(BF16),
                      tm=min(512, b * mlen), tn=1024, out_dtype=BF16).reshape(b, mlen, -1)
    x2b, h2 = _xattn(x, o_a, o_b, w_out[:GDN_WIDTH].astype(BF16), w_out[GDN_WIDTH:].astype(BF16),
                     xattn_norm_g, w_xq.astype(BF16), kv, w_xo.astype(BF16), ffn_norm_g, ts=min(512, s))
    assert last, "the final rmsnorm is fused into the last layer's ffn"
    y = _ffn(x2b.reshape(m, d), h2.reshape(m, d), w_gate.astype(BF16), w_up.astype(BF16), ffn_conv_w,
             ffn_conv_b, w_down.astype(BF16), final_g, seq=s, tm=min(1024, s), tf=512)
    return y.reshape(b, s, d)


def kernel(x, mem, mix_norm_g, w_in, gdn_conv_w, gdn_a_log, gdn_dt_bias, gdn_norm_g, moba_norm_g,
           rel_bias, w_out, xattn_norm_g, mem_norm_g, w_xq, w_xkv, w_xo, ffn_norm_g, w_gate, w_up,
           ffn_conv_w, ffn_conv_b, w_down, final_norm_g):
    depth = mix_norm_g.shape[0]
    assert depth == 1
    l = 0
    return _layer(x, mem, mix_norm_g[l], w_in[l], gdn_conv_w[l], gdn_a_log[l], gdn_dt_bias[l],
                  gdn_norm_g[l], moba_norm_g[l], rel_bias, w_out[l], xattn_norm_g[l], mem_norm_g[l],
                  w_xq[l], w_xkv[l], w_xo[l], ffn_norm_g[l], w_gate[l], w_up[l], ffn_conv_w[l],
                  ffn_conv_b[l], w_down[l], final_norm_g, last=True)
```

```python
import functools
import math

import jax
import jax.numpy as jnp
import numpy as np
from jax import lax
from jax.experimental import pallas as pl
from jax.experimental.pallas import tpu as pltpu

HEAD_DIM = 128
N_GDN_HEADS = 8
N_MOBA_HEADS = 8
GDN_WIDTH = N_GDN_HEADS * HEAD_DIM
MOBA_WIDTH = N_MOBA_HEADS * HEAD_DIM
GDN_CONV = 4
GDN_CHUNK = 256
GDN_FILL_PER_STEP = 2
GDN_HALF_ROWS_FROM = 64
MOBA_BLOCK = 256
MOBA_TOPK = 3
MOBA_TILES_IN_FLIGHT = 8
REL_BUCKETS = 32
REL_MAX_DIST = 128
N_XATTN_HEADS = 4
FFN_CONV = 3
EPS = 1e-6
NEG = -1e30
LOG2E = math.log2(math.e)
SUBLANES = 8
BF16_ROWS = 16
ROW_SUB = 256
VMEM_LIMIT = 58 * 1024 * 1024

F32 = jnp.float32
BF16 = jnp.bfloat16


def _bdot(a, b):
    return jnp.dot(a.astype(BF16), b.astype(BF16), preferred_element_type=F32)


def _bdot_nt(a, b):
    return lax.dot_general(a.astype(BF16), b.astype(BF16), (((1,), (1,)), ((), ())),
                           preferred_element_type=F32)


def _bdot_tn(a, b):
    return lax.dot_general(a.astype(BF16), b.astype(BF16), (((0,), (0,)), ((), ())),
                           preferred_element_type=F32)


def _sigmoid(x):
    return 1.0 / (1.0 + jnp.exp(-x))


def _silu(x):
    return x * _sigmoid(x)


def _rms(x, g):
    return x * lax.rsqrt(jnp.mean(x * x, axis=-1, keepdims=True) + EPS) * g


def _norm_matmul_kernel(x_ref, g_ref, w_ref, o_ref, *, rs):
    for r in range(x_ref.shape[0] // rs):
        rows = slice(r * rs, (r + 1) * rs)
        hn = _rms(x_ref[rows, :], g_ref[...]).astype(BF16)
        o_ref[rows, :] = jnp.dot(hn, w_ref[...], preferred_element_type=F32).astype(o_ref.dtype)


def _norm_matmul(x, g, w, *, tm, tn, n=None, out_dtype=F32):
    m, k = x.shape
    n = w.shape[1] if n is None else n
    assert m % tm == 0 and n % tn == 0 and n <= w.shape[1]
    rs = math.gcd(tm, ROW_SUB)
    return pl.pallas_call(
        functools.partial(_norm_matmul_kernel, rs=rs),
        grid=(m // tm, n // tn),
        in_specs=[pl.BlockSpec((tm, k), lambda i, j: (i, 0)),
                  pl.BlockSpec((1, k), lambda i, j: (0, 0)),
                  pl.BlockSpec((k, tn), lambda i, j: (0, j))],
        out_specs=pl.BlockSpec((tm, tn), lambda i, j: (i, j)),
        out_shape=jax.ShapeDtypeStruct((m, n), out_dtype),
        compiler_params=pltpu.CompilerParams(
            dimension_semantics=("parallel", "arbitrary"), vmem_limit_bytes=VMEM_LIMIT),
        name="norm_matmul",
    )(x, g.reshape(1, k), w)


def _unit_lower_inverse(mats, row, col, fill=lambda: None):
    c = mats[0].shape[0]
    eye = (row == col).astype(F32)
    blk = lambda n: (row // n) == (col // n)
    inner = blk(16)
    ds = [jnp.where(inner, a, 0.0) for a in mats]
    ts = [eye - d for d in ds]
    ps = [_bdot(d, d) for d in ds]
    fill()
    for step in range(3):
        ts = [t + _bdot(t, p) for t, p in zip(ts, ps)]
        fill()
        if step < 2:
            ps = [_bdot(p, p) for p in ps]
            fill()
    half_row = lax.broadcasted_iota(jnp.int32, (c // 2, c), 0)
    half_col = lax.broadcasted_iota(jnp.int32, (c // 2, c), 1)
    n = 32
    while n <= c:
        h = n // 2
        pairs = range(c // n)
        lower = lambda m: jnp.concatenate([m[(2 * k + 1) * h:(2 * k + 2) * h] for k in pairs], axis=0)
        if n < GDN_HALF_ROWS_FROM:
            keep = blk(n) & ~inner
            ys = [lower(_bdot(jnp.where(keep, a, 0.0), t)) for a, t in zip(mats, ts)]
        else:
            left = (half_col // h) == 2 * (half_row // h)
            ys = [_bdot(jnp.where(left, lower(a), 0.0), t) for a, t in zip(mats, ts)]
        fill()
        zero = jnp.zeros((h, c), F32)
        spread = lambda y: jnp.concatenate(
            [part for k in pairs for part in (zero, y[k * h:(k + 1) * h])], axis=0)
        zs = [_bdot(lower(t), spread(y)) for t, y in zip(ts, ys)]
        ts = [jnp.concatenate([part for k in pairs for part in
                               (t[2 * k * h:(2 * k + 1) * h],
                                t[(2 * k + 1) * h:(2 * k + 2) * h] - z[k * h:(k + 1) * h])], axis=0)
              for t, z in zip(ts, zs)]
        fill()
        inner = blk(n)
        n *= 2
    return ts


def _gdn_body(h, alog_ref, dtb_ref, q_ref, k_ref, v_ref, z_ref, bat_ref, wq_ref, wk_ref, wv_ref,
              ng_ref, o_ref, u_ref, wqd_ref, qk_ref, kw_ref, bc_ref, gl_ref, sb_ref, *, group,
              fill=lambda n=1: None):
    seq = q_ref.shape[1]
    C = GDN_CHUNK
    D = HEAD_DIM
    pad = SUBLANES
    n_chunks = seq // C

    static_chunks = n_chunks == group
    row = lax.broadcasted_iota(jnp.int32, (C, C), 0)
    col = lax.broadcasted_iota(jnp.int32, (C, C), 1)
    tri_incl = row >= col
    tri_strict = row > col
    lane = lax.broadcasted_iota(jnp.int32, (SUBLANES, C), 1)
    neg_a = -jnp.exp(jnp.full((1, C), alog_ref[h], F32))
    dt_bias = dtb_ref[h]
    scale = D ** -0.5

    def conv_silu(x_ref, w_ref, c, r0):
        if static_chunks:
            win = (jnp.concatenate([jnp.zeros((pad, D), F32), x_ref[0, 0:C, :]], axis=0) if c == 0
                   else x_ref[0, r0 - pad:r0 + C, :])
        else:
            halo = x_ref[0, pl.ds(jnp.maximum(r0 - pad, 0), pad), :]
            win = jnp.concatenate([jnp.where(c > 0, halo, 0.0), x_ref[0, pl.ds(r0, C), :]], axis=0)
        w = w_ref[...]
        prev = pltpu.roll(win, 1, axis=0)
        near = win * w[3:4] + prev * w[2:3]
        far = pltpu.roll(win * w[1:2] + prev * w[0:1], 2, axis=0)
        return _silu(near[pad:pad + C] + far[pad:pad + C])

    def l2n(x):
        return x * lax.rsqrt(jnp.sum(x * x, axis=-1, keepdims=True) + EPS)

    def cumsum_lanes(x):
        x = jnp.broadcast_to(x, (SUBLANES, C))
        s = 1
        while s < C:
            x = x + jnp.where(lane >= s, pltpu.roll(x, s, axis=1), 0.0)
            s *= 2
        return x[0:1, :]

    reps = C // D

    def rows_to_cols(x):
        return jnp.concatenate(
            [jnp.broadcast_to(x[:, n * D:(n + 1) * D], (D, D)).T for n in range(reps)], axis=0)

    def chunk_ids(grp):
        cs = [grp * group + i for i in range(group)]
        return cs, [c * C if static_chunks else pl.multiple_of(c * C, C) for c in cs]

    def for_each_group(body):
        if static_chunks:
            body(0, 0)
        else:
            lax.fori_loop(0, n_chunks // group, body, 0)

    def prepare(grp, carry):
        cs, r0s = chunk_ids(grp)
        pre, gbs, g_rows = [], [], []
        for c, r0 in zip(cs, r0s):
            q = l2n(conv_silu(q_ref, wq_ref, c, r0)) * scale
            k = l2n(conv_silu(k_ref, wk_ref, c, r0))
            v = conv_silu(v_ref, wv_ref, c, r0)
            b_row = bat_ref[0, h, pl.ds(c, 1), :]
            xs = bat_ref[0, h + N_GDN_HEADS, pl.ds(c, 1), :] + dt_bias
            softplus = jnp.maximum(xs, 0.0) + jnp.log1p(jnp.exp(-jnp.abs(xs)))
            g_row = cumsum_lanes(neg_a * softplus)
            pre.append((q, k, v, rows_to_cols(_sigmoid(b_row)), None))
            g_rows.append(g_row)
            gbs.append(rows_to_cols(g_row))
            fill()
        kk_qks = [_bdot_nt(jnp.concatenate([k * beta, q], axis=0), k) for q, k, v, beta, _ in pre]
        mats, mids = [], []
        for (q, k, v, beta, _), gb, g_row, kk_qk in zip(pre, gbs, g_rows, kk_qks):
            g_i = jnp.concatenate([gb] * reps, axis=1)
            g_j = jnp.broadcast_to(g_row, (C, C))
            decay = jnp.exp(jnp.where(tri_incl, g_i - g_j, NEG))
            mats.append(jnp.where(tri_strict, kk_qk[:C] * decay, 0.0))
            mids.append((kk_qk[C:] * decay, jnp.exp(gb), gb[C - 1:C, :]))
        ts = _unit_lower_inverse(mats, row, col, fill=fill)
        uws = [_bdot(t, jnp.concatenate([v * beta, k * beta * eg], axis=1))
               for t, (q, k, v, beta, _), (_, eg, _) in zip(ts, pre, mids)]
        transs = [_bdot_tn(k * jnp.exp(g_last - gb), uw)
                  for uw, (q, k, v, beta, _), gb, (_, _, g_last) in zip(uws, pre, gbs, mids)]
        for c, r0, uw, trans, (q, k, v, beta, _), (qk, eg, g_last) in zip(cs, r0s, uws, transs, pre, mids):
            u_ref[pl.ds(r0, C), :] = uw[:, :D]
            wqd_ref[c, 0:C, :] = uw[:, D:].astype(BF16)
            wqd_ref[c, C:2 * C, :] = (q * eg).astype(BF16)
            qk_ref[pl.ds(r0, C), :] = qk.astype(BF16)
            bc_ref[c] = trans[:, :D]
            kw_ref[c] = trans[:, D:].astype(BF16)
            gl_ref[c] = jnp.exp(g_last)
        return carry

    for_each_group(prepare)

    def chain(c, state):
        sb = state.astype(BF16)
        sb_ref[c] = sb
        return state * gl_ref[c] - jnp.dot(kw_ref[c], sb, preferred_element_type=F32) + bc_ref[c]

    if static_chunks:
        state = jnp.zeros((D, D), F32)
        for c in range(n_chunks):
            state = chain(c, state)
            fill(GDN_FILL_PER_STEP)
    else:
        lax.fori_loop(0, n_chunks, chain, jnp.zeros((D, D), F32))

    def outputs(grp, carry):
        cs, r0s = chunk_ids(grp)
        ws_qs = [jnp.dot(wqd_ref[c], sb_ref[c], preferred_element_type=F32) for c in cs]
        v_news = [(u_ref[pl.ds(r0, C), :] - wq[:C]).astype(BF16) for r0, wq in zip(r0s, ws_qs)]
        os = [wq[C:] + jnp.dot(qk_ref[pl.ds(r0, C), :], vn, preferred_element_type=F32)
              for r0, wq, vn in zip(r0s, ws_qs, v_news)]
        for r0, o in zip(r0s, os):
            z = z_ref[0, pl.ds(r0, C), :]
            o_ref[0, pl.ds(r0, C), :] = (_rms(o, ng_ref[...]) * _silu(z)).astype(o_ref.dtype)
        return carry

    for_each_group(outputs)


def _bucket_upper_bounds():
    n = np.arange(0, 4 * REL_MAX_DIST, dtype=np.int64)
    max_exact = REL_BUCKETS // 2
    nf = np.maximum(n, 1).astype(np.float32)
    large = max_exact + (np.log(nf / np.float32(max_exact)) / np.float32(math.log(REL_MAX_DIST / max_exact))
                         * np.float32(REL_BUCKETS - max_exact)).astype(np.int32)
    large = np.minimum(large, REL_BUCKETS - 1)
    bucket = np.where(n < max_exact, n, large)
    assert np.all(np.diff(bucket) >= 0) and bucket[-1] == REL_BUCKETS - 1
    return [int(np.searchsorted(bucket, b, side="right")) for b in range(REL_BUCKETS - 1)]


_BUCKET_UPPER = _bucket_upper_bounds()


def _moba_steps(h, b, rb_ref, q_ref, k_ref, v_ref, ng_ref, o_ref, vt_ref, km_ref, bd_ref, bl_ref):
    seq = k_ref.shape[1]
    T = MOBA_BLOCK
    nb = seq // T
    scale = HEAD_DIM ** -0.5 * LOG2E
    kk = lax.broadcasted_iota(jnp.int32, (T, T), 0)
    qq = lax.broadcasted_iota(jnp.int32, (T, T), 1)

    @pl.when(b == 0)
    def _():
        def bias_of(n):
            val = jnp.full((T, T), rb_ref[REL_BUCKETS - 1, h], F32)
            for bkt in range(REL_BUCKETS - 2, -1, -1):
                val = jnp.where(n < _BUCKET_UPPER[bkt], rb_ref[bkt, h], val)
            return val
        bd_ref[...] = jnp.where(qq >= kk, bias_of(qq - kk) * LOG2E, NEG)
        bl_ref[...] = bias_of(qq - kk + T) * LOG2E

    for n in range(nb):
        blk_rows = slice(n * T, (n + 1) * T)
        vt_ref[0:HEAD_DIM, blk_rows] = v_ref[0, blk_rows, :].astype(F32).T.astype(BF16)
        km_ref[n:n + 1, :] = jnp.mean(k_ref[0, blk_rows, :].astype(F32), axis=0, keepdims=True)
    vt_ref[HEAD_DIM:HEAD_DIM + BF16_ROWS, :] = jnp.ones((BF16_ROWS, seq), BF16)
    far_bias = rb_ref[REL_BUCKETS - 1, h] * LOG2E

    def additive_mask(i, qf):
        if i <= MOBA_TOPK:
            return None
        gate = lax.dot_general(km_ref[0:i, :], qf, (((1,), (1,)), ((), ())),
                               preferred_element_type=F32, precision=lax.Precision.HIGHEST)
        blk = lax.broadcasted_iota(jnp.int32, (i, T), 0)
        rank = jnp.zeros((i, T), jnp.int32)
        for m in range(i):
            gm = gate[m:m + 1, :]
            rank = rank + jnp.where((gm > gate) | ((gm == gate) & (m < blk)), 1, 0)
        return jnp.where(rank < MOBA_TOPK, 0.0, NEG)

    def biased(i, s_all, keep):
        parts, maxes = [], []
        for j in range(i + 1):
            sj = s_all[j * T:(j + 1) * T, :]
            if j >= i - 1:
                sj = sj + (bd_ref[...] if j == i else bl_ref[...])
                row = keep[j:j + 1, :] if (keep is not None and j < i) else 0.0
            else:
                row = far_bias if keep is None else keep[j:j + 1, :] + far_bias
            parts.append((sj, row))
            maxes.append(jnp.max(sj, axis=0, keepdims=True) + row)
        return parts, functools.reduce(jnp.maximum, maxes)

    order = [t for pair in zip(range(nb // 2), range(nb - 1, nb // 2 - 1, -1)) for t in pair]
    if nb % 2:
        order.append(nb // 2)
    for g0 in range(0, nb, MOBA_TILES_IN_FLIGHT):
        tiles = order[g0:g0 + MOBA_TILES_IN_FLIGHT]
        qfs = [q_ref[0, i * T:(i + 1) * T, :].astype(F32) for i in tiles]
        keeps, s_alls, partss, m_rows, p_alls, o_augs = [], [], [], [], [], []
        for i, qf in zip(tiles, qfs):
            keeps.append(additive_mask(i, qf))
            s_alls.append(_bdot_nt(k_ref[0, 0:(i + 1) * T, :], qf * scale))
            yield
        for i, s_all, keep in zip(tiles, s_alls, keeps):
            parts, m_row = biased(i, s_all, keep)
            partss.append(parts)
            m_rows.append(m_row)
            yield
        for parts, m_row in zip(partss, m_rows):
            p_alls.append(jnp.concatenate(
                [jnp.exp2((sj + (row - m_row)).astype(BF16)) for sj, row in parts], axis=0))
            yield
        for i, p_all in zip(tiles, p_alls):
            o_augs.append(jnp.dot(vt_ref[:, 0:(i + 1) * T], p_all, preferred_element_type=F32))
            yield
        for i, o in zip(tiles, o_augs):
            o_t = o[0:HEAD_DIM] / o[HEAD_DIM:HEAD_DIM + 1]
            o_ref[0, i * T:(i + 1) * T, :] = _rms(o_t.T, ng_ref[...]).astype(o_ref.dtype)
            yield


def _mixer_kernel(alog_ref, dtb_ref, rb_ref, q_ref, k_ref, v_ref, z_ref, bat_ref, wq_ref, wk_ref, wv_ref,
                  gng_ref, mq_ref, mk_ref, mv_ref, mng_ref, oa_ref, ob_ref,
                  u_ref, wqd_ref, qk_ref, kw_ref, bc_ref, gl_ref, sb_ref, vt_ref, km_ref, bd_ref, bl_ref,
                  *, group):
    h = pl.program_id(0)
    b = pl.program_id(1)
    moba = _moba_steps(h, b, rb_ref, mq_ref, mk_ref, mv_ref, mng_ref, ob_ref, vt_ref, km_ref, bd_ref, bl_ref)

    def fill(n=1):
        for _ in range(n):
            next(moba, None)

    _gdn_body(h, alog_ref, dtb_ref, q_ref, k_ref, v_ref, z_ref, bat_ref, wq_ref, wk_ref, wv_ref,
              gng_ref, oa_ref, u_ref, wqd_ref, qk_ref, kw_ref, bc_ref, gl_ref, sb_ref, group=group, fill=fill)
    for _ in moba:
        pass


def _mixer(pa, pb, conv_w, a_log, dt_bias, gdn_norm_g, rel_bias, moba_norm_g, *, group=8):
    b, s, _ = pa.shape
    H = N_GDN_HEADS
    assert N_MOBA_HEADS == H
    C, T = GDN_CHUNK, MOBA_BLOCK
    assert s % C == 0 and s % T == 0
    n_chunks, nb = s // C, s // T
    bat = pa[:, :, 4 * GDN_WIDTH:4 * GDN_WIDTH + 2 * H].transpose(0, 2, 1).reshape(b, 2 * H, n_chunks, C)
    head = lambda off: pl.BlockSpec((1, s, HEAD_DIM), lambda h, i: (i, 0, off + h))
    cw = lambda off: pl.BlockSpec((GDN_CONV, HEAD_DIM), lambda h, i: (0, off + h))
    vec = pl.BlockSpec((1, HEAD_DIM), lambda h, i: (0, 0))
    smem = pl.BlockSpec(memory_space=pltpu.SMEM)
    out = pl.BlockSpec((1, s, HEAD_DIM), lambda h, i: (i, 0, h))
    return pl.pallas_call(
        functools.partial(_mixer_kernel, group=math.gcd(group, n_chunks)),
        grid=(H, b),
        in_specs=[smem, smem, smem, head(0), head(H), head(2 * H), head(3 * H),
                  pl.BlockSpec((1, 2 * H, n_chunks, C), lambda h, i: (i, 0, 0, 0)),
                  cw(0), cw(H), cw(2 * H), vec,
                  head(0), head(H), head(2 * H), vec],
        out_specs=[out, out],
        out_shape=[jax.ShapeDtypeStruct((b, s, GDN_WIDTH), BF16), jax.ShapeDtypeStruct((b, s, MOBA_WIDTH), BF16)],
        scratch_shapes=[
            pltpu.VMEM((s, HEAD_DIM), F32),
            pltpu.VMEM((n_chunks, 2 * C, HEAD_DIM), BF16),
            pltpu.VMEM((s, C), BF16),
            pltpu.VMEM((n_chunks, HEAD_DIM, HEAD_DIM), BF16),
            pltpu.VMEM((n_chunks, HEAD_DIM, HEAD_DIM), F32),
            pltpu.VMEM((n_chunks, 1, HEAD_DIM), F32),
            pltpu.VMEM((n_chunks, HEAD_DIM, HEAD_DIM), BF16),
            pltpu.VMEM((HEAD_DIM + BF16_ROWS, s), BF16),
            pltpu.VMEM((nb, HEAD_DIM), F32),
            pltpu.VMEM((T, T), F32), pltpu.VMEM((T, T), F32)],
        compiler_params=pltpu.CompilerParams(
            dimension_semantics=("arbitrary", "arbitrary"), vmem_limit_bytes=VMEM_LIMIT),
        name="mixer",
    )(a_log, dt_bias, rel_bias, pa, pa, pa, pa, bat, conv_w, conv_w, conv_w, gdn_norm_g.reshape(1, HEAD_DIM),
      pb, pb, pb, moba_norm_g.reshape(1, HEAD_DIM))


def _xattn_kernel(x_ref, oa_ref, ob_ref, wa_ref, wb_ref, g_ref, wq_ref, kv_ref, wo_ref, ng_ref,
                  o_ref, hn_ref, *, rs):
    width = N_XATTN_HEADS * HEAD_DIM
    scale = HEAD_DIM ** -0.5 * LOG2E
    blocks = [slice(r * rs, (r + 1) * rs) for r in range(x_ref.shape[1] // rs)]
    x1s = [x_ref[0, rows, :]
           + jnp.dot(oa_ref[0, rows, :], wa_ref[...], preferred_element_type=F32)
           + jnp.dot(ob_ref[0, rows, :], wb_ref[...], preferred_element_type=F32) for rows in blocks]
    qs = [_bdot(_rms(x1, g_ref[...]), wq_ref[...]) * scale for x1 in x1s]
    outs = [[] for _ in blocks]
    for hd in range(N_XATTN_HEADS):
        sl = slice(hd * HEAD_DIM, (hd + 1) * HEAD_DIM)
        ss = [_bdot_nt(q[:, sl], kv_ref[0, :, sl]) for q in qs]
        ps = [jnp.exp2(s - jnp.max(s, axis=-1, keepdims=True)) for s in ss]
        for out, p in zip(outs, ps):
            l = jnp.sum(p, axis=-1, keepdims=True)
            out.append(_bdot(p, kv_ref[0, :, width + hd * HEAD_DIM: width + (hd + 1) * HEAD_DIM]) / l)
    ys = [x1 + _bdot(jnp.concatenate(out, axis=-1), wo_ref[...]) for x1, out in zip(x1s, outs)]
    for rows, y in zip(blocks, ys):
        o_ref[0, rows, :] = y
        hn_ref[0, rows, :] = _rms(y, ng_ref[...]).astype(BF16)


def _xattn(x, oa, ob, wa, wb, g, wq, kv, wo, next_g, *, ts):
    b, s, d = x.shape
    mlen = kv.shape[1]
    width = wq.shape[1]
    ka, kb = oa.shape[2], ob.shape[2]
    tile = pl.BlockSpec((1, ts, d), lambda i, j: (i, j, 0))
    vec = pl.BlockSpec((1, d), lambda i, j: (0, 0))
    whole = lambda r, c: pl.BlockSpec((r, c), lambda i, j: (0, 0))
    return pl.pallas_call(
        functools.partial(_xattn_kernel, rs=math.gcd(ts, ROW_SUB)),
        grid=(b, s // ts),
        in_specs=[tile,
                  pl.BlockSpec((1, ts, ka), lambda i, j: (i, j, 0)),
                  pl.BlockSpec((1, ts, kb), lambda i, j: (i, j, 0)),
                  whole(ka, d), whole(kb, d), vec, whole(d, width),
                  pl.BlockSpec((1, mlen, 2 * width), lambda i, j: (i, 0, 0)),
                  whole(width, d), vec],
        out_specs=[tile, tile],
        out_shape=[jax.ShapeDtypeStruct((b, s, d), F32), jax.ShapeDtypeStruct((b, s, d), BF16)],
        compiler_params=pltpu.CompilerParams(
            dimension_semantics=("parallel", "parallel"), vmem_limit_bytes=VMEM_LIMIT),
        name="xattn",
    )(x, oa, ob, wa, wb, g.reshape(1, d), wq, kv, wo, next_g.reshape(1, d))


def _ffn_kernel(x_hbm, h_ref, halo_ref, wg_ref, wu_ref, cw_ref, cb_ref, wd_ref, fg_ref, o_ref,
                hx_ref, xres_ref, sem, *, tiles_per_seq, rs):
    i = pl.program_id(0)
    f = pl.program_id(1)
    tm = h_ref.shape[0]
    pad = BF16_ROWS

    def x_copy():
        return pltpu.make_async_copy(x_hbm.at[pl.ds(i * tm, tm), :], xres_ref, sem)

    @pl.when(f == 0)
    def _():
        x_copy().start()
        halo = halo_ref[...]
        hx_ref[0:pad, :] = jnp.where(i % tiles_per_seq == 0, jnp.zeros_like(halo), halo)
        hx_ref[pad:pad + tm, :] = h_ref[...]
        o_ref[...] = jnp.zeros_like(o_ref)

    cw = cw_ref[...]
    keep = SUBLANES
    tail = None
    for r in range(tm // rs):
        rows = slice(r * rs, (r + 1) * rs)
        hrows = slice(pad + r * rs, pad + (r + 1) * rs)
        if r == 0:
            gp = jnp.dot(hx_ref[0:pad + rs, :], wg_ref[...], preferred_element_type=F32)[pad - keep:]
        else:
            gp = jnp.concatenate(
                [tail, jnp.dot(hx_ref[hrows, :], wg_ref[...], preferred_element_type=F32)], axis=0)
        tail = gp[rs:rs + keep]
        up = jnp.dot(hx_ref[hrows, :], wu_ref[...], preferred_element_type=F32)
        gate = gp[keep:keep + rs] * cw[2:3] + gp[keep - 1:keep - 1 + rs] * cw[1:2] \
            + gp[keep - 2:keep - 2 + rs] * cw[0:1] + cb_ref[...]
        act = _silu(gate) * up
        o_ref[rows, :] += jnp.dot(act.astype(BF16), wd_ref[...], preferred_element_type=F32)

    @pl.when(f == pl.num_programs(1) - 1)
    def _():
        x_copy().wait()

        def finish(r, carry):
            rows = pl.ds(pl.multiple_of(r * HEAD_DIM, HEAD_DIM), HEAD_DIM)
            o_ref[rows, :] = _rms(xres_ref[rows, :] + o_ref[rows, :], fg_ref[...])
            return carry

        lax.fori_loop(0, tm // HEAD_DIM, finish, 0)


def _ffn(x, h, wg, wu, cw, cb, wd, fg, *, seq, tm, tf):
    m, d = x.shape
    ff = wg.shape[1]
    assert seq % tm == 0 and ff % tf == 0 and tm % BF16_ROWS == 0
    hb = tm // BF16_ROWS
    return pl.pallas_call(
        functools.partial(_ffn_kernel, tiles_per_seq=seq // tm, rs=tm),
        grid=(m // tm, ff // tf),
        in_specs=[pl.BlockSpec(memory_space=pl.ANY),
                  pl.BlockSpec((tm, d), lambda i, f: (i, 0)),
                  pl.BlockSpec((BF16_ROWS, d), lambda i, f: (jnp.maximum(i * hb - 1, 0), 0)),
                  pl.BlockSpec((d, tf), lambda i, f: (0, f)),
                  pl.BlockSpec((d, tf), lambda i, f: (0, f)),
                  pl.BlockSpec((FFN_CONV, tf), lambda i, f: (0, f)),
                  pl.BlockSpec((1, tf), lambda i, f: (0, f)),
                  pl.BlockSpec((tf, d), lambda i, f: (f, 0)),
                  pl.BlockSpec((1, d), lambda i, f: (0, 0))],
        out_specs=pl.BlockSpec((tm, d), lambda i, f: (i, 0)),
        out_shape=jax.ShapeDtypeStruct((m, d), F32),
        scratch_shapes=[pltpu.VMEM((tm + BF16_ROWS, d), BF16), pltpu.VMEM((tm, d), F32),
                        pltpu.SemaphoreType.DMA(())],
        compiler_params=pltpu.CompilerParams(
            dimension_semantics=("arbitrary", "arbitrary"), vmem_limit_bytes=VMEM_LIMIT),
        name="ffn",
    )(x, h, h, wg, wu, cw, cb.reshape(1, ff), wd, fg.reshape(1, d))


def _layer(x, mem, mix_norm_g, w_in, gdn_conv_w, a_log, dt_bias, gdn_norm_g, moba_norm_g, rel_bias,
           w_out, xattn_norm_g, mem_norm_g, w_xq, w_xkv, w_xo, ffn_norm_g, w_gate, w_up, ffn_conv_w,
           ffn_conv_b, w_down, final_g, *, last):
    b, s, d = x.shape
    m = b * s
    x2 = x.reshape(m, d)
    i1 = 4 * GDN_WIDTH
    i3 = i1 + 2 * N_GDN_HEADS
    w_all = w_in.astype(BF16)
    w_b = w_all[:, i3:]
    tm = min(1024, m)
    wa_cols = i1 + HEAD_DIM
    pa = _norm_matmul(x2, mix_norm_g, w_all, tm=tm, tn=wa_cols // 3, n=wa_cols).reshape(b, s, wa_cols)
    pb = _norm_matmul(x2, mix_norm_g, w_b, tm=tm, tn=1024, out_dtype=BF16).reshape(b, s, 3 * MOBA_WIDTH)
    o_a, o_b = _mixer(pa, pb, gdn_conv_w, a_log, dt_bias, gdn_norm_g, rel_bias, moba_norm_g)
    mlen = mem.shape[1]
    kv = _norm_matmul(mem.reshape(b * mlen, d), mem_norm_g, w_xkv.astype(BF16),
                      tm=min(512, b * mlen), tn=1024, out_dtype=BF16).reshape(b, mlen, -1)
    x2b, h2 = _xattn(x, o_a, o_b, w_out[:GDN_WIDTH].astype(BF16), w_out[GDN_WIDTH:].astype(BF16),
                     xattn_norm_g, w_xq.astype(BF16), kv, w_xo.astype(BF16), ffn_norm_g, ts=min(512, s))
    assert last, "the final rmsnorm is fused into the last layer's ffn"
    y = _ffn(x2b.reshape(m, d), h2.reshape(m, d), w_gate.astype(BF16), w_up.astype(BF16), ffn_conv_w,
             ffn_conv_b, w_down.astype(BF16), final_g, seq=s, tm=min(1024, s), tf=512)
    return y.reshape(b, s, d)


def kernel(x, mem, mix_norm_g, w_in, gdn_conv_w, gdn_a_log, gdn_dt_bias, gdn_norm_g, moba_norm_g,
           rel_bias, w_out, xattn_norm_g, mem_norm_g, w_xq, w_xkv, w_xo, ffn_norm_g, w_gate, w_up,
           ffn_conv_w, ffn_conv_b, w_down, final_norm_g):
    depth = mix_norm_g.shape[0]
    assert depth == 1
    l = 0
    return _layer(x, mem, mix_norm_g[l], w_in[l], gdn_conv_w[l], gdn_a_log[l], gdn_dt_bias[l],
                  gdn_norm_g[l], moba_norm_g[l], rel_bias, w_out[l], xattn_norm_g[l], mem_norm_g[l],
                  w_xq[l], w_xkv[l], w_xo[l], ffn_norm_g[l], w_gate[l], w_up[l], ffn_conv_w[l],
                  ffn_conv_b[l], w_down[l], final_norm_g, last=True)
```

```python
import functools
import math

import jax
import jax.numpy as jnp
import numpy as np
from jax import lax
from jax.experimental import pallas as pl
from jax.experimental.pallas import tpu as pltpu

HEAD_DIM = 128
N_GDN_HEADS = 8
N_MOBA_HEADS = 8
GDN_WIDTH = N_GDN_HEADS * HEAD_DIM
MOBA_WIDTH = N_MOBA_HEADS * HEAD_DIM
GDN_CONV = 4
GDN_CHUNK = 256
GDN_FILL_PER_STEP = 2
GDN_HALF_ROWS_FROM = 64
MOBA_BLOCK = 256
MOBA_TOPK = 3
MOBA_TILES_IN_FLIGHT = 8
REL_BUCKETS = 32
REL_MAX_DIST = 128
N_XATTN_HEADS = 4
FFN_CONV = 3
EPS = 1e-6
NEG = -1e30
LOG2E = math.log2(math.e)
SUBLANES = 8
BF16_ROWS = 16
ROW_SUB = 256
VMEM_LIMIT = 58 * 1024 * 1024

F32 = jnp.float32
BF16 = jnp.bfloat16


def _bdot(a, b):
    return jnp.dot(a.astype(BF16), b.astype(BF16), preferred_element_type=F32)


def _bdot_nt(a, b):
    return lax.dot_general(a.astype(BF16), b.astype(BF16), (((1,), (1,)), ((), ())),
                           preferred_element_type=F32)


def _bdot_tn(a, b):
    return lax.dot_general(a.astype(BF16), b.astype(BF16), (((0,), (0,)), ((), ())),
                           preferred_element_type=F32)


def _sigmoid(x):
    return 1.0 / (1.0 + jnp.exp(-x))


def _silu(x):
    return x * _sigmoid(x)


def _rms(x, g):
    return x * lax.rsqrt(jnp.mean(x * x, axis=-1, keepdims=True) + EPS) * g


def _cast_split_kernel(w_ref, oa_ref, ob_ref, *, b_start):
    oa_ref[...] = w_ref[:, 0:oa_ref.shape[1]].astype(BF16)
    ob_ref[...] = w_ref[:, b_start:b_start + ob_ref.shape[1]].astype(BF16)


def _cast_split(w, *, a_cols, b_start, tr):
    k, n = w.shape
    b_cols = n - b_start
    assert k % tr == 0
    return pl.pallas_call(
        functools.partial(_cast_split_kernel, b_start=b_start),
        grid=(k // tr,),
        in_specs=[pl.BlockSpec((tr, n), lambda i: (i, 0))],
        out_specs=[pl.BlockSpec((tr, a_cols), lambda i: (i, 0)), pl.BlockSpec((tr, b_cols), lambda i: (i, 0))],
        out_shape=[jax.ShapeDtypeStruct((k, a_cols), BF16), jax.ShapeDtypeStruct((k, b_cols), BF16)],
        compiler_params=pltpu.CompilerParams(
            dimension_semantics=("parallel",), vmem_limit_bytes=VMEM_LIMIT),
        name="cast_split",
    )(w)


def _norm_matmul_kernel(x_ref, g_ref, w_ref, o_ref, *, rs):
    for r in range(x_ref.shape[0] // rs):
        rows = slice(r * rs, (r + 1) * rs)
        hn = _rms(x_ref[rows, :], g_ref[...]).astype(BF16)
        o_ref[rows, :] = jnp.dot(hn, w_ref[...], preferred_element_type=F32).astype(o_ref.dtype)


def _norm_matmul(x, g, w, *, tm, tn, n=None, out_dtype=F32):
    m, k = x.shape
    n = w.shape[1] if n is None else n
    assert m % tm == 0 and n % tn == 0 and n <= w.shape[1]
    rs = math.gcd(tm, ROW_SUB)
    return pl.pallas_call(
        functools.partial(_norm_matmul_kernel, rs=rs),
        grid=(m // tm, n // tn),
        in_specs=[pl.BlockSpec((tm, k), lambda i, j: (i, 0)),
                  pl.BlockSpec((1, k), lambda i, j: (0, 0)),
                  pl.BlockSpec((k, tn), lambda i, j: (0, j))],
        out_specs=pl.BlockSpec((tm, tn), lambda i, j: (i, j)),
        out_shape=jax.ShapeDtypeStruct((m, n), out_dtype),
        compiler_params=pltpu.CompilerParams(
            dimension_semantics=("parallel", "arbitrary"), vmem_limit_bytes=VMEM_LIMIT),
        name="norm_matmul",
    )(x, g.reshape(1, k), w)


def _unit_lower_inverse(mats, row, col, fill=lambda: None):
    c = mats[0].shape[0]
    eye = (row == col).astype(F32)
    blk = lambda n: (row // n) == (col // n)
    inner = blk(16)
    ds = [jnp.where(inner, a, 0.0) for a in mats]
    ts = [eye - d for d in ds]
    ps = [_bdot(d, d) for d in ds]
    fill()
    for step in range(3):
        ts = [t + _bdot(t, p) for t, p in zip(ts, ps)]
        fill()
        if step < 2:
            ps = [_bdot(p, p) for p in ps]
            fill()
    half_row = lax.broadcasted_iota(jnp.int32, (c // 2, c), 0)
    half_col = lax.broadcasted_iota(jnp.int32, (c // 2, c), 1)
    n = 32
    while n <= c:
        h = n // 2
        pairs = range(c // n)
        lower = lambda m: jnp.concatenate([m[(2 * k + 1) * h:(2 * k + 2) * h] for k in pairs], axis=0)
        if n < GDN_HALF_ROWS_FROM:
            keep = blk(n) & ~inner
            ys = [lower(_bdot(jnp.where(keep, a, 0.0), t)) for a, t in zip(mats, ts)]
        else:
            left = (half_col // h) == 2 * (half_row // h)
            ys = [_bdot(jnp.where(left, lower(a), 0.0), t) for a, t in zip(mats, ts)]
        fill()
        zero = jnp.zeros((h, c), F32)
        spread = lambda y: jnp.concatenate(
            [part for k in pairs for part in (zero, y[k * h:(k + 1) * h])], axis=0)
        zs = [_bdot(lower(t), spread(y)) for t, y in zip(ts, ys)]
        ts = [jnp.concatenate([part for k in pairs for part in
                               (t[2 * k * h:(2 * k + 1) * h],
                                t[(2 * k + 1) * h:(2 * k + 2) * h] - z[k * h:(k + 1) * h])], axis=0)
              for t, z in zip(ts, zs)]
        fill()
        inner = blk(n)
        n *= 2
    return ts


def _gdn_body(h, alog_ref, dtb_ref, q_ref, k_ref, v_ref, z_ref, bat_ref, wq_ref, wk_ref, wv_ref,
              ng_ref, o_ref, u_ref, wqd_ref, qk_ref, kw_ref, bc_ref, gl_ref, sb_ref, *, group,
              fill=lambda n=1: None):
    seq = q_ref.shape[1]
    C = GDN_CHUNK
    D = HEAD_DIM
    pad = SUBLANES
    n_chunks = seq // C

    static_chunks = n_chunks == group
    row = lax.broadcasted_iota(jnp.int32, (C, C), 0)
    col = lax.broadcasted_iota(jnp.int32, (C, C), 1)
    tri_incl = row >= col
    tri_strict = row > col
    lane = lax.broadcasted_iota(jnp.int32, (SUBLANES, C), 1)
    neg_a = -jnp.exp(jnp.full((1, C), alog_ref[h], F32))
    dt_bias = dtb_ref[h]
    scale = D ** -0.5

    def conv_silu(x_ref, w_ref, c, r0):
        if static_chunks:
            win = (jnp.concatenate([jnp.zeros((pad, D), F32), x_ref[0, 0:C, :]], axis=0) if c == 0
                   else x_ref[0, r0 - pad:r0 + C, :])
        else:
            halo = x_ref[0, pl.ds(jnp.maximum(r0 - pad, 0), pad), :]
            win = jnp.concatenate([jnp.where(c > 0, halo, 0.0), x_ref[0, pl.ds(r0, C), :]], axis=0)
        w = w_ref[...]
        prev = pltpu.roll(win, 1, axis=0)
        near = win * w[3:4] + prev * w[2:3]
        far = pltpu.roll(win * w[1:2] + prev * w[0:1], 2, axis=0)
        return _silu(near[pad:pad + C] + far[pad:pad + C])

    def l2n(x):
        return x * lax.rsqrt(jnp.sum(x * x, axis=-1, keepdims=True) + EPS)

    def cumsum_lanes(x):
        x = jnp.broadcast_to(x, (SUBLANES, C))
        s = 1
        while s < C:
            x = x + jnp.where(lane >= s, pltpu.roll(x, s, axis=1), 0.0)
            s *= 2
        return x[0:1, :]

    reps = C // D

    def rows_to_cols(x):
        return jnp.concatenate(
            [jnp.broadcast_to(x[:, n * D:(n + 1) * D], (D, D)).T for n in range(reps)], axis=0)

    def chunk_ids(grp):
        cs = [grp * group + i for i in range(group)]
        return cs, [c * C if static_chunks else pl.multiple_of(c * C, C) for c in cs]

    def for_each_group(body):
        if static_chunks:
            body(0, 0)
        else:
            lax.fori_loop(0, n_chunks // group, body, 0)

    def prepare(grp, carry):
        cs, r0s = chunk_ids(grp)
        pre, gbs, g_rows = [], [], []
        for c, r0 in zip(cs, r0s):
            q = l2n(conv_silu(q_ref, wq_ref, c, r0)) * scale
            k = l2n(conv_silu(k_ref, wk_ref, c, r0))
            v = conv_silu(v_ref, wv_ref, c, r0)
            b_row = bat_ref[0, h, pl.ds(c, 1), :]
            xs = bat_ref[0, h + N_GDN_HEADS, pl.ds(c, 1), :] + dt_bias
            softplus = jnp.maximum(xs, 0.0) + jnp.log1p(jnp.exp(-jnp.abs(xs)))
            g_row = cumsum_lanes(neg_a * softplus)
            pre.append((q, k, v, rows_to_cols(_sigmoid(b_row)), None))
            g_rows.append(g_row)
            gbs.append(rows_to_cols(g_row))
            fill()
        kk_qks = [_bdot_nt(jnp.concatenate([k * beta, q], axis=0), k) for q, k, v, beta, _ in pre]
        mats, mids = [], []
        for (q, k, v, beta, _), gb, g_row, kk_qk in zip(pre, gbs, g_rows, kk_qks):
            g_i = jnp.concatenate([gb] * reps, axis=1)
            g_j = jnp.broadcast_to(g_row, (C, C))
            decay = jnp.exp(jnp.where(tri_incl, g_i - g_j, NEG))
            mats.append(jnp.where(tri_strict, kk_qk[:C] * decay, 0.0))
            mids.append((kk_qk[C:] * decay, jnp.exp(gb), gb[C - 1:C, :]))
        ts = _unit_lower_inverse(mats, row, col, fill=fill)
        uws = [_bdot(t, jnp.concatenate([v * beta, k * beta * eg], axis=1))
               for t, (q, k, v, beta, _), (_, eg, _) in zip(ts, pre, mids)]
        transs = [_bdot_tn(k * jnp.exp(g_last - gb), uw)
                  for uw, (q, k, v, beta, _), gb, (_, _, g_last) in zip(uws, pre, gbs, mids)]
        for c, r0, uw, trans, (q, k, v, beta, _), (qk, eg, g_last) in zip(cs, r0s, uws, transs, pre, mids):
            u_ref[pl.ds(r0, C), :] = uw[:, :D]
            wqd_ref[c, 0:C, :] = uw[:, D:].astype(BF16)
            wqd_ref[c, C:2 * C, :] = (q * eg).astype(BF16)
            qk_ref[pl.ds(r0, C), :] = qk.astype(BF16)
            bc_ref[c] = trans[:, :D]
            kw_ref[c] = trans[:, D:].astype(BF16)
            gl_ref[c] = jnp.exp(g_last)
        return carry

    for_each_group(prepare)

    def chain(c, state):
        sb = state.astype(BF16)
        sb_ref[c] = sb
        return state * gl_ref[c] - jnp.dot(kw_ref[c], sb, preferred_element_type=F32) + bc_ref[c]

    if static_chunks:
        state = jnp.zeros((D, D), F32)
        for c in range(n_chunks):
            state = chain(c, state)
            fill(GDN_FILL_PER_STEP)
    else:
        lax.fori_loop(0, n_chunks, chain, jnp.zeros((D, D), F32))

    def outputs(grp, carry):
        cs, r0s = chunk_ids(grp)
        ws_qs = [jnp.dot(wqd_ref[c], sb_ref[c], preferred_element_type=F32) for c in cs]
        v_news = [(u_ref[pl.ds(r0, C), :] - wq[:C]).astype(BF16) for r0, wq in zip(r0s, ws_qs)]
        os = [wq[C:] + jnp.dot(qk_ref[pl.ds(r0, C), :], vn, preferred_element_type=F32)
              for r0, wq, vn in zip(r0s, ws_qs, v_news)]
        for r0, o in zip(r0s, os):
            z = z_ref[0, pl.ds(r0, C), :]
            o_ref[0, pl.ds(r0, C), :] = (_rms(o, ng_ref[...]) * _silu(z)).astype(o_ref.dtype)
        return carry

    for_each_group(outputs)


def _bucket_upper_bounds():
    n = np.arange(0, 4 * REL_MAX_DIST, dtype=np.int64)
    max_exact = REL_BUCKETS // 2
    nf = np.maximum(n, 1).astype(np.float32)
    large = max_exact + (np.log(nf / np.float32(max_exact)) / np.float32(math.log(REL_MAX_DIST / max_exact))
                         * np.float32(REL_BUCKETS - max_exact)).astype(np.int32)
    large = np.minimum(large, REL_BUCKETS - 1)
    bucket = np.where(n < max_exact, n, large)
    assert np.all(np.diff(bucket) >= 0) and bucket[-1] == REL_BUCKETS - 1
    return [int(np.searchsorted(bucket, b, side="right")) for b in range(REL_BUCKETS - 1)]


_BUCKET_UPPER = _bucket_upper_bounds()


def _moba_steps(h, b, rb_ref, q_ref, k_ref, v_ref, ng_ref, o_ref, vt_ref, km_ref, bd_ref, bl_ref):
    seq = k_ref.shape[1]
    T = MOBA_BLOCK
    nb = seq // T
    scale = HEAD_DIM ** -0.5 * LOG2E
    kk = lax.broadcasted_iota(jnp.int32, (T, T), 0)
    qq = lax.broadcasted_iota(jnp.int32, (T, T), 1)

    @pl.when(b == 0)
    def _():
        def bias_of(n):
            val = jnp.full((T, T), rb_ref[REL_BUCKETS - 1, h], F32)
            for bkt in range(REL_BUCKETS - 2, -1, -1):
                val = jnp.where(n < _BUCKET_UPPER[bkt], rb_ref[bkt, h], val)
            return val
        bd_ref[...] = jnp.where(qq >= kk, bias_of(qq - kk) * LOG2E, NEG)
        bl_ref[...] = bias_of(qq - kk + T) * LOG2E

    for n in range(nb):
        blk_rows = slice(n * T, (n + 1) * T)
        vt_ref[0:HEAD_DIM, blk_rows] = v_ref[0, blk_rows, :].astype(F32).T.astype(BF16)
        km_ref[n:n + 1, :] = jnp.mean(k_ref[0, blk_rows, :].astype(F32), axis=0, keepdims=True)
    vt_ref[HEAD_DIM:HEAD_DIM + BF16_ROWS, :] = jnp.ones((BF16_ROWS, seq), BF16)
    far_bias = rb_ref[REL_BUCKETS - 1, h] * LOG2E

    def additive_mask(i, qf):
        if i <= MOBA_TOPK:
            return None
        gate = lax.dot_general(km_ref[0:i, :], qf, (((1,), (1,)), ((), ())),
                               preferred_element_type=F32, precision=lax.Precision.HIGHEST)
        blk = lax.broadcasted_iota(jnp.int32, (i, T), 0)
        rank = jnp.zeros((i, T), jnp.int32)
        for m in range(i):
            gm = gate[m:m + 1, :]
            rank = rank + jnp.where((gm > gate) | ((gm == gate) & (m < blk)), 1, 0)
        return jnp.where(rank < MOBA_TOPK, 0.0, NEG)

    def biased(i, s_all, keep):
        parts, maxes = [], []
        for j in range(i + 1):
            sj = s_all[j * T:(j + 1) * T, :]
            if j >= i - 1:
                sj = sj + (bd_ref[...] if j == i else bl_ref[...])
                row = keep[j:j + 1, :] if (keep is not None and j < i) else 0.0
            else:
                row = far_bias if keep is None else keep[j:j + 1, :] + far_bias
            parts.append((sj, row))
            maxes.append(jnp.max(sj, axis=0, keepdims=True) + row)
        return parts, functools.reduce(jnp.maximum, maxes)

    order = [t for pair in zip(range(nb // 2), range(nb - 1, nb // 2 - 1, -1)) for t in pair]
    if nb % 2:
        order.append(nb // 2)
    for g0 in range(0, nb, MOBA_TILES_IN_FLIGHT):
        tiles = order[g0:g0 + MOBA_TILES_IN_FLIGHT]
        qfs = [q_ref[0, i * T:(i + 1) * T, :].astype(F32) for i in tiles]
        keeps, s_alls, partss, m_rows, p_alls, o_augs = [], [], [], [], [], []
        for i, qf in zip(tiles, qfs):
            keeps.append(additive_mask(i, qf))
            s_alls.append(_bdot_nt(k_ref[0, 0:(i + 1) * T, :], qf * scale))
            yield
        for i, s_all, keep in zip(tiles, s_alls, keeps):
            parts, m_row = biased(i, s_all, keep)
            partss.append(parts)
            m_rows.append(m_row)
            yield
        for parts, m_row in zip(partss, m_rows):
            p_alls.append(jnp.concatenate(
                [jnp.exp2((sj + (row - m_row)).astype(BF16)) for sj, row in parts], axis=0))
            yield
        for i, p_all in zip(tiles, p_alls):
            o_augs.append(jnp.dot(vt_ref[:, 0:(i + 1) * T], p_all, preferred_element_type=F32))
            yield
        for i, o in zip(tiles, o_augs):
            o_t = o[0:HEAD_DIM] / o[HEAD_DIM:HEAD_DIM + 1]
            o_ref[0, i * T:(i + 1) * T, :] = _rms(o_t.T, ng_ref[...]).astype(o_ref.dtype)
            yield


def _mixer_kernel(alog_ref, dtb_ref, rb_ref, q_ref, k_ref, v_ref, z_ref, bat_ref, wq_ref, wk_ref, wv_ref,
                  gng_ref, mq_ref, mk_ref, mv_ref, mng_ref, oa_ref, ob_ref,
                  u_ref, wqd_ref, qk_ref, kw_ref, bc_ref, gl_ref, sb_ref, vt_ref, km_ref, bd_ref, bl_ref,
                  *, group):
    h = pl.program_id(0)
    b = pl.program_id(1)
    moba = _moba_steps(h, b, rb_ref, mq_ref, mk_ref, mv_ref, mng_ref, ob_ref, vt_ref, km_ref, bd_ref, bl_ref)

    def fill(n=1):
        for _ in range(n):
            next(moba, None)

    _gdn_body(h, alog_ref, dtb_ref, q_ref, k_ref, v_ref, z_ref, bat_ref, wq_ref, wk_ref, wv_ref,
              gng_ref, oa_ref, u_ref, wqd_ref, qk_ref, kw_ref, bc_ref, gl_ref, sb_ref, group=group, fill=fill)
    for _ in moba:
        pass


def _mixer(pa, pb, conv_w, a_log, dt_bias, gdn_norm_g, rel_bias, moba_norm_g, *, group=8):
    b, s, _ = pa.shape
    H = N_GDN_HEADS
    assert N_MOBA_HEADS == H
    C, T = GDN_CHUNK, MOBA_BLOCK
    assert s % C == 0 and s % T == 0
    n_chunks, nb = s // C, s // T
    bat = pa[:, :, 4 * GDN_WIDTH:4 * GDN_WIDTH + 2 * H].transpose(0, 2, 1).reshape(b, 2 * H, n_chunks, C)
    head = lambda off: pl.BlockSpec((1, s, HEAD_DIM), lambda h, i: (i, 0, off + h))
    cw = lambda off: pl.BlockSpec((GDN_CONV, HEAD_DIM), lambda h, i: (0, off + h))
    vec = pl.BlockSpec((1, HEAD_DIM), lambda h, i: (0, 0))
    smem = pl.BlockSpec(memory_space=pltpu.SMEM)
    out = pl.BlockSpec((1, s, HEAD_DIM), lambda h, i: (i, 0, h))
    return pl.pallas_call(
        functools.partial(_mixer_kernel, group=math.gcd(group, n_chunks)),
        grid=(H, b),
        in_specs=[smem, smem, smem, head(0), head(H), head(2 * H), head(3 * H),
                  pl.BlockSpec((1, 2 * H, n_chunks, C), lambda h, i: (i, 0, 0, 0)),
                  cw(0), cw(H), cw(2 * H), vec,
                  head(0), head(H), head(2 * H), vec],
        out_specs=[out, out],
        out_shape=[jax.ShapeDtypeStruct((b, s, GDN_WIDTH), BF16), jax.ShapeDtypeStruct((b, s, MOBA_WIDTH), BF16)],
        scratch_shapes=[
            pltpu.VMEM((s, HEAD_DIM), F32),
            pltpu.VMEM((n_chunks, 2 * C, HEAD_DIM), BF16),
            pltpu.VMEM((s, C), BF16),
            pltpu.VMEM((n_chunks, HEAD_DIM, HEAD_DIM), BF16),
            pltpu.VMEM((n_chunks, HEAD_DIM, HEAD_DIM), F32),
            pltpu.VMEM((n_chunks, 1, HEAD_DIM), F32),
            pltpu.VMEM((n_chunks, HEAD_DIM, HEAD_DIM), BF16),
            pltpu.VMEM((HEAD_DIM + BF16_ROWS, s), BF16),
            pltpu.VMEM((nb, HEAD_DIM), F32),
            pltpu.VMEM((T, T), F32), pltpu.VMEM((T, T), F32)],
        compiler_params=pltpu.CompilerParams(
            dimension_semantics=("arbitrary", "arbitrary"), vmem_limit_bytes=VMEM_LIMIT),
        name="mixer",
    )(a_log, dt_bias, rel_bias, pa, pa, pa, pa, bat, conv_w, conv_w, conv_w, gdn_norm_g.reshape(1, HEAD_DIM),
      pb, pb, pb, moba_norm_g.reshape(1, HEAD_DIM))


def _xattn_kernel(x_ref, oa_ref, ob_ref, wa_ref, wb_ref, g_ref, wq_ref, kv_ref, wo_ref, ng_ref,
                  o_ref, hn_ref, *, rs):
    width = N_XATTN_HEADS * HEAD_DIM
    scale = HEAD_DIM ** -0.5 * LOG2E
    blocks = [slice(r * rs, (r + 1) * rs) for r in range(x_ref.shape[1] // rs)]
    x1s = [x_ref[0, rows, :]
           + jnp.dot(oa_ref[0, rows, :], wa_ref[...], preferred_element_type=F32)
           + jnp.dot(ob_ref[0, rows, :], wb_ref[...], preferred_element_type=F32) for rows in blocks]
    qs = [_bdot(_rms(x1, g_ref[...]), wq_ref[...]) * scale for x1 in x1s]
    outs = [[] for _ in blocks]
    for hd in range(N_XATTN_HEADS):
        sl = slice(hd * HEAD_DIM, (hd + 1) * HEAD_DIM)
        ss = [_bdot_nt(q[:, sl], kv_ref[0, :, sl]) for q in qs]
        ps = [jnp.exp2(s - jnp.max(s, axis=-1, keepdims=True)) for s in ss]
        for out, p in zip(outs, ps):
            l = jnp.sum(p, axis=-1, keepdims=True)
            out.append(_bdot(p, kv_ref[0, :, width + hd * HEAD_DIM: width + (hd + 1) * HEAD_DIM]) / l)
    ys = [x1 + _bdot(jnp.concatenate(out, axis=-1), wo_ref[...]) for x1, out in zip(x1s, outs)]
    for rows, y in zip(blocks, ys):
        o_ref[0, rows, :] = y
        hn_ref[0, rows, :] = _rms(y, ng_ref[...]).astype(BF16)


def _xattn(x, oa, ob, wa, wb, g, wq, kv, wo, next_g, *, ts):
    b, s, d = x.shape
    mlen = kv.shape[1]
    width = wq.shape[1]
    ka, kb = oa.shape[2], ob.shape[2]
    tile = pl.BlockSpec((1, ts, d), lambda i, j: (i, j, 0))
    vec = pl.BlockSpec((1, d), lambda i, j: (0, 0))
    whole = lambda r, c: pl.BlockSpec((r, c), lambda i, j: (0, 0))
    return pl.pallas_call(
        functools.partial(_xattn_kernel, rs=math.gcd(ts, ROW_SUB)),
        grid=(b, s // ts),
        in_specs=[tile,
                  pl.BlockSpec((1, ts, ka), lambda i, j: (i, j, 0)),
                  pl.BlockSpec((1, ts, kb), lambda i, j: (i, j, 0)),
                  whole(ka, d), whole(kb, d), vec, whole(d, width),
                  pl.BlockSpec((1, mlen, 2 * width), lambda i, j: (i, 0, 0)),
                  whole(width, d), vec],
        out_specs=[tile, tile],
        out_shape=[jax.ShapeDtypeStruct((b, s, d), F32), jax.ShapeDtypeStruct((b, s, d), BF16)],
        compiler_params=pltpu.CompilerParams(
            dimension_semantics=("parallel", "parallel"), vmem_limit_bytes=VMEM_LIMIT),
        name="xattn",
    )(x, oa, ob, wa, wb, g.reshape(1, d), wq, kv, wo, next_g.reshape(1, d))


def _ffn_kernel(x_hbm, h_ref, halo_ref, wg_ref, wu_ref, cw_ref, cb_ref, wd_ref, fg_ref, o_ref,
                hx_ref, xres_ref, sem, *, tiles_per_seq, rs):
    i = pl.program_id(0)
    f = pl.program_id(1)
    tm = h_ref.shape[0]
    pad = BF16_ROWS

    def x_copy():
        return pltpu.make_async_copy(x_hbm.at[pl.ds(i * tm, tm), :], xres_ref, sem)

    @pl.when(f == 0)
    def _():
        x_copy().start()
        halo = halo_ref[...]
        hx_ref[0:pad, :] = jnp.where(i % tiles_per_seq == 0, jnp.zeros_like(halo), halo)
        hx_ref[pad:pad + tm, :] = h_ref[...]
        o_ref[...] = jnp.zeros_like(o_ref)

    cw = cw_ref[...]
    keep = SUBLANES
    tail = None
    for r in range(tm // rs):
        rows = slice(r * rs, (r + 1) * rs)
        hrows = slice(pad + r * rs, pad + (r + 1) * rs)
        if r == 0:
            gp = jnp.dot(hx_ref[0:pad + rs, :], wg_ref[...], preferred_element_type=F32)[pad - keep:]
        else:
            gp = jnp.concatenate(
                [tail, jnp.dot(hx_ref[hrows, :], wg_ref[...], preferred_element_type=F32)], axis=0)
        tail = gp[rs:rs + keep]
        up = jnp.dot(hx_ref[hrows, :], wu_ref[...], preferred_element_type=F32)
        gate = gp[keep:keep + rs] * cw[2:3] + gp[keep - 1:keep - 1 + rs] * cw[1:2] \
            + gp[keep - 2:keep - 2 + rs] * cw[0:1] + cb_ref[...]
        act = _silu(gate) * up
        o_ref[rows, :] += jnp.dot(act.astype(BF16), wd_ref[...], preferred_element_type=F32)

    @pl.when(f == pl.num_programs(1) - 1)
    def _():
        x_copy().wait()

        def finish(r, carry):
            rows = pl.ds(pl.multiple_of(r * HEAD_DIM, HEAD_DIM), HEAD_DIM)
            o_ref[rows, :] = _rms(xres_ref[rows, :] + o_ref[rows, :], fg_ref[...])
            return carry

        lax.fori_loop(0, tm // HEAD_DIM, finish, 0)


def _ffn(x, h, wg, wu, cw, cb, wd, fg, *, seq, tm, tf):
    m, d = x.shape
    ff = wg.shape[1]
    assert seq % tm == 0 and ff % tf == 0 and tm % BF16_ROWS == 0
    hb = tm // BF16_ROWS
    return pl.pallas_call(
        functools.partial(_ffn_kernel, tiles_per_seq=seq // tm, rs=tm),
        grid=(m // tm, ff // tf),
        in_specs=[pl.BlockSpec(memory_space=pl.ANY),
                  pl.BlockSpec((tm, d), lambda i, f: (i, 0)),
                  pl.BlockSpec((BF16_ROWS, d), lambda i, f: (jnp.maximum(i * hb - 1, 0), 0)),
                  pl.BlockSpec((d, tf), lambda i, f: (0, f)),
                  pl.BlockSpec((d, tf), lambda i, f: (0, f)),
                  pl.BlockSpec((FFN_CONV, tf), lambda i, f: (0, f)),
                  pl.BlockSpec((1, tf), lambda i, f: (0, f)),
                  pl.BlockSpec((tf, d), lambda i, f: (f, 0)),
                  pl.BlockSpec((1, d), lambda i, f: (0, 0))],
        out_specs=pl.BlockSpec((tm, d), lambda i, f: (i, 0)),
        out_shape=jax.ShapeDtypeStruct((m, d), F32),
        scratch_shapes=[pltpu.VMEM((tm + BF16_ROWS, d), BF16), pltpu.VMEM((tm, d), F32),
                        pltpu.SemaphoreType.DMA(())],
        compiler_params=pltpu.CompilerParams(
            dimension_semantics=("arbitrary", "arbitrary"), vmem_limit_bytes=VMEM_LIMIT),
        name="ffn",
    )(x, h, h, wg, wu, cw, cb.reshape(1, ff), wd, fg.reshape(1, d))


def _layer(x, mem, mix_norm_g, w_in, gdn_conv_w, a_log, dt_bias, gdn_norm_g, moba_norm_g, rel_bias,
           w_out, xattn_norm_g, mem_norm_g, w_xq, w_xkv, w_xo, ffn_norm_g, w_gate, w_up, ffn_conv_w,
           ffn_conv_b, w_down, final_g, *, last):
    b, s, d = x.shape
    m = b * s
    x2 = x.reshape(m, d)
    i1 = 4 * GDN_WIDTH
    i3 = i1 + 2 * N_GDN_HEADS
    tm = min(1024, m)
    wa_cols = i1 + HEAD_DIM
    w_a, w_b = _cast_split(w_in, a_cols=wa_cols, b_start=i3, tr=256)
    pa = _norm_matmul(x2, mix_norm_g, w_a, tm=tm, tn=wa_cols // 3).reshape(b, s, wa_cols)
    pb = _norm_matmul(x2, mix_norm_g, w_b, tm=tm, tn=1024, out_dtype=BF16).reshape(b, s, 3 * MOBA_WIDTH)
    o_a, o_b = _mixer(pa, pb, gdn_conv_w, a_log, dt_bias, gdn_norm_g, rel_bias, moba_norm_g)
    mlen = mem.shape[1]
    kv = _norm_matmul(mem.reshape(b * mlen, d), mem_norm_g, w_xkv.astype(BF16),
                      tm=min(512, b * mlen), tn=1024, out_dtype=BF16).reshape(b, mlen, -1)
    x2b, h2 = _xattn(x, o_a, o_b, w_out[:GDN_WIDTH].astype(BF16), w_out[GDN_WIDTH:].astype(BF16),
                     xattn_norm_g, w_xq.astype(BF16), kv, w_xo.astype(BF16), ffn_norm_g, ts=min(512, s))
    assert last, "the final rmsnorm is fused into the last layer's ffn"
    y = _ffn(x2b.reshape(m, d), h2.reshape(m, d), w_gate.astype(BF16), w_up.astype(BF16), ffn_conv_w,
             ffn_conv_b, w_down.astype(BF16), final_g, seq=s, tm=min(1024, s), tf=512)
    return y.reshape(b, s, d)


def kernel(x, mem, mix_norm_g, w_in, gdn_conv_w, gdn_a_log, gdn_dt_bias, gdn_norm_g, moba_norm_g,
           rel_bias, w_out, xattn_norm_g, mem_norm_g, w_xq, w_xkv, w_xo, ffn_norm_g, w_gate, w_up,
           ffn_conv_w, ffn_conv_b, w_down, final_norm_g):
    depth = mix_norm_g.shape[0]
    assert depth == 1
    l = 0
    return _layer(x, mem, mix_norm_g[l], w_in[l], gdn_conv_w[l], gdn_a_log[l], gdn_dt_bias[l],
                  gdn_norm_g[l], moba_norm_g[l], rel_bias, w_out[l], xattn_norm_g[l], mem_norm_g[l],
                  w_xq[l], w_xkv[l], w_xo[l], ffn_norm_g[l], w_gate[l], w_up[l], ffn_conv_w[l],
                  ffn_conv_b[l], w_down[l], final_norm_g, last=True)
```

```python
import functools
import math

import jax
import jax.numpy as jnp
import numpy as np
from jax import lax
from jax.experimental import pallas as pl
from jax.experimental.pallas import tpu as pltpu

HEAD_DIM = 128
N_GDN_HEADS = 8
N_MOBA_HEADS = 8
GDN_WIDTH = N_GDN_HEADS * HEAD_DIM
MOBA_WIDTH = N_MOBA_HEADS * HEAD_DIM
GDN_CONV = 4
GDN_CHUNK = 256
GDN_FILL_PER_STEP = 2
GDN_HALF_ROWS_FROM = 64
MOBA_BLOCK = 256
MOBA_TOPK = 3
MOBA_TILES_IN_FLIGHT = 8
REL_BUCKETS = 32
REL_MAX_DIST = 128
N_XATTN_HEADS = 4
FFN_CONV = 3
EPS = 1e-6
NEG = -1e30
LOG2E = math.log2(math.e)
SUBLANES = 8
BF16_ROWS = 16
ROW_SUB = 256
VMEM_LIMIT = 58 * 1024 * 1024

F32 = jnp.float32
BF16 = jnp.bfloat16


def _bdot(a, b):
    return jnp.dot(a.astype(BF16), b.astype(BF16), preferred_element_type=F32)


def _bdot_nt(a, b):
    return lax.dot_general(a.astype(BF16), b.astype(BF16), (((1,), (1,)), ((), ())),
                           preferred_element_type=F32)


def _bdot_tn(a, b):
    return lax.dot_general(a.astype(BF16), b.astype(BF16), (((0,), (0,)), ((), ())),
                           preferred_element_type=F32)


def _sigmoid(x):
    return 1.0 / (1.0 + jnp.exp(-x))


def _silu(x):
    return x * _sigmoid(x)


def _rms(x, g):
    return x * lax.rsqrt(jnp.mean(x * x, axis=-1, keepdims=True) + EPS) * g


def _norm_matmul_kernel(x_ref, g_ref, w_ref, o_ref, *, rs):
    for r in range(x_ref.shape[0] // rs):
        rows = slice(r * rs, (r + 1) * rs)
        hn = _rms(x_ref[rows, :], g_ref[...]).astype(BF16)
        o_ref[rows, :] = jnp.dot(hn, w_ref[...], preferred_element_type=F32).astype(o_ref.dtype)


def _norm_matmul(x, g, w, *, tm, tn, n=None, out_dtype=F32, row_sub=ROW_SUB):
    m, k = x.shape
    n = w.shape[1] if n is None else n
    assert m % tm == 0 and n % tn == 0 and n <= w.shape[1]
    rs = math.gcd(tm, row_sub)
    return pl.pallas_call(
        functools.partial(_norm_matmul_kernel, rs=rs),
        grid=(m // tm, n // tn),
        in_specs=[pl.BlockSpec((tm, k), lambda i, j: (i, 0)),
                  pl.BlockSpec((1, k), lambda i, j: (0, 0)),
                  pl.BlockSpec((k, tn), lambda i, j: (0, j))],
        out_specs=pl.BlockSpec((tm, tn), lambda i, j: (i, j)),
        out_shape=jax.ShapeDtypeStruct((m, n), out_dtype),
        compiler_params=pltpu.CompilerParams(
            dimension_semantics=("parallel", "arbitrary"), vmem_limit_bytes=VMEM_LIMIT),
        name="norm_matmul",
    )(x, g.reshape(1, k), w)


def _unit_lower_inverse(mats, row, col, fill=lambda: None):
    c = mats[0].shape[0]
    eye = (row == col).astype(F32)
    blk = lambda n: (row // n) == (col // n)
    inner = blk(16)
    ds = [jnp.where(inner, a, 0.0) for a in mats]
    ts = [eye - d for d in ds]
    ps = [_bdot(d, d) for d in ds]
    fill()
    for step in range(3):
        ts = [t + _bdot(t, p) for t, p in zip(ts, ps)]
        fill()
        if step < 2:
            ps = [_bdot(p, p) for p in ps]
            fill()
    half_row = lax.broadcasted_iota(jnp.int32, (c // 2, c), 0)
    half_col = lax.broadcasted_iota(jnp.int32, (c // 2, c), 1)
    n = 32
    while n <= c:
        h = n // 2
        pairs = range(c // n)
        lower = lambda m: jnp.concatenate([m[(2 * k + 1) * h:(2 * k + 2) * h] for k in pairs], axis=0)
        if n < GDN_HALF_ROWS_FROM:
            keep = blk(n) & ~inner
            ys = [lower(_bdot(jnp.where(keep, a, 0.0), t)) for a, t in zip(mats, ts)]
        else:
            left = (half_col // h) == 2 * (half_row // h)
            ys = [_bdot(jnp.where(left, lower(a), 0.0), t) for a, t in zip(mats, ts)]
        fill()
        zero = jnp.zeros((h, c), F32)
        spread = lambda y: jnp.concatenate(
            [part for k in pairs for part in (zero, y[k * h:(k + 1) * h])], axis=0)
        zs = [_bdot(lower(t), spread(y)) for t, y in zip(ts, ys)]
        ts = [jnp.concatenate([part for k in pairs for part in
                               (t[2 * k * h:(2 * k + 1) * h],
                                t[(2 * k + 1) * h:(2 * k + 2) * h] - z[k * h:(k + 1) * h])], axis=0)
              for t, z in zip(ts, zs)]
        fill()
        inner = blk(n)
        n *= 2
    return ts


def _gdn_body(h, alog_ref, dtb_ref, q_ref, k_ref, v_ref, z_ref, bat_ref, wq_ref, wk_ref, wv_ref,
              ng_ref, o_ref, u_ref, wqd_ref, qk_ref, kw_ref, bc_ref, gl_ref, sb_ref, *, group,
              fill=lambda n=1: None):
    seq = q_ref.shape[1]
    C = GDN_CHUNK
    D = HEAD_DIM
    pad = SUBLANES
    n_chunks = seq // C

    static_chunks = n_chunks == group
    row = lax.broadcasted_iota(jnp.int32, (C, C), 0)
    col = lax.broadcasted_iota(jnp.int32, (C, C), 1)
    tri_incl = row >= col
    tri_strict = row > col
    lane = lax.broadcasted_iota(jnp.int32, (SUBLANES, C), 1)
    neg_a = -jnp.exp(jnp.full((1, C), alog_ref[h], F32))
    dt_bias = dtb_ref[h]
    scale = D ** -0.5

    def conv_silu(x_ref, w_ref, c, r0):
        if static_chunks:
            win = (jnp.concatenate([jnp.zeros((pad, D), F32), x_ref[0, 0:C, :]], axis=0) if c == 0
                   else x_ref[0, r0 - pad:r0 + C, :])
        else:
            halo = x_ref[0, pl.ds(jnp.maximum(r0 - pad, 0), pad), :]
            win = jnp.concatenate([jnp.where(c > 0, halo, 0.0), x_ref[0, pl.ds(r0, C), :]], axis=0)
        w = w_ref[...]
        prev = pltpu.roll(win, 1, axis=0)
        near = win * w[3:4] + prev * w[2:3]
        far = pltpu.roll(win * w[1:2] + prev * w[0:1], 2, axis=0)
        return _silu(near[pad:pad + C] + far[pad:pad + C])

    def l2n(x):
        return x * lax.rsqrt(jnp.sum(x * x, axis=-1, keepdims=True) + EPS)

    def cumsum_lanes(x):
        x = jnp.broadcast_to(x, (SUBLANES, C))
        s = 1
        while s < C:
            x = x + jnp.where(lane >= s, pltpu.roll(x, s, axis=1), 0.0)
            s *= 2
        return x[0:1, :]

    reps = C // D

    def rows_to_cols(x):
        return jnp.concatenate(
            [jnp.broadcast_to(x[:, n * D:(n + 1) * D], (D, D)).T for n in range(reps)], axis=0)

    def chunk_ids(grp):
        cs = [grp * group + i for i in range(group)]
        return cs, [c * C if static_chunks else pl.multiple_of(c * C, C) for c in cs]

    def for_each_group(body):
        if static_chunks:
            body(0, 0)
        else:
            lax.fori_loop(0, n_chunks // group, body, 0)

    def prepare(grp, carry):
        cs, r0s = chunk_ids(grp)
        pre, gbs, g_rows = [], [], []
        for c, r0 in zip(cs, r0s):
            q = l2n(conv_silu(q_ref, wq_ref, c, r0)) * scale
            k = l2n(conv_silu(k_ref, wk_ref, c, r0))
            v = conv_silu(v_ref, wv_ref, c, r0)
            b_row = bat_ref[0, h, pl.ds(c, 1), :]
            xs = bat_ref[0, h + N_GDN_HEADS, pl.ds(c, 1), :] + dt_bias
            softplus = jnp.maximum(xs, 0.0) + jnp.log1p(jnp.exp(-jnp.abs(xs)))
            g_row = cumsum_lanes(neg_a * softplus)
            pre.append((q, k, v, rows_to_cols(_sigmoid(b_row)), None))
            g_rows.append(g_row)
            gbs.append(rows_to_cols(g_row))
            fill()
        kk_qks = [_bdot_nt(jnp.concatenate([k * beta, q], axis=0), k) for q, k, v, beta, _ in pre]
        mats, mids = [], []
        for (q, k, v, beta, _), gb, g_row, kk_qk in zip(pre, gbs, g_rows, kk_qks):
            g_i = jnp.concatenate([gb] * reps, axis=1)
            g_j = jnp.broadcast_to(g_row, (C, C))
            decay = jnp.exp(jnp.where(tri_incl, g_i - g_j, NEG))
            mats.append(jnp.where(tri_strict, kk_qk[:C] * decay, 0.0))
            mids.append((kk_qk[C:] * decay, jnp.exp(gb), gb[C - 1:C, :]))
        ts = _unit_lower_inverse(mats, row, col, fill=fill)
        uws = [_bdot(t, jnp.concatenate([v * beta, k * beta * eg], axis=1))
               for t, (q, k, v, beta, _), (_, eg, _) in zip(ts, pre, mids)]
        transs = [_bdot_tn(k * jnp.exp(g_last - gb), uw)
                  for uw, (q, k, v, beta, _), gb, (_, _, g_last) in zip(uws, pre, gbs, mids)]
        for c, r0, uw, trans, (q, k, v, beta, _), (qk, eg, g_last) in zip(cs, r0s, uws, transs, pre, mids):
            u_ref[pl.ds(r0, C), :] = uw[:, :D]
            wqd_ref[c, 0:C, :] = uw[:, D:].astype(BF16)
            wqd_ref[c, C:2 * C, :] = (q * eg).astype(BF16)
            qk_ref[pl.ds(r0, C), :] = qk.astype(BF16)
            bc_ref[c] = trans[:, :D]
            kw_ref[c] = trans[:, D:].astype(BF16)
            gl_ref[c] = jnp.exp(g_last)
        return carry

    for_each_group(prepare)

    def chain(c, state):
        sb = state.astype(BF16)
        sb_ref[c] = sb
        return state * gl_ref[c] - jnp.dot(kw_ref[c], sb, preferred_element_type=F32) + bc_ref[c]

    if static_chunks:
        state = jnp.zeros((D, D), F32)
        for c in range(n_chunks):
            state = chain(c, state)
            fill(GDN_FILL_PER_STEP)
    else:
        lax.fori_loop(0, n_chunks, chain, jnp.zeros((D, D), F32))

    def outputs(grp, carry):
        cs, r0s = chunk_ids(grp)
        ws_qs = [jnp.dot(wqd_ref[c], sb_ref[c], preferred_element_type=F32) for c in cs]
        v_news = [(u_ref[pl.ds(r0, C), :] - wq[:C]).astype(BF16) for r0, wq in zip(r0s, ws_qs)]
        os = [wq[C:] + jnp.dot(qk_ref[pl.ds(r0, C), :], vn, preferred_element_type=F32)
              for r0, wq, vn in zip(r0s, ws_qs, v_news)]
        for r0, o in zip(r0s, os):
            z = z_ref[0, pl.ds(r0, C), :]
            o_ref[0, pl.ds(r0, C), :] = (_rms(o, ng_ref[...]) * _silu(z)).astype(o_ref.dtype)
        return carry

    for_each_group(outputs)


def _bucket_upper_bounds():
    n = np.arange(0, 4 * REL_MAX_DIST, dtype=np.int64)
    max_exact = REL_BUCKETS // 2
    nf = np.maximum(n, 1).astype(np.float32)
    large = max_exact + (np.log(nf / np.float32(max_exact)) / np.float32(math.log(REL_MAX_DIST / max_exact))
                         * np.float32(REL_BUCKETS - max_exact)).astype(np.int32)
    large = np.minimum(large, REL_BUCKETS - 1)
    bucket = np.where(n < max_exact, n, large)
    assert np.all(np.diff(bucket) >= 0) and bucket[-1] == REL_BUCKETS - 1
    return [int(np.searchsorted(bucket, b, side="right")) for b in range(REL_BUCKETS - 1)]


_BUCKET_UPPER = _bucket_upper_bounds()


def _moba_steps(h, b, rb_ref, q_ref, k_ref, v_ref, ng_ref, o_ref, vt_ref, km_ref, bd_ref, bl_ref):
    seq = k_ref.shape[1]
    T = MOBA_BLOCK
    nb = seq // T
    scale = HEAD_DIM ** -0.5 * LOG2E
    kk = lax.broadcasted_iota(jnp.int32, (T, T), 0)
    qq = lax.broadcasted_iota(jnp.int32, (T, T), 1)

    @pl.when(b == 0)
    def _():
        def bias_of(n):
            val = jnp.full((T, T), rb_ref[REL_BUCKETS - 1, h], F32)
            for bkt in range(REL_BUCKETS - 2, -1, -1):
                val = jnp.where(n < _BUCKET_UPPER[bkt], rb_ref[bkt, h], val)
            return val
        bd_ref[...] = jnp.where(qq >= kk, bias_of(qq - kk) * LOG2E, NEG)
        bl_ref[...] = bias_of(qq - kk + T) * LOG2E

    for n in range(nb):
        blk_rows = slice(n * T, (n + 1) * T)
        vt_ref[0:HEAD_DIM, blk_rows] = v_ref[0, blk_rows, :].astype(F32).T.astype(BF16)
        km_ref[n:n + 1, :] = jnp.mean(k_ref[0, blk_rows, :].astype(F32), axis=0, keepdims=True)
    vt_ref[HEAD_DIM:HEAD_DIM + BF16_ROWS, :] = jnp.ones((BF16_ROWS, seq), BF16)
    far_bias = rb_ref[REL_BUCKETS - 1, h] * LOG2E

    def additive_mask(i, qf):
        if i <= MOBA_TOPK:
            return None
        gate = lax.dot_general(km_ref[0:i, :], qf, (((1,), (1,)), ((), ())),
                               preferred_element_type=F32, precision=lax.Precision.HIGHEST)
        blk = lax.broadcasted_iota(jnp.int32, (i, T), 0)
        rank = jnp.zeros((i, T), jnp.int32)
        for m in range(i):
            gm = gate[m:m + 1, :]
            rank = rank + jnp.where((gm > gate) | ((gm == gate) & (m < blk)), 1, 0)
        return jnp.where(rank < MOBA_TOPK, 0.0, NEG)

    def biased(i, s_all, keep):
        parts, maxes = [], []
        for j in range(i + 1):
            sj = s_all[j * T:(j + 1) * T, :]
            if j >= i - 1:
                sj = sj + (bd_ref[...] if j == i else bl_ref[...])
                row = keep[j:j + 1, :] if (keep is not None and j < i) else 0.0
            else:
                row = far_bias if keep is None else keep[j:j + 1, :] + far_bias
            parts.append((sj, row))
            maxes.append(jnp.max(sj, axis=0, keepdims=True) + row)
        return parts, functools.reduce(jnp.maximum, maxes)

    order = [t for pair in zip(range(nb // 2), range(nb - 1, nb // 2 - 1, -1)) for t in pair]
    if nb % 2:
        order.append(nb // 2)
    for g0 in range(0, nb, MOBA_TILES_IN_FLIGHT):
        tiles = order[g0:g0 + MOBA_TILES_IN_FLIGHT]
        qfs = [q_ref[0, i * T:(i + 1) * T, :].astype(F32) for i in tiles]
        keeps, s_alls, partss, m_rows, p_alls, o_augs = [], [], [], [], [], []
        for i, qf in zip(tiles, qfs):
            keeps.append(additive_mask(i, qf))
            s_alls.append(_bdot_nt(k_ref[0, 0:(i + 1) * T, :], qf * scale))
            yield
        for i, s_all, keep in zip(tiles, s_alls, keeps):
            parts, m_row = biased(i, s_all, keep)
            partss.append(parts)
            m_rows.append(m_row)
            yield
        for parts, m_row in zip(partss, m_rows):
            p_alls.append(jnp.concatenate(
                [jnp.exp2((sj + (row - m_row)).astype(BF16)) for sj, row in parts], axis=0))
            yield
        for i, p_all in zip(tiles, p_alls):
            o_augs.append(jnp.dot(vt_ref[:, 0:(i + 1) * T], p_all, preferred_element_type=F32))
            yield
        for i, o in zip(tiles, o_augs):
            o_t = o[0:HEAD_DIM] / o[HEAD_DIM:HEAD_DIM + 1]
            o_ref[0, i * T:(i + 1) * T, :] = _rms(o_t.T, ng_ref[...]).astype(o_ref.dtype)
            yield


def _mixer_kernel(alog_ref, dtb_ref, rb_ref, q_ref, k_ref, v_ref, z_ref, bat_ref, wq_ref, wk_ref, wv_ref,
                  gng_ref, mq_ref, mk_ref, mv_ref, mng_ref, oa_ref, ob_ref,
                  u_ref, wqd_ref, qk_ref, kw_ref, bc_ref, gl_ref, sb_ref, vt_ref, km_ref, bd_ref, bl_ref,
                  *, group):
    h = pl.program_id(0)
    b = pl.program_id(1)
    moba = _moba_steps(h, b, rb_ref, mq_ref, mk_ref, mv_ref, mng_ref, ob_ref, vt_ref, km_ref, bd_ref, bl_ref)

    def fill(n=1):
        for _ in range(n):
            next(moba, None)

    _gdn_body(h, alog_ref, dtb_ref, q_ref, k_ref, v_ref, z_ref, bat_ref, wq_ref, wk_ref, wv_ref,
              gng_ref, oa_ref, u_ref, wqd_ref, qk_ref, kw_ref, bc_ref, gl_ref, sb_ref, group=group, fill=fill)
    for _ in moba:
        pass


def _mixer(pa, pb, conv_w, a_log, dt_bias, gdn_norm_g, rel_bias, moba_norm_g, *, group=8):
    b, s, _ = pa.shape
    H = N_GDN_HEADS
    assert N_MOBA_HEADS == H
    C, T = GDN_CHUNK, MOBA_BLOCK
    assert s % C == 0 and s % T == 0
    n_chunks, nb = s // C, s // T
    bat = pa[:, :, 4 * GDN_WIDTH:4 * GDN_WIDTH + 2 * H].transpose(0, 2, 1).reshape(b, 2 * H, n_chunks, C)
    head = lambda off: pl.BlockSpec((1, s, HEAD_DIM), lambda h, i: (i, 0, off + h))
    cw = lambda off: pl.BlockSpec((GDN_CONV, HEAD_DIM), lambda h, i: (0, off + h))
    vec = pl.BlockSpec((1, HEAD_DIM), lambda h, i: (0, 0))
    smem = pl.BlockSpec(memory_space=pltpu.SMEM)
    out = pl.BlockSpec((1, s, HEAD_DIM), lambda h, i: (i, 0, h))
    return pl.pallas_call(
        functools.partial(_mixer_kernel, group=math.gcd(group, n_chunks)),
        grid=(H, b),
        in_specs=[smem, smem, smem, head(0), head(H), head(2 * H), head(3 * H),
                  pl.BlockSpec((1, 2 * H, n_chunks, C), lambda h, i: (i, 0, 0, 0)),
                  cw(0), cw(H), cw(2 * H), vec,
                  head(0), head(H), head(2 * H), vec],
        out_specs=[out, out],
        out_shape=[jax.ShapeDtypeStruct((b, s, GDN_WIDTH), BF16), jax.ShapeDtypeStruct((b, s, MOBA_WIDTH), BF16)],
        scratch_shapes=[
            pltpu.VMEM((s, HEAD_DIM), F32),
            pltpu.VMEM((n_chunks, 2 * C, HEAD_DIM), BF16),
            pltpu.VMEM((s, C), BF16),
            pltpu.VMEM((n_chunks, HEAD_DIM, HEAD_DIM), BF16),
            pltpu.VMEM((n_chunks, HEAD_DIM, HEAD_DIM), F32),
            pltpu.VMEM((n_chunks, 1, HEAD_DIM), F32),
            pltpu.VMEM((n_chunks, HEAD_DIM, HEAD_DIM), BF16),
            pltpu.VMEM((HEAD_DIM + BF16_ROWS, s), BF16),
            pltpu.VMEM((nb, HEAD_DIM), F32),
            pltpu.VMEM((T, T), F32), pltpu.VMEM((T, T), F32)],
        compiler_params=pltpu.CompilerParams(
            dimension_semantics=("arbitrary", "arbitrary"), vmem_limit_bytes=VMEM_LIMIT),
        name="mixer",
    )(a_log, dt_bias, rel_bias, pa, pa, pa, pa, bat, conv_w, conv_w, conv_w, gdn_norm_g.reshape(1, HEAD_DIM),
      pb, pb, pb, moba_norm_g.reshape(1, HEAD_DIM))


def _xattn_kernel(x_ref, oa_ref, ob_ref, wa_ref, wb_ref, g_ref, wq_ref, kv_ref, wo_ref, ng_ref,
                  o_ref, hn_ref, *, rs):
    width = N_XATTN_HEADS * HEAD_DIM
    scale = HEAD_DIM ** -0.5 * LOG2E
    blocks = [slice(r * rs, (r + 1) * rs) for r in range(x_ref.shape[1] // rs)]
    x1s = [x_ref[0, rows, :]
           + jnp.dot(oa_ref[0, rows, :], wa_ref[...], preferred_element_type=F32)
           + jnp.dot(ob_ref[0, rows, :], wb_ref[...], preferred_element_type=F32) for rows in blocks]
    qs = [_bdot(_rms(x1, g_ref[...]), wq_ref[...]) * scale for x1 in x1s]
    outs = [[] for _ in blocks]
    for hd in range(N_XATTN_HEADS):
        sl = slice(hd * HEAD_DIM, (hd + 1) * HEAD_DIM)
        ss = [_bdot_nt(q[:, sl], kv_ref[0, :, sl]) for q in qs]
        ps = [jnp.exp2(s - jnp.max(s, axis=-1, keepdims=True)) for s in ss]
        for out, p in zip(outs, ps):
            l = jnp.sum(p, axis=-1, keepdims=True)
            out.append(_bdot(p, kv_ref[0, :, width + hd * HEAD_DIM: width + (hd + 1) * HEAD_DIM]) / l)
    ys = [x1 + _bdot(jnp.concatenate(out, axis=-1), wo_ref[...]) for x1, out in zip(x1s, outs)]
    for rows, y in zip(blocks, ys):
        o_ref[0, rows, :] = y
        hn_ref[0, rows, :] = _rms(y, ng_ref[...]).astype(BF16)


def _xattn(x, oa, ob, wa, wb, g, wq, kv, wo, next_g, *, ts):
    b, s, d = x.shape
    mlen = kv.shape[1]
    width = wq.shape[1]
    ka, kb = oa.shape[2], ob.shape[2]
    tile = pl.BlockSpec((1, ts, d), lambda i, j: (i, j, 0))
    vec = pl.BlockSpec((1, d), lambda i, j: (0, 0))
    whole = lambda r, c: pl.BlockSpec((r, c), lambda i, j: (0, 0))
    return pl.pallas_call(
        functools.partial(_xattn_kernel, rs=math.gcd(ts, 2 * ROW_SUB)),
        grid=(b, s // ts),
        in_specs=[tile,
                  pl.BlockSpec((1, ts, ka), lambda i, j: (i, j, 0)),
                  pl.BlockSpec((1, ts, kb), lambda i, j: (i, j, 0)),
                  whole(ka, d), whole(kb, d), vec, whole(d, width),
                  pl.BlockSpec((1, mlen, 2 * width), lambda i, j: (i, 0, 0)),
                  whole(width, d), vec],
        out_specs=[tile, tile],
        out_shape=[jax.ShapeDtypeStruct((b, s, d), F32), jax.ShapeDtypeStruct((b, s, d), BF16)],
        compiler_params=pltpu.CompilerParams(
            dimension_semantics=("parallel", "parallel"), vmem_limit_bytes=VMEM_LIMIT),
        name="xattn",
    )(x, oa, ob, wa, wb, g.reshape(1, d), wq, kv, wo, next_g.reshape(1, d))


def _ffn_kernel(x_hbm, h_ref, halo_ref, wg_ref, wu_ref, cw_ref, cb_ref, wd_ref, fg_ref, o_ref,
                hx_ref, xres_ref, sem, *, tiles_per_seq, rs):
    i = pl.program_id(0)
    f = pl.program_id(1)
    tm = h_ref.shape[0]
    pad = BF16_ROWS

    def x_copy():
        return pltpu.make_async_copy(x_hbm.at[pl.ds(i * tm, tm), :], xres_ref, sem)

    @pl.when(f == 0)
    def _():
        x_copy().start()
        halo = halo_ref[...]
        hx_ref[0:pad, :] = jnp.where(i % tiles_per_seq == 0, jnp.zeros_like(halo), halo)
        hx_ref[pad:pad + tm, :] = h_ref[...]
        o_ref[...] = jnp.zeros_like(o_ref)

    cw = cw_ref[...]
    keep = SUBLANES
    tail = None
    for r in range(tm // rs):
        rows = slice(r * rs, (r + 1) * rs)
        hrows = slice(pad + r * rs, pad + (r + 1) * rs)
        if r == 0:
            gp = jnp.dot(hx_ref[0:pad + rs, :], wg_ref[...], preferred_element_type=F32)[pad - keep:]
        else:
            gp = jnp.concatenate(
                [tail, jnp.dot(hx_ref[hrows, :], wg_ref[...], preferred_element_type=F32)], axis=0)
        tail = gp[rs:rs + keep]
        up = jnp.dot(hx_ref[hrows, :], wu_ref[...], preferred_element_type=F32)
        gate = gp[keep:keep + rs] * cw[2:3] + gp[keep - 1:keep - 1 + rs] * cw[1:2] \
            + gp[keep - 2:keep - 2 + rs] * cw[0:1] + cb_ref[...]
        act = _silu(gate) * up
        o_ref[rows, :] += jnp.dot(act.astype(BF16), wd_ref[...], preferred_element_type=F32)

    @pl.when(f == pl.num_programs(1) - 1)
    def _():
        x_copy().wait()

        def finish(r, carry):
            rows = pl.ds(pl.multiple_of(r * HEAD_DIM, HEAD_DIM), HEAD_DIM)
            o_ref[rows, :] = _rms(xres_ref[rows, :] + o_ref[rows, :], fg_ref[...])
            return carry

        lax.fori_loop(0, tm // HEAD_DIM, finish, 0)


def _ffn(x, h, wg, wu, cw, cb, wd, fg, *, seq, tm, tf):
    m, d = x.shape
    ff = wg.shape[1]
    assert seq % tm == 0 and ff % tf == 0 and tm % BF16_ROWS == 0
    hb = tm // BF16_ROWS
    return pl.pallas_call(
        functools.partial(_ffn_kernel, tiles_per_seq=seq // tm, rs=tm),
        grid=(m // tm, ff // tf),
        in_specs=[pl.BlockSpec(memory_space=pl.ANY),
                  pl.BlockSpec((tm, d), lambda i, f: (i, 0)),
                  pl.BlockSpec((BF16_ROWS, d), lambda i, f: (jnp.maximum(i * hb - 1, 0), 0)),
                  pl.BlockSpec((d, tf), lambda i, f: (0, f)),
                  pl.BlockSpec((d, tf), lambda i, f: (0, f)),
                  pl.BlockSpec((FFN_CONV, tf), lambda i, f: (0, f)),
                  pl.BlockSpec((1, tf), lambda i, f: (0, f)),
                  pl.BlockSpec((tf, d), lambda i, f: (f, 0)),
                  pl.BlockSpec((1, d), lambda i, f: (0, 0))],
        out_specs=pl.BlockSpec((tm, d), lambda i, f: (i, 0)),
        out_shape=jax.ShapeDtypeStruct((m, d), F32),
        scratch_shapes=[pltpu.VMEM((tm + BF16_ROWS, d), BF16), pltpu.VMEM((tm, d), F32),
                        pltpu.SemaphoreType.DMA(())],
        compiler_params=pltpu.CompilerParams(
            dimension_semantics=("arbitrary", "arbitrary"), vmem_limit_bytes=VMEM_LIMIT),
        name="ffn",
    )(x, h, h, wg, wu, cw, cb.reshape(1, ff), wd, fg.reshape(1, d))


def _layer(x, mem, mix_norm_g, w_in, gdn_conv_w, a_log, dt_bias, gdn_norm_g, moba_norm_g, rel_bias,
           w_out, xattn_norm_g, mem_norm_g, w_xq, w_xkv, w_xo, ffn_norm_g, w_gate, w_up, ffn_conv_w,
           ffn_conv_b, w_down, final_g, *, last):
    b, s, d = x.shape
    m = b * s
    x2 = x.reshape(m, d)
    i1 = 4 * GDN_WIDTH
    i3 = i1 + 2 * N_GDN_HEADS
    w_all = w_in.astype(BF16)
    w_b = w_all[:, i3:]
    tm = min(1024, m)
    wa_cols = i1 + HEAD_DIM
    pa = _norm_matmul(x2, mix_norm_g, w_all, tm=tm, tn=wa_cols // 3, n=wa_cols,
                      row_sub=ROW_SUB // 2).reshape(b, s, wa_cols)
    pb = _norm_matmul(x2, mix_norm_g, w_b, tm=tm, tn=1024, out_dtype=BF16).reshape(b, s, 3 * MOBA_WIDTH)
    o_a, o_b = _mixer(pa, pb, gdn_conv_w, a_log, dt_bias, gdn_norm_g, rel_bias, moba_norm_g)
    mlen = mem.shape[1]
    kv = _norm_matmul(mem.reshape(b * mlen, d), mem_norm_g, w_xkv.astype(BF16),
                      tm=min(512, b * mlen), tn=1024, out_dtype=BF16).reshape(b, mlen, -1)
    x2b, h2 = _xattn(x, o_a, o_b, w_out[:GDN_WIDTH].astype(BF16), w_out[GDN_WIDTH:].astype(BF16),
                     xattn_norm_g, w_xq.astype(BF16), kv, w_xo.astype(BF16), ffn_norm_g, ts=min(512, s))
    assert last, "the final rmsnorm is fused into the last layer's ffn"
    y = _ffn(x2b.reshape(m, d), h2.reshape(m, d), w_gate.astype(BF16), w_up.astype(BF16), ffn_conv_w,
             ffn_conv_b, w_down.astype(BF16), final_g, seq=s, tm=min(1024, s), tf=512)
    return y.reshape(b, s, d)


def kernel(x, mem, mix_norm_g, w_in, gdn_conv_w, gdn_a_log, gdn_dt_bias, gdn_norm_g, moba_norm_g,
           rel_bias, w_out, xattn_norm_g, mem_norm_g, w_xq, w_xkv, w_xo, ffn_norm_g, w_gate, w_up,
           ffn_conv_w, ffn_conv_b, w_down, final_norm_g):
    depth = mix_norm_g.shape[0]
    assert depth == 1
    l = 0
    return _layer(x, mem, mix_norm_g[l], w_in[l], gdn_conv_w[l], gdn_a_log[l], gdn_dt_bias[l],
                  gdn_norm_g[l], moba_norm_g[l], rel_bias, w_out[l], xattn_norm_g[l], mem_norm_g[l],
                  w_xq[l], w_xkv[l], w_xo[l], ffn_norm_g[l], w_gate[l], w_up[l], ffn_conv_w[l],
                  ffn_conv_b[l], w_down[l], final_norm_g, last=True)
```

```python
import functools
import math

import jax
import jax.numpy as jnp
import numpy as np
from jax import lax
from jax.experimental import pallas as pl
from jax.experimental.pallas import tpu as pltpu

HEAD_DIM = 128
N_GDN_HEADS = 8
N_MOBA_HEADS = 8
GDN_WIDTH = N_GDN_HEADS * HEAD_DIM
MOBA_WIDTH = N_MOBA_HEADS * HEAD_DIM
GDN_CONV = 4
GDN_CHUNK = 256
GDN_FILL_PER_STEP = 2
GDN_HALF_ROWS_FROM = 64
MOBA_BLOCK = 256
MOBA_TOPK = 3
MOBA_TILES_IN_FLIGHT = 8
REL_BUCKETS = 32
REL_MAX_DIST = 128
N_XATTN_HEADS = 4
FFN_CONV = 3
EPS = 1e-6
NEG = -1e30
LOG2E = math.log2(math.e)
SUBLANES = 8
BF16_ROWS = 16
ROW_SUB = 256
VMEM_LIMIT = 58 * 1024 * 1024

F32 = jnp.float32
BF16 = jnp.bfloat16


def _bdot(a, b):
    return jnp.dot(a.astype(BF16), b.astype(BF16), preferred_element_type=F32)


def _bdot_nt(a, b):
    return lax.dot_general(a.astype(BF16), b.astype(BF16), (((1,), (1,)), ((), ())),
                           preferred_element_type=F32)


def _bdot_tn(a, b):
    return lax.dot_general(a.astype(BF16), b.astype(BF16), (((0,), (0,)), ((), ())),
                           preferred_element_type=F32)


def _sigmoid(x):
    return 1.0 / (1.0 + jnp.exp(-x))


def _silu(x):
    return x * _sigmoid(x)


def _rms(x, g):
    return x * lax.rsqrt(jnp.mean(x * x, axis=-1, keepdims=True) + EPS) * g


def _norm_matmul_kernel(x_ref, g_ref, w_ref, o_ref, *, rs):
    for r in range(x_ref.shape[0] // rs):
        rows = slice(r * rs, (r + 1) * rs)
        hn = _rms(x_ref[rows, :], g_ref[...]).astype(BF16)
        o_ref[rows, :] = jnp.dot(hn, w_ref[...], preferred_element_type=F32).astype(o_ref.dtype)


def _norm_matmul(x, g, w, *, tm, tn, n=None, out_dtype=F32, row_sub=ROW_SUB):
    m, k = x.shape
    n = w.shape[1] if n is None else n
    assert m % tm == 0 and n % tn == 0 and n <= w.shape[1]
    rs = math.gcd(tm, row_sub)
    return pl.pallas_call(
        functools.partial(_norm_matmul_kernel, rs=rs),
        grid=(m // tm, n // tn),
        in_specs=[pl.BlockSpec((tm, k), lambda i, j: (i, 0)),
                  pl.BlockSpec((1, k), lambda i, j: (0, 0)),
                  pl.BlockSpec((k, tn), lambda i, j: (0, j))],
        out_specs=pl.BlockSpec((tm, tn), lambda i, j: (i, j)),
        out_shape=jax.ShapeDtypeStruct((m, n), out_dtype),
        compiler_params=pltpu.CompilerParams(
            dimension_semantics=("parallel", "arbitrary"), vmem_limit_bytes=VMEM_LIMIT),
        name="norm_matmul",
    )(x, g.reshape(1, k), w)


def _unit_lower_inverse(mats, row, col, fill=lambda: None):
    c = mats[0].shape[0]
    eye = (row == col).astype(F32)
    blk = lambda n: (row // n) == (col // n)
    inner = blk(16)
    ds = [jnp.where(inner, a, 0.0) for a in mats]
    ts = [eye - d for d in ds]
    ps = [_bdot(d, d) for d in ds]
    fill()
    for step in range(3):
        ts = [t + _bdot(t, p) for t, p in zip(ts, ps)]
        fill()
        if step < 2:
            ps = [_bdot(p, p) for p in ps]
            fill()
    half_row = lax.broadcasted_iota(jnp.int32, (c // 2, c), 0)
    half_col = lax.broadcasted_iota(jnp.int32, (c // 2, c), 1)
    n = 32
    while n <= c:
        h = n // 2
        pairs = range(c // n)
        lower = lambda m: jnp.concatenate([m[(2 * k + 1) * h:(2 * k + 2) * h] for k in pairs], axis=0)
        if n < GDN_HALF_ROWS_FROM:
            keep = blk(n) & ~inner
            ys = [lower(_bdot(jnp.where(keep, a, 0.0), t)) for a, t in zip(mats, ts)]
        else:
            left = (half_col // h) == 2 * (half_row // h)
            ys = [_bdot(jnp.where(left, lower(a), 0.0), t) for a, t in zip(mats, ts)]
        fill()
        zero = jnp.zeros((h, c), F32)
        spread = lambda y: jnp.concatenate(
            [part for k in pairs for part in (zero, y[k * h:(k + 1) * h])], axis=0)
        zs = [_bdot(lower(t), spread(y)) for t, y in zip(ts, ys)]
        ts = [jnp.concatenate([part for k in pairs for part in
                               (t[2 * k * h:(2 * k + 1) * h],
                                t[(2 * k + 1) * h:(2 * k + 2) * h] - z[k * h:(k + 1) * h])], axis=0)
              for t, z in zip(ts, zs)]
        fill()
        inner = blk(n)
        n *= 2
    return ts


def _gdn_body(h, alog_ref, dtb_ref, q_ref, k_ref, v_ref, z_ref, bat_ref, wq_ref, wk_ref, wv_ref,
              ng_ref, o_ref, u_ref, wqd_ref, qk_ref, kw_ref, bc_ref, gl_ref, sb_ref, *, group,
              fill=lambda n=1: None):
    seq = q_ref.shape[1]
    C = GDN_CHUNK
    D = HEAD_DIM
    pad = SUBLANES
    n_chunks = seq // C

    static_chunks = n_chunks == group
    if not static_chunks:
        fill = lambda n=1: None
    row = lax.broadcasted_iota(jnp.int32, (C, C), 0)
    col = lax.broadcasted_iota(jnp.int32, (C, C), 1)
    tri_incl = row >= col
    tri_strict = row > col
    lane = lax.broadcasted_iota(jnp.int32, (SUBLANES, C), 1)
    neg_a = -jnp.exp(jnp.full((1, C), alog_ref[h], F32))
    dt_bias = dtb_ref[h]
    scale = D ** -0.5

    def conv_silu(x_ref, w_ref, c, r0):
        if static_chunks:
            win = (jnp.concatenate([jnp.zeros((pad, D), F32), x_ref[0, 0:C, :]], axis=0) if c == 0
                   else x_ref[0, r0 - pad:r0 + C, :])
        else:
            halo = x_ref[0, pl.ds(jnp.maximum(r0 - pad, 0), pad), :]
            win = jnp.concatenate([jnp.where(c > 0, halo, 0.0), x_ref[0, pl.ds(r0, C), :]], axis=0)
        w = w_ref[...]
        prev = pltpu.roll(win, 1, axis=0)
        near = win * w[3:4] + prev * w[2:3]
        far = pltpu.roll(win * w[1:2] + prev * w[0:1], 2, axis=0)
        return _silu(near[pad:pad + C] + far[pad:pad + C])

    def l2n(x):
        return x * lax.rsqrt(jnp.sum(x * x, axis=-1, keepdims=True) + EPS)

    def cumsum_lanes(x):
        x = jnp.broadcast_to(x, (SUBLANES, C))
        s = 1
        while s < C:
            x = x + jnp.where(lane >= s, pltpu.roll(x, s, axis=1), 0.0)
            s *= 2
        return x[0:1, :]

    reps = C // D

    def rows_to_cols(x):
        return jnp.concatenate(
            [jnp.broadcast_to(x[:, n * D:(n + 1) * D], (D, D)).T for n in range(reps)], axis=0)

    def chunk_ids(grp):
        cs = [grp * group + i for i in range(group)]
        return cs, [c * C if static_chunks else pl.multiple_of(c * C, C) for c in cs]

    def for_each_group(body):
        if static_chunks:
            body(0, 0)
        else:
            lax.fori_loop(0, n_chunks // group, body, 0)

    def prepare(grp, carry):
        cs, r0s = chunk_ids(grp)
        pre, gbs, g_rows = [], [], []
        for c, r0 in zip(cs, r0s):
            q = l2n(conv_silu(q_ref, wq_ref, c, r0)) * scale
            k = l2n(conv_silu(k_ref, wk_ref, c, r0))
            v = conv_silu(v_ref, wv_ref, c, r0)
            b_row = bat_ref[0, h, pl.ds(c, 1), :]
            xs = bat_ref[0, h + N_GDN_HEADS, pl.ds(c, 1), :] + dt_bias
            softplus = jnp.maximum(xs, 0.0) + jnp.log1p(jnp.exp(-jnp.abs(xs)))
            g_row = cumsum_lanes(neg_a * softplus)
            pre.append((q, k, v, rows_to_cols(_sigmoid(b_row)), None))
            g_rows.append(g_row)
            gbs.append(rows_to_cols(g_row))
            fill()
        kk_qks = [_bdot_nt(jnp.concatenate([k * beta, q], axis=0), k) for q, k, v, beta, _ in pre]
        mats, mids = [], []
        for (q, k, v, beta, _), gb, g_row, kk_qk in zip(pre, gbs, g_rows, kk_qks):
            g_i = jnp.concatenate([gb] * reps, axis=1)
            g_j = jnp.broadcast_to(g_row, (C, C))
            decay = jnp.exp(jnp.where(tri_incl, g_i - g_j, NEG))
            mats.append(jnp.where(tri_strict, kk_qk[:C] * decay, 0.0))
            mids.append((kk_qk[C:] * decay, jnp.exp(gb), gb[C - 1:C, :]))
        ts = _unit_lower_inverse(mats, row, col, fill=fill)
        uws = [_bdot(t, jnp.concatenate([v * beta, k * beta * eg], axis=1))
               for t, (q, k, v, beta, _), (_, eg, _) in zip(ts, pre, mids)]
        transs = [_bdot_tn(k * jnp.exp(g_last - gb), uw)
                  for uw, (q, k, v, beta, _), gb, (_, _, g_last) in zip(uws, pre, gbs, mids)]
        for c, r0, uw, trans, (q, k, v, beta, _), (qk, eg, g_last) in zip(cs, r0s, uws, transs, pre, mids):
            u_ref[pl.ds(r0, C), :] = uw[:, :D]
            wqd_ref[c, 0:C, :] = uw[:, D:].astype(BF16)
            wqd_ref[c, C:2 * C, :] = (q * eg).astype(BF16)
            qk_ref[pl.ds(r0, C), :] = qk.astype(BF16)
            bc_ref[c] = trans[:, :D]
            kw_ref[c] = trans[:, D:].astype(BF16)
            gl_ref[c] = jnp.exp(g_last)
        return carry

    for_each_group(prepare)

    def chain(c, state):
        sb = state.astype(BF16)
        sb_ref[c] = sb
        return state * gl_ref[c] - jnp.dot(kw_ref[c], sb, preferred_element_type=F32) + bc_ref[c]

    if static_chunks:
        state = jnp.zeros((D, D), F32)
        for c in range(n_chunks):
            state = chain(c, state)
            fill(GDN_FILL_PER_STEP)
    else:
        lax.fori_loop(0, n_chunks, chain, jnp.zeros((D, D), F32))

    def outputs(grp, carry):
        cs, r0s = chunk_ids(grp)
        ws_qs = [jnp.dot(wqd_ref[c], sb_ref[c], preferred_element_type=F32) for c in cs]
        v_news = [(u_ref[pl.ds(r0, C), :] - wq[:C]).astype(BF16) for r0, wq in zip(r0s, ws_qs)]
        os = [wq[C:] + jnp.dot(qk_ref[pl.ds(r0, C), :], vn, preferred_element_type=F32)
              for r0, wq, vn in zip(r0s, ws_qs, v_news)]
        for r0, o in zip(r0s, os):
            z = z_ref[0, pl.ds(r0, C), :]
            o_ref[0, pl.ds(r0, C), :] = (_rms(o, ng_ref[...]) * _silu(z)).astype(o_ref.dtype)
        return carry

    for_each_group(outputs)


def _bucket_upper_bounds():
    n = np.arange(0, 4 * REL_MAX_DIST, dtype=np.int64)
    max_exact = REL_BUCKETS // 2
    nf = np.maximum(n, 1).astype(np.float32)
    large = max_exact + (np.log(nf / np.float32(max_exact)) / np.float32(math.log(REL_MAX_DIST / max_exact))
                         * np.float32(REL_BUCKETS - max_exact)).astype(np.int32)
    large = np.minimum(large, REL_BUCKETS - 1)
    bucket = np.where(n < max_exact, n, large)
    assert np.all(np.diff(bucket) >= 0) and bucket[-1] == REL_BUCKETS - 1
    return [int(np.searchsorted(bucket, b, side="right")) for b in range(REL_BUCKETS - 1)]


_BUCKET_UPPER = _bucket_upper_bounds()


def _moba_steps(h, b, rb_ref, q_ref, k_ref, v_ref, ng_ref, o_ref, vt_ref, km_ref, bd_ref, bl_ref):
    seq = k_ref.shape[1]
    T = MOBA_BLOCK
    nb = seq // T
    scale = HEAD_DIM ** -0.5 * LOG2E
    kk = lax.broadcasted_iota(jnp.int32, (T, T), 0)
    qq = lax.broadcasted_iota(jnp.int32, (T, T), 1)

    @pl.when(b == 0)
    def _():
        def bias_of(n):
            val = jnp.full((T, T), rb_ref[REL_BUCKETS - 1, h], F32)
            for bkt in range(REL_BUCKETS - 2, -1, -1):
                val = jnp.where(n < _BUCKET_UPPER[bkt], rb_ref[bkt, h], val)
            return val
        bd_ref[...] = jnp.where(qq >= kk, bias_of(qq - kk) * LOG2E, NEG)
        bl_ref[...] = bias_of(qq - kk + T) * LOG2E

    for n in range(nb):
        blk_rows = slice(n * T, (n + 1) * T)
        vt_ref[0:HEAD_DIM, blk_rows] = v_ref[0, blk_rows, :].astype(F32).T.astype(BF16)
        km_ref[n:n + 1, :] = jnp.mean(k_ref[0, blk_rows, :].astype(F32), axis=0, keepdims=True)
    vt_ref[HEAD_DIM:HEAD_DIM + BF16_ROWS, :] = jnp.ones((BF16_ROWS, seq), BF16)
    far_bias = rb_ref[REL_BUCKETS - 1, h] * LOG2E

    def additive_mask(i, qf):
        if i <= MOBA_TOPK:
            return None
        gate = lax.dot_general(km_ref[0:i, :], qf, (((1,), (1,)), ((), ())),
                               preferred_element_type=F32, precision=lax.Precision.HIGHEST)
        blk = lax.broadcasted_iota(jnp.int32, (i, T), 0)
        rank = jnp.zeros((i, T), jnp.int32)
        for m in range(i):
            gm = gate[m:m + 1, :]
            rank = rank + jnp.where((gm > gate) | ((gm == gate) & (m < blk)), 1, 0)
        return jnp.where(rank < MOBA_TOPK, 0.0, NEG)

    def biased(i, s_all, keep):
        parts, maxes = [], []
        for j in range(i + 1):
            sj = s_all[j * T:(j + 1) * T, :]
            if j >= i - 1:
                sj = sj + (bd_ref[...] if j == i else bl_ref[...])
                row = keep[j:j + 1, :] if (keep is not None and j < i) else 0.0
            else:
                row = far_bias if keep is None else keep[j:j + 1, :] + far_bias
            parts.append((sj, row))
            maxes.append(jnp.max(sj, axis=0, keepdims=True) + row)
        return parts, functools.reduce(jnp.maximum, maxes)

    order = [t for pair in zip(range(nb // 2), range(nb - 1, nb // 2 - 1, -1)) for t in pair]
    if nb % 2:
        order.append(nb // 2)
    for g0 in range(0, nb, MOBA_TILES_IN_FLIGHT):
        tiles = order[g0:g0 + MOBA_TILES_IN_FLIGHT]
        qfs = [q_ref[0, i * T:(i + 1) * T, :].astype(F32) for i in tiles]
        keeps, s_alls, partss, m_rows, p_alls, o_augs = [], [], [], [], [], []
        for i, qf in zip(tiles, qfs):
            keeps.append(additive_mask(i, qf))
            s_alls.append(_bdot_nt(k_ref[0, 0:(i + 1) * T, :], qf * scale))
            yield
        for i, s_all, keep in zip(tiles, s_alls, keeps):
            parts, m_row = biased(i, s_all, keep)
            partss.append(parts)
            m_rows.append(m_row)
            yield
        for parts, m_row in zip(partss, m_rows):
            p_alls.append(jnp.concatenate(
                [jnp.exp2((sj + (row - m_row)).astype(BF16)) for sj, row in parts], axis=0))
            yield
        for i, p_all in zip(tiles, p_alls):
            o_augs.append(jnp.dot(vt_ref[:, 0:(i + 1) * T], p_all, preferred_element_type=F32))
            yield
        for i, o in zip(tiles, o_augs):
            o_t = o[0:HEAD_DIM] / o[HEAD_DIM:HEAD_DIM + 1]
            o_ref[0, i * T:(i + 1) * T, :] = _rms(o_t.T, ng_ref[...]).astype(o_ref.dtype)
            yield


def _mixer_kernel(alog_ref, dtb_ref, rb_ref, q_ref, k_ref, v_ref, z_ref, bat_ref, wq_ref, wk_ref, wv_ref,
                  gng_ref, mq_ref, mk_ref, mv_ref, mng_ref, oa_ref, ob_ref,
                  u_ref, wqd_ref, qk_ref, kw_ref, bc_ref, gl_ref, sb_ref, vt_ref, km_ref, bd_ref, bl_ref,
                  *, group):
    h = pl.program_id(0)
    b = pl.program_id(1)
    moba = _moba_steps(h, b, rb_ref, mq_ref, mk_ref, mv_ref, mng_ref, ob_ref, vt_ref, km_ref, bd_ref, bl_ref)

    def fill(n=1):
        for _ in range(n):
            next(moba, None)

    _gdn_body(h, alog_ref, dtb_ref, q_ref, k_ref, v_ref, z_ref, bat_ref, wq_ref, wk_ref, wv_ref,
              gng_ref, oa_ref, u_ref, wqd_ref, qk_ref, kw_ref, bc_ref, gl_ref, sb_ref, group=group, fill=fill)
    for _ in moba:
        pass


def _mixer(pa, pb, conv_w, a_log, dt_bias, gdn_norm_g, rel_bias, moba_norm_g, *, group=8):
    b, s, _ = pa.shape
    H = N_GDN_HEADS
    assert N_MOBA_HEADS == H
    C, T = GDN_CHUNK, MOBA_BLOCK
    assert s % C == 0 and s % T == 0
    n_chunks, nb = s // C, s // T
    bat = pa[:, :, 4 * GDN_WIDTH:4 * GDN_WIDTH + 2 * H].transpose(0, 2, 1).reshape(b, 2 * H, n_chunks, C)
    head = lambda off: pl.BlockSpec((1, s, HEAD_DIM), lambda h, i: (i, 0, off + h))
    cw = lambda off: pl.BlockSpec((GDN_CONV, HEAD_DIM), lambda h, i: (0, off + h))
    vec = pl.BlockSpec((1, HEAD_DIM), lambda h, i: (0, 0))
    smem = pl.BlockSpec(memory_space=pltpu.SMEM)
    out = pl.BlockSpec((1, s, HEAD_DIM), lambda h, i: (i, 0, h))
    return pl.pallas_call(
        functools.partial(_mixer_kernel, group=math.gcd(group, n_chunks)),
        grid=(H, b),
        in_specs=[smem, smem, smem, head(0), head(H), head(2 * H), head(3 * H),
                  pl.BlockSpec((1, 2 * H, n_chunks, C), lambda h, i: (i, 0, 0, 0)),
                  cw(0), cw(H), cw(2 * H), vec,
                  head(0), head(H), head(2 * H), vec],
        out_specs=[out, out],
        out_shape=[jax.ShapeDtypeStruct((b, s, GDN_WIDTH), BF16), jax.ShapeDtypeStruct((b, s, MOBA_WIDTH), BF16)],
        scratch_shapes=[
            pltpu.VMEM((s, HEAD_DIM), F32),
            pltpu.VMEM((n_chunks, 2 * C, HEAD_DIM), BF16),
            pltpu.VMEM((s, C), BF16),
            pltpu.VMEM((n_chunks, HEAD_DIM, HEAD_DIM), BF16),
            pltpu.VMEM((n_chunks, HEAD_DIM, HEAD_DIM), F32),
            pltpu.VMEM((n_chunks, 1, HEAD_DIM), F32),
            pltpu.VMEM((n_chunks, HEAD_DIM, HEAD_DIM), BF16),
            pltpu.VMEM((HEAD_DIM + BF16_ROWS, s), BF16),
            pltpu.VMEM((nb, HEAD_DIM), F32),
            pltpu.VMEM((T, T), F32), pltpu.VMEM((T, T), F32)],
        compiler_params=pltpu.CompilerParams(
            dimension_semantics=("arbitrary", "arbitrary"), vmem_limit_bytes=VMEM_LIMIT),
        name="mixer",
    )(a_log, dt_bias, rel_bias, pa, pa, pa, pa, bat, conv_w, conv_w, conv_w, gdn_norm_g.reshape(1, HEAD_DIM),
      pb, pb, pb, moba_norm_g.reshape(1, HEAD_DIM))


def _xattn_kernel(x_ref, oa_ref, ob_ref, wa_ref, wb_ref, g_ref, wq_ref, kv_ref, wo_ref, ng_ref,
                  o_ref, hn_ref, *, rs):
    width = N_XATTN_HEADS * HEAD_DIM
    scale = HEAD_DIM ** -0.5 * LOG2E
    blocks = [slice(r * rs, (r + 1) * rs) for r in range(x_ref.shape[1] // rs)]
    x1s = [x_ref[0, rows, :]
           + jnp.dot(oa_ref[0, rows, :], wa_ref[...], preferred_element_type=F32)
           + jnp.dot(ob_ref[0, rows, :], wb_ref[...], preferred_element_type=F32) for rows in blocks]
    qs = [_bdot(_rms(x1, g_ref[...]), wq_ref[...]) * scale for x1 in x1s]
    outs = [[] for _ in blocks]
    for hd in range(N_XATTN_HEADS):
        sl = slice(hd * HEAD_DIM, (hd + 1) * HEAD_DIM)
        ss = [_bdot_nt(q[:, sl], kv_ref[0, :, sl]) for q in qs]
        ps = [jnp.exp2(s - jnp.max(s, axis=-1, keepdims=True)) for s in ss]
        for out, p in zip(outs, ps):
            l = jnp.sum(p, axis=-1, keepdims=True)
            out.append(_bdot(p, kv_ref[0, :, width + hd * HEAD_DIM: width + (hd + 1) * HEAD_DIM]) / l)
    ys = [x1 + _bdot(jnp.concatenate(out, axis=-1), wo_ref[...]) for x1, out in zip(x1s, outs)]
    for rows, y in zip(blocks, ys):
        o_ref[0, rows, :] = y
        hn_ref[0, rows, :] = _rms(y, ng_ref[...]).astype(BF16)


def _xattn(x, oa, ob, wa, wb, g, wq, kv, wo, next_g, *, ts):
    b, s, d = x.shape
    mlen = kv.shape[1]
    width = wq.shape[1]
    ka, kb = oa.shape[2], ob.shape[2]
    tile = pl.BlockSpec((1, ts, d), lambda i, j: (i, j, 0))
    vec = pl.BlockSpec((1, d), lambda i, j: (0, 0))
    whole = lambda r, c: pl.BlockSpec((r, c), lambda i, j: (0, 0))
    return pl.pallas_call(
        functools.partial(_xattn_kernel, rs=math.gcd(ts, 2 * ROW_SUB)),
        grid=(b, s // ts),
        in_specs=[tile,
                  pl.BlockSpec((1, ts, ka), lambda i, j: (i, j, 0)),
                  pl.BlockSpec((1, ts, kb), lambda i, j: (i, j, 0)),
                  whole(ka, d), whole(kb, d), vec, whole(d, width),
                  pl.BlockSpec((1, mlen, 2 * width), lambda i, j: (i, 0, 0)),
                  whole(width, d), vec],
        out_specs=[tile, tile],
        out_shape=[jax.ShapeDtypeStruct((b, s, d), F32), jax.ShapeDtypeStruct((b, s, d), BF16)],
        compiler_params=pltpu.CompilerParams(
            dimension_semantics=("parallel", "parallel"), vmem_limit_bytes=VMEM_LIMIT),
        name="xattn",
    )(x, oa, ob, wa, wb, g.reshape(1, d), wq, kv, wo, next_g.reshape(1, d))


def _ffn_kernel(x_hbm, h_ref, halo_ref, wg_ref, wu_ref, cw_ref, cb_ref, wd_ref, fg_ref, o_ref,
                hx_ref, xres_ref, sem, *, tiles_per_seq, rs):
    i = pl.program_id(0)
    f = pl.program_id(1)
    tm = h_ref.shape[0]
    pad = BF16_ROWS

    def x_copy():
        return pltpu.make_async_copy(x_hbm.at[pl.ds(i * tm, tm), :], xres_ref, sem)

    @pl.when(f == 0)
    def _():
        x_copy().start()
        halo = halo_ref[...]
        hx_ref[0:pad, :] = jnp.where(i % tiles_per_seq == 0, jnp.zeros_like(halo), halo)
        hx_ref[pad:pad + tm, :] = h_ref[...]
        o_ref[...] = jnp.zeros_like(o_ref)

    cw = cw_ref[...]
    keep = SUBLANES
    tail = None
    for r in range(tm // rs):
        rows = slice(r * rs, (r + 1) * rs)
        hrows = slice(pad + r * rs, pad + (r + 1) * rs)
        if r == 0:
            gp = jnp.dot(hx_ref[0:pad + rs, :], wg_ref[...], preferred_element_type=F32)[pad - keep:]
        else:
            gp = jnp.concatenate(
                [tail, jnp.dot(hx_ref[hrows, :], wg_ref[...], preferred_element_type=F32)], axis=0)
        tail = gp[rs:rs + keep]
        up = jnp.dot(hx_ref[hrows, :], wu_ref[...], preferred_element_type=F32)
        gate = gp[keep:keep + rs] * cw[2:3] + gp[keep - 1:keep - 1 + rs] * cw[1:2] \
            + gp[keep - 2:keep - 2 + rs] * cw[0:1] + cb_ref[...]
        act = _silu(gate) * up
        o_ref[rows, :] += jnp.dot(act.astype(BF16), wd_ref[...], preferred_element_type=F32)

    @pl.when(f == pl.num_programs(1) - 1)
    def _():
        x_copy().wait()

        def finish(r, carry):
            rows = pl.ds(pl.multiple_of(r * HEAD_DIM, HEAD_DIM), HEAD_DIM)
            o_ref[rows, :] = _rms(xres_ref[rows, :] + o_ref[rows, :], fg_ref[...])
            return carry

        lax.fori_loop(0, tm // HEAD_DIM, finish, 0)


def _ffn(x, h, wg, wu, cw, cb, wd, fg, *, seq, tm, tf):
    m, d = x.shape
    ff = wg.shape[1]
    assert seq % tm == 0 and ff % tf == 0 and tm % BF16_ROWS == 0
    hb = tm // BF16_ROWS
    return pl.pallas_call(
        functools.partial(_ffn_kernel, tiles_per_seq=seq // tm, rs=tm),
        grid=(m // tm, ff // tf),
        in_specs=[pl.BlockSpec(memory_space=pl.ANY),
                  pl.BlockSpec((tm, d), lambda i, f: (i, 0)),
                  pl.BlockSpec((BF16_ROWS, d), lambda i, f: (jnp.maximum(i * hb - 1, 0), 0)),
                  pl.BlockSpec((d, tf), lambda i, f: (0, f)),
                  pl.BlockSpec((d, tf), lambda i, f: (0, f)),
                  pl.BlockSpec((FFN_CONV, tf), lambda i, f: (0, f)),
                  pl.BlockSpec((1, tf), lambda i, f: (0, f)),
                  pl.BlockSpec((tf, d), lambda i, f: (f, 0)),
                  pl.BlockSpec((1, d), lambda i, f: (0, 0))],
        out_specs=pl.BlockSpec((tm, d), lambda i, f: (i, 0)),
        out_shape=jax.ShapeDtypeStruct((m, d), F32),
        scratch_shapes=[pltpu.VMEM((tm + BF16_ROWS, d), BF16), pltpu.VMEM((tm, d), F32),
                        pltpu.SemaphoreType.DMA(())],
        compiler_params=pltpu.CompilerParams(
            dimension_semantics=("arbitrary", "arbitrary"), vmem_limit_bytes=VMEM_LIMIT),
        name="ffn",
    )(x, h, h, wg, wu, cw, cb.reshape(1, ff), wd, fg.reshape(1, d))


def _layer(x, mem, mix_norm_g, w_in, gdn_conv_w, a_log, dt_bias, gdn_norm_g, moba_norm_g, rel_bias,
           w_out, xattn_norm_g, mem_norm_g, w_xq, w_xkv, w_xo, ffn_norm_g, w_gate, w_up, ffn_conv_w,
           ffn_conv_b, w_down, final_g, *, last):
    b, s, d = x.shape
    m = b * s
    x2 = x.reshape(m, d)
    i1 = 4 * GDN_WIDTH
    i3 = i1 + 2 * N_GDN_HEADS
    w_all = w_in.astype(BF16)
    w_b = w_all[:, i3:]
    tm = min(1024, m)
    wa_cols = i1 + HEAD_DIM
    pa = _norm_matmul(x2, mix_norm_g, w_all, tm=tm, tn=wa_cols // 3, n=wa_cols,
                      row_sub=ROW_SUB // 2).reshape(b, s, wa_cols)
    pb = _norm_matmul(x2, mix_norm_g, w_b, tm=tm, tn=1024, out_dtype=BF16).reshape(b, s, 3 * MOBA_WIDTH)
    o_a, o_b = _mixer(pa, pb, gdn_conv_w, a_log, dt_bias, gdn_norm_g, rel_bias, moba_norm_g)
    mlen = mem.shape[1]
    kv = _norm_matmul(mem.reshape(b * mlen, d), mem_norm_g, w_xkv.astype(BF16),
                      tm=min(512, b * mlen), tn=1024, out_dtype=BF16).reshape(b, mlen, -1)
    x2b, h2 = _xattn(x, o_a, o_b, w_out[:GDN_WIDTH].astype(BF16), w_out[GDN_WIDTH:].astype(BF16),
                     xattn_norm_g, w_xq.astype(BF16), kv, w_xo.astype(BF16), ffn_norm_g, ts=min(512, s))
    assert last, "the final rmsnorm is fused into the last layer's ffn"
    y = _ffn(x2b.reshape(m, d), h2.reshape(m, d), w_gate.astype(BF16), w_up.astype(BF16), ffn_conv_w,
             ffn_conv_b, w_down.astype(BF16), final_g, seq=s, tm=min(1024, s), tf=512)
    return y.reshape(b, s, d)


def kernel(x, mem, mix_norm_g, w_in, gdn_conv_w, gdn_a_log, gdn_dt_bias, gdn_norm_g, moba_norm_g,
           rel_bias, w_out, xattn_norm_g, mem_norm_g, w_xq, w_xkv, w_xo, ffn_norm_g, w_gate, w_up,
           ffn_conv_w, ffn_conv_b, w_down, final_norm_g):
    depth = mix_norm_g.shape[0]
    assert depth == 1
    l = 0
    return _layer(x, mem, mix_norm_g[l], w_in[l], gdn_conv_w[l], gdn_a_log[l], gdn_dt_bias[l],
                  gdn_norm_g[l], moba_norm_g[l], rel_bias, w_out[l], xattn_norm_g[l], mem_norm_g[l],
                  w_xq[l], w_xkv[l], w_xo[l], ffn_norm_g[l], w_gate[l], w_up[l], ffn_conv_w[l],
                  ffn_conv_b[l], w_down[l], final_norm_g, last=True)
```

```python
import functools
import math

import jax
import jax.numpy as jnp
import numpy as np
from jax import lax
from jax.experimental import pallas as pl
from jax.experimental.pallas import tpu as pltpu

HEAD_DIM = 128
N_GDN_HEADS = 8
N_MOBA_HEADS = 8
GDN_WIDTH = N_GDN_HEADS * HEAD_DIM
MOBA_WIDTH = N_MOBA_HEADS * HEAD_DIM
GDN_CONV = 4
GDN_CHUNK = 256
GDN_FILL_PER_STEP = 2
GDN_HALF_ROWS_FROM = 64
MOBA_BLOCK = 256
MOBA_TOPK = 3
MOBA_TILES_IN_FLIGHT = 8
REL_BUCKETS = 32
REL_MAX_DIST = 128
N_XATTN_HEADS = 4
FFN_CONV = 3
EPS = 1e-6
NEG = -1e30
LOG2E = math.log2(math.e)
SUBLANES = 8
BF16_ROWS = 16
ROW_SUB = 256
VMEM_LIMIT = 58 * 1024 * 1024

F32 = jnp.float32
BF16 = jnp.bfloat16


def _bdot(a, b):
    return jnp.dot(a.astype(BF16), b.astype(BF16), preferred_element_type=F32)


def _bdot_nt(a, b):
    return lax.dot_general(a.astype(BF16), b.astype(BF16), (((1,), (1,)), ((), ())),
                           preferred_element_type=F32)


def _bdot_tn(a, b):
    return lax.dot_general(a.astype(BF16), b.astype(BF16), (((0,), (0,)), ((), ())),
                           preferred_element_type=F32)


def _sigmoid(x):
    return 1.0 / (1.0 + jnp.exp(-x))


def _silu(x):
    return x * _sigmoid(x)


def _rms(x, g):
    return x * lax.rsqrt(jnp.mean(x * x, axis=-1, keepdims=True) + EPS) * g


def _norm_matmul_kernel(x_ref, g_ref, w_ref, o_ref, *, rs):
    for r in range(x_ref.shape[0] // rs):
        rows = slice(r * rs, (r + 1) * rs)
        hn = _rms(x_ref[rows, :], g_ref[...]).astype(BF16)
        o_ref[rows, :] = jnp.dot(hn, w_ref[...], preferred_element_type=F32).astype(o_ref.dtype)


def _norm_matmul(x, g, w, *, tm, tn, n=None, out_dtype=F32, row_sub=ROW_SUB):
    m, k = x.shape
    n = w.shape[1] if n is None else n
    assert m % tm == 0 and n % tn == 0 and n <= w.shape[1]
    rs = math.gcd(tm, row_sub)
    return pl.pallas_call(
        functools.partial(_norm_matmul_kernel, rs=rs),
        grid=(m // tm, n // tn),
        in_specs=[pl.BlockSpec((tm, k), lambda i, j: (i, 0)),
                  pl.BlockSpec((1, k), lambda i, j: (0, 0)),
                  pl.BlockSpec((k, tn), lambda i, j: (0, j))],
        out_specs=pl.BlockSpec((tm, tn), lambda i, j: (i, j)),
        out_shape=jax.ShapeDtypeStruct((m, n), out_dtype),
        compiler_params=pltpu.CompilerParams(
            dimension_semantics=("parallel", "arbitrary"), vmem_limit_bytes=VMEM_LIMIT),
        name="norm_matmul",
    )(x, g.reshape(1, k), w)


def _unit_lower_inverse(mats, row, col, fill=lambda: None):
    c = mats[0].shape[0]
    eye = (row == col).astype(F32)
    blk = lambda n: (row // n) == (col // n)
    inner = blk(16)
    ds = [jnp.where(inner, a, 0.0) for a in mats]
    ts = [eye - d for d in ds]
    ps = [_bdot(d, d) for d in ds]
    fill()
    for step in range(3):
        ts = [t + _bdot(t, p) for t, p in zip(ts, ps)]
        fill()
        if step < 2:
            ps = [_bdot(p, p) for p in ps]
            fill()
    half_row = lax.broadcasted_iota(jnp.int32, (c // 2, c), 0)
    half_col = lax.broadcasted_iota(jnp.int32, (c // 2, c), 1)
    n = 32
    while n <= c:
        h = n // 2
        pairs = range(c // n)
        lower = lambda m: jnp.concatenate([m[(2 * k + 1) * h:(2 * k + 2) * h] for k in pairs], axis=0)
        if n < GDN_HALF_ROWS_FROM:
            keep = blk(n) & ~inner
            ys = [lower(_bdot(jnp.where(keep, a, 0.0), t)) for a, t in zip(mats, ts)]
        else:
            left = (half_col // h) == 2 * (half_row // h)
            ys = [_bdot(jnp.where(left, lower(a), 0.0), t) for a, t in zip(mats, ts)]
        fill()
        zero = jnp.zeros((h, c), F32)
        spread = lambda y: jnp.concatenate(
            [part for k in pairs for part in (zero, y[k * h:(k + 1) * h])], axis=0)
        zs = [_bdot(lower(t), spread(y)) for t, y in zip(ts, ys)]
        ts = [jnp.concatenate([part for k in pairs for part in
                               (t[2 * k * h:(2 * k + 1) * h],
                                t[(2 * k + 1) * h:(2 * k + 2) * h] - z[k * h:(k + 1) * h])], axis=0)
              for t, z in zip(ts, zs)]
        fill()
        inner = blk(n)
        n *= 2
    return ts


def _gdn_body(h, alog_ref, dtb_ref, q_ref, k_ref, v_ref, z_ref, bat_ref, wq_ref, wk_ref, wv_ref,
              ng_ref, o_ref, u_ref, wqd_ref, qk_ref, kw_ref, bc_ref, gl_ref, sb_ref, *, group,
              fill=lambda n=1: None):
    seq = q_ref.shape[1]
    C = GDN_CHUNK
    D = HEAD_DIM
    pad = SUBLANES
    n_chunks = seq // C

    static_chunks = n_chunks == group
    if not static_chunks:
        fill = lambda n=1: None
    row = lax.broadcasted_iota(jnp.int32, (C, C), 0)
    col = lax.broadcasted_iota(jnp.int32, (C, C), 1)
    tri_incl = row >= col
    tri_strict = row > col
    lane = lax.broadcasted_iota(jnp.int32, (SUBLANES, C), 1)
    neg_a = -jnp.exp(jnp.full((1, C), alog_ref[h], F32))
    dt_bias = dtb_ref[h]
    scale = D ** -0.5

    def conv_silu(x_ref, w_ref, c, r0):
        if static_chunks:
            win = (jnp.concatenate([jnp.zeros((pad, D), F32), x_ref[0, 0:C, :]], axis=0) if c == 0
                   else x_ref[0, r0 - pad:r0 + C, :])
        else:
            halo = x_ref[0, pl.ds(jnp.maximum(r0 - pad, 0), pad), :]
            win = jnp.concatenate([jnp.where(c > 0, halo, 0.0), x_ref[0, pl.ds(r0, C), :]], axis=0)
        w = w_ref[...]
        prev = pltpu.roll(win, 1, axis=0)
        near = win * w[3:4] + prev * w[2:3]
        far = pltpu.roll(win * w[1:2] + prev * w[0:1], 2, axis=0)
        return _silu(near[pad:pad + C] + far[pad:pad + C])

    def l2n(x):
        return x * lax.rsqrt(jnp.sum(x * x, axis=-1, keepdims=True) + EPS)

    def cumsum_lanes(x):
        x = jnp.broadcast_to(x, (SUBLANES, C))
        s = 1
        while s < C:
            x = x + jnp.where(lane >= s, pltpu.roll(x, s, axis=1), 0.0)
            s *= 2
        return x[0:1, :]

    reps = C // D

    def rows_to_cols(x):
        return jnp.concatenate(
            [jnp.broadcast_to(x[:, n * D:(n + 1) * D], (D, D)).T for n in range(reps)], axis=0)

    def chunk_ids(grp):
        cs = [grp * group + i for i in range(group)]
        return cs, [c * C if static_chunks else pl.multiple_of(c * C, C) for c in cs]

    def for_each_group(body):
        if static_chunks:
            body(0, 0)
        else:
            lax.fori_loop(0, n_chunks // group, body, 0)

    def prepare(grp, carry):
        cs, r0s = chunk_ids(grp)
        pre, gbs, g_rows = [], [], []
        for c, r0 in zip(cs, r0s):
            q = l2n(conv_silu(q_ref, wq_ref, c, r0)) * scale
            k = l2n(conv_silu(k_ref, wk_ref, c, r0))
            v = conv_silu(v_ref, wv_ref, c, r0)
            b_row = bat_ref[0, h, pl.ds(c, 1), :]
            xs = bat_ref[0, h + N_GDN_HEADS, pl.ds(c, 1), :] + dt_bias
            softplus = jnp.maximum(xs, 0.0) + jnp.log1p(jnp.exp(-jnp.abs(xs)))
            g_row = cumsum_lanes(neg_a * softplus)
            pre.append((q, k, v, rows_to_cols(_sigmoid(b_row)), None))
            g_rows.append(g_row)
            gbs.append(rows_to_cols(g_row))
            fill()
        kk_qks = [_bdot_nt(jnp.concatenate([k * beta, q], axis=0), k) for q, k, v, beta, _ in pre]
        mats, mids = [], []
        for (q, k, v, beta, _), gb, g_row, kk_qk in zip(pre, gbs, g_rows, kk_qks):
            g_i = jnp.concatenate([gb] * reps, axis=1)
            g_j = jnp.broadcast_to(g_row, (C, C))
            decay = jnp.exp(jnp.where(tri_incl, g_i - g_j, NEG))
            mats.append(jnp.where(tri_strict, kk_qk[:C] * decay, 0.0))
            mids.append((kk_qk[C:] * decay, jnp.exp(gb), gb[C - 1:C, :]))
        ts = _unit_lower_inverse(mats, row, col, fill=fill)
        uws = [_bdot(t, jnp.concatenate([v * beta, k * beta * eg], axis=1))
               for t, (q, k, v, beta, _), (_, eg, _) in zip(ts, pre, mids)]
        transs = [_bdot_tn(k * jnp.exp(g_last - gb), uw)
                  for uw, (q, k, v, beta, _), gb, (_, _, g_last) in zip(uws, pre, gbs, mids)]
        for c, r0, uw, trans, (q, k, v, beta, _), (qk, eg, g_last) in zip(cs, r0s, uws, transs, pre, mids):
            u_ref[pl.ds(r0, C), :] = uw[:, :D]
            wqd_ref[c, 0:C, :] = uw[:, D:].astype(BF16)
            wqd_ref[c, C:2 * C, :] = (q * eg).astype(BF16)
            qk_ref[pl.ds(r0, C), :] = qk.astype(BF16)
            bc_ref[c] = trans[:, :D]
            kw_ref[c] = trans[:, D:].astype(BF16)
            gl_ref[c] = jnp.exp(g_last)
        return carry

    for_each_group(prepare)

    def chain(c, state):
        sb = state.astype(BF16)
        sb_ref[c] = sb
        return state * gl_ref[c] - jnp.dot(kw_ref[c], sb, preferred_element_type=F32) + bc_ref[c]

    if static_chunks:
        state = jnp.zeros((D, D), F32)
        for c in range(n_chunks):
            state = chain(c, state)
            fill(GDN_FILL_PER_STEP)
    else:
        lax.fori_loop(0, n_chunks, chain, jnp.zeros((D, D), F32))

    def outputs(grp, carry):
        cs, r0s = chunk_ids(grp)
        ws_qs = [jnp.dot(wqd_ref[c], sb_ref[c], preferred_element_type=F32) for c in cs]
        v_news = [(u_ref[pl.ds(r0, C), :] - wq[:C]).astype(BF16) for r0, wq in zip(r0s, ws_qs)]
        os = [wq[C:] + jnp.dot(qk_ref[pl.ds(r0, C), :], vn, preferred_element_type=F32)
              for r0, wq, vn in zip(r0s, ws_qs, v_news)]
        for r0, o in zip(r0s, os):
            z = z_ref[0, pl.ds(r0, C), :]
            o_ref[0, pl.ds(r0, C), :] = (_rms(o, ng_ref[...]) * _silu(z)).astype(o_ref.dtype)
        return carry

    for_each_group(outputs)


def _bucket_upper_bounds():
    n = np.arange(0, 4 * REL_MAX_DIST, dtype=np.int64)
    max_exact = REL_BUCKETS // 2
    nf = np.maximum(n, 1).astype(np.float32)
    large = max_exact + (np.log(nf / np.float32(max_exact)) / np.float32(math.log(REL_MAX_DIST / max_exact))
                         * np.float32(REL_BUCKETS - max_exact)).astype(np.int32)
    large = np.minimum(large, REL_BUCKETS - 1)
    bucket = np.where(n < max_exact, n, large)
    assert np.all(np.diff(bucket) >= 0) and bucket[-1] == REL_BUCKETS - 1
    return [int(np.searchsorted(bucket, b, side="right")) for b in range(REL_BUCKETS - 1)]


_BUCKET_UPPER = _bucket_upper_bounds()


def _moba_steps(h, b, rb_ref, q_ref, k_ref, v_ref, ng_ref, o_ref, vt_ref, km_ref, bd_ref, bl_ref):
    seq = k_ref.shape[1]
    T = MOBA_BLOCK
    nb = seq // T
    scale = HEAD_DIM ** -0.5 * LOG2E
    kk = lax.broadcasted_iota(jnp.int32, (T, T), 0)
    qq = lax.broadcasted_iota(jnp.int32, (T, T), 1)

    @pl.when(b == 0)
    def _():
        def bias_of(n):
            val = jnp.full((T, T), rb_ref[REL_BUCKETS - 1, h], F32)
            for bkt in range(REL_BUCKETS - 2, -1, -1):
                val = jnp.where(n < _BUCKET_UPPER[bkt], rb_ref[bkt, h], val)
            return val
        bd_ref[...] = jnp.where(qq >= kk, bias_of(qq - kk) * LOG2E, NEG)
        bl_ref[...] = bias_of(qq - kk + T) * LOG2E

    for n in range(nb):
        blk_rows = slice(n * T, (n + 1) * T)
        vt_ref[0:HEAD_DIM, blk_rows] = v_ref[0, blk_rows, :].astype(F32).T.astype(BF16)
        km_ref[n:n + 1, :] = jnp.mean(k_ref[0, blk_rows, :].astype(F32), axis=0, keepdims=True)
    vt_ref[HEAD_DIM:HEAD_DIM + BF16_ROWS, :] = jnp.ones((BF16_ROWS, seq), BF16)
    far_bias = rb_ref[REL_BUCKETS - 1, h] * LOG2E

    def additive_mask(i, qf):
        if i <= MOBA_TOPK:
            return None
        gate = lax.dot_general(km_ref[0:i, :], qf, (((1,), (1,)), ((), ())),
                               preferred_element_type=F32, precision=lax.Precision.HIGHEST)
        blk = lax.broadcasted_iota(jnp.int32, (i, T), 0)
        rank = jnp.zeros((i, T), jnp.int32)
        for m in range(i):
            gm = gate[m:m + 1, :]
            rank = rank + jnp.where((gm > gate) | ((gm == gate) & (m < blk)), 1, 0)
        return jnp.where(rank < MOBA_TOPK, 0.0, NEG)

    def biased(i, s_all, keep):
        parts, maxes = [], []
        for j in range(i + 1):
            sj = s_all[j * T:(j + 1) * T, :]
            if j >= i - 1:
                sj = sj + (bd_ref[...] if j == i else bl_ref[...])
                row = keep[j:j + 1, :] if (keep is not None and j < i) else 0.0
            else:
                row = far_bias if keep is None else keep[j:j + 1, :] + far_bias
            parts.append((sj, row))
            maxes.append(jnp.max(sj, axis=0, keepdims=True) + row)
        return parts, functools.reduce(jnp.maximum, maxes)

    order = [t for pair in zip(range(nb // 2), range(nb - 1, nb // 2 - 1, -1)) for t in pair]
    if nb % 2:
        order.append(nb // 2)
    for g0 in range(0, nb, MOBA_TILES_IN_FLIGHT):
        tiles = order[g0:g0 + MOBA_TILES_IN_FLIGHT]
        qfs = [q_ref[0, i * T:(i + 1) * T, :].astype(F32) for i in tiles]
        keeps, s_alls, partss, m_rows, p_alls, o_augs = [], [], [], [], [], []
        for i, qf in zip(tiles, qfs):
            keeps.append(additive_mask(i, qf))
            s_alls.append(_bdot_nt(k_ref[0, 0:(i + 1) * T, :], qf * scale))
            yield
        for i, s_all, keep in zip(tiles, s_alls, keeps):
            parts, m_row = biased(i, s_all, keep)
            partss.append(parts)
            m_rows.append(m_row)
            yield
        for parts, m_row in zip(partss, m_rows):
            p_alls.append(jnp.concatenate(
                [jnp.exp2((sj + (row - m_row)).astype(BF16)) for sj, row in parts], axis=0))
            yield
        for i, p_all in zip(tiles, p_alls):
            o_augs.append(jnp.dot(vt_ref[:, 0:(i + 1) * T], p_all, preferred_element_type=F32))
            yield
        for i, o in zip(tiles, o_augs):
            o_t = o[0:HEAD_DIM] / o[HEAD_DIM:HEAD_DIM + 1]
            o_ref[0, i * T:(i + 1) * T, :] = _rms(o_t.T, ng_ref[...]).astype(o_ref.dtype)
            yield


def _mixer_kernel(alog_ref, dtb_ref, rb_ref, q_ref, k_ref, v_ref, z_ref, bat_ref, wq_ref, wk_ref, wv_ref,
                  gng_ref, mq_ref, mk_ref, mv_ref, mng_ref, oa_ref, ob_ref,
                  u_ref, wqd_ref, qk_ref, kw_ref, bc_ref, gl_ref, sb_ref, vt_ref, km_ref, bd_ref, bl_ref,
                  *, group):
    h = pl.program_id(0)
    b = pl.program_id(1)
    moba = _moba_steps(h, b, rb_ref, mq_ref, mk_ref, mv_ref, mng_ref, ob_ref, vt_ref, km_ref, bd_ref, bl_ref)

    def fill(n=1):
        for _ in range(n):
            next(moba, None)

    _gdn_body(h, alog_ref, dtb_ref, q_ref, k_ref, v_ref, z_ref, bat_ref, wq_ref, wk_ref, wv_ref,
              gng_ref, oa_ref, u_ref, wqd_ref, qk_ref, kw_ref, bc_ref, gl_ref, sb_ref, group=group, fill=fill)
    for _ in moba:
        pass


def _mixer(pa, pb, conv_w, a_log, dt_bias, gdn_norm_g, rel_bias, moba_norm_g, *, group=8):
    b, s, _ = pa.shape
    H = N_GDN_HEADS
    assert N_MOBA_HEADS == H
    C, T = GDN_CHUNK, MOBA_BLOCK
    assert s % C == 0 and s % T == 0
    n_chunks, nb = s // C, s // T
    bat = pa[:, :, 4 * GDN_WIDTH:4 * GDN_WIDTH + 2 * H].transpose(0, 2, 1).reshape(b, 2 * H, n_chunks, C)
    head = lambda off: pl.BlockSpec((1, s, HEAD_DIM), lambda h, i: (i, 0, off + h))
    cw = lambda off: pl.BlockSpec((GDN_CONV, HEAD_DIM), lambda h, i: (0, off + h))
    vec = pl.BlockSpec((1, HEAD_DIM), lambda h, i: (0, 0))
    smem = pl.BlockSpec(memory_space=pltpu.SMEM)
    out = pl.BlockSpec((1, s, HEAD_DIM), lambda h, i: (i, 0, h))
    return pl.pallas_call(
        functools.partial(_mixer_kernel, group=math.gcd(group, n_chunks)),
        grid=(H, b),
        in_specs=[smem, smem, smem, head(0), head(H), head(2 * H), head(3 * H),
                  pl.BlockSpec((1, 2 * H, n_chunks, C), lambda h, i: (i, 0, 0, 0)),
                  cw(0), cw(H), cw(2 * H), vec,
                  head(0), head(H), head(2 * H), vec],
        out_specs=[out, out],
        out_shape=[jax.ShapeDtypeStruct((b, s, GDN_WIDTH), BF16), jax.ShapeDtypeStruct((b, s, MOBA_WIDTH), BF16)],
        scratch_shapes=[
            pltpu.VMEM((s, HEAD_DIM), F32),
            pltpu.VMEM((n_chunks, 2 * C, HEAD_DIM), BF16),
            pltpu.VMEM((s, C), BF16),
            pltpu.VMEM((n_chunks, HEAD_DIM, HEAD_DIM), BF16),
            pltpu.VMEM((n_chunks, HEAD_DIM, HEAD_DIM), F32),
            pltpu.VMEM((n_chunks, 1, HEAD_DIM), F32),
            pltpu.VMEM((n_chunks, HEAD_DIM, HEAD_DIM), BF16),
            pltpu.VMEM((HEAD_DIM + BF16_ROWS, s), BF16),
            pltpu.VMEM((nb, HEAD_DIM), F32),
            pltpu.VMEM((T, T), F32), pltpu.VMEM((T, T), F32)],
        compiler_params=pltpu.CompilerParams(
            dimension_semantics=("arbitrary", "arbitrary"), vmem_limit_bytes=VMEM_LIMIT),
        name="mixer",
    )(a_log, dt_bias, rel_bias, pa, pa, pa, pa, bat, conv_w, conv_w, conv_w, gdn_norm_g.reshape(1, HEAD_DIM),
      pb, pb, pb, moba_norm_g.reshape(1, HEAD_DIM))


def _xattn_kernel(x_ref, oa_ref, ob_ref, wa_ref, wb_ref, g_ref, wq_ref, kv_ref, wo_ref, ng_ref,
                  o_ref, hn_ref, *, rs):
    width = N_XATTN_HEADS * HEAD_DIM
    scale = HEAD_DIM ** -0.5 * LOG2E
    blocks = [slice(r * rs, (r + 1) * rs) for r in range(x_ref.shape[1] // rs)]
    x1s = [x_ref[0, rows, :]
           + jnp.dot(oa_ref[0, rows, :], wa_ref[...], preferred_element_type=F32)
           + jnp.dot(ob_ref[0, rows, :], wb_ref[...], preferred_element_type=F32) for rows in blocks]
    qs = [_bdot(_rms(x1, g_ref[...]), wq_ref[...]) * scale for x1 in x1s]
    outs = [[] for _ in blocks]
    for hd in range(N_XATTN_HEADS):
        sl = slice(hd * HEAD_DIM, (hd + 1) * HEAD_DIM)
        ss = [_bdot_nt(q[:, sl], kv_ref[0, :, sl]) for q in qs]
        ps = [jnp.exp2(s - jnp.max(s, axis=-1, keepdims=True)) for s in ss]
        for out, p in zip(outs, ps):
            l = jnp.sum(p, axis=-1, keepdims=True)
            out.append(_bdot(p, kv_ref[0, :, width + hd * HEAD_DIM: width + (hd + 1) * HEAD_DIM]) / l)
    ys = [x1 + _bdot(jnp.concatenate(out, axis=-1), wo_ref[...]) for x1, out in zip(x1s, outs)]
    for rows, y in zip(blocks, ys):
        o_ref[0, rows, :] = y
        hn_ref[0, rows, :] = _rms(y, ng_ref[...]).astype(BF16)


def _xattn(x, oa, ob, wa, wb, g, wq, kv, wo, next_g, *, ts):
    b, s, d = x.shape
    mlen = kv.shape[1]
    width = wq.shape[1]
    ka, kb = oa.shape[2], ob.shape[2]
    tile = pl.BlockSpec((1, ts, d), lambda i, j: (i, j, 0))
    vec = pl.BlockSpec((1, d), lambda i, j: (0, 0))
    whole = lambda r, c: pl.BlockSpec((r, c), lambda i, j: (0, 0))
    return pl.pallas_call(
        functools.partial(_xattn_kernel, rs=math.gcd(ts, 2 * ROW_SUB)),
        grid=(b, s // ts),
        in_specs=[tile,
                  pl.BlockSpec((1, ts, ka), lambda i, j: (i, j, 0)),
                  pl.BlockSpec((1, ts, kb), lambda i, j: (i, j, 0)),
                  whole(ka, d), whole(kb, d), vec, whole(d, width),
                  pl.BlockSpec((1, mlen, 2 * width), lambda i, j: (i, 0, 0)),
                  whole(width, d), vec],
        out_specs=[tile, tile],
        out_shape=[jax.ShapeDtypeStruct((b, s, d), F32), jax.ShapeDtypeStruct((b, s, d), BF16)],
        compiler_params=pltpu.CompilerParams(
            dimension_semantics=("parallel", "parallel"), vmem_limit_bytes=VMEM_LIMIT),
        name="xattn",
    )(x, oa, ob, wa, wb, g.reshape(1, d), wq, kv, wo, next_g.reshape(1, d))


def _ffn_kernel(x_hbm, h_ref, halo_ref, wg_ref, wu_ref, cw_ref, cb_ref, wd_ref, fg_ref, o_ref,
                hx_ref, xres_ref, sem, *, tiles_per_seq, rs):
    i = pl.program_id(0)
    f = pl.program_id(1)
    tm = h_ref.shape[0]
    pad = BF16_ROWS

    def x_copy():
        return pltpu.make_async_copy(x_hbm.at[pl.ds(i * tm, tm), :], xres_ref, sem)

    @pl.when(f == 0)
    def _():
        x_copy().start()
        halo = halo_ref[...]
        hx_ref[0:pad, :] = jnp.where(i % tiles_per_seq == 0, jnp.zeros_like(halo), halo)
        hx_ref[pad:pad + tm, :] = h_ref[...]
        o_ref[...] = jnp.zeros_like(o_ref)

    cw = cw_ref[...]
    keep = SUBLANES
    tail = None
    for r in range(tm // rs):
        rows = slice(r * rs, (r + 1) * rs)
        hrows = slice(pad + r * rs, pad + (r + 1) * rs)
        if r == 0:
            gp = jnp.dot(hx_ref[0:pad + rs, :], wg_ref[...], preferred_element_type=F32)[pad - keep:]
        else:
            gp = jnp.concatenate(
                [tail, jnp.dot(hx_ref[hrows, :], wg_ref[...], preferred_element_type=F32)], axis=0)
        tail = gp[rs:rs + keep]
        up = jnp.dot(hx_ref[hrows, :], wu_ref[...], preferred_element_type=F32)
        gate = gp[keep:keep + rs] * cw[2:3] + gp[keep - 1:keep - 1 + rs] * cw[1:2] \
            + gp[keep - 2:keep - 2 + rs] * cw[0:1] + cb_ref[...]
        act = _silu(gate) * up
        o_ref[rows, :] += jnp.dot(act.astype(BF16), wd_ref[...], preferred_element_type=F32)

    @pl.when(f == pl.num_programs(1) - 1)
    def _():
        x_copy().wait()

        def finish(r, carry):
            rows = pl.ds(pl.multiple_of(r * HEAD_DIM, HEAD_DIM), HEAD_DIM)
            o_ref[rows, :] = _rms(xres_ref[rows, :] + o_ref[rows, :], fg_ref[...])
            return carry

        lax.fori_loop(0, tm // HEAD_DIM, finish, 0)


def _ffn(x, h, wg, wu, cw, cb, wd, fg, *, seq, tm, tf):
    m, d = x.shape
    ff = wg.shape[1]
    assert seq % tm == 0 and ff % tf == 0 and tm % BF16_ROWS == 0
    hb = tm // BF16_ROWS
    return pl.pallas_call(
        functools.partial(_ffn_kernel, tiles_per_seq=seq // tm, rs=tm),
        grid=(m // tm, ff // tf),
        in_specs=[pl.BlockSpec(memory_space=pl.ANY),
                  pl.BlockSpec((tm, d), lambda i, f: (i, 0)),
                  pl.BlockSpec((BF16_ROWS, d), lambda i, f: (jnp.maximum(i * hb - 1, 0), 0)),
                  pl.BlockSpec((d, tf), lambda i, f: (0, f)),
                  pl.BlockSpec((d, tf), lambda i, f: (0, f)),
                  pl.BlockSpec((FFN_CONV, tf), lambda i, f: (0, f)),
                  pl.BlockSpec((1, tf), lambda i, f: (0, f)),
                  pl.BlockSpec((tf, d), lambda i, f: (f, 0)),
                  pl.BlockSpec((1, d), lambda i, f: (0, 0))],
        out_specs=pl.BlockSpec((tm, d), lambda i, f: (i, 0)),
        out_shape=jax.ShapeDtypeStruct((m, d), F32),
        scratch_shapes=[pltpu.VMEM((tm + BF16_ROWS, d), BF16), pltpu.VMEM((tm, d), F32),
                        pltpu.SemaphoreType.DMA(())],
        compiler_params=pltpu.CompilerParams(
            dimension_semantics=("arbitrary", "arbitrary"), vmem_limit_bytes=VMEM_LIMIT),
        name="ffn",
    )(x, h, h, wg, wu, cw, cb.reshape(1, ff), wd, fg.reshape(1, d))


def _layer(x, mem, mix_norm_g, w_in, gdn_conv_w, a_log, dt_bias, gdn_norm_g, moba_norm_g, rel_bias,
           w_out, xattn_norm_g, mem_norm_g, w_xq, w_xkv, w_xo, ffn_norm_g, w_gate, w_up, ffn_conv_w,
           ffn_conv_b, w_down, final_g, *, last):
    b, s, d = x.shape
    m = b * s
    x2 = x.reshape(m, d)
    i1 = 4 * GDN_WIDTH
    i3 = i1 + 2 * N_GDN_HEADS
    w_all = w_in.astype(BF16)
    w_b = w_all[:, i3:]
    tm = min(1024, m)
    wa_cols = i1 + HEAD_DIM
    pa = _norm_matmul(x2, mix_norm_g, w_all, tm=tm, tn=wa_cols // 3, n=wa_cols,
                      row_sub=ROW_SUB // 2).reshape(b, s, wa_cols)
    pb = _norm_matmul(x2, mix_norm_g, w_b, tm=tm, tn=1536, out_dtype=BF16).reshape(b, s, 3 * MOBA_WIDTH)
    o_a, o_b = _mixer(pa, pb, gdn_conv_w, a_log, dt_bias, gdn_norm_g, rel_bias, moba_norm_g)
    mlen = mem.shape[1]
    kv = _norm_matmul(mem.reshape(b * mlen, d), mem_norm_g, w_xkv.astype(BF16),
                      tm=min(512, b * mlen), tn=1024, out_dtype=BF16).reshape(b, mlen, -1)
    x2b, h2 = _xattn(x, o_a, o_b, w_out[:GDN_WIDTH].astype(BF16), w_out[GDN_WIDTH:].astype(BF16),
                     xattn_norm_g, w_xq.astype(BF16), kv, w_xo.astype(BF16), ffn_norm_g, ts=min(512, s))
    assert last, "the final rmsnorm is fused into the last layer's ffn"
    y = _ffn(x2b.reshape(m, d), h2.reshape(m, d), w_gate.astype(BF16), w_up.astype(BF16), ffn_conv_w,
             ffn_conv_b, w_down.astype(BF16), final_g, seq=s, tm=min(1024, s), tf=512)
    return y.reshape(b, s, d)


def kernel(x, mem, mix_norm_g, w_in, gdn_conv_w, gdn_a_log, gdn_dt_bias, gdn_norm_g, moba_norm_g,
           rel_bias, w_out, xattn_norm_g, mem_norm_g, w_xq, w_xkv, w_xo, ffn_norm_g, w_gate, w_up,
           ffn_conv_w, ffn_conv_b, w_down, final_norm_g):
    depth = mix_norm_g.shape[0]
    assert depth == 1
    l = 0
    return _layer(x, mem, mix_norm_g[l], w_in[l], gdn_conv_w[l], gdn_a_log[l], gdn_dt_bias[l],
                  gdn_norm_g[l], moba_norm_g[l], rel_bias, w_out[l], xattn_norm_g[l], mem_norm_g[l],
                  w_xq[l], w_xkv[l], w_xo[l], ffn_norm_g[l], w_gate[l], w_up[l], ffn_conv_w[l],
                  ffn_conv_b[l], w_down[l], final_norm_g, last=True)
```

```python
import functools
import math

import jax
import jax.numpy as jnp
import numpy as np
from jax import lax
from jax.experimental import pallas as pl
from jax.experimental.pallas import tpu as pltpu

HEAD_DIM = 128
N_GDN_HEADS = 8
N_MOBA_HEADS = 8
GDN_WIDTH = N_GDN_HEADS * HEAD_DIM
MOBA_WIDTH = N_MOBA_HEADS * HEAD_DIM
GDN_CONV = 4
GDN_CHUNK = 256
GDN_FILL_PER_STEP = 2
GDN_HALF_ROWS_FROM = 64
MOBA_BLOCK = 256
MOBA_TOPK = 3
MOBA_TILES_IN_FLIGHT = 8
REL_BUCKETS = 32
REL_MAX_DIST = 128
N_XATTN_HEADS = 4
FFN_CONV = 3
EPS = 1e-6
NEG = -1e30
LOG2E = math.log2(math.e)
SUBLANES = 8
BF16_ROWS = 16
ROW_SUB = 256
VMEM_LIMIT = 58 * 1024 * 1024

F32 = jnp.float32
BF16 = jnp.bfloat16


def _bdot(a, b):
    return jnp.dot(a.astype(BF16), b.astype(BF16), preferred_element_type=F32)


def _bdot_nt(a, b):
    return lax.dot_general(a.astype(BF16), b.astype(BF16), (((1,), (1,)), ((), ())),
                           preferred_element_type=F32)


def _bdot_tn(a, b):
    return lax.dot_general(a.astype(BF16), b.astype(BF16), (((0,), (0,)), ((), ())),
                           preferred_element_type=F32)


def _sigmoid(x):
    return 1.0 / (1.0 + jnp.exp(-x))


def _silu(x):
    return x * _sigmoid(x)


def _rms(x, g):
    return x * lax.rsqrt(jnp.mean(x * x, axis=-1, keepdims=True) + EPS) * g


def _norm_matmul_kernel(x_ref, g_ref, w_ref, o_ref, *, rs):
    for r in range(x_ref.shape[0] // rs):
        rows = slice(r * rs, (r + 1) * rs)
        hn = _rms(x_ref[rows, :], g_ref[...]).astype(BF16)
        o_ref[rows, :] = jnp.dot(hn, w_ref[...], preferred_element_type=F32).astype(o_ref.dtype)


def _norm_matmul(x, g, w, *, tm, tn, n=None, out_dtype=F32, row_sub=ROW_SUB):
    m, k = x.shape
    n = w.shape[1] if n is None else n
    assert m % tm == 0 and n % tn == 0 and n <= w.shape[1]
    rs = math.gcd(tm, row_sub)
    return pl.pallas_call(
        functools.partial(_norm_matmul_kernel, rs=rs),
        grid=(m // tm, n // tn),
        in_specs=[pl.BlockSpec((tm, k), lambda i, j: (i, 0)),
                  pl.BlockSpec((1, k), lambda i, j: (0, 0)),
                  pl.BlockSpec((k, tn), lambda i, j: (0, j))],
        out_specs=pl.BlockSpec((tm, tn), lambda i, j: (i, j)),
        out_shape=jax.ShapeDtypeStruct((m, n), out_dtype),
        compiler_params=pltpu.CompilerParams(
            dimension_semantics=("parallel", "arbitrary"), vmem_limit_bytes=VMEM_LIMIT),
        name="norm_matmul",
    )(x, g.reshape(1, k), w)


def _unit_lower_inverse(mats, row, col, fill=lambda: None):
    c = mats[0].shape[0]
    eye = (row == col).astype(F32)
    blk = lambda n: (row // n) == (col // n)
    inner = blk(16)
    ds = [jnp.where(inner, a, 0.0) for a in mats]
    ts = [eye - d for d in ds]
    ps = [_bdot(d, d) for d in ds]
    fill()
    for step in range(3):
        ts = [t + _bdot(t, p) for t, p in zip(ts, ps)]
        fill()
        if step < 2:
            ps = [_bdot(p, p) for p in ps]
            fill()
    half_row = lax.broadcasted_iota(jnp.int32, (c // 2, c), 0)
    half_col = lax.broadcasted_iota(jnp.int32, (c // 2, c), 1)
    n = 32
    while n <= c:
        h = n // 2
        pairs = range(c // n)
        lower = lambda m: jnp.concatenate([m[(2 * k + 1) * h:(2 * k + 2) * h] for k in pairs], axis=0)
        if n < GDN_HALF_ROWS_FROM:
            keep = blk(n) & ~inner
            ys = [lower(_bdot(jnp.where(keep, a, 0.0), t)) for a, t in zip(mats, ts)]
        else:
            left = (half_col // h) == 2 * (half_row // h)
            ys = [_bdot(jnp.where(left, lower(a), 0.0), t) for a, t in zip(mats, ts)]
        fill()
        zero = jnp.zeros((h, c), F32)
        spread = lambda y: jnp.concatenate(
            [part for k in pairs for part in (zero, y[k * h:(k + 1) * h])], axis=0)
        zs = [_bdot(lower(t), spread(y)) for t, y in zip(ts, ys)]
        ts = [jnp.concatenate([part for k in pairs for part in
                               (t[2 * k * h:(2 * k + 1) * h],
                                t[(2 * k + 1) * h:(2 * k + 2) * h] - z[k * h:(k + 1) * h])], axis=0)
              for t, z in zip(ts, zs)]
        fill()
        inner = blk(n)
        n *= 2
    return ts


def _gdn_body(h, alog_ref, dtb_ref, q_ref, k_ref, v_ref, z_ref, bat_ref, wq_ref, wk_ref, wv_ref,
              ng_ref, o_ref, u_ref, wqd_ref, qk_ref, kw_ref, bc_ref, gl_ref, sb_ref, *, group,
              fill=lambda n=1: None):
    seq = q_ref.shape[1]
    C = GDN_CHUNK
    D = HEAD_DIM
    pad = SUBLANES
    n_chunks = seq // C

    static_chunks = n_chunks == group
    if not static_chunks:
        fill = lambda n=1: None
    row = lax.broadcasted_iota(jnp.int32, (C, C), 0)
    col = lax.broadcasted_iota(jnp.int32, (C, C), 1)
    tri_incl = row >= col
    tri_strict = row > col
    lane = lax.broadcasted_iota(jnp.int32, (SUBLANES, C), 1)
    neg_a = -jnp.exp(jnp.full((1, C), alog_ref[h], F32))
    dt_bias = dtb_ref[h]
    scale = D ** -0.5

    def conv_silu(x_ref, w_ref, c, r0):
        if static_chunks:
            win = (jnp.concatenate([jnp.zeros((pad, D), F32), x_ref[0, 0:C, :]], axis=0) if c == 0
                   else x_ref[0, r0 - pad:r0 + C, :])
        else:
            halo = x_ref[0, pl.ds(jnp.maximum(r0 - pad, 0), pad), :]
            win = jnp.concatenate([jnp.where(c > 0, halo, 0.0), x_ref[0, pl.ds(r0, C), :]], axis=0)
        w = w_ref[...]
        prev = pltpu.roll(win, 1, axis=0)
        near = win * w[3:4] + prev * w[2:3]
        far = pltpu.roll(win * w[1:2] + prev * w[0:1], 2, axis=0)
        return _silu(near[pad:pad + C] + far[pad:pad + C])

    def l2n(x):
        return x * lax.rsqrt(jnp.sum(x * x, axis=-1, keepdims=True) + EPS)

    def cumsum_lanes(x):
        x = jnp.broadcast_to(x, (SUBLANES, C))
        s = 1
        while s < C:
            x = x + jnp.where(lane >= s, pltpu.roll(x, s, axis=1), 0.0)
            s *= 2
        return x[0:1, :]

    reps = C // D

    def rows_to_cols(x):
        return jnp.concatenate(
            [jnp.broadcast_to(x[:, n * D:(n + 1) * D], (D, D)).T for n in range(reps)], axis=0)

    def chunk_ids(grp):
        cs = [grp * group + i for i in range(group)]
        return cs, [c * C if static_chunks else pl.multiple_of(c * C, C) for c in cs]

    def for_each_group(body):
        if static_chunks:
            body(0, 0)
        else:
            lax.fori_loop(0, n_chunks // group, body, 0)

    def prepare(grp, carry):
        cs, r0s = chunk_ids(grp)
        pre, gbs, g_rows = [], [], []
        for c, r0 in zip(cs, r0s):
            q = l2n(conv_silu(q_ref, wq_ref, c, r0)) * scale
            k = l2n(conv_silu(k_ref, wk_ref, c, r0))
            v = conv_silu(v_ref, wv_ref, c, r0)
            b_row = bat_ref[0, h, pl.ds(c, 1), :]
            xs = bat_ref[0, h + N_GDN_HEADS, pl.ds(c, 1), :] + dt_bias
            softplus = jnp.maximum(xs, 0.0) + jnp.log1p(jnp.exp(-jnp.abs(xs)))
            g_row = cumsum_lanes(neg_a * softplus)
            pre.append((q, k, v, rows_to_cols(_sigmoid(b_row)), None))
            g_rows.append(g_row)
            gbs.append(rows_to_cols(g_row))
            fill()
        kk_qks = [_bdot_nt(jnp.concatenate([k * beta, q], axis=0), k) for q, k, v, beta, _ in pre]
        mats, mids = [], []
        for (q, k, v, beta, _), gb, g_row, kk_qk in zip(pre, gbs, g_rows, kk_qks):
            g_i = jnp.concatenate([gb] * reps, axis=1)
            g_j = jnp.broadcast_to(g_row, (C, C))
            decay = jnp.exp(jnp.where(tri_incl, g_i - g_j, NEG))
            mats.append(jnp.where(tri_strict, kk_qk[:C] * decay, 0.0))
            mids.append((kk_qk[C:] * decay, jnp.exp(gb), gb[C - 1:C, :]))
        ts = _unit_lower_inverse(mats, row, col, fill=fill)
        uws = [_bdot(t, jnp.concatenate([v * beta, k * beta * eg], axis=1))
               for t, (q, k, v, beta, _), (_, eg, _) in zip(ts, pre, mids)]
        transs = [_bdot_tn(k * jnp.exp(g_last - gb), uw)
                  for uw, (q, k, v, beta, _), gb, (_, _, g_last) in zip(uws, pre, gbs, mids)]
        for c, r0, uw, trans, (q, k, v, beta, _), (qk, eg, g_last) in zip(cs, r0s, uws, transs, pre, mids):
            u_ref[pl.ds(r0, C), :] = uw[:, :D]
            wqd_ref[c, 0:C, :] = uw[:, D:].astype(BF16)
            wqd_ref[c, C:2 * C, :] = (q * eg).astype(BF16)
            qk_ref[pl.ds(r0, C), :] = qk.astype(BF16)
            bc_ref[c] = trans[:, :D]
            kw_ref[c] = trans[:, D:].astype(BF16)
            gl_ref[c] = jnp.exp(g_last)
        return carry

    for_each_group(prepare)

    def chain(c, state):
        sb = state.astype(BF16)
        sb_ref[c] = sb
        return state * gl_ref[c] - jnp.dot(kw_ref[c], sb, preferred_element_type=F32) + bc_ref[c]

    if static_chunks:
        state = jnp.zeros((D, D), F32)
        for c in range(n_chunks):
            state = chain(c, state)
            fill(GDN_FILL_PER_STEP)
    else:
        lax.fori_loop(0, n_chunks, chain, jnp.zeros((D, D), F32))

    def outputs(grp, carry):
        cs, r0s = chunk_ids(grp)
        ws_qs = [jnp.dot(wqd_ref[c], sb_ref[c], preferred_element_type=F32) for c in cs]
        v_news = [(u_ref[pl.ds(r0, C), :] - wq[:C]).astype(BF16) for r0, wq in zip(r0s, ws_qs)]
        os = [wq[C:] + jnp.dot(qk_ref[pl.ds(r0, C), :], vn, preferred_element_type=F32)
              for r0, wq, vn in zip(r0s, ws_qs, v_news)]
        for r0, o in zip(r0s, os):
            z = z_ref[0, pl.ds(r0, C), :]
            o_ref[0, pl.ds(r0, C), :] = (_rms(o, ng_ref[...]) * _silu(z)).astype(o_ref.dtype)
        return carry

    for_each_group(outputs)


def _bucket_upper_bounds():
    n = np.arange(0, 4 * REL_MAX_DIST, dtype=np.int64)
    max_exact = REL_BUCKETS // 2
    nf = np.maximum(n, 1).astype(np.float32)
    large = max_exact + (np.log(nf / np.float32(max_exact)) / np.float32(math.log(REL_MAX_DIST / max_exact))
                         * np.float32(REL_BUCKETS - max_exact)).astype(np.int32)
    large = np.minimum(large, REL_BUCKETS - 1)
    bucket = np.where(n < max_exact, n, large)
    assert np.all(np.diff(bucket) >= 0) and bucket[-1] == REL_BUCKETS - 1
    return [int(np.searchsorted(bucket, b, side="right")) for b in range(REL_BUCKETS - 1)]


_BUCKET_UPPER = _bucket_upper_bounds()


def _moba_steps(h, b, rb_ref, q_ref, k_ref, v_ref, ng_ref, o_ref, vt_ref, km_ref, bd_ref, bl_ref):
    seq = k_ref.shape[1]
    T = MOBA_BLOCK
    nb = seq // T
    scale = HEAD_DIM ** -0.5 * LOG2E
    kk = lax.broadcasted_iota(jnp.int32, (T, T), 0)
    qq = lax.broadcasted_iota(jnp.int32, (T, T), 1)

    @pl.when(b == 0)
    def _():
        def bias_of(n):
            val = jnp.full((T, T), rb_ref[REL_BUCKETS - 1, h], F32)
            for bkt in range(REL_BUCKETS - 2, -1, -1):
                val = jnp.where(n < _BUCKET_UPPER[bkt], rb_ref[bkt, h], val)
            return val
        bd_ref[...] = jnp.where(qq >= kk, bias_of(qq - kk) * LOG2E, NEG)
        bl_ref[...] = bias_of(qq - kk + T) * LOG2E

    for n in range(nb):
        blk_rows = slice(n * T, (n + 1) * T)
        vt_ref[0:HEAD_DIM, blk_rows] = v_ref[0, blk_rows, :].astype(F32).T.astype(BF16)
        km_ref[n:n + 1, :] = jnp.mean(k_ref[0, blk_rows, :].astype(F32), axis=0, keepdims=True)
    vt_ref[HEAD_DIM:HEAD_DIM + BF16_ROWS, :] = jnp.ones((BF16_ROWS, seq), BF16)
    far_bias = rb_ref[REL_BUCKETS - 1, h] * LOG2E

    def additive_mask(i, qf):
        if i <= MOBA_TOPK:
            return None
        gate = lax.dot_general(km_ref[0:i, :], qf, (((1,), (1,)), ((), ())),
                               preferred_element_type=F32, precision=lax.Precision.HIGHEST)
        blk = lax.broadcasted_iota(jnp.int32, (i, T), 0)
        rank = jnp.zeros((i, T), jnp.int32)
        for m in range(i):
            gm = gate[m:m + 1, :]
            rank = rank + jnp.where((gm > gate) | ((gm == gate) & (m < blk)), 1, 0)
        return jnp.where(rank < MOBA_TOPK, 0.0, NEG)

    def biased(i, s_all, keep):
        parts, maxes = [], []
        for j in range(i + 1):
            sj = s_all[j * T:(j + 1) * T, :]
            if j >= i - 1:
                sj = sj + (bd_ref[...] if j == i else bl_ref[...])
                row = keep[j:j + 1, :] if (keep is not None and j < i) else 0.0
            else:
                row = far_bias if keep is None else keep[j:j + 1, :] + far_bias
            parts.append((sj, row))
            maxes.append(jnp.max(sj, axis=0, keepdims=True) + row)
        return parts, functools.reduce(jnp.maximum, maxes)

    order = [t for pair in zip(range(nb // 2), range(nb - 1, nb // 2 - 1, -1)) for t in pair]
    if nb % 2:
        order.append(nb // 2)
    for g0 in range(0, nb, MOBA_TILES_IN_FLIGHT):
        tiles = order[g0:g0 + MOBA_TILES_IN_FLIGHT]
        qfs = [q_ref[0, i * T:(i + 1) * T, :].astype(F32) for i in tiles]
        keeps, s_alls, partss, m_rows, p_alls, o_augs = [], [], [], [], [], []
        for i, qf in zip(tiles, qfs):
            keeps.append(additive_mask(i, qf))
            s_alls.append(_bdot_nt(k_ref[0, 0:(i + 1) * T, :], qf * scale))
            yield
        for i, s_all, keep in zip(tiles, s_alls, keeps):
            parts, m_row = biased(i, s_all, keep)
            partss.append(parts)
            m_rows.append(m_row)
            yield
        for parts, m_row in zip(partss, m_rows):
            p_alls.append(jnp.concatenate(
                [jnp.exp2((sj + (row - m_row)).astype(BF16)) for sj, row in parts], axis=0))
            yield
        for i, p_all in zip(tiles, p_alls):
            o_augs.append(jnp.dot(vt_ref[:, 0:(i + 1) * T], p_all, preferred_element_type=F32))
            yield
        for i, o in zip(tiles, o_augs):
            o_t = o[0:HEAD_DIM] / o[HEAD_DIM:HEAD_DIM + 1]
            o_ref[0, i * T:(i + 1) * T, :] = _rms(o_t.T, ng_ref[...]).astype(o_ref.dtype)
            yield


def _mixer_kernel(alog_ref, dtb_ref, rb_ref, q_ref, k_ref, v_ref, z_ref, bat_ref, wq_ref, wk_ref, wv_ref,
                  gng_ref, mq_ref, mk_ref, mv_ref, mng_ref, oa_ref, ob_ref,
                  u_ref, wqd_ref, qk_ref, kw_ref, bc_ref, gl_ref, sb_ref, vt_ref, km_ref, bd_ref, bl_ref,
                  *, group):
    h = pl.program_id(0)
    b = pl.program_id(1)
    moba = _moba_steps(h, b, rb_ref, mq_ref, mk_ref, mv_ref, mng_ref, ob_ref, vt_ref, km_ref, bd_ref, bl_ref)

    def fill(n=1):
        for _ in range(n):
            next(moba, None)

    _gdn_body(h, alog_ref, dtb_ref, q_ref, k_ref, v_ref, z_ref, bat_ref, wq_ref, wk_ref, wv_ref,
              gng_ref, oa_ref, u_ref, wqd_ref, qk_ref, kw_ref, bc_ref, gl_ref, sb_ref, group=group, fill=fill)
    for _ in moba:
        pass


def _mixer(pa, pb, conv_w, a_log, dt_bias, gdn_norm_g, rel_bias, moba_norm_g, *, group=8):
    b, s, _ = pa.shape
    H = N_GDN_HEADS
    assert N_MOBA_HEADS == H
    C, T = GDN_CHUNK, MOBA_BLOCK
    assert s % C == 0 and s % T == 0
    n_chunks, nb = s // C, s // T
    bat = pa[:, :, 4 * GDN_WIDTH:4 * GDN_WIDTH + 2 * H].transpose(0, 2, 1).reshape(b, 2 * H, n_chunks, C)
    head = lambda off: pl.BlockSpec((1, s, HEAD_DIM), lambda h, i: (i, 0, off + h))
    cw = lambda off: pl.BlockSpec((GDN_CONV, HEAD_DIM), lambda h, i: (0, off + h))
    vec = pl.BlockSpec((1, HEAD_DIM), lambda h, i: (0, 0))
    smem = pl.BlockSpec(memory_space=pltpu.SMEM)
    out = pl.BlockSpec((1, s, HEAD_DIM), lambda h, i: (i, 0, h))
    return pl.pallas_call(
        functools.partial(_mixer_kernel, group=math.gcd(group, n_chunks)),
        grid=(H, b),
        in_specs=[smem, smem, smem, head(0), head(H), head(2 * H), head(3 * H),
                  pl.BlockSpec((1, 2 * H, n_chunks, C), lambda h, i: (i, 0, 0, 0)),
                  cw(0), cw(H), cw(2 * H), vec,
                  head(0), head(H), head(2 * H), vec],
        out_specs=[out, out],
        out_shape=[jax.ShapeDtypeStruct((b, s, GDN_WIDTH), BF16), jax.ShapeDtypeStruct((b, s, MOBA_WIDTH), BF16)],
        scratch_shapes=[
            pltpu.VMEM((s, HEAD_DIM), F32),
            pltpu.VMEM((n_chunks, 2 * C, HEAD_DIM), BF16),
            pltpu.VMEM((s, C), BF16),
            pltpu.VMEM((n_chunks, HEAD_DIM, HEAD_DIM), BF16),
            pltpu.VMEM((n_chunks, HEAD_DIM, HEAD_DIM), F32),
            pltpu.VMEM((n_chunks, 1, HEAD_DIM), F32),
            pltpu.VMEM((n_chunks, HEAD_DIM, HEAD_DIM), BF16),
            pltpu.VMEM((HEAD_DIM + BF16_ROWS, s), BF16),
            pltpu.VMEM((nb, HEAD_DIM), F32),
            pltpu.VMEM((T, T), F32), pltpu.VMEM((T, T), F32)],
        compiler_params=pltpu.CompilerParams(
            dimension_semantics=("arbitrary", "arbitrary"), vmem_limit_bytes=VMEM_LIMIT),
        name="mixer",
    )(a_log, dt_bias, rel_bias, pa, pa, pa, pa, bat, conv_w, conv_w, conv_w, gdn_norm_g.reshape(1, HEAD_DIM),
      pb, pb, pb, moba_norm_g.reshape(1, HEAD_DIM))


def _xattn_kernel(x_ref, oa_ref, ob_ref, wa_ref, wb_ref, g_ref, wq_ref, kv_ref, wo_ref, ng_ref,
                  o_ref, hn_ref, *, rs):
    width = N_XATTN_HEADS * HEAD_DIM
    scale = HEAD_DIM ** -0.5 * LOG2E
    blocks = [slice(r * rs, (r + 1) * rs) for r in range(x_ref.shape[1] // rs)]
    x1s = [x_ref[0, rows, :]
           + jnp.dot(oa_ref[0, rows, :], wa_ref[...], preferred_element_type=F32)
           + jnp.dot(ob_ref[0, rows, :], wb_ref[...], preferred_element_type=F32) for rows in blocks]
    qs = [_bdot(_rms(x1, g_ref[...]), wq_ref[...]) * scale for x1 in x1s]
    outs = [[] for _ in blocks]
    for hd in range(N_XATTN_HEADS):
        sl = slice(hd * HEAD_DIM, (hd + 1) * HEAD_DIM)
        ss = [_bdot_nt(q[:, sl], kv_ref[0, :, sl]) for q in qs]
        ps = [jnp.exp2(s - jnp.max(s, axis=-1, keepdims=True)) for s in ss]
        for out, p in zip(outs, ps):
            l = jnp.sum(p, axis=-1, keepdims=True)
            out.append(_bdot(p, kv_ref[0, :, width + hd * HEAD_DIM: width + (hd + 1) * HEAD_DIM]) / l)
    ys = [x1 + _bdot(jnp.concatenate(out, axis=-1), wo_ref[...]) for x1, out in zip(x1s, outs)]
    for rows, y in zip(blocks, ys):
        o_ref[0, rows, :] = y
        hn_ref[0, rows, :] = _rms(y, ng_ref[...]).astype(BF16)


def _xattn(x, oa, ob, wa, wb, g, wq, kv, wo, next_g, *, ts):
    b, s, d = x.shape
    mlen = kv.shape[1]
    width = wq.shape[1]
    ka, kb = oa.shape[2], ob.shape[2]
    tile = pl.BlockSpec((1, ts, d), lambda i, j: (i, j, 0))
    vec = pl.BlockSpec((1, d), lambda i, j: (0, 0))
    whole = lambda r, c: pl.BlockSpec((r, c), lambda i, j: (0, 0))
    return pl.pallas_call(
        functools.partial(_xattn_kernel, rs=math.gcd(ts, 2 * ROW_SUB)),
        grid=(b, s // ts),
        in_specs=[tile,
                  pl.BlockSpec((1, ts, ka), lambda i, j: (i, j, 0)),
                  pl.BlockSpec((1, ts, kb), lambda i, j: (i, j, 0)),
                  whole(ka, d), whole(kb, d), vec, whole(d, width),
                  pl.BlockSpec((1, mlen, 2 * width), lambda i, j: (i, 0, 0)),
                  whole(width, d), vec],
        out_specs=[tile, tile],
        out_shape=[jax.ShapeDtypeStruct((b, s, d), F32), jax.ShapeDtypeStruct((b, s, d), BF16)],
        compiler_params=pltpu.CompilerParams(
            dimension_semantics=("parallel", "parallel"), vmem_limit_bytes=VMEM_LIMIT),
        name="xattn",
    )(x, oa, ob, wa, wb, g.reshape(1, d), wq, kv, wo, next_g.reshape(1, d))


def _ffn_kernel(x_hbm, h_ref, halo_ref, wg_ref, wu_ref, cw_ref, cb_ref, wd_ref, fg_ref, o_ref,
                hx_ref, xres_ref, sem, *, tiles_per_seq, rs):
    i = pl.program_id(0)
    f = pl.program_id(1)
    tm = h_ref.shape[0]
    pad = BF16_ROWS

    def x_copy():
        return pltpu.make_async_copy(x_hbm.at[pl.ds(i * tm, tm), :], xres_ref, sem)

    @pl.when(f == 0)
    def _():
        x_copy().start()
        halo = halo_ref[...]
        hx_ref[0:pad, :] = jnp.where(i % tiles_per_seq == 0, jnp.zeros_like(halo), halo)
        hx_ref[pad:pad + tm, :] = h_ref[...]
        o_ref[...] = jnp.zeros_like(o_ref)

    cw = cw_ref[...]
    keep = SUBLANES
    tail = None
    for r in range(tm // rs):
        rows = slice(r * rs, (r + 1) * rs)
        hrows = slice(pad + r * rs, pad + (r + 1) * rs)
        if r == 0:
            gp = jnp.dot(hx_ref[0:pad + rs, :], wg_ref[...], preferred_element_type=F32)[pad - keep:]
        else:
            gp = jnp.concatenate(
                [tail, jnp.dot(hx_ref[hrows, :], wg_ref[...], preferred_element_type=F32)], axis=0)
        tail = gp[rs:rs + keep]
        up = jnp.dot(hx_ref[hrows, :], wu_ref[...], preferred_element_type=F32)
        gate = gp[keep:keep + rs] * cw[2:3] + gp[keep - 1:keep - 1 + rs] * cw[1:2] \
            + gp[keep - 2:keep - 2 + rs] * cw[0:1] + cb_ref[...]
        act = _silu(gate) * up
        o_ref[rows, :] += jnp.dot(act.astype(BF16), wd_ref[...], preferred_element_type=F32)

    @pl.when(f == pl.num_programs(1) - 1)
    def _():
        x_copy().wait()

        def finish(r, carry):
            rows = pl.ds(pl.multiple_of(r * HEAD_DIM, HEAD_DIM), HEAD_DIM)
            o_ref[rows, :] = _rms(xres_ref[rows, :] + o_ref[rows, :], fg_ref[...])
            return carry

        lax.fori_loop(0, tm // HEAD_DIM, finish, 0)


def _ffn(x, h, wg, wu, cw, cb, wd, fg, *, seq, tm, tf):
    m, d = x.shape
    ff = wg.shape[1]
    assert seq % tm == 0 and ff % tf == 0 and tm % BF16_ROWS == 0
    hb = tm // BF16_ROWS
    return pl.pallas_call(
        functools.partial(_ffn_kernel, tiles_per_seq=seq // tm, rs=tm),
        grid=(m // tm, ff // tf),
        in_specs=[pl.BlockSpec(memory_space=pl.ANY),
                  pl.BlockSpec((tm, d), lambda i, f: (i, 0)),
                  pl.BlockSpec((BF16_ROWS, d), lambda i, f: (jnp.maximum(i * hb - 1, 0), 0)),
                  pl.BlockSpec((d, tf), lambda i, f: (0, f)),
                  pl.BlockSpec((d, tf), lambda i, f: (0, f)),
                  pl.BlockSpec((FFN_CONV, tf), lambda i, f: (0, f)),
                  pl.BlockSpec((1, tf), lambda i, f: (0, f)),
                  pl.BlockSpec((tf, d), lambda i, f: (f, 0)),
                  pl.BlockSpec((1, d), lambda i, f: (0, 0))],
        out_specs=pl.BlockSpec((tm, d), lambda i, f: (i, 0)),
        out_shape=jax.ShapeDtypeStruct((m, d), F32),
        scratch_shapes=[pltpu.VMEM((tm + BF16_ROWS, d), BF16), pltpu.VMEM((tm, d), F32),
                        pltpu.SemaphoreType.DMA(())],
        compiler_params=pltpu.CompilerParams(
            dimension_semantics=("arbitrary", "arbitrary"), vmem_limit_bytes=VMEM_LIMIT),
        name="ffn",
    )(x, h, h, wg, wu, cw, cb.reshape(1, ff), wd, fg.reshape(1, d))


def _layer(x, mem, mix_norm_g, w_in, gdn_conv_w, a_log, dt_bias, gdn_norm_g, moba_norm_g, rel_bias,
           w_out, xattn_norm_g, mem_norm_g, w_xq, w_xkv, w_xo, ffn_norm_g, w_gate, w_up, ffn_conv_w,
           ffn_conv_b, w_down, final_g, *, last):
    b, s, d = x.shape
    m = b * s
    x2 = x.reshape(m, d)
    i1 = 4 * GDN_WIDTH
    i3 = i1 + 2 * N_GDN_HEADS
    w_all = w_in.astype(BF16)
    w_b = w_all[:, i3:]
    tm = min(1024, m)
    wa_cols = i1 + HEAD_DIM
    pa = _norm_matmul(x2, mix_norm_g, w_all, tm=min(256, m), tn=wa_cols, n=wa_cols,
                      row_sub=ROW_SUB // 2).reshape(b, s, wa_cols)
    pb = _norm_matmul(x2, mix_norm_g, w_b, tm=min(512, m), tn=3 * MOBA_WIDTH,
                      out_dtype=BF16).reshape(b, s, 3 * MOBA_WIDTH)
    o_a, o_b = _mixer(pa, pb, gdn_conv_w, a_log, dt_bias, gdn_norm_g, rel_bias, moba_norm_g)
    mlen = mem.shape[1]
    kv = _norm_matmul(mem.reshape(b * mlen, d), mem_norm_g, w_xkv.astype(BF16),
                      tm=min(512, b * mlen), tn=1024, out_dtype=BF16).reshape(b, mlen, -1)
    x2b, h2 = _xattn(x, o_a, o_b, w_out[:GDN_WIDTH].astype(BF16), w_out[GDN_WIDTH:].astype(BF16),
                     xattn_norm_g, w_xq.astype(BF16), kv, w_xo.astype(BF16), ffn_norm_g, ts=min(512, s))
    assert last, "the final rmsnorm is fused into the last layer's ffn"
    y = _ffn(x2b.reshape(m, d), h2.reshape(m, d), w_gate.astype(BF16), w_up.astype(BF16), ffn_conv_w,
             ffn_conv_b, w_down.astype(BF16), final_g, seq=s, tm=min(1024, s), tf=512)
    return y.reshape(b, s, d)


def kernel(x, mem, mix_norm_g, w_in, gdn_conv_w, gdn_a_log, gdn_dt_bias, gdn_norm_g, moba_norm_g,
           rel_bias, w_out, xattn_norm_g, mem_norm_g, w_xq, w_xkv, w_xo, ffn_norm_g, w_gate, w_up,
           ffn_conv_w, ffn_conv_b, w_down, final_norm_g):
    depth = mix_norm_g.shape[0]
    assert depth == 1
    l = 0
    return _layer(x, mem, mix_norm_g[l], w_in[l], gdn_conv_w[l], gdn_a_log[l], gdn_dt_bias[l],
                  gdn_norm_g[l], moba_norm_g[l], rel_bias, w_out[l], xattn_norm_g[l], mem_norm_g[l],
                  w_xq[l], w_xkv[l], w_xo[l], ffn_norm_g[l], w_gate[l], w_up[l], ffn_conv_w[l],
                  ffn_conv_b[l], w_down[l], final_norm_g, last=True)
```

```python
import functools
import math

import jax
import jax.numpy as jnp
import numpy as np
from jax import lax
from jax.experimental import pallas as pl
from jax.experimental.pallas import tpu as pltpu

HEAD_DIM = 128
N_GDN_HEADS = 8
N_MOBA_HEADS = 8
GDN_WIDTH = N_GDN_HEADS * HEAD_DIM
MOBA_WIDTH = N_MOBA_HEADS * HEAD_DIM
GDN_CONV = 4
GDN_CHUNK = 256
GDN_FILL_PER_STEP = 2
GDN_HALF_ROWS_FROM = 64
MOBA_BLOCK = 256
MOBA_TOPK = 3
MOBA_TILES_IN_FLIGHT = 8
REL_BUCKETS = 32
REL_MAX_DIST = 128
N_XATTN_HEADS = 4
FFN_CONV = 3
EPS = 1e-6
NEG = -1e30
LOG2E = math.log2(math.e)
SUBLANES = 8
BF16_ROWS = 16
ROW_SUB = 256
VMEM_LIMIT = 58 * 1024 * 1024

F32 = jnp.float32
BF16 = jnp.bfloat16


def _bdot(a, b):
    return jnp.dot(a.astype(BF16), b.astype(BF16), preferred_element_type=F32)


def _bdot_nt(a, b):
    return lax.dot_general(a.astype(BF16), b.astype(BF16), (((1,), (1,)), ((), ())),
                           preferred_element_type=F32)


def _bdot_tn(a, b):
    return lax.dot_general(a.astype(BF16), b.astype(BF16), (((0,), (0,)), ((), ())),
                           preferred_element_type=F32)


def _sigmoid(x):
    return 1.0 / (1.0 + jnp.exp(-x))


def _silu(x):
    return x * _sigmoid(x)


def _rms(x, g):
    return x * lax.rsqrt(jnp.mean(x * x, axis=-1, keepdims=True) + EPS) * g


def _norm_matmul_kernel(x_ref, g_ref, w_ref, o_ref, *, rs):
    for r in range(x_ref.shape[0] // rs):
        rows = slice(r * rs, (r + 1) * rs)
        hn = _rms(x_ref[rows, :], g_ref[...]).astype(BF16)
        o_ref[rows, :] = jnp.dot(hn, w_ref[...], preferred_element_type=F32).astype(o_ref.dtype)


def _norm_matmul(x, g, w, *, tm, tn, n=None, out_dtype=F32, row_sub=ROW_SUB):
    m, k = x.shape
    n = w.shape[1] if n is None else n
    assert m % tm == 0 and n % tn == 0 and n <= w.shape[1]
    rs = math.gcd(tm, row_sub)
    return pl.pallas_call(
        functools.partial(_norm_matmul_kernel, rs=rs),
        grid=(m // tm, n // tn),
        in_specs=[pl.BlockSpec((tm, k), lambda i, j: (i, 0)),
                  pl.BlockSpec((1, k), lambda i, j: (0, 0)),
                  pl.BlockSpec((k, tn), lambda i, j: (0, j))],
        out_specs=pl.BlockSpec((tm, tn), lambda i, j: (i, j)),
        out_shape=jax.ShapeDtypeStruct((m, n), out_dtype),
        compiler_params=pltpu.CompilerParams(
            dimension_semantics=("parallel", "arbitrary"), vmem_limit_bytes=VMEM_LIMIT),
        name="norm_matmul",
    )(x, g.reshape(1, k), w)


def _unit_lower_inverse(mats, row, col, fill=lambda: None):
    c = mats[0].shape[0]
    eye = (row == col).astype(F32)
    blk = lambda n: (row // n) == (col // n)
    inner = blk(16)
    ds = [jnp.where(inner, a, 0.0) for a in mats]
    ts = [eye - d for d in ds]
    ps = [_bdot(d, d) for d in ds]
    fill()
    for step in range(3):
        ts = [t + _bdot(t, p) for t, p in zip(ts, ps)]
        fill()
        if step < 2:
            ps = [_bdot(p, p) for p in ps]
            fill()
    half_row = lax.broadcasted_iota(jnp.int32, (c // 2, c), 0)
    half_col = lax.broadcasted_iota(jnp.int32, (c // 2, c), 1)
    n = 32
    while n <= c:
        h = n // 2
        pairs = range(c // n)
        lower = lambda m: jnp.concatenate([m[(2 * k + 1) * h:(2 * k + 2) * h] for k in pairs], axis=0)
        if n < GDN_HALF_ROWS_FROM:
            keep = blk(n) & ~inner
            ys = [lower(_bdot(jnp.where(keep, a, 0.0), t)) for a, t in zip(mats, ts)]
        else:
            left = (half_col // h) == 2 * (half_row // h)
            ys = [_bdot(jnp.where(left, lower(a), 0.0), t) for a, t in zip(mats, ts)]
        fill()
        zero = jnp.zeros((h, c), F32)
        spread = lambda y: jnp.concatenate(
            [part for k in pairs for part in (zero, y[k * h:(k + 1) * h])], axis=0)
        zs = [_bdot(lower(t), spread(y)) for t, y in zip(ts, ys)]
        ts = [jnp.concatenate([part for k in pairs for part in
                               (t[2 * k * h:(2 * k + 1) * h],
                                t[(2 * k + 1) * h:(2 * k + 2) * h] - z[k * h:(k + 1) * h])], axis=0)
              for t, z in zip(ts, zs)]
        fill()
        inner = blk(n)
        n *= 2
    return ts


def _gdn_body(h, alog_ref, dtb_ref, q_ref, k_ref, v_ref, z_ref, bat_ref, wq_ref, wk_ref, wv_ref,
              ng_ref, o_ref, u_ref, wqd_ref, qk_ref, kw_ref, bc_ref, gl_ref, sb_ref, *, group,
              fill=lambda n=1: None):
    seq = q_ref.shape[1]
    C = GDN_CHUNK
    D = HEAD_DIM
    pad = SUBLANES
    n_chunks = seq // C

    static_chunks = n_chunks == group
    if not static_chunks:
        fill = lambda n=1: None
    row = lax.broadcasted_iota(jnp.int32, (C, C), 0)
    col = lax.broadcasted_iota(jnp.int32, (C, C), 1)
    tri_incl = row >= col
    tri_strict = row > col
    lane = lax.broadcasted_iota(jnp.int32, (SUBLANES, C), 1)
    neg_a = -jnp.exp(jnp.full((1, C), alog_ref[h], F32))
    dt_bias = dtb_ref[h]
    scale = D ** -0.5

    def conv_silu(x_ref, w_ref, c, r0):
        if static_chunks:
            win = (jnp.concatenate([jnp.zeros((pad, D), F32), x_ref[0, 0:C, :]], axis=0) if c == 0
                   else x_ref[0, r0 - pad:r0 + C, :])
        else:
            halo = x_ref[0, pl.ds(jnp.maximum(r0 - pad, 0), pad), :]
            win = jnp.concatenate([jnp.where(c > 0, halo, 0.0), x_ref[0, pl.ds(r0, C), :]], axis=0)
        w = w_ref[...]
        prev = pltpu.roll(win, 1, axis=0)
        near = win * w[3:4] + prev * w[2:3]
        far = pltpu.roll(win * w[1:2] + prev * w[0:1], 2, axis=0)
        return _silu(near[pad:pad + C] + far[pad:pad + C])

    def l2n(x):
        return x * lax.rsqrt(jnp.sum(x * x, axis=-1, keepdims=True) + EPS)

    def cumsum_lanes(x):
        x = jnp.broadcast_to(x, (SUBLANES, C))
        s = 1
        while s < C:
            x = x + jnp.where(lane >= s, pltpu.roll(x, s, axis=1), 0.0)
            s *= 2
        return x[0:1, :]

    reps = C // D

    def rows_to_cols(x):
        return jnp.concatenate(
            [jnp.broadcast_to(x[:, n * D:(n + 1) * D], (D, D)).T for n in range(reps)], axis=0)

    def chunk_ids(grp):
        cs = [grp * group + i for i in range(group)]
        return cs, [c * C if static_chunks else pl.multiple_of(c * C, C) for c in cs]

    def for_each_group(body):
        if static_chunks:
            body(0, 0)
        else:
            lax.fori_loop(0, n_chunks // group, body, 0)

    def prepare(grp, carry):
        cs, r0s = chunk_ids(grp)
        pre, gbs, g_rows = [], [], []
        for c, r0 in zip(cs, r0s):
            q = l2n(conv_silu(q_ref, wq_ref, c, r0)) * scale
            k = l2n(conv_silu(k_ref, wk_ref, c, r0))
            v = conv_silu(v_ref, wv_ref, c, r0)
            b_row = bat_ref[0, h, pl.ds(c, 1), :]
            xs = bat_ref[0, h + N_GDN_HEADS, pl.ds(c, 1), :] + dt_bias
            softplus = jnp.maximum(xs, 0.0) + jnp.log1p(jnp.exp(-jnp.abs(xs)))
            g_row = cumsum_lanes(neg_a * softplus)
            pre.append((q, k, v, rows_to_cols(_sigmoid(b_row)), None))
            g_rows.append(g_row)
            gbs.append(rows_to_cols(g_row))
            fill()
        kk_qks = [_bdot_nt(jnp.concatenate([k * beta, q], axis=0), k) for q, k, v, beta, _ in pre]
        mats, mids = [], []
        for (q, k, v, beta, _), gb, g_row, kk_qk in zip(pre, gbs, g_rows, kk_qks):
            g_i = jnp.concatenate([gb] * reps, axis=1)
            g_j = jnp.broadcast_to(g_row, (C, C))
            decay = jnp.exp(jnp.where(tri_incl, g_i - g_j, NEG))
            mats.append(jnp.where(tri_strict, kk_qk[:C] * decay, 0.0))
            mids.append((kk_qk[C:] * decay, jnp.exp(gb), gb[C - 1:C, :]))
        ts = _unit_lower_inverse(mats, row, col, fill=fill)
        uws = [_bdot(t, jnp.concatenate([v * beta, k * beta * eg], axis=1))
               for t, (q, k, v, beta, _), (_, eg, _) in zip(ts, pre, mids)]
        transs = [_bdot_tn(k * jnp.exp(g_last - gb), uw)
                  for uw, (q, k, v, beta, _), gb, (_, _, g_last) in zip(uws, pre, gbs, mids)]
        for c, r0, uw, trans, (q, k, v, beta, _), (qk, eg, g_last) in zip(cs, r0s, uws, transs, pre, mids):
            u_ref[pl.ds(r0, C), :] = uw[:, :D]
            wqd_ref[c, 0:C, :] = uw[:, D:].astype(BF16)
            wqd_ref[c, C:2 * C, :] = (q * eg).astype(BF16)
            qk_ref[pl.ds(r0, C), :] = qk.astype(BF16)
            bc_ref[c] = trans[:, :D]
            kw_ref[c] = trans[:, D:].astype(BF16)
            gl_ref[c] = jnp.exp(g_last)
        return carry

    for_each_group(prepare)

    def chain(c, state):
        sb = state.astype(BF16)
        sb_ref[c] = sb
        return state * gl_ref[c] - jnp.dot(kw_ref[c], sb, preferred_element_type=F32) + bc_ref[c]

    if static_chunks:
        state = jnp.zeros((D, D), F32)
        for c in range(n_chunks):
            state = chain(c, state)
            fill(GDN_FILL_PER_STEP)
    else:
        lax.fori_loop(0, n_chunks, chain, jnp.zeros((D, D), F32))

    def outputs(grp, carry):
        cs, r0s = chunk_ids(grp)
        ws_qs = [jnp.dot(wqd_ref[c], sb_ref[c], preferred_element_type=F32) for c in cs]
        v_news = [(u_ref[pl.ds(r0, C), :] - wq[:C]).astype(BF16) for r0, wq in zip(r0s, ws_qs)]
        os = [wq[C:] + jnp.dot(qk_ref[pl.ds(r0, C), :], vn, preferred_element_type=F32)
              for r0, wq, vn in zip(r0s, ws_qs, v_news)]
        for r0, o in zip(r0s, os):
            z = z_ref[0, pl.ds(r0, C), :]
            o_ref[0, pl.ds(r0, C), :] = (_rms(o, ng_ref[...]) * _silu(z)).astype(o_ref.dtype)
        return carry

    for_each_group(outputs)


def _bucket_upper_bounds():
    n = np.arange(0, 4 * REL_MAX_DIST, dtype=np.int64)
    max_exact = REL_BUCKETS // 2
    nf = np.maximum(n, 1).astype(np.float32)
    large = max_exact + (np.log(nf / np.float32(max_exact)) / np.float32(math.log(REL_MAX_DIST / max_exact))
                         * np.float32(REL_BUCKETS - max_exact)).astype(np.int32)
    large = np.minimum(large, REL_BUCKETS - 1)
    bucket = np.where(n < max_exact, n, large)
    assert np.all(np.diff(bucket) >= 0) and bucket[-1] == REL_BUCKETS - 1
    return [int(np.searchsorted(bucket, b, side="right")) for b in range(REL_BUCKETS - 1)]


_BUCKET_UPPER = _bucket_upper_bounds()


def _moba_steps(h, b, rb_ref, q_ref, k_ref, v_ref, ng_ref, o_ref, vt_ref, km_ref, bd_ref, bl_ref):
    seq = k_ref.shape[1]
    T = MOBA_BLOCK
    nb = seq // T
    scale = HEAD_DIM ** -0.5 * LOG2E
    kk = lax.broadcasted_iota(jnp.int32, (T, T), 0)
    qq = lax.broadcasted_iota(jnp.int32, (T, T), 1)

    @pl.when(b == 0)
    def _():
        def bias_of(n):
            val = jnp.full((T, T), rb_ref[REL_BUCKETS - 1, h], F32)
            for bkt in range(REL_BUCKETS - 2, -1, -1):
                val = jnp.where(n < _BUCKET_UPPER[bkt], rb_ref[bkt, h], val)
            return val
        bd_ref[...] = jnp.where(qq >= kk, bias_of(qq - kk) * LOG2E, NEG)
        bl_ref[...] = bias_of(qq - kk + T) * LOG2E

    for n in range(nb):
        blk_rows = slice(n * T, (n + 1) * T)
        vt_ref[0:HEAD_DIM, blk_rows] = v_ref[0, blk_rows, :].astype(F32).T.astype(BF16)
        km_ref[n:n + 1, :] = jnp.mean(k_ref[0, blk_rows, :].astype(F32), axis=0, keepdims=True)
    vt_ref[HEAD_DIM:HEAD_DIM + BF16_ROWS, :] = jnp.ones((BF16_ROWS, seq), BF16)
    far_bias = rb_ref[REL_BUCKETS - 1, h] * LOG2E

    def additive_mask(i, qf):
        if i <= MOBA_TOPK:
            return None
        gate = lax.dot_general(km_ref[0:i, :], qf, (((1,), (1,)), ((), ())),
                               preferred_element_type=F32, precision=lax.Precision.HIGHEST)
        blk = lax.broadcasted_iota(jnp.int32, (i, T), 0)
        rank = jnp.zeros((i, T), jnp.int32)
        for m in range(i):
            gm = gate[m:m + 1, :]
            rank = rank + jnp.where((gm > gate) | ((gm == gate) & (m < blk)), 1, 0)
        return jnp.where(rank < MOBA_TOPK, 0.0, NEG)

    def biased(i, s_all, keep):
        parts, maxes = [], []
        for j in range(i + 1):
            sj = s_all[j * T:(j + 1) * T, :]
            if j >= i - 1:
                sj = sj + (bd_ref[...] if j == i else bl_ref[...])
                row = keep[j:j + 1, :] if (keep is not None and j < i) else 0.0
            else:
                row = far_bias if keep is None else keep[j:j + 1, :] + far_bias
            parts.append((sj, row))
            maxes.append(jnp.max(sj, axis=0, keepdims=True) + row)
        return parts, functools.reduce(jnp.maximum, maxes)

    order = [t for pair in zip(range(nb // 2), range(nb - 1, nb // 2 - 1, -1)) for t in pair]
    if nb % 2:
        order.append(nb // 2)
    for g0 in range(0, nb, MOBA_TILES_IN_FLIGHT):
        tiles = order[g0:g0 + MOBA_TILES_IN_FLIGHT]
        qfs = [q_ref[0, i * T:(i + 1) * T, :].astype(F32) for i in tiles]
        keeps, s_alls, partss, m_rows, p_alls, o_augs = [], [], [], [], [], []
        for i, qf in zip(tiles, qfs):
            keeps.append(additive_mask(i, qf))
            s_alls.append(_bdot_nt(k_ref[0, 0:(i + 1) * T, :], qf * scale))
            yield
        for i, s_all, keep in zip(tiles, s_alls, keeps):
            parts, m_row = biased(i, s_all, keep)
            partss.append(parts)
            m_rows.append(m_row)
            yield
        for parts, m_row in zip(partss, m_rows):
            p_alls.append(jnp.concatenate(
                [jnp.exp2((sj + (row - m_row)).astype(BF16)) for sj, row in parts], axis=0))
            yield
        for i, p_all in zip(tiles, p_alls):
            o_augs.append(jnp.dot(vt_ref[:, 0:(i + 1) * T], p_all, preferred_element_type=F32))
            yield
        for i, o in zip(tiles, o_augs):
            o_t = o[0:HEAD_DIM] / o[HEAD_DIM:HEAD_DIM + 1]
            o_ref[0, i * T:(i + 1) * T, :] = _rms(o_t.T, ng_ref[...]).astype(o_ref.dtype)
            yield


def _mixer_kernel(alog_ref, dtb_ref, rb_ref, q_ref, k_ref, v_ref, z_ref, bat_ref, wq_ref, wk_ref, wv_ref,
                  gng_ref, mq_ref, mk_ref, mv_ref, mng_ref, oa_ref, ob_ref,
                  u_ref, wqd_ref, qk_ref, kw_ref, bc_ref, gl_ref, sb_ref, vt_ref, km_ref, bd_ref, bl_ref,
                  *, group):
    h = pl.program_id(0)
    b = pl.program_id(1)
    moba = _moba_steps(h, b, rb_ref, mq_ref, mk_ref, mv_ref, mng_ref, ob_ref, vt_ref, km_ref, bd_ref, bl_ref)

    def fill(n=1):
        for _ in range(n):
            next(moba, None)

    _gdn_body(h, alog_ref, dtb_ref, q_ref, k_ref, v_ref, z_ref, bat_ref, wq_ref, wk_ref, wv_ref,
              gng_ref, oa_ref, u_ref, wqd_ref, qk_ref, kw_ref, bc_ref, gl_ref, sb_ref, group=group, fill=fill)
    for _ in moba:
        pass


def _mixer(pa, pb, conv_w, a_log, dt_bias, gdn_norm_g, rel_bias, moba_norm_g, *, group=8):
    b, s, _ = pa.shape
    H = N_GDN_HEADS
    assert N_MOBA_HEADS == H
    C, T = GDN_CHUNK, MOBA_BLOCK
    assert s % C == 0 and s % T == 0
    n_chunks, nb = s // C, s // T
    bat = pa[:, :, 4 * GDN_WIDTH:4 * GDN_WIDTH + 2 * H].transpose(0, 2, 1).reshape(b, 2 * H, n_chunks, C)
    head = lambda off: pl.BlockSpec((1, s, HEAD_DIM), lambda h, i: (i, 0, off + h))
    cw = lambda off: pl.BlockSpec((GDN_CONV, HEAD_DIM), lambda h, i: (0, off + h))
    vec = pl.BlockSpec((1, HEAD_DIM), lambda h, i: (0, 0))
    smem = pl.BlockSpec(memory_space=pltpu.SMEM)
    out = pl.BlockSpec((1, s, HEAD_DIM), lambda h, i: (i, 0, h))
    return pl.pallas_call(
        functools.partial(_mixer_kernel, group=math.gcd(group, n_chunks)),
        grid=(H, b),
        in_specs=[smem, smem, smem, head(0), head(H), head(2 * H), head(3 * H),
                  pl.BlockSpec((1, 2 * H, n_chunks, C), lambda h, i: (i, 0, 0, 0)),
                  cw(0), cw(H), cw(2 * H), vec,
                  head(0), head(H), head(2 * H), vec],
        out_specs=[out, out],
        out_shape=[jax.ShapeDtypeStruct((b, s, GDN_WIDTH), BF16), jax.ShapeDtypeStruct((b, s, MOBA_WIDTH), BF16)],
        scratch_shapes=[
            pltpu.VMEM((s, HEAD_DIM), F32),
            pltpu.VMEM((n_chunks, 2 * C, HEAD_DIM), BF16),
            pltpu.VMEM((s, C), BF16),
            pltpu.VMEM((n_chunks, HEAD_DIM, HEAD_DIM), BF16),
            pltpu.VMEM((n_chunks, HEAD_DIM, HEAD_DIM), F32),
            pltpu.VMEM((n_chunks, 1, HEAD_DIM), F32),
            pltpu.VMEM((n_chunks, HEAD_DIM, HEAD_DIM), BF16),
            pltpu.VMEM((HEAD_DIM + BF16_ROWS, s), BF16),
            pltpu.VMEM((nb, HEAD_DIM), F32),
            pltpu.VMEM((T, T), F32), pltpu.VMEM((T, T), F32)],
        compiler_params=pltpu.CompilerParams(
            dimension_semantics=("arbitrary", "arbitrary"), vmem_limit_bytes=VMEM_LIMIT),
        name="mixer",
    )(a_log, dt_bias, rel_bias, pa, pa, pa, pa, bat, conv_w, conv_w, conv_w, gdn_norm_g.reshape(1, HEAD_DIM),
      pb, pb, pb, moba_norm_g.reshape(1, HEAD_DIM))


def _xattn_kernel(x_ref, oa_ref, ob_ref, wa_ref, wb_ref, g_ref, wq_ref, kv_ref, wo_ref, ng_ref,
                  o_ref, hn_ref, *, rs):
    width = N_XATTN_HEADS * HEAD_DIM
    scale = HEAD_DIM ** -0.5 * LOG2E
    blocks = [slice(r * rs, (r + 1) * rs) for r in range(x_ref.shape[1] // rs)]
    x1s = [x_ref[0, rows, :]
           + jnp.dot(oa_ref[0, rows, :], wa_ref[...], preferred_element_type=F32)
           + jnp.dot(ob_ref[0, rows, :], wb_ref[...], preferred_element_type=F32) for rows in blocks]
    qs = [_bdot(_rms(x1, g_ref[...]), wq_ref[...]) * scale for x1 in x1s]
    outs = [[] for _ in blocks]
    for hd in range(N_XATTN_HEADS):
        sl = slice(hd * HEAD_DIM, (hd + 1) * HEAD_DIM)
        ss = [_bdot_nt(q[:, sl], kv_ref[0, :, sl]) for q in qs]
        ps = [jnp.exp2(s - jnp.max(s, axis=-1, keepdims=True)) for s in ss]
        for out, p in zip(outs, ps):
            l = jnp.sum(p, axis=-1, keepdims=True)
            out.append(_bdot(p, kv_ref[0, :, width + hd * HEAD_DIM: width + (hd + 1) * HEAD_DIM]) / l)
    ys = [x1 + _bdot(jnp.concatenate(out, axis=-1), wo_ref[...]) for x1, out in zip(x1s, outs)]
    for rows, y in zip(blocks, ys):
        o_ref[0, rows, :] = y
        hn_ref[0, rows, :] = _rms(y, ng_ref[...]).astype(BF16)


def _xattn(x, oa, ob, wa, wb, g, wq, kv, wo, next_g, *, ts):
    b, s, d = x.shape
    mlen = kv.shape[1]
    width = wq.shape[1]
    ka, kb = oa.shape[2], ob.shape[2]
    tile = pl.BlockSpec((1, ts, d), lambda i, j: (i, j, 0))
    vec = pl.BlockSpec((1, d), lambda i, j: (0, 0))
    whole = lambda r, c: pl.BlockSpec((r, c), lambda i, j: (0, 0))
    return pl.pallas_call(
        functools.partial(_xattn_kernel, rs=math.gcd(ts, 2 * ROW_SUB)),
        grid=(b, s // ts),
        in_specs=[tile,
                  pl.BlockSpec((1, ts, ka), lambda i, j: (i, j, 0)),
                  pl.BlockSpec((1, ts, kb), lambda i, j: (i, j, 0)),
                  whole(ka, d), whole(kb, d), vec, whole(d, width),
                  pl.BlockSpec((1, mlen, 2 * width), lambda i, j: (i, 0, 0)),
                  whole(width, d), vec],
        out_specs=[tile, tile],
        out_shape=[jax.ShapeDtypeStruct((b, s, d), F32), jax.ShapeDtypeStruct((b, s, d), BF16)],
        compiler_params=pltpu.CompilerParams(
            dimension_semantics=("parallel", "parallel"), vmem_limit_bytes=VMEM_LIMIT),
        name="xattn",
    )(x, oa, ob, wa, wb, g.reshape(1, d), wq, kv, wo, next_g.reshape(1, d))


def _ffn_kernel(x_hbm, h_ref, halo_ref, wg_ref, wu_ref, cw_ref, cb_ref, wd_ref, fg_ref, o_ref,
                hx_ref, xres_ref, sem, *, tiles_per_seq, rs):
    i = pl.program_id(0)
    f = pl.program_id(1)
    tm = h_ref.shape[0]
    pad = BF16_ROWS

    def x_copy():
        return pltpu.make_async_copy(x_hbm.at[pl.ds(i * tm, tm), :], xres_ref, sem)

    @pl.when(f == 0)
    def _():
        x_copy().start()
        halo = halo_ref[...]
        hx_ref[0:pad, :] = jnp.where(i % tiles_per_seq == 0, jnp.zeros_like(halo), halo)
        hx_ref[pad:pad + tm, :] = h_ref[...]
        o_ref[...] = jnp.zeros_like(o_ref)

    cw = cw_ref[...]
    keep = SUBLANES
    tail = None
    for r in range(tm // rs):
        rows = slice(r * rs, (r + 1) * rs)
        hrows = slice(pad + r * rs, pad + (r + 1) * rs)
        if r == 0:
            gp = jnp.dot(hx_ref[0:pad + rs, :], wg_ref[...], preferred_element_type=F32)[pad - keep:]
        else:
            gp = jnp.concatenate(
                [tail, jnp.dot(hx_ref[hrows, :], wg_ref[...], preferred_element_type=F32)], axis=0)
        tail = gp[rs:rs + keep]
        up = jnp.dot(hx_ref[hrows, :], wu_ref[...], preferred_element_type=F32)
        gate = gp[keep:keep + rs] * cw[2:3] + gp[keep - 1:keep - 1 + rs] * cw[1:2] \
            + gp[keep - 2:keep - 2 + rs] * cw[0:1] + cb_ref[...]
        act = _silu(gate) * up
        o_ref[rows, :] += jnp.dot(act.astype(BF16), wd_ref[...], preferred_element_type=F32)

    @pl.when(f == pl.num_programs(1) - 1)
    def _():
        x_copy().wait()

        def finish(r, carry):
            rows = pl.ds(pl.multiple_of(r * HEAD_DIM, HEAD_DIM), HEAD_DIM)
            o_ref[rows, :] = _rms(xres_ref[rows, :] + o_ref[rows, :], fg_ref[...])
            return carry

        lax.fori_loop(0, tm // HEAD_DIM, finish, 0)


def _ffn(x, h, wg, wu, cw, cb, wd, fg, *, seq, tm, tf):
    m, d = x.shape
    ff = wg.shape[1]
    assert seq % tm == 0 and ff % tf == 0 and tm % BF16_ROWS == 0
    hb = tm // BF16_ROWS
    return pl.pallas_call(
        functools.partial(_ffn_kernel, tiles_per_seq=seq // tm, rs=tm),
        grid=(m // tm, ff // tf),
        in_specs=[pl.BlockSpec(memory_space=pl.ANY),
                  pl.BlockSpec((tm, d), lambda i, f: (i, 0)),
                  pl.BlockSpec((BF16_ROWS, d), lambda i, f: (jnp.maximum(i * hb - 1, 0), 0)),
                  pl.BlockSpec((d, tf), lambda i, f: (0, f)),
                  pl.BlockSpec((d, tf), lambda i, f: (0, f)),
                  pl.BlockSpec((FFN_CONV, tf), lambda i, f: (0, f)),
                  pl.BlockSpec((1, tf), lambda i, f: (0, f)),
                  pl.BlockSpec((tf, d), lambda i, f: (f, 0)),
                  pl.BlockSpec((1, d), lambda i, f: (0, 0))],
        out_specs=pl.BlockSpec((tm, d), lambda i, f: (i, 0)),
        out_shape=jax.ShapeDtypeStruct((m, d), F32),
        scratch_shapes=[pltpu.VMEM((tm + BF16_ROWS, d), BF16), pltpu.VMEM((tm, d), F32),
                        pltpu.SemaphoreType.DMA(())],
        compiler_params=pltpu.CompilerParams(
            dimension_semantics=("arbitrary", "arbitrary"), vmem_limit_bytes=VMEM_LIMIT),
        name="ffn",
    )(x, h, h, wg, wu, cw, cb.reshape(1, ff), wd, fg.reshape(1, d))


def _layer(x, mem, mix_norm_g, w_in, gdn_conv_w, a_log, dt_bias, gdn_norm_g, moba_norm_g, rel_bias,
           w_out, xattn_norm_g, mem_norm_g, w_xq, w_xkv, w_xo, ffn_norm_g, w_gate, w_up, ffn_conv_w,
           ffn_conv_b, w_down, final_g, *, last):
    b, s, d = x.shape
    m = b * s
    x2 = x.reshape(m, d)
    i1 = 4 * GDN_WIDTH
    i3 = i1 + 2 * N_GDN_HEADS
    w_all = w_in.astype(BF16)
    w_b = w_all[:, i3:]
    wa_cols = i1 + HEAD_DIM
    pa = _norm_matmul(x2, mix_norm_g, w_all, tm=min(256, m), tn=wa_cols, n=wa_cols,
                      row_sub=ROW_SUB // 2).reshape(b, s, wa_cols)
    pb = _norm_matmul(x2, mix_norm_g, w_b, tm=min(512, m), tn=3 * MOBA_WIDTH,
                      out_dtype=BF16).reshape(b, s, 3 * MOBA_WIDTH)
    o_a, o_b = _mixer(pa, pb, gdn_conv_w, a_log, dt_bias, gdn_norm_g, rel_bias, moba_norm_g)
    mlen = mem.shape[1]
    kv = _norm_matmul(mem.reshape(b * mlen, d), mem_norm_g, w_xkv.astype(BF16),
                      tm=min(512, b * mlen), tn=1024, out_dtype=BF16).reshape(b, mlen, -1)
    x2b, h2 = _xattn(x, o_a, o_b, w_out[:GDN_WIDTH].astype(BF16), w_out[GDN_WIDTH:].astype(BF16),
                     xattn_norm_g, w_xq.astype(BF16), kv, w_xo.astype(BF16), ffn_norm_g, ts=min(512, s))
    assert last, "the final rmsnorm is fused into the last layer's ffn"
    y = _ffn(x2b.reshape(m, d), h2.reshape(m, d), w_gate.astype(BF16), w_up.astype(BF16), ffn_conv_w,
             ffn_conv_b, w_down.astype(BF16), final_g, seq=s, tm=min(1024, s), tf=512)
    return y.reshape(b, s, d)


def kernel(x, mem, mix_norm_g, w_in, gdn_conv_w, gdn_a_log, gdn_dt_bias, gdn_norm_g, moba_norm_g,
           rel_bias, w_out, xattn_norm_g, mem_norm_g, w_xq, w_xkv, w_xo, ffn_norm_g, w_gate, w_up,
           ffn_conv_w, ffn_conv_b, w_down, final_norm_g):
    depth = mix_norm_g.shape[0]
    assert depth == 1
    l = 0
    return _layer(x, mem, mix_norm_g[l], w_in[l], gdn_conv_w[l], gdn_a_log[l], gdn_dt_bias[l],
                  gdn_norm_g[l], moba_norm_g[l], rel_bias, w_out[l], xattn_norm_g[l], mem_norm_g[l],
                  w_xq[l], w_xkv[l], w_xo[l], ffn_norm_g[l], w_gate[l], w_up[l], ffn_conv_w[l],
                  ffn_conv_b[l], w_down[l], final_norm_g, last=True)
```

```python
import functools
import math

import jax
import jax.numpy as jnp
import numpy as np
from jax import lax
from jax.experimental import pallas as pl
from jax.experimental.pallas import tpu as pltpu

HEAD_DIM = 128
N_GDN_HEADS = 8
N_MOBA_HEADS = 8
GDN_WIDTH = N_GDN_HEADS * HEAD_DIM
MOBA_WIDTH = N_MOBA_HEADS * HEAD_DIM
GDN_CONV = 4
GDN_CHUNK = 256
GDN_FILL_PER_STEP = 2
GDN_HALF_ROWS_FROM = 64
MOBA_BLOCK = 256
MOBA_TOPK = 3
MOBA_TILES_IN_FLIGHT = 8
REL_BUCKETS = 32
REL_MAX_DIST = 128
N_XATTN_HEADS = 4
FFN_CONV = 3
EPS = 1e-6
NEG = -1e30
LOG2E = math.log2(math.e)
SUBLANES = 8
BF16_ROWS = 16
ROW_SUB = 256
VMEM_LIMIT = 58 * 1024 * 1024

F32 = jnp.float32
BF16 = jnp.bfloat16


def _bdot(a, b):
    return jnp.dot(a.astype(BF16), b.astype(BF16), preferred_element_type=F32)


def _bdot_nt(a, b):
    return lax.dot_general(a.astype(BF16), b.astype(BF16), (((1,), (1,)), ((), ())),
                           preferred_element_type=F32)


def _bdot_tn(a, b):
    return lax.dot_general(a.astype(BF16), b.astype(BF16), (((0,), (0,)), ((), ())),
                           preferred_element_type=F32)


def _sigmoid(x):
    return 1.0 / (1.0 + jnp.exp(-x))


def _silu(x):
    return x * _sigmoid(x)


def _rms(x, g):
    return x * lax.rsqrt(jnp.mean(x * x, axis=-1, keepdims=True) + EPS) * g


def _norm_matmul_kernel(x_ref, g_ref, w_ref, o_ref, *, rs):
    for r in range(x_ref.shape[0] // rs):
        rows = slice(r * rs, (r + 1) * rs)
        hn = _rms(x_ref[rows, :], g_ref[...]).astype(BF16)
        o_ref[rows, :] = jnp.dot(hn, w_ref[...], preferred_element_type=F32).astype(o_ref.dtype)


def _norm_matmul(x, g, w, *, tm, tn, n=None, out_dtype=F32, row_sub=ROW_SUB):
    m, k = x.shape
    n = w.shape[1] if n is None else n
    assert m % tm == 0 and n % tn == 0 and n <= w.shape[1]
    rs = math.gcd(tm, row_sub)
    return pl.pallas_call(
        functools.partial(_norm_matmul_kernel, rs=rs),
        grid=(m // tm, n // tn),
        in_specs=[pl.BlockSpec((tm, k), lambda i, j: (i, 0)),
                  pl.BlockSpec((1, k), lambda i, j: (0, 0)),
                  (pl.BlockSpec((k, tn), lambda i, j: (0, j), pipeline_mode=pl.Buffered(1)) if n == tn
                   else pl.BlockSpec((k, tn), lambda i, j: (0, j)))],
        out_specs=pl.BlockSpec((tm, tn), lambda i, j: (i, j)),
        out_shape=jax.ShapeDtypeStruct((m, n), out_dtype),
        compiler_params=pltpu.CompilerParams(
            dimension_semantics=("parallel", "arbitrary"), vmem_limit_bytes=VMEM_LIMIT),
        name="norm_matmul",
    )(x, g.reshape(1, k), w)


def _unit_lower_inverse(mats, row, col, fill=lambda: None):
    c = mats[0].shape[0]
    eye = (row == col).astype(F32)
    blk = lambda n: (row // n) == (col // n)
    inner = blk(16)
    ds = [jnp.where(inner, a, 0.0) for a in mats]
    ts = [eye - d for d in ds]
    ps = [_bdot(d, d) for d in ds]
    fill()
    for step in range(3):
        ts = [t + _bdot(t, p) for t, p in zip(ts, ps)]
        fill()
        if step < 2:
            ps = [_bdot(p, p) for p in ps]
            fill()
    half_row = lax.broadcasted_iota(jnp.int32, (c // 2, c), 0)
    half_col = lax.broadcasted_iota(jnp.int32, (c // 2, c), 1)
    n = 32
    while n <= c:
        h = n // 2
        pairs = range(c // n)
        lower = lambda m: jnp.concatenate([m[(2 * k + 1) * h:(2 * k + 2) * h] for k in pairs], axis=0)
        if n < GDN_HALF_ROWS_FROM:
            keep = blk(n) & ~inner
            ys = [lower(_bdot(jnp.where(keep, a, 0.0), t)) for a, t in zip(mats, ts)]
        else:
            left = (half_col // h) == 2 * (half_row // h)
            ys = [_bdot(jnp.where(left, lower(a), 0.0), t) for a, t in zip(mats, ts)]
        fill()
        zero = jnp.zeros((h, c), F32)
        spread = lambda y: jnp.concatenate(
            [part for k in pairs for part in (zero, y[k * h:(k + 1) * h])], axis=0)
        zs = [_bdot(lower(t), spread(y)) for t, y in zip(ts, ys)]
        ts = [jnp.concatenate([part for k in pairs for part in
                               (t[2 * k * h:(2 * k + 1) * h],
                                t[(2 * k + 1) * h:(2 * k + 2) * h] - z[k * h:(k + 1) * h])], axis=0)
              for t, z in zip(ts, zs)]
        fill()
        inner = blk(n)
        n *= 2
    return ts


def _gdn_body(h, alog_ref, dtb_ref, q_ref, k_ref, v_ref, z_ref, bat_ref, wq_ref, wk_ref, wv_ref,
              ng_ref, o_ref, u_ref, wqd_ref, qk_ref, kw_ref, bc_ref, gl_ref, sb_ref, *, group,
              fill=lambda n=1: None):
    seq = q_ref.shape[1]
    C = GDN_CHUNK
    D = HEAD_DIM
    pad = SUBLANES
    n_chunks = seq // C

    static_chunks = n_chunks == group
    if not static_chunks:
        fill = lambda n=1: None
    row = lax.broadcasted_iota(jnp.int32, (C, C), 0)
    col = lax.broadcasted_iota(jnp.int32, (C, C), 1)
    tri_incl = row >= col
    tri_strict = row > col
    lane = lax.broadcasted_iota(jnp.int32, (SUBLANES, C), 1)
    neg_a = -jnp.exp(jnp.full((1, C), alog_ref[h], F32))
    dt_bias = dtb_ref[h]
    scale = D ** -0.5

    def conv_silu(x_ref, w_ref, c, r0):
        if static_chunks:
            win = (jnp.concatenate([jnp.zeros((pad, D), F32), x_ref[0, 0:C, :]], axis=0) if c == 0
                   else x_ref[0, r0 - pad:r0 + C, :])
        else:
            halo = x_ref[0, pl.ds(jnp.maximum(r0 - pad, 0), pad), :]
            win = jnp.concatenate([jnp.where(c > 0, halo, 0.0), x_ref[0, pl.ds(r0, C), :]], axis=0)
        w = w_ref[...]
        prev = pltpu.roll(win, 1, axis=0)
        near = win * w[3:4] + prev * w[2:3]
        far = pltpu.roll(win * w[1:2] + prev * w[0:1], 2, axis=0)
        return _silu(near[pad:pad + C] + far[pad:pad + C])

    def l2n(x):
        return x * lax.rsqrt(jnp.sum(x * x, axis=-1, keepdims=True) + EPS)

    def cumsum_lanes(x):
        x = jnp.broadcast_to(x, (SUBLANES, C))
        s = 1
        while s < C:
            x = x + jnp.where(lane >= s, pltpu.roll(x, s, axis=1), 0.0)
            s *= 2
        return x[0:1, :]

    reps = C // D

    def rows_to_cols(x):
        return jnp.concatenate(
            [jnp.broadcast_to(x[:, n * D:(n + 1) * D], (D, D)).T for n in range(reps)], axis=0)

    def chunk_ids(grp):
        cs = [grp * group + i for i in range(group)]
        return cs, [c * C if static_chunks else pl.multiple_of(c * C, C) for c in cs]

    def for_each_group(body):
        if static_chunks:
            body(0, 0)
        else:
            lax.fori_loop(0, n_chunks // group, body, 0)

    def prepare(grp, carry):
        cs, r0s = chunk_ids(grp)
        pre, gbs, g_rows = [], [], []
        for c, r0 in zip(cs, r0s):
            q = l2n(conv_silu(q_ref, wq_ref, c, r0)) * scale
            k = l2n(conv_silu(k_ref, wk_ref, c, r0))
            v = conv_silu(v_ref, wv_ref, c, r0)
            b_row = bat_ref[0, h, pl.ds(c, 1), :]
            xs = bat_ref[0, h + N_GDN_HEADS, pl.ds(c, 1), :] + dt_bias
            softplus = jnp.maximum(xs, 0.0) + jnp.log1p(jnp.exp(-jnp.abs(xs)))
            g_row = cumsum_lanes(neg_a * softplus)
            pre.append((q, k, v, rows_to_cols(_sigmoid(b_row)), None))
            g_rows.append(g_row)
            gbs.append(rows_to_cols(g_row))
            fill()
        kk_qks = [_bdot_nt(jnp.concatenate([k * beta, q], axis=0), k) for q, k, v, beta, _ in pre]
        mats, mids = [], []
        for (q, k, v, beta, _), gb, g_row, kk_qk in zip(pre, gbs, g_rows, kk_qks):
            g_i = jnp.concatenate([gb] * reps, axis=1)
            g_j = jnp.broadcast_to(g_row, (C, C))
            decay = jnp.exp(jnp.where(tri_incl, g_i - g_j, NEG))
            mats.append(jnp.where(tri_strict, kk_qk[:C] * decay, 0.0))
            mids.append((kk_qk[C:] * decay, jnp.exp(gb), gb[C - 1:C, :]))
        ts = _unit_lower_inverse(mats, row, col, fill=fill)
        uws = [_bdot(t, jnp.concatenate([v * beta, k * beta * eg], axis=1))
               for t, (q, k, v, beta, _), (_, eg, _) in zip(ts, pre, mids)]
        transs = [_bdot_tn(k * jnp.exp(g_last - gb), uw)
                  for uw, (q, k, v, beta, _), gb, (_, _, g_last) in zip(uws, pre, gbs, mids)]
        for c, r0, uw, trans, (q, k, v, beta, _), (qk, eg, g_last) in zip(cs, r0s, uws, transs, pre, mids):
            u_ref[pl.ds(r0, C), :] = uw[:, :D]
            wqd_ref[c, 0:C, :] = uw[:, D:].astype(BF16)
            wqd_ref[c, C:2 * C, :] = (q * eg).astype(BF16)
            qk_ref[pl.ds(r0, C), :] = qk.astype(BF16)
            bc_ref[c] = trans[:, :D]
            kw_ref[c] = trans[:, D:].astype(BF16)
            gl_ref[c] = jnp.exp(g_last)
        return carry

    for_each_group(prepare)

    def chain(c, state):
        sb = state.astype(BF16)
        sb_ref[c] = sb
        return state * gl_ref[c] - jnp.dot(kw_ref[c], sb, preferred_element_type=F32) + bc_ref[c]

    if static_chunks:
        state = jnp.zeros((D, D), F32)
        for c in range(n_chunks):
            state = chain(c, state)
            fill(GDN_FILL_PER_STEP)
    else:
        lax.fori_loop(0, n_chunks, chain, jnp.zeros((D, D), F32))

    def outputs(grp, carry):
        cs, r0s = chunk_ids(grp)
        ws_qs = [jnp.dot(wqd_ref[c], sb_ref[c], preferred_element_type=F32) for c in cs]
        v_news = [(u_ref[pl.ds(r0, C), :] - wq[:C]).astype(BF16) for r0, wq in zip(r0s, ws_qs)]
        os = [wq[C:] + jnp.dot(qk_ref[pl.ds(r0, C), :], vn, preferred_element_type=F32)
              for r0, wq, vn in zip(r0s, ws_qs, v_news)]
        for r0, o in zip(r0s, os):
            z = z_ref[0, pl.ds(r0, C), :]
            o_ref[0, pl.ds(r0, C), :] = (_rms(o, ng_ref[...]) * _silu(z)).astype(o_ref.dtype)
        return carry

    for_each_group(outputs)


def _bucket_upper_bounds():
    n = np.arange(0, 4 * REL_MAX_DIST, dtype=np.int64)
    max_exact = REL_BUCKETS // 2
    nf = np.maximum(n, 1).astype(np.float32)
    large = max_exact + (np.log(nf / np.float32(max_exact)) / np.float32(math.log(REL_MAX_DIST / max_exact))
                         * np.float32(REL_BUCKETS - max_exact)).astype(np.int32)
    large = np.minimum(large, REL_BUCKETS - 1)
    bucket = np.where(n < max_exact, n, large)
    assert np.all(np.diff(bucket) >= 0) and bucket[-1] == REL_BUCKETS - 1
    return [int(np.searchsorted(bucket, b, side="right")) for b in range(REL_BUCKETS - 1)]


_BUCKET_UPPER = _bucket_upper_bounds()


def _moba_steps(h, b, rb_ref, q_ref, k_ref, v_ref, ng_ref, o_ref, vt_ref, km_ref, bd_ref, bl_ref):
    seq = k_ref.shape[1]
    T = MOBA_BLOCK
    nb = seq // T
    scale = HEAD_DIM ** -0.5 * LOG2E
    kk = lax.broadcasted_iota(jnp.int32, (T, T), 0)
    qq = lax.broadcasted_iota(jnp.int32, (T, T), 1)

    @pl.when(b == 0)
    def _():
        def bias_of(n):
            val = jnp.full((T, T), rb_ref[REL_BUCKETS - 1, h], F32)
            for bkt in range(REL_BUCKETS - 2, -1, -1):
                val = jnp.where(n < _BUCKET_UPPER[bkt], rb_ref[bkt, h], val)
            return val
        bd_ref[...] = jnp.where(qq >= kk, bias_of(qq - kk) * LOG2E, NEG)
        bl_ref[...] = bias_of(qq - kk + T) * LOG2E

    for n in range(nb):
        blk_rows = slice(n * T, (n + 1) * T)
        vt_ref[0:HEAD_DIM, blk_rows] = v_ref[0, blk_rows, :].astype(F32).T.astype(BF16)
        km_ref[n:n + 1, :] = jnp.mean(k_ref[0, blk_rows, :].astype(F32), axis=0, keepdims=True)
    vt_ref[HEAD_DIM:HEAD_DIM + BF16_ROWS, :] = jnp.ones((BF16_ROWS, seq), BF16)
    far_bias = rb_ref[REL_BUCKETS - 1, h] * LOG2E

    def additive_mask(i, qf):
        if i <= MOBA_TOPK:
            return None
        gate = lax.dot_general(km_ref[0:i, :], qf, (((1,), (1,)), ((), ())),
                               preferred_element_type=F32, precision=lax.Precision.HIGHEST)
        blk = lax.broadcasted_iota(jnp.int32, (i, T), 0)
        rank = jnp.zeros((i, T), jnp.int32)
        for m in range(i):
            gm = gate[m:m + 1, :]
            rank = rank + jnp.where((gm > gate) | ((gm == gate) & (m < blk)), 1, 0)
        return jnp.where(rank < MOBA_TOPK, 0.0, NEG)

    def biased(i, s_all, keep):
        parts, maxes = [], []
        for j in range(i + 1):
            sj = s_all[j * T:(j + 1) * T, :]
            if j >= i - 1:
                sj = sj + (bd_ref[...] if j == i else bl_ref[...])
                row = keep[j:j + 1, :] if (keep is not None and j < i) else 0.0
            else:
                row = far_bias if keep is None else keep[j:j + 1, :] + far_bias
            parts.append((sj, row))
            maxes.append(jnp.max(sj, axis=0, keepdims=True) + row)
        return parts, functools.reduce(jnp.maximum, maxes)

    order = [t for pair in zip(range(nb // 2), range(nb - 1, nb // 2 - 1, -1)) for t in pair]
    if nb % 2:
        order.append(nb // 2)
    for g0 in range(0, nb, MOBA_TILES_IN_FLIGHT):
        tiles = order[g0:g0 + MOBA_TILES_IN_FLIGHT]
        qfs = [q_ref[0, i * T:(i + 1) * T, :].astype(F32) for i in tiles]
        keeps, s_alls, partss, m_rows, p_alls, o_augs = [], [], [], [], [], []
        for i, qf in zip(tiles, qfs):
            keeps.append(additive_mask(i, qf))
            s_alls.append(_bdot_nt(k_ref[0, 0:(i + 1) * T, :], qf * scale))
            yield
        for i, s_all, keep in zip(tiles, s_alls, keeps):
            parts, m_row = biased(i, s_all, keep)
            partss.append(parts)
            m_rows.append(m_row)
            yield
        for parts, m_row in zip(partss, m_rows):
            p_alls.append(jnp.concatenate(
                [jnp.exp2((sj + (row - m_row)).astype(BF16)) for sj, row in parts], axis=0))
            yield
        for i, p_all in zip(tiles, p_alls):
            o_augs.append(jnp.dot(vt_ref[:, 0:(i + 1) * T], p_all, preferred_element_type=F32))
            yield
        for i, o in zip(tiles, o_augs):
            o_t = o[0:HEAD_DIM] / o[HEAD_DIM:HEAD_DIM + 1]
            o_ref[0, i * T:(i + 1) * T, :] = _rms(o_t.T, ng_ref[...]).astype(o_ref.dtype)
            yield


def _mixer_kernel(alog_ref, dtb_ref, rb_ref, q_ref, k_ref, v_ref, z_ref, bat_ref, wq_ref, wk_ref, wv_ref,
                  gng_ref, mq_ref, mk_ref, mv_ref, mng_ref, oa_ref, ob_ref,
                  u_ref, wqd_ref, qk_ref, kw_ref, bc_ref, gl_ref, sb_ref, vt_ref, km_ref, bd_ref, bl_ref,
                  *, group):
    h = pl.program_id(0)
    b = pl.program_id(1)
    moba = _moba_steps(h, b, rb_ref, mq_ref, mk_ref, mv_ref, mng_ref, ob_ref, vt_ref, km_ref, bd_ref, bl_ref)

    def fill(n=1):
        for _ in range(n):
            next(moba, None)

    _gdn_body(h, alog_ref, dtb_ref, q_ref, k_ref, v_ref, z_ref, bat_ref, wq_ref, wk_ref, wv_ref,
              gng_ref, oa_ref, u_ref, wqd_ref, qk_ref, kw_ref, bc_ref, gl_ref, sb_ref, group=group, fill=fill)
    for _ in moba:
        pass


def _mixer(pa, pb, conv_w, a_log, dt_bias, gdn_norm_g, rel_bias, moba_norm_g, *, group=8):
    b, s, _ = pa.shape
    H = N_GDN_HEADS
    assert N_MOBA_HEADS == H
    C, T = GDN_CHUNK, MOBA_BLOCK
    assert s % C == 0 and s % T == 0
    n_chunks, nb = s // C, s // T
    bat = pa[:, :, 4 * GDN_WIDTH:4 * GDN_WIDTH + 2 * H].transpose(0, 2, 1).reshape(b, 2 * H, n_chunks, C)
    head = lambda off: pl.BlockSpec((1, s, HEAD_DIM), lambda h, i: (i, 0, off + h))
    cw = lambda off: pl.BlockSpec((GDN_CONV, HEAD_DIM), lambda h, i: (0, off + h))
    vec = pl.BlockSpec((1, HEAD_DIM), lambda h, i: (0, 0))
    smem = pl.BlockSpec(memory_space=pltpu.SMEM)
    out = pl.BlockSpec((1, s, HEAD_DIM), lambda h, i: (i, 0, h))
    return pl.pallas_call(
        functools.partial(_mixer_kernel, group=math.gcd(group, n_chunks)),
        grid=(H, b),
        in_specs=[smem, smem, smem, head(0), head(H), head(2 * H), head(3 * H),
                  pl.BlockSpec((1, 2 * H, n_chunks, C), lambda h, i: (i, 0, 0, 0)),
                  cw(0), cw(H), cw(2 * H), vec,
                  head(0), head(H), head(2 * H), vec],
        out_specs=[out, out],
        out_shape=[jax.ShapeDtypeStruct((b, s, GDN_WIDTH), BF16), jax.ShapeDtypeStruct((b, s, MOBA_WIDTH), BF16)],
        scratch_shapes=[
            pltpu.VMEM((s, HEAD_DIM), F32),
            pltpu.VMEM((n_chunks, 2 * C, HEAD_DIM), BF16),
            pltpu.VMEM((s, C), BF16),
            pltpu.VMEM((n_chunks, HEAD_DIM, HEAD_DIM), BF16),
            pltpu.VMEM((n_chunks, HEAD_DIM, HEAD_DIM), F32),
            pltpu.VMEM((n_chunks, 1, HEAD_DIM), F32),
            pltpu.VMEM((n_chunks, HEAD_DIM, HEAD_DIM), BF16),
            pltpu.VMEM((HEAD_DIM + BF16_ROWS, s), BF16),
            pltpu.VMEM((nb, HEAD_DIM), F32),
            pltpu.VMEM((T, T), F32), pltpu.VMEM((T, T), F32)],
        compiler_params=pltpu.CompilerParams(
            dimension_semantics=("arbitrary", "arbitrary"), vmem_limit_bytes=VMEM_LIMIT),
        name="mixer",
    )(a_log, dt_bias, rel_bias, pa, pa, pa, pa, bat, conv_w, conv_w, conv_w, gdn_norm_g.reshape(1, HEAD_DIM),
      pb, pb, pb, moba_norm_g.reshape(1, HEAD_DIM))


def _xattn_kernel(x_ref, oa_ref, ob_ref, wa_ref, wb_ref, g_ref, wq_ref, kv_ref, wo_ref, ng_ref,
                  o_ref, hn_ref, *, rs):
    width = N_XATTN_HEADS * HEAD_DIM
    scale = HEAD_DIM ** -0.5 * LOG2E
    blocks = [slice(r * rs, (r + 1) * rs) for r in range(x_ref.shape[1] // rs)]
    x1s = [x_ref[0, rows, :]
           + jnp.dot(oa_ref[0, rows, :], wa_ref[...], preferred_element_type=F32)
           + jnp.dot(ob_ref[0, rows, :], wb_ref[...], preferred_element_type=F32) for rows in blocks]
    qs = [_bdot(_rms(x1, g_ref[...]), wq_ref[...]) * scale for x1 in x1s]
    outs = [[] for _ in blocks]
    for hd in range(N_XATTN_HEADS):
        sl = slice(hd * HEAD_DIM, (hd + 1) * HEAD_DIM)
        ss = [_bdot_nt(q[:, sl], kv_ref[0, :, sl]) for q in qs]
        ps = [jnp.exp2(s - jnp.max(s, axis=-1, keepdims=True)) for s in ss]
        for out, p in zip(outs, ps):
            l = jnp.sum(p, axis=-1, keepdims=True)
            out.append(_bdot(p, kv_ref[0, :, width + hd * HEAD_DIM: width + (hd + 1) * HEAD_DIM]) / l)
    ys = [x1 + _bdot(jnp.concatenate(out, axis=-1), wo_ref[...]) for x1, out in zip(x1s, outs)]
    for rows, y in zip(blocks, ys):
        o_ref[0, rows, :] = y
        hn_ref[0, rows, :] = _rms(y, ng_ref[...]).astype(BF16)


def _xattn(x, oa, ob, wa, wb, g, wq, kv, wo, next_g, *, ts):
    b, s, d = x.shape
    mlen = kv.shape[1]
    width = wq.shape[1]
    ka, kb = oa.shape[2], ob.shape[2]
    tile = pl.BlockSpec((1, ts, d), lambda i, j: (i, j, 0))
    vec = pl.BlockSpec((1, d), lambda i, j: (0, 0))
    whole = lambda r, c: pl.BlockSpec((r, c), lambda i, j: (0, 0))
    return pl.pallas_call(
        functools.partial(_xattn_kernel, rs=math.gcd(ts, 2 * ROW_SUB)),
        grid=(b, s // ts),
        in_specs=[tile,
                  pl.BlockSpec((1, ts, ka), lambda i, j: (i, j, 0)),
                  pl.BlockSpec((1, ts, kb), lambda i, j: (i, j, 0)),
                  whole(ka, d), whole(kb, d), vec, whole(d, width),
                  pl.BlockSpec((1, mlen, 2 * width), lambda i, j: (i, 0, 0)),
                  whole(width, d), vec],
        out_specs=[tile, tile],
        out_shape=[jax.ShapeDtypeStruct((b, s, d), F32), jax.ShapeDtypeStruct((b, s, d), BF16)],
        compiler_params=pltpu.CompilerParams(
            dimension_semantics=("parallel", "parallel"), vmem_limit_bytes=VMEM_LIMIT),
        name="xattn",
    )(x, oa, ob, wa, wb, g.reshape(1, d), wq, kv, wo, next_g.reshape(1, d))


def _ffn_kernel(x_hbm, h_ref, halo_ref, wg_ref, wu_ref, cw_ref, cb_ref, wd_ref, fg_ref, o_ref,
                hx_ref, xres_ref, sem, *, tiles_per_seq, rs):
    i = pl.program_id(0)
    f = pl.program_id(1)
    tm = h_ref.shape[0]
    pad = BF16_ROWS

    def x_copy():
        return pltpu.make_async_copy(x_hbm.at[pl.ds(i * tm, tm), :], xres_ref, sem)

    @pl.when(f == 0)
    def _():
        x_copy().start()
        halo = halo_ref[...]
        hx_ref[0:pad, :] = jnp.where(i % tiles_per_seq == 0, jnp.zeros_like(halo), halo)
        hx_ref[pad:pad + tm, :] = h_ref[...]
        o_ref[...] = jnp.zeros_like(o_ref)

    cw = cw_ref[...]
    keep = SUBLANES
    tail = None
    for r in range(tm // rs):
        rows = slice(r * rs, (r + 1) * rs)
        hrows = slice(pad + r * rs, pad + (r + 1) * rs)
        if r == 0:
            gp = jnp.dot(hx_ref[0:pad + rs, :], wg_ref[...], preferred_element_type=F32)[pad - keep:]
        else:
            gp = jnp.concatenate(
                [tail, jnp.dot(hx_ref[hrows, :], wg_ref[...], preferred_element_type=F32)], axis=0)
        tail = gp[rs:rs + keep]
        up = jnp.dot(hx_ref[hrows, :], wu_ref[...], preferred_element_type=F32)
        gate = gp[keep:keep + rs] * cw[2:3] + gp[keep - 1:keep - 1 + rs] * cw[1:2] \
            + gp[keep - 2:keep - 2 + rs] * cw[0:1] + cb_ref[...]
        act = _silu(gate) * up
        o_ref[rows, :] += jnp.dot(act.astype(BF16), wd_ref[...], preferred_element_type=F32)

    @pl.when(f == pl.num_programs(1) - 1)
    def _():
        x_copy().wait()

        def finish(r, carry):
            rows = pl.ds(pl.multiple_of(r * HEAD_DIM, HEAD_DIM), HEAD_DIM)
            o_ref[rows, :] = _rms(xres_ref[rows, :] + o_ref[rows, :], fg_ref[...])
            return carry

        lax.fori_loop(0, tm // HEAD_DIM, finish, 0)


def _ffn(x, h, wg, wu, cw, cb, wd, fg, *, seq, tm, tf):
    m, d = x.shape
    ff = wg.shape[1]
    assert seq % tm == 0 and ff % tf == 0 and tm % BF16_ROWS == 0
    hb = tm // BF16_ROWS
    return pl.pallas_call(
        functools.partial(_ffn_kernel, tiles_per_seq=seq // tm, rs=tm),
        grid=(m // tm, ff // tf),
        in_specs=[pl.BlockSpec(memory_space=pl.ANY),
                  pl.BlockSpec((tm, d), lambda i, f: (i, 0)),
                  pl.BlockSpec((BF16_ROWS, d), lambda i, f: (jnp.maximum(i * hb - 1, 0), 0)),
                  pl.BlockSpec((d, tf), lambda i, f: (0, f)),
                  pl.BlockSpec((d, tf), lambda i, f: (0, f)),
                  pl.BlockSpec((FFN_CONV, tf), lambda i, f: (0, f)),
                  pl.BlockSpec((1, tf), lambda i, f: (0, f)),
                  pl.BlockSpec((tf, d), lambda i, f: (f, 0)),
                  pl.BlockSpec((1, d), lambda i, f: (0, 0))],
        out_specs=pl.BlockSpec((tm, d), lambda i, f: (i, 0)),
        out_shape=jax.ShapeDtypeStruct((m, d), F32),
        scratch_shapes=[pltpu.VMEM((tm + BF16_ROWS, d), BF16), pltpu.VMEM((tm, d), F32),
                        pltpu.SemaphoreType.DMA(())],
        compiler_params=pltpu.CompilerParams(
            dimension_semantics=("arbitrary", "arbitrary"), vmem_limit_bytes=VMEM_LIMIT),
        name="ffn",
    )(x, h, h, wg, wu, cw, cb.reshape(1, ff), wd, fg.reshape(1, d))


def _layer(x, mem, mix_norm_g, w_in, gdn_conv_w, a_log, dt_bias, gdn_norm_g, moba_norm_g, rel_bias,
           w_out, xattn_norm_g, mem_norm_g, w_xq, w_xkv, w_xo, ffn_norm_g, w_gate, w_up, ffn_conv_w,
           ffn_conv_b, w_down, final_g, *, last):
    b, s, d = x.shape
    m = b * s
    x2 = x.reshape(m, d)
    i1 = 4 * GDN_WIDTH
    i3 = i1 + 2 * N_GDN_HEADS
    w_all = w_in.astype(BF16)
    w_b = w_all[:, i3:]
    wa_cols = i1 + HEAD_DIM
    pa = _norm_matmul(x2, mix_norm_g, w_all, tm=min(512, m), tn=wa_cols, n=wa_cols,
                      row_sub=ROW_SUB // 2).reshape(b, s, wa_cols)
    pb = _norm_matmul(x2, mix_norm_g, w_b, tm=min(512, m), tn=3 * MOBA_WIDTH,
                      out_dtype=BF16).reshape(b, s, 3 * MOBA_WIDTH)
    o_a, o_b = _mixer(pa, pb, gdn_conv_w, a_log, dt_bias, gdn_norm_g, rel_bias, moba_norm_g)
    mlen = mem.shape[1]
    kv = _norm_matmul(mem.reshape(b * mlen, d), mem_norm_g, w_xkv.astype(BF16),
                      tm=min(512, b * mlen), tn=1024, out_dtype=BF16).reshape(b, mlen, -1)
    x2b, h2 = _xattn(x, o_a, o_b, w_out[:GDN_WIDTH].astype(BF16), w_out[GDN_WIDTH:].astype(BF16),
                     xattn_norm_g, w_xq.astype(BF16), kv, w_xo.astype(BF16), ffn_norm_g, ts=min(512, s))
    assert last, "the final rmsnorm is fused into the last layer's ffn"
    y = _ffn(x2b.reshape(m, d), h2.reshape(m, d), w_gate.astype(BF16), w_up.astype(BF16), ffn_conv_w,
             ffn_conv_b, w_down.astype(BF16), final_g, seq=s, tm=min(1024, s), tf=512)
    return y.reshape(b, s, d)


def kernel(x, mem, mix_norm_g, w_in, gdn_conv_w, gdn_a_log, gdn_dt_bias, gdn_norm_g, moba_norm_g,
           rel_bias, w_out, xattn_norm_g, mem_norm_g, w_xq, w_xkv, w_xo, ffn_norm_g, w_gate, w_up,
           ffn_conv_w, ffn_conv_b, w_down, final_norm_g):
    depth = mix_norm_g.shape[0]
    assert depth == 1
    l = 0
    return _layer(x, mem, mix_norm_g[l], w_in[l], gdn_conv_w[l], gdn_a_log[l], gdn_dt_bias[l],
                  gdn_norm_g[l], moba_norm_g[l], rel_bias, w_out[l], xattn_norm_g[l], mem_norm_g[l],
                  w_xq[l], w_xkv[l], w_xo[l], ffn_norm_g[l], w_gate[l], w_up[l], ffn_conv_w[l],
                  ffn_conv_b[l], w_down[l], final_norm_g, last=True)
```
